```python
import jax, jax.numpy as jnp
from jax import lax
import numpy as np

D_MODEL = 1024
BATCH = 32
SEQ = 2048
DEPTH = 2
DEC_BATCH = 32
DEC_SEQ = 32
PAST_LEN = 4096

CHUNK = 64
D_MIX = D_MODEL
W_GROUP = D_MIX // 4
A_HEADS = 4
A_HEAD_DIM = W_GROUP // A_HEADS
A_CHUNK = 128
B_HEADS = 4
B_HEAD_DIM = W_GROUP // B_HEADS
BAND_CHUNKS = 8
B_WINDOW = BAND_CHUNKS * CHUNK
B_KBAND = (BAND_CHUNKS + 1) * CHUNK
REL_CLIP = 128
C_CHANNELS = W_GROUP
C_WIDTH = 31
C_BUF = C_WIDTH - 1
D_HEADS = 4
D_KEY = W_GROUP // D_HEADS
D_VAL = W_GROUP // D_HEADS
D_GATE_RANK = 16
GLA_TAU = 16.0
GLA_CHUNK = 16
N_EXPERT_GROUPS = 4
EXPERTS_PER_GROUP = 8
N_EXPERTS = N_EXPERT_GROUPS * EXPERTS_PER_GROUP
TOP_K_IN_GROUP = 2
D_EXPERT = 512
MOE_BLOCK = 128

EPS = 1e-6
NEG_INF = -1e30
IN_SIZES = (W_GROUP, W_GROUP, W_GROUP, W_GROUP, W_GROUP, W_GROUP, W_GROUP,
            W_GROUP, W_GROUP, W_GROUP, W_GROUP, D_GATE_RANK)
D_IN = sum(IN_SIZES)
IN_SPLITS = tuple(int(s) for s in np.cumsum(IN_SIZES)[:-1])

kernel_name = "hybrid_streaming_encoder_step"


def rmsnorm(x, g):
    x32 = x.astype(jnp.float32)
    y = x32 * lax.rsqrt(jnp.mean(x32 * x32, axis=-1, keepdims=True) + EPS)
    return (y * g.astype(jnp.float32)).astype(x.dtype)


def layernorm(x, g, b):
    x32 = x.astype(jnp.float32)
    mu = jnp.mean(x32, axis=-1, keepdims=True)
    xc = x32 - mu
    y = xc * lax.rsqrt(jnp.mean(xc * xc, axis=-1, keepdims=True) + EPS)
    return (y * g.astype(jnp.float32) + b.astype(jnp.float32)).astype(x.dtype)


def spatial_gate(u, v, ws, bs):
    bsz, T, _ = v.shape
    L = min(T, A_CHUNK)
    vh = v.reshape(bsz, T // L, L, A_HEADS, A_HEAD_DIM)
    ws_m = ws[:, :L, :L] * jnp.tril(jnp.ones((L, L), ws.dtype))
    sv = jnp.einsum('hij,bnjhd->bnihd', ws_m, vh) + bs[:, :L].T[:, :, None]
    return u * sv.reshape(bsz, T, W_GROUP)


def rel_bias(table, d):
    return table[:, jnp.clip(d, -REL_CLIP, REL_CLIP) + REL_CLIP].astype(jnp.float32)


def band_attention_prompt(q, k, v, rel_table):
    bsz, T, H, DH = q.shape
    n_chunks = T // CHUNK
    pad = ((0, 0), (B_WINDOW, 0), (0, 0), (0, 0))
    kp, vp = jnp.pad(k, pad), jnp.pad(v, pad)
    qc = q.reshape(bsz, n_chunks, CHUNK, H, DH).transpose(1, 0, 2, 3, 4)
    i = jnp.arange(CHUNK)[:, None]
    m = jnp.arange(B_KBAND)[None, :]
    bias = rel_bias(rel_table, i + B_WINDOW - m)

    def one_chunk(args):
        c, qb = args
        kb = lax.dynamic_slice_in_dim(kp, c * CHUNK, B_KBAND, axis=1)
        vb = lax.dynamic_slice_in_dim(vp, c * CHUNK, B_KBAND, axis=1)
        s = jnp.einsum('bihd,bjhd->bhij', qb, kb).astype(jnp.float32) + bias
        visible = (c - BAND_CHUNKS) * CHUNK + jnp.arange(B_KBAND) >= 0
        p = jax.nn.softmax(jnp.where(visible, s, NEG_INF), axis=-1)
        return jnp.einsum('bhij,bjhd->bihd', p.astype(vb.dtype), vb)

    o = lax.map(one_chunk, (jnp.arange(n_chunks), qc))
    return o.transpose(1, 0, 2, 3, 4).reshape(bsz, T, H, DH)


def band_attention_step(q, k, v, cache_k, cache_v, rel_table):
    T = q.shape[1]
    LB = cache_k.shape[1]
    k_all = jnp.concatenate([cache_k.astype(k.dtype), k], axis=1)
    v_all = jnp.concatenate([cache_v.astype(v.dtype), v], axis=1)
    qpos = PAST_LEN + jnp.arange(T)
    kpos = jnp.concatenate([PAST_LEN - LB + jnp.arange(LB), PAST_LEN + jnp.arange(T)])
    s = jnp.einsum('bihd,bjhd->bhij', q, k_all).astype(jnp.float32) + rel_bias(rel_table, qpos[:, None] - kpos[None, :])
    p = jax.nn.softmax(s, axis=-1)
    return jnp.einsum('bhij,bjhd->bihd', p.astype(v_all.dtype), v_all)


def conv_module(glu, buf, dw, dw_b, ln_g, ln_b):
    xp = jnp.concatenate([buf.astype(glu.dtype), glu], axis=1)
    y = lax.conv_general_dilated(xp, dw[:, None, :].astype(xp.dtype), window_strides=(1,), padding='VALID',
                                 dimension_numbers=('NWC', 'WIO', 'NWC'), feature_group_count=C_CHANNELS)
    y = layernorm(y + dw_b.astype(y.dtype), ln_g, ln_b)
    return jax.nn.silu(y), xp[:, -C_BUF:]


def gla_chunked(q, k, v, log_a, s0):
    bsz, T, H, _ = q.shape
    L = GLA_CHUNK
    pad = (-T) % L

    def blocks(t):
        t = jnp.pad(t.astype(jnp.float32), ((0, 0), (0, pad), (0, 0), (0, 0)))
        return t.reshape(bsz, -1, L, H, t.shape[-1]).transpose(1, 0, 3, 2, 4)

    qb, kb, vb, gb = blocks(q), blocks(k), blocks(v), blocks(log_a)
    b = jnp.cumsum(gb, axis=3)
    b_last = b[:, :, :, -1:, :]
    q_in = qb * jnp.exp(b)
    k_in = kb * jnp.exp(-b)
    k_st = kb * jnp.exp(b_last - b)
    causal = jnp.tril(jnp.ones((L, L), bool))
    att = jnp.where(causal, jnp.einsum('nbhid,nbhjd->nbhij', q_in, k_in), 0.0)
    o_intra = jnp.einsum('nbhij,nbhjv->nbhiv', att, vb)

    def step(S, xs):
        qi, ks, vv, bl = xs
        o = jnp.einsum('bhid,bhdv->bhiv', qi, S)
        S = S * jnp.exp(bl)[:, :, 0, :, None] + jnp.einsum('bhid,bhiv->bhdv', ks, vv)
        return S, o

    S, o_inter = lax.scan(step, s0.astype(jnp.float32), (q_in, k_st, vb, b_last))
    o = (o_intra + o_inter).transpose(1, 0, 3, 2, 4).reshape(bsz, -1, H, vb.shape[-1])[:, :T]
    return o.astype(v.dtype), S.astype(s0.dtype)


def routed_experts(x, eid, gate, w_gate, w_up, w_down):
    n, d = x.shape
    M = eid.shape[0]
    K = M // n
    tok = jnp.arange(M, dtype=jnp.int32) // K
    counts = jnp.bincount(eid, length=N_EXPERTS)
    padded = (counts + MOE_BLOCK - 1) // MOE_BLOCK * MOE_BLOCK
    pad_end = jnp.cumsum(padded)
    pad_start = pad_end - padded
    cnt_start = jnp.cumsum(counts) - counts
    order = jnp.argsort(eid)
    e_s = eid[order]
    dest = pad_start[e_s] + jnp.arange(M) - cnt_start[e_s]
    n_blocks = -(-(M + N_EXPERTS * (MOE_BLOCK - 1)) // MOE_BLOCK)
    P = n_blocks * MOE_BLOCK
    slot_tok = jnp.full((P,), n, jnp.int32).at[dest].set(tok[order])
    slot_gate = jnp.zeros((P,), jnp.float32).at[dest].set(gate[order])
    blk_exp = jnp.minimum(jnp.searchsorted(pad_end, jnp.arange(n_blocks) * MOE_BLOCK, side='right'), N_EXPERTS - 1)
    xs = jnp.concatenate([x, jnp.zeros((1, d), x.dtype)], axis=0)[slot_tok].reshape(n_blocks, MOE_BLOCK, d)

    def expert_block(args):
        xb, e = args
        h = jax.nn.silu(xb @ w_gate[e]) * (xb @ w_up[e])
        return h @ w_down[e]

    ys = lax.map(expert_block, (xs, blk_exp)).reshape(P, d)
    out = jax.ops.segment_sum(ys * slot_gate[:, None].astype(ys.dtype), slot_tok, num_segments=n + 1)
    return out[:n]


def hier_moe(x, wg, bg, we, be, w_gate, w_up, w_down):
    shape = x.shape
    xt = x.reshape(-1, shape[-1])
    n = xt.shape[0]
    g_logits = (xt @ wg).astype(jnp.float32) + bg.astype(jnp.float32)
    grp = jnp.argmax(g_logits, axis=-1)
    p_grp = jnp.take_along_axis(jax.nn.softmax(g_logits, axis=-1), grp[:, None], axis=-1)
    e_logits = ((xt @ we).astype(jnp.float32) + be.astype(jnp.float32)).reshape(n, N_EXPERT_GROUPS, EXPERTS_PER_GROUP)
    e_logits = jnp.take_along_axis(e_logits, grp[:, None, None], axis=1)[:, 0]
    top_v, top_i = lax.top_k(e_logits, TOP_K_IN_GROUP)
    gate = jax.nn.softmax(top_v, axis=-1) * p_grp
    eid = (grp[:, None] * EXPERTS_PER_GROUP + top_i).astype(jnp.int32)
    y = routed_experts(xt, eid.reshape(-1), gate.reshape(-1), w_gate, w_up, w_down)
    return y.reshape(shape)


def mix_layer(h, lp, b_cache_k, b_cache_v, c_buf, d_s0):
    bsz, T, _ = h.shape
    (a_u, a_v, b_q, b_k, b_v, c_a, c_g, d_q, d_k, d_v, d_r, d_g) = jnp.split(h @ lp['w_in'], IN_SPLITS, axis=-1)
    a_u = jax.nn.gelu(a_u)
    a_v = rmsnorm(jax.nn.gelu(a_v), lp['a_vnorm_g'])
    y_a = spatial_gate(a_u, a_v, lp['a_ws'], lp['a_bs'])
    q = rmsnorm(b_q.reshape(bsz, T, B_HEADS, B_HEAD_DIM), lp['b_qnorm_g']) * (B_HEAD_DIM ** -0.5)
    k = rmsnorm(b_k.reshape(bsz, T, B_HEADS, B_HEAD_DIM), lp['b_knorm_g'])
    v = b_v.reshape(bsz, T, B_HEADS, B_HEAD_DIM)
    if b_cache_k is None:
        o_b = band_attention_prompt(q, k, v, lp['b_rel_bias'])
        keep = min(B_WINDOW, T)
        new_k, new_v = k[:, T - keep:], v[:, T - keep:]
    else:
        o_b = band_attention_step(q, k, v, b_cache_k, b_cache_v, lp['b_rel_bias'])
        new_k, new_v = k, v
    y_b = o_b.reshape(bsz, T, W_GROUP)
    y_c, new_buf = conv_module(c_a * jax.nn.sigmoid(c_g), c_buf, lp['c_dw'], lp['c_dw_b'], lp['c_ln_g'], lp['c_ln_b'])
    log_a = jax.nn.log_sigmoid((d_g @ lp['d_wg2'] + lp['d_bg']).astype(jnp.float32)) / GLA_TAU
    o_d, new_s = gla_chunked(d_q.reshape(bsz, T, D_HEADS, D_KEY) * (D_KEY ** -0.5),
                             d_k.reshape(bsz, T, D_HEADS, D_KEY),
                             d_v.reshape(bsz, T, D_HEADS, D_VAL),
                             log_a.reshape(bsz, T, D_HEADS, D_KEY), d_s0)
    y_d = rmsnorm(o_d, lp['d_onorm_g']).reshape(bsz, T, W_GROUP) * jax.nn.silu(d_r)
    y = jnp.concatenate([y_a, y_b, y_c, y_d], axis=-1) @ lp['w_out']
    return y, (new_k, new_v, new_buf, new_s, a_v)


def setup_inputs(seed: int = 0) -> dict:
    key = jax.random.key(seed)
    ks = jax.random.split(key, 32)

    def nrm(k, shape, scale):
        return scale * jax.random.normal(k, shape, jnp.float32)

    LB = min(B_WINDOW, PAST_LEN)
    return {
        "x_prompt": nrm(ks[0], (BATCH, SEQ, D_MODEL), 1.0),
        "x_sample": nrm(ks[1], (DEC_BATCH, DEC_SEQ, D_MODEL), 1.0),
        "cache_b_k": nrm(ks[2], (DEPTH, DEC_BATCH, LB, B_HEADS, B_HEAD_DIM), 1.0),
        "cache_b_v": nrm(ks[3], (DEPTH, DEC_BATCH, LB, B_HEADS, B_HEAD_DIM), 1.0),
        "state_c_conv": nrm(ks[4], (DEPTH, DEC_BATCH, C_BUF, C_CHANNELS), 0.5),
        "state_d_gla": nrm(ks[5], (DEPTH, DEC_BATCH, D_HEADS, D_KEY, D_VAL), 0.5),
        "norm1_g": 1.0 + nrm(ks[6], (DEPTH, D_MODEL), 0.02),
        "w_in": nrm(ks[7], (DEPTH, D_MODEL, D_IN), D_MODEL ** -0.5),
        "a_vnorm_g": 1.0 + nrm(ks[8], (DEPTH, W_GROUP), 0.02),
        "a_ws": nrm(ks[9], (DEPTH, A_HEADS, A_CHUNK, A_CHUNK), A_CHUNK ** -0.5),
        "a_bs": 1.0 + nrm(ks[10], (DEPTH, A_HEADS, A_CHUNK), 0.02),
        "b_qnorm_g": 1.0 + nrm(ks[11], (DEPTH, B_HEAD_DIM), 0.02),
        "b_knorm_g": 1.0 + nrm(ks[12], (DEPTH, B_HEAD_DIM), 0.02),
        "b_rel_bias": nrm(ks[13], (DEPTH, B_HEADS, 2 * REL_CLIP + 1), 0.1),
        "c_dw": nrm(ks[14], (DEPTH, C_WIDTH, C_CHANNELS), C_WIDTH ** -0.5),
        "c_dw_b": nrm(ks[15], (DEPTH, C_CHANNELS), 0.02),
        "c_ln_g": 1.0 + nrm(ks[16], (DEPTH, C_CHANNELS), 0.02),
        "c_ln_b": nrm(ks[17], (DEPTH, C_CHANNELS), 0.02),
        "d_wg2": nrm(ks[18], (DEPTH, D_GATE_RANK, W_GROUP), D_GATE_RANK ** -0.5),
        "d_bg": nrm(ks[19], (DEPTH, W_GROUP), 0.1),
        "d_onorm_g": 1.0 + nrm(ks[20], (DEPTH, D_VAL), 0.02),
        "w_out": nrm(ks[21], (DEPTH, D_MIX, D_MODEL), D_MIX ** -0.5),
        "norm2_g": 1.0 + nrm(ks[22], (DEPTH, D_MODEL), 0.02),
        "r_group_w": nrm(ks[23], (DEPTH, D_MODEL, N_EXPERT_GROUPS), D_MODEL ** -0.5),
        "r_group_b": nrm(ks[24], (DEPTH, N_EXPERT_GROUPS), 0.01),
        "r_expert_w": nrm(ks[25], (DEPTH, D_MODEL, N_EXPERTS), D_MODEL ** -0.5),
        "r_expert_b": nrm(ks[26], (DEPTH, N_EXPERTS), 0.01),
        "e_w_gate": nrm(ks[27], (DEPTH, N_EXPERTS, D_MODEL, D_EXPERT), D_MODEL ** -0.5),
        "e_w_up": nrm(ks[28], (DEPTH, N_EXPERTS, D_MODEL, D_EXPERT), D_MODEL ** -0.5),
        "e_w_down": nrm(ks[29], (DEPTH, N_EXPERTS, D_EXPERT, D_MODEL), D_EXPERT ** -0.5),
    }


def reference(x_prompt, x_sample, cache_b_k, cache_b_v, state_c_conv, state_d_gla,
              norm1_g, w_in, a_vnorm_g, a_ws, a_bs, b_qnorm_g, b_knorm_g, b_rel_bias,
              c_dw, c_dw_b, c_ln_g, c_ln_b, d_wg2, d_bg, d_onorm_g, w_out, norm2_g,
              r_group_w, r_group_b, r_expert_w, r_expert_b, e_w_gate, e_w_up, e_w_down):
    x_p, x_s = x_prompt, x_sample
    pk, pv, pc, pd = [], [], [], []
    sk, sv, sc, sd, sa = [], [], [], [], []
    for l in range(DEPTH):
        lp = {"w_in": w_in[l], "a_vnorm_g": a_vnorm_g[l], "a_ws": a_ws[l], "a_bs": a_bs[l],
              "b_qnorm_g": b_qnorm_g[l], "b_knorm_g": b_knorm_g[l], "b_rel_bias": b_rel_bias[l],
              "c_dw": c_dw[l], "c_dw_b": c_dw_b[l], "c_ln_g": c_ln_g[l], "c_ln_b": c_ln_b[l],
              "d_wg2": d_wg2[l], "d_bg": d_bg[l], "d_onorm_g": d_onorm_g[l], "w_out": w_out[l]}
        h_p = rmsnorm(x_p, norm1_g[l])
        zero_buf = jnp.zeros((x_p.shape[0], C_BUF, C_CHANNELS), h_p.dtype)
        zero_s = jnp.zeros((x_p.shape[0], D_HEADS, D_KEY, D_VAL), h_p.dtype)
        m_p, st_p = mix_layer(h_p, lp, None, None, zero_buf, zero_s)
        x_p = x_p + m_p
        x_p = x_p + hier_moe(rmsnorm(x_p, norm2_g[l]), r_group_w[l], r_group_b[l], r_expert_w[l], r_expert_b[l],
                             e_w_gate[l], e_w_up[l], e_w_down[l])
        h_s = rmsnorm(x_s, norm1_g[l])
        m_s, st_s = mix_layer(h_s, lp, cache_b_k[l], cache_b_v[l], state_c_conv[l], state_d_gla[l])
        x_s = x_s + m_s
        x_s = x_s + hier_moe(rmsnorm(x_s, norm2_g[l]), r_group_w[l], r_group_b[l], r_expert_w[l], r_expert_b[l],
                             e_w_gate[l], e_w_up[l], e_w_down[l])
        pk.append(st_p[0]); pv.append(st_p[1]); pc.append(st_p[2]); pd.append(st_p[3])
        sk.append(st_s[0]); sv.append(st_s[1]); sc.append(st_s[2]); sd.append(st_s[3]); sa.append(st_s[4])
    return (x_p, x_s, jnp.stack(pk), jnp.stack(pv), jnp.stack(pc), jnp.stack(pd),
            jnp.stack(sk), jnp.stack(sv), jnp.stack(sc), jnp.stack(sd), jnp.stack(sa))
```

```python
import functools

import numpy as np
import jax
import jax.numpy as jnp
from jax import lax
from jax.experimental import pallas as pl
from jax.experimental.pallas import tpu as pltpu
from jax.experimental.pallas import tpu_sc as plsc

F32 = jnp.float32
BF16 = jnp.bfloat16
I32 = jnp.int32
U32 = jnp.uint32

D_MODEL = 1024
GRP = 256
HEADS = 4
HEAD_DIM = 64
CHUNK = 64
A_CHUNK = 128
B_WINDOW = 512
REL_CLIP = 128
C_WIDTH = 31
C_BUF = C_WIDTH - 1
HALO = 32
GATE_RANK = 16
GLA_TAU = 16.0
GLA_SUB = 16
N_GROUPS = 4
PER_GROUP = 8
N_EXPERTS = 32
D_EXPERT = 512
EPS = 1e-6
NEG_INF = -1e30
LANES = 128
VMEM_LIMIT = 48 * 1024 * 1024

PK, PV, PQ, PAU, PAV, PGLU, PDQ, PDK, PDV, PDR, PLA = range(11)
N_PROJ = 11
_REF_GROUPS = (PAU, PAV, PQ, PK, PV, None, None, PDQ, PDK, PDV, PDR)

SC_WORKERS = 32
SC_WIN = 128


def _cparams(sem):
    return pltpu.CompilerParams(dimension_semantics=sem, vmem_limit_bytes=VMEM_LIMIT)


def _sigmoid(x):
    return 1.0 / (1.0 + jnp.exp(-x))


def _gelu_tanh(x):
    c = np.float32(np.sqrt(2.0 / np.pi))
    return 0.5 * x * (1.0 + jnp.tanh(c * (x + np.float32(0.044715) * (x * x * x))))


def _pack_halves(y):
    half = y.shape[1] // 2
    hi = pltpu.bitcast(y[:, :half].astype(BF16).astype(F32), U32)
    lo = pltpu.bitcast(y[:, half:].astype(BF16).astype(F32), U32)
    return hi | (lo >> np.uint32(16))


def _unpack_hi(w):
    return pltpu.bitcast(w & np.uint32(0xFFFF0000), F32)


def _unpack_lo(w):
    return pltpu.bitcast(w << np.uint32(16), F32)


def _head_id(shape, axis, size):
    return lax.broadcasted_iota(I32, shape, axis) // size


def _bd_stack(x, rows):
    x4 = jnp.concatenate([x] * HEADS, axis=0)
    shape = (HEADS * rows, GRP)
    keep = _head_id(shape, 0, rows) == _head_id(shape, 1, HEAD_DIM)
    return jnp.where(keep, x4, jnp.zeros_like(x4))


def _bd_unstack(o, rows):
    lane_h = _head_id((rows, GRP), 1, HEAD_DIM)
    out = o[(HEADS - 1) * rows:HEADS * rows]
    for h in range(HEADS - 2, -1, -1):
        out = jnp.where(lane_h == h, o[h * rows:(h + 1) * rows], out)
    return out


def _head_meansq(o, hsum_ref):
    sq = (o * o).astype(BF16)
    return jnp.dot(sq, hsum_ref[...], preferred_element_type=F32) * np.float32(1.0 / HEAD_DIM)


def _in_kernel(*refs, combine):
    if combine:
        (x_ref, y_ref, gate_ref, g1_ref, w_ref, wcg_ref, wdg_ref, wg2_ref, bg_ref, gq_ref, gk_ref, gav_ref,
         hsum_ref, xo_ref, p_ref) = refs
    else:
        (x_ref, g1_ref, w_ref, wcg_ref, wdg_ref, wg2_ref, bg_ref, gq_ref, gk_ref, gav_ref,
         hsum_ref, p_ref) = refs
    x = x_ref[...]
    if combine:
        half = D_MODEL // 2
        g = gate_ref[...]
        g0 = g[:, 0:1]
        g1 = g[:, 1:2]
        w0 = y_ref[0]
        w1 = y_ref[1]
        xa = x[:, :half] + g0 * _unpack_hi(w0) + g1 * _unpack_hi(w1)
        xb = x[:, half:] + g0 * _unpack_lo(w0) + g1 * _unpack_lo(w1)
        xo_ref[:, :half] = xa
        xo_ref[:, half:] = xb
        x = jnp.concatenate([xa, xb], axis=1)
    ms = jnp.mean(x * x, axis=-1, keepdims=True)
    h = (x * lax.rsqrt(ms + EPS) * g1_ref[...]).astype(BF16)

    def proj(g):
        return jnp.dot(h, w_ref[:, g * GRP:(g + 1) * GRP], preferred_element_type=F32)

    def put(g, val):
        p_ref[:, g * GRP:(g + 1) * GRP] = val.astype(BF16)

    r = proj(PK)
    put(PK, r * lax.rsqrt(_head_meansq(r, hsum_ref) + EPS) * gk_ref[...])
    put(PV, proj(PV))
    r = proj(PQ)
    put(PQ, r * lax.rsqrt(_head_meansq(r, hsum_ref) + EPS) * (gq_ref[...] * np.float32(HEAD_DIM ** -0.5)))
    put(PAU, _gelu_tanh(proj(PAU)))
    r = _gelu_tanh(proj(PAV))
    put(PAV, r * lax.rsqrt(jnp.mean(r * r, axis=-1, keepdims=True) + EPS) * gav_ref[...])
    ca = proj(PGLU)
    cg = jnp.dot(h, wcg_ref[...], preferred_element_type=F32)
    put(PGLU, ca * _sigmoid(cg))
    put(PDQ, proj(PDQ) * np.float32(HEAD_DIM ** -0.5))
    put(PDK, proj(PDK))
    put(PDV, proj(PDV))
    r = proj(PDR)
    put(PDR, r * _sigmoid(r))
    dg = jnp.dot(h, wdg_ref[...], preferred_element_type=F32)
    z = jnp.dot(dg.astype(BF16), wg2_ref[...], preferred_element_type=F32) + bg_ref[...]
    logsig = jnp.minimum(z, 0.0) - jnp.log(1.0 + jnp.exp(-jnp.abs(z)))
    put(PLA, logsig * np.float32(1.0 / GLA_TAU))


def _in_proj(x, lw, y=None, gates=None):
    n = x.shape[0]
    tm = min(512, n)
    combine = y is not None
    row = lambda i: (i, 0)
    const = lambda i: (0, 0)
    ins, specs = [x], [pl.BlockSpec((tm, D_MODEL), row)]
    if combine:
        ins += [y, gates]
        specs += [pl.BlockSpec((2, tm, D_MODEL // 2), lambda i: (0, i, 0)), pl.BlockSpec((tm, LANES), row)]
    consts = [lw["g1"], lw["w_in"], lw["w_cg"], lw["w_dg"], lw["wg2"], lw["bg"], lw["gq"], lw["gk"], lw["gav"], lw["hsum"]]
    ins += consts
    specs += [pl.BlockSpec(c.shape, const) for c in consts]
    out_shape = [jax.ShapeDtypeStruct((n, N_PROJ * GRP), BF16)]
    out_specs = [pl.BlockSpec((tm, N_PROJ * GRP), row)]
    if combine:
        out_shape = [jax.ShapeDtypeStruct((n, D_MODEL), F32)] + out_shape
        out_specs = [pl.BlockSpec((tm, D_MODEL), row)] + out_specs
    outs = pl.pallas_call(
        functools.partial(_in_kernel, combine=combine),
        grid=(n // tm,), in_specs=specs, out_specs=out_specs, out_shape=out_shape,
        compiler_params=_cparams(("arbitrary",)), name="in_proj",
    )(*ins)
    if combine:
        return outs[0], outs[1]
    return x, outs[0]


def _gmlp_kernel(u_ref, v_ref, ws_ref, bs_ref, o_ref, *, chunk, n_chunks):
    lane_h = _head_id((chunk, GRP), 1, HEAD_DIM)
    ri = lax.broadcasted_iota(I32, (chunk, chunk), 0)
    ci = lax.broadcasted_iota(I32, (chunk, chunk), 1)
    wm = [jnp.where(ci <= ri, ws_ref[h], 0.0).astype(BF16) for h in range(HEADS)]
    for c in range(n_chunks):
        rows = slice(c * chunk, (c + 1) * chunk)
        v = v_ref[rows, :]
        sv = jnp.dot(wm[HEADS - 1], v, preferred_element_type=F32)
        for h in range(HEADS - 2, -1, -1):
            sv = jnp.where(lane_h == h, jnp.dot(wm[h], v, preferred_element_type=F32), sv)
        o_ref[rows, :] = (u_ref[rows, :].astype(F32) * (sv + bs_ref[...])).astype(BF16)


def _gmlp(proj, lw, t):
    n = proj.shape[0]
    chunk = min(t, A_CHUNK)
    ta = min(512, n)
    ws = lw["a_ws"][:, :chunk, :chunk]
    bs = lw["a_bs_rows"][:chunk]
    return pl.pallas_call(
        functools.partial(_gmlp_kernel, chunk=chunk, n_chunks=ta // chunk),
        grid=(n // ta,),
        in_specs=[pl.BlockSpec((ta, GRP), lambda i: (i, PAU)), pl.BlockSpec((ta, GRP), lambda i: (i, PAV)),
                  pl.BlockSpec(ws.shape, lambda i: (0, 0, 0)), pl.BlockSpec(bs.shape, lambda i: (0, 0))],
        out_specs=pl.BlockSpec((ta, GRP), lambda i: (i, 0)),
        out_shape=jax.ShapeDtypeStruct((n, GRP), BF16),
        compiler_params=_cparams(("arbitrary",)), name="gmlp",
    )(proj, proj, ws, bs)


def _attn_kernel(q_ref, kc_ref, vc_ref, kp_ref, vp_ref, bias_ref, o_ref, kbuf, vbuf, *, chunk, n_chunks, first_has_past):
    tq = chunk * n_chunks
    win = B_WINDOW + chunk
    kbuf[0:B_WINDOW, :] = kp_ref[...].astype(BF16)
    vbuf[0:B_WINDOW, :] = vp_ref[...].astype(BF16)
    kbuf[B_WINDOW:B_WINDOW + tq, :] = kc_ref[...]
    vbuf[B_WINDOW:B_WINDOW + tq, :] = vc_ref[...]
    has_past = jnp.logical_or(pl.program_id(1) > 0, first_has_past)
    col = lax.broadcasted_iota(I32, (HEADS * chunk, win), 1)
    for c in range(n_chunks):
        q = q_ref[c * chunk:(c + 1) * chunk, :]
        kk = kbuf[c * chunk:c * chunk + win, :]
        vv = vbuf[c * chunk:c * chunk + win, :]
        s = lax.dot_general(_bd_stack(q, chunk), kk, (((1,), (1,)), ((), ())), preferred_element_type=F32)
        s = s + bias_ref[...]
        visible = jnp.logical_or(has_past, col + c * chunk >= B_WINDOW)
        s = jnp.where(visible, s, NEG_INF)
        m = jnp.max(s, axis=-1, keepdims=True)
        p = jnp.exp(s - m)
        l = jnp.sum(p, axis=-1, keepdims=True)
        o = jnp.dot(p.astype(BF16), vv, preferred_element_type=F32) * (1.0 / l)
        o_ref[c * chunk:(c + 1) * chunk, :] = _bd_unstack(o, chunk).astype(BF16)


def _attention(proj, lw, b, t, cache_k=None, cache_v=None):
    n = proj.shape[0]
    step = cache_k is not None
    chunk = min(t, CHUNK)
    tq = min(t, B_WINDOW)
    nt = t // tq
    rel = lw["b_rel"]
    i = np.arange(chunk)[:, None]
    m = np.arange(B_WINDOW + chunk)[None, :]
    dist = np.clip(i + B_WINDOW - m, -REL_CLIP, REL_CLIP) + REL_CLIP
    bias = rel[:, dist].astype(F32).reshape(HEADS * chunk, B_WINDOW + chunk)
    cur = lambda g: pl.BlockSpec((tq, GRP), lambda bi, j: (bi * nt + j, g))
    if step:
        prev_k = pl.BlockSpec((B_WINDOW, GRP), lambda bi, j: (bi, 0))
        prev_v = prev_k
        pk_arr, pv_arr = cache_k, cache_v
    else:
        assert tq == B_WINDOW
        prev_k = pl.BlockSpec((B_WINDOW, GRP), lambda bi, j: (bi * nt + jnp.maximum(j - 1, 0), PK))
        prev_v = pl.BlockSpec((B_WINDOW, GRP), lambda bi, j: (bi * nt + jnp.maximum(j - 1, 0), PV))
        pk_arr, pv_arr = proj, proj
    return pl.pallas_call(
        functools.partial(_attn_kernel, chunk=chunk, n_chunks=tq // chunk, first_has_past=step),
        grid=(b, nt),
        in_specs=[cur(PQ), cur(PK), cur(PV), prev_k, prev_v, pl.BlockSpec(bias.shape, lambda bi, j: (0, 0))],
        out_specs=pl.BlockSpec((tq, GRP), lambda bi, j: (bi * nt + j, 0)),
        out_shape=jax.ShapeDtypeStruct((n, GRP), BF16),
        scratch_shapes=[pltpu.VMEM((B_WINDOW + tq, GRP), BF16), pltpu.VMEM((B_WINDOW + tq, GRP), BF16)],
        compiler_params=_cparams(("arbitrary", "arbitrary")), name="band_attn",
    )(proj, proj, proj, pk_arr, pv_arr, bias)


def _conv_kernel(g_ref, halo_ref, dw_ref, dwb_ref, lng_ref, lnb_ref, o_ref, xp, *, tc, sub, first_has_past):
    halo = halo_ref[...].astype(F32)
    has_past = jnp.logical_or(pl.program_id(1) > 0, first_has_past)
    xp[0:HALO, :] = jnp.where(has_past, halo, 0.0)
    xp[HALO:HALO + tc, :] = g_ref[...].astype(F32)
    for s in range(tc // sub):
        base = s * sub + HALO - C_BUF
        acc = dw_ref[0:1, :] * xp[base:base + sub, :]
        for w in range(1, C_WIDTH):
            acc = acc + dw_ref[w:w + 1, :] * xp[base + w:base + w + sub, :]
        y = acc + dwb_ref[...]
        mu = jnp.mean(y, axis=-1, keepdims=True)
        yc = y - mu
        y = yc * lax.rsqrt(jnp.mean(yc * yc, axis=-1, keepdims=True) + EPS) * lng_ref[...] + lnb_ref[...]
        o_ref[s * sub:(s + 1) * sub, :] = (y * _sigmoid(y)).astype(BF16)


def _conv(proj, lw, b, t, state=None):
    n = proj.shape[0]
    step = state is not None
    tc = min(t, 512)
    nt = t // tc
    sub = min(tc, 64)
    if step:
        halo_arr = state
        halo_spec = pl.BlockSpec((HALO, GRP), lambda bi, j: (bi, 0))
    else:
        per = tc // HALO
        halo_arr = proj
        halo_spec = pl.BlockSpec((HALO, GRP), lambda bi, j: (jnp.maximum((bi * nt + j) * per - 1, 0), PGLU))
    vec = pl.BlockSpec((1, GRP), lambda bi, j: (0, 0))
    return pl.pallas_call(
        functools.partial(_conv_kernel, tc=tc, sub=sub, first_has_past=step),
        grid=(b, nt),
        in_specs=[pl.BlockSpec((tc, GRP), lambda bi, j: (bi * nt + j, PGLU)), halo_spec,
                  pl.BlockSpec((C_WIDTH, GRP), lambda bi, j: (0, 0)), vec, vec, vec],
        out_specs=pl.BlockSpec((tc, GRP), lambda bi, j: (bi * nt + j, 0)),
        out_shape=jax.ShapeDtypeStruct((n, GRP), BF16),
        scratch_shapes=[pltpu.VMEM((HALO + tc, GRP), F32)],
        compiler_params=_cparams(("arbitrary", "arbitrary")), name="conv_module",
    )(proj, halo_arr, lw["c_dw"], lw["c_dw_b"], lw["c_ln_g"], lw["c_ln_b"])


def _gla_tables(L):
    i = np.arange(L)[:, None]
    t = np.arange(L)[None, :]
    mats = [(t <= i), (t > i)]
    masks = []
    s = GLA_SUB
    same = (i // s) == (t // s)
    mats += [same & (t <= i), same & (t <= i)]
    masks.append(same & (t <= i))
    s *= 2
    while s <= L:
        h = s // 2
        same = (i // s) == (t // s)
        anchor = (i // s) * s + h - 1
        mats.append(same & (t > anchor) & (t <= i) & (i % s >= h))
        mats.append(same & (t > i) & (t <= anchor) & (i % s < h))
        jj = t
        masks.append(((i // s) == (jj // s)) & (i % s >= h) & (jj % s < h))
        s *= 2
    mat = np.concatenate([m.astype(np.float32) for m in mats], axis=0)
    mask = np.stack([np.tile(m.astype(np.float32), (HEADS, 1)) for m in masks], axis=0)
    return mat, mask


def _gla_kernel(q_ref, k_ref, v_ref, la_ref, dr_ref, s0_ref, emat_ref, lmask_ref, bdmask_ref, hsum_ref, gon_ref,
                o_ref, sf_ref, st, *, L, n_chunks, n_levels, first_has_state):
    j = pl.program_id(1)

    @pl.when(j == 0)
    def _():
        if first_has_state:
            st[...] = s0_ref[...]
        else:
            st[...] = jnp.zeros_like(st)

    row = lax.broadcasted_iota(I32, (L, GRP), 0)

    def chunk_body(c, carry):
        r0 = pl.multiple_of(c * L, L)
        rows = pl.ds(r0, L)
        la = la_ref[rows, :]
        q = q_ref[rows, :].astype(F32)
        k = k_ref[rows, :].astype(F32)
        v = v_ref[rows, :]
        e = jnp.dot(emat_ref[...], la, preferred_element_type=F32)
        blk = lambda idx: e[idx * L:(idx + 1) * L]
        e_in = blk(0)
        s_t = st[...]
        qp = (q * jnp.exp(e_in)).astype(BF16)
        o = lax.dot_general(qp, s_t.astype(BF16), (((1,), (1,)), ((), ())), preferred_element_type=F32)
        att = None
        for lvl in range(n_levels):
            eq = blk(2 + 2 * lvl)
            ek = blk(3 + 2 * lvl)
            if lvl == 0:
                ql = q * jnp.exp(eq)
                kl = k * jnp.exp(-ek)
            else:
                size = GLA_SUB << lvl
                upper = (row & (size - 1)) >= (size // 2)
                ql = jnp.where(upper, q * jnp.exp(eq), 0.0)
                kl = jnp.where(upper, 0.0, k * jnp.exp(ek))
            a = lax.dot_general(_bd_stack(ql.astype(BF16), L), kl.astype(BF16), (((1,), (1,)), ((), ())),
                                preferred_element_type=F32) * lmask_ref[lvl]
            att = a if att is None else att + a
        o = o + _bd_unstack(jnp.dot(att.astype(BF16), v, preferred_element_type=F32), L)
        kst = (k * jnp.exp(blk(1))).astype(BF16)
        upd = lax.dot_general(v, kst, (((0,), (0,)), ((), ())), preferred_element_type=F32)
        st[...] = s_t * jnp.exp(e_in[L - 1:L, :]) + upd * bdmask_ref[...]
        y = o * lax.rsqrt(_head_meansq(o, hsum_ref) + EPS) * gon_ref[...] * dr_ref[rows, :].astype(F32)
        o_ref[rows, :] = y.astype(BF16)
        return carry

    lax.fori_loop(0, n_chunks, chunk_body, 0)

    @pl.when(j == pl.num_programs(1) - 1)
    def _():
        sf_ref[...] = st[...]


def _gla(proj, lw, b, t, s0=None):
    n = proj.shape[0]
    step = s0 is not None
    L = min(t, 64)
    td = min(t, 512)
    nt = t // td
    n_levels = int(np.log2(L // GLA_SUB)) + 1
    emat, lmask = _gla_tables(L)
    emat = jnp.asarray(emat, BF16)
    lmask = jnp.asarray(lmask, F32)
    if not step:
        s0 = jnp.zeros((GRP, GRP), F32)
        s0_spec = pl.BlockSpec((GRP, GRP), lambda bi, j: (0, 0))
    else:
        s0_spec = pl.BlockSpec((GRP, GRP), lambda bi, j: (bi, 0))
    cur = lambda g: pl.BlockSpec((td, GRP), lambda bi, j: (bi * nt + j, g))
    c2 = lambda bi, j: (0, 0)
    return pl.pallas_call(
        functools.partial(_gla_kernel, L=L, n_chunks=td // L, n_levels=n_levels, first_has_state=step),
        grid=(b, nt),
        in_specs=[cur(PDQ), cur(PDK), cur(PDV), cur(PLA), cur(PDR), s0_spec,
                  pl.BlockSpec(emat.shape, c2), pl.BlockSpec(lmask.shape, lambda bi, j: (0, 0, 0)),
                  pl.BlockSpec((GRP, GRP), c2), pl.BlockSpec((GRP, GRP), c2), pl.BlockSpec((1, GRP), c2)],
        out_specs=[pl.BlockSpec((td, GRP), lambda bi, j: (bi * nt + j, 0)),
                   pl.BlockSpec((GRP, GRP), lambda bi, j: (bi, 0))],
        out_shape=[jax.ShapeDtypeStruct((n, GRP), BF16), jax.ShapeDtypeStruct((b * GRP, GRP), F32)],
        scratch_shapes=[pltpu.VMEM((GRP, GRP), F32)],
        compiler_params=_cparams(("arbitrary", "arbitrary")), name="gla",
    )(proj, proj, proj, proj, proj, s0, emat, lmask, lw["bdmask"], lw["hsum"], lw["gon"])


def _out_kernel(ya_ref, yb_ref, yc_ref, yd_ref, x_ref, wo_ref, g2_ref, wr_hi_ref, wr_lo_ref, br_ref, tri_ref,
                xo_ref, xn_ref, ri_ref, rf_ref, cnt_ref, *, tm):
    @pl.when(pl.program_id(0) == 0)
    def _():
        cnt_ref[...] = jnp.zeros_like(cnt_ref)

    y = jnp.dot(ya_ref[...], wo_ref[0 * GRP:1 * GRP, :], preferred_element_type=F32)
    y = y + jnp.dot(yb_ref[...], wo_ref[1 * GRP:2 * GRP, :], preferred_element_type=F32)
    y = y + jnp.dot(yc_ref[...], wo_ref[2 * GRP:3 * GRP, :], preferred_element_type=F32)
    y = y + jnp.dot(yd_ref[...], wo_ref[3 * GRP:4 * GRP, :], preferred_element_type=F32)
    x = x_ref[...] + y
    xo_ref[...] = x
    xn = x * lax.rsqrt(jnp.mean(x * x, axis=-1, keepdims=True) + EPS) * g2_ref[...]
    xn_ref[...] = _pack_halves(xn)

    xh = xn.astype(BF16)
    xl = (xn - xh.astype(F32)).astype(BF16)
    logits = (jnp.dot(xh, wr_hi_ref[...], preferred_element_type=F32)
              + jnp.dot(xl, wr_hi_ref[...], preferred_element_type=F32)
              + jnp.dot(xh, wr_lo_ref[...], preferred_element_type=F32)) + br_ref[...]
    lane = lax.broadcasted_iota(I32, (tm, LANES), 1)
    big = np.int32(1 << 20)

    def first_max(mask):
        v = jnp.max(jnp.where(mask, logits, -jnp.inf), axis=-1, keepdims=True)
        idx = jnp.min(jnp.where(jnp.logical_and(mask, logits == v), lane, big), axis=-1, keepdims=True)
        return v, idx

    is_grp = jnp.logical_and(lane >= N_EXPERTS, lane < N_EXPERTS + N_GROUPS)
    gmax, gidx = first_max(is_grp)
    p_grp = 1.0 / jnp.sum(jnp.where(is_grp, jnp.exp(logits - gmax), 0.0), axis=-1, keepdims=True)
    grp = gidx - N_EXPERTS
    in_grp = (lane // PER_GROUP) == grp
    v1, i1 = first_max(in_grp)
    v2, i2 = first_max(jnp.logical_and(in_grp, lane != i1))
    e21 = jnp.exp(v2 - v1)
    gate1 = p_grp / (1.0 + e21)
    gate2 = p_grp * e21 / (1.0 + e21)

    oh1 = lane == i1
    oh2 = lane == i2
    both = jnp.logical_or(oh1, oh2)
    ones = jnp.where(both, 1.0, 0.0).astype(BF16)
    before = jnp.dot(tri_ref[...], ones, preferred_element_type=F32) + cnt_ref[...].astype(F32)
    rank1 = jnp.sum(jnp.where(oh1, before, 0.0), axis=-1, keepdims=True)
    rank2 = jnp.sum(jnp.where(oh2, before, 0.0), axis=-1, keepdims=True)
    cnt_ref[...] = cnt_ref[...] + jnp.sum(jnp.where(both, 1.0, 0.0), axis=0, keepdims=True).astype(I32)

    ri = jnp.where(lane == 0, i1, jnp.where(lane == 1, i2, jnp.where(lane == 2, rank1.astype(I32),
                                                                      jnp.where(lane == 3, rank2.astype(I32), 0))))
    ri_ref[...] = ri
    rf_ref[...] = jnp.where(lane == 0, gate1, jnp.where(lane == 1, gate2, 0.0))


def _out_proj(ya, yb, yc, yd, x, lw):
    n = x.shape[0]
    tm = min(512, n)
    row = lambda i: (i, 0)
    const = lambda i: (0, 0)
    tri = jnp.asarray(np.tril(np.ones((tm, tm), np.float32), -1), BF16)
    consts = [lw["w_out"], lw["g2"], lw["wr_hi"], lw["wr_lo"], lw["br"], tri]
    yspec = pl.BlockSpec((tm, GRP), row)
    return pl.pallas_call(
        functools.partial(_out_kernel, tm=tm),
        grid=(n // tm,),
        in_specs=[yspec, yspec, yspec, yspec, pl.BlockSpec((tm, D_MODEL), row)] + [pl.BlockSpec(c.shape, const) for c in consts],
        out_specs=[pl.BlockSpec((tm, D_MODEL), row), pl.BlockSpec((tm, D_MODEL // 2), row),
                   pl.BlockSpec((tm, LANES), row), pl.BlockSpec((tm, LANES), row), pl.BlockSpec((1, LANES), const)],
        out_shape=[jax.ShapeDtypeStruct((n, D_MODEL), F32), jax.ShapeDtypeStruct((n, D_MODEL // 2), U32),
                   jax.ShapeDtypeStruct((n, LANES), I32), jax.ShapeDtypeStruct((n, LANES), F32),
                   jax.ShapeDtypeStruct((1, LANES), I32)],
        compiler_params=_cparams(("arbitrary",)), name="out_proj_router",
    )(ya, yb, yc, yd, x, *consts)


def _sc_scatter_rows(x, idx, n_out):
    n, d = x.shape
    kk = idx.shape[0]
    per_w = n // SC_WORKERS
    win = min(SC_WIN, per_w)
    n_win = per_w // win
    assert n_win * win * SC_WORKERS == n
    mesh = plsc.VectorSubcoreMesh(core_axis_name="c", subcore_axis_name="s")

    @functools.partial(
        pl.kernel, mesh=mesh, out_type=jax.ShapeDtypeStruct((n_out, d), x.dtype),
        scratch_types=[pltpu.VMEM((kk, win), I32), pltpu.VMEM((win, d), x.dtype)],
        name="sc_scatter_rows")
    def k(x_hbm, idx_hbm, o_hbm, idx_v, rows_v):
        wid = lax.axis_index("s") * 2 + lax.axis_index("c")
        base = wid * per_w

        @pl.loop(0, n_win)
        def _(w):
            off = base + w * win
            pltpu.sync_copy(x_hbm.at[pl.ds(off, win)], rows_v)
            for j in range(kk):
                pltpu.sync_copy(idx_hbm.at[j, pl.ds(off, win)], idx_v.at[j])
                pltpu.sync_copy(rows_v, o_hbm.at[idx_v.at[j]])

    return k(x, idx)


def _sc_gather_rows(y, idx):
    _, d = y.shape
    kk, n = idx.shape
    per_w = n // SC_WORKERS
    win = min(SC_WIN, per_w)
    n_win = per_w // win
    assert n_win * win * SC_WORKERS == n
    mesh = plsc.VectorSubcoreMesh(core_axis_name="c", subcore_axis_name="s")

    @functools.partial(
        pl.kernel, mesh=mesh, out_type=jax.ShapeDtypeStruct((kk, n, d), y.dtype),
        scratch_types=[pltpu.VMEM((kk, win), I32), pltpu.VMEM((win, d), y.dtype)],
        name="sc_gather_rows")
    def k(y_hbm, idx_hbm, o_hbm, idx_v, rows_v):
        wid = lax.axis_index("s") * 2 + lax.axis_index("c")
        base = wid * per_w

        @pl.loop(0, n_win)
        def _(w):
            off = base + w * win
            for j in range(kk):
                pltpu.sync_copy(idx_hbm.at[j, pl.ds(off, win)], idx_v.at[j])
                pltpu.sync_copy(y_hbm.at[idx_v.at[j]], rows_v)
                pltpu.sync_copy(rows_v, o_hbm.at[j, pl.ds(off, win)])

    return k(y, idx)


def _moe_kernel(bexp_ref, nused_ref, x_ref, wg_ref, wu_ref, wd_ref, o_ref, wg_s, wu_s, wd_s):
    i = pl.program_id(0)
    prev = bexp_ref[jnp.maximum(i - 1, 0)]
    fresh = jnp.logical_or(i == 0, bexp_ref[i] != prev)

    @pl.when(jnp.logical_and(fresh, i < nused_ref[0]))
    def _():
        wg_s[...] = wg_ref[0].astype(BF16)
        wu_s[...] = wu_ref[0].astype(BF16)
        wd_s[...] = wd_ref[0].astype(BF16)

    @pl.when(i < nused_ref[0])
    def _():
        half = D_MODEL // 2
        w = x_ref[...]
        xa = _unpack_hi(w).astype(BF16)
        xb = _unpack_lo(w).astype(BF16)
        hg = (jnp.dot(xa, wg_s[:half, :], preferred_element_type=F32)
              + jnp.dot(xb, wg_s[half:, :], preferred_element_type=F32))
        hu = (jnp.dot(xa, wu_s[:half, :], preferred_element_type=F32)
              + jnp.dot(xb, wu_s[half:, :], preferred_element_type=F32))
        h = (hg * _sigmoid(hg) * hu).astype(BF16)
        o_ref[...] = _pack_halves(jnp.dot(h, wd_s[...], preferred_element_type=F32))


def _moe_experts(xs, blk_exp, n_used, w_gate, w_up, w_down, bm):
    p = xs.shape[0]
    n_blocks = p // bm
    live = lambda i, be, nu: jnp.minimum(i, jnp.maximum(nu[0] - 1, 0))
    wspec = lambda shape: pl.BlockSpec((1,) + shape, lambda i, be, nu: (be[live(i, be, nu)], 0, 0))
    grid_spec = pltpu.PrefetchScalarGridSpec(
        num_scalar_prefetch=2, grid=(n_blocks,),
        in_specs=[pl.BlockSpec((bm, D_MODEL // 2), lambda i, be, nu: (live(i, be, nu), 0)),
                  wspec((D_MODEL, D_EXPERT)), wspec((D_MODEL, D_EXPERT)), wspec((D_EXPERT, D_MODEL))],
        out_specs=pl.BlockSpec((bm, D_MODEL // 2), lambda i, be, nu: (live(i, be, nu), 0)),
        scratch_shapes=[pltpu.VMEM((D_MODEL, D_EXPERT), BF16), pltpu.VMEM((D_MODEL, D_EXPERT), BF16),
                        pltpu.VMEM((D_EXPERT, D_MODEL), BF16)])
    return pl.pallas_call(
        _moe_kernel, grid_spec=grid_spec, out_shape=jax.ShapeDtypeStruct((p, D_MODEL // 2), U32),
        compiler_params=_cparams(("arbitrary",)), name="moe_experts",
    )(blk_exp, n_used, xs, w_gate, w_up, w_down)


def _moe_block_rows(n):
    return 512 if n >= 16384 else 128


def _moe(xn_packed, route_i, counts, lw):
    n = xn_packed.shape[0]
    bm = _moe_block_rows(n)
    n_blocks = -(-(2 * n + N_EXPERTS * (bm - 1)) // bm)
    cnt = counts[0, :N_EXPERTS]
    padded = (cnt + bm - 1) // bm * bm
    pad_end = jnp.cumsum(padded)
    pad_start = pad_end - padded
    eid = route_i[:, 0:2]
    dest = (pad_start[eid] + route_i[:, 2:4]).T.astype(I32)
    blk_exp = jnp.minimum(jnp.searchsorted(pad_end, jnp.arange(n_blocks, dtype=I32) * bm, side="right"),
                          N_EXPERTS - 1).astype(I32)
    n_used = (pad_end[-1:] // bm).astype(I32)
    xs = _sc_scatter_rows(xn_packed, dest, n_blocks * bm)
    ys = _moe_experts(xs, blk_exp, n_used, lw["e_w_gate"], lw["e_w_up"], lw["e_w_down"], bm)
    return _sc_gather_rows(ys, dest)


def _combine_kernel(x_ref, y_ref, gate_ref, o_ref):
    half = D_MODEL // 2
    x = x_ref[...]
    g = gate_ref[...]
    g0 = g[:, 0:1]
    g1 = g[:, 1:2]
    w0 = y_ref[0]
    w1 = y_ref[1]
    o_ref[:, :half] = x[:, :half] + g0 * _unpack_hi(w0) + g1 * _unpack_hi(w1)
    o_ref[:, half:] = x[:, half:] + g0 * _unpack_lo(w0) + g1 * _unpack_lo(w1)


def _combine(x, y, gates):
    n = x.shape[0]
    tm = min(512, n)
    row = lambda i: (i, 0)
    return pl.pallas_call(
        _combine_kernel, grid=(n // tm,),
        in_specs=[pl.BlockSpec((tm, D_MODEL), row), pl.BlockSpec((2, tm, D_MODEL // 2), lambda i: (0, i, 0)),
                  pl.BlockSpec((tm, LANES), row)],
        out_specs=pl.BlockSpec((tm, D_MODEL), row), out_shape=jax.ShapeDtypeStruct((n, D_MODEL), F32),
        compiler_params=_cparams(("arbitrary",)), name="moe_combine",
    )(x, y, gates)


def _layer_weights(l, p):
    w_in = p["w_in"][l]
    cols = [w_in[:, i * GRP:(i + 1) * GRP] for i in range(11)]
    by_group = [None] * N_PROJ
    for ref_i, g in enumerate(_REF_GROUPS):
        if g is not None:
            by_group[g] = cols[ref_i]
    by_group[PGLU] = cols[5]
    w_dg = jnp.zeros((D_MODEL, LANES), F32).at[:, :GATE_RANK].set(w_in[:, 11 * GRP:])
    wg2 = jnp.zeros((LANES, GRP), F32).at[:GATE_RANK].set(p["d_wg2"][l])
    tile4 = lambda v: jnp.tile(v, HEADS)[None, :]
    hid = np.arange(GRP) // HEAD_DIM
    bd = (hid[:, None] == hid[None, :]).astype(np.float32)
    wr = jnp.zeros((D_MODEL, LANES), F32).at[:, :N_EXPERTS].set(p["r_expert_w"][l])
    wr = wr.at[:, N_EXPERTS:N_EXPERTS + N_GROUPS].set(p["r_group_w"][l])
    wr_hi = wr.astype(BF16)
    br = jnp.zeros((1, LANES), F32).at[0, :N_EXPERTS].set(p["r_expert_b"][l])
    br = br.at[0, N_EXPERTS:N_EXPERTS + N_GROUPS].set(p["r_group_b"][l])
    return {
        "g1": p["norm1_g"][l][None, :],
        "w_in": jnp.concatenate(by_group[:PLA], axis=1).astype(BF16),
        "w_cg": cols[6].astype(BF16),
        "w_dg": w_dg.astype(BF16),
        "wg2": wg2.astype(BF16),
        "bg": p["d_bg"][l][None, :],
        "gq": tile4(p["b_qnorm_g"][l]), "gk": tile4(p["b_knorm_g"][l]), "gav": p["a_vnorm_g"][l][None, :],
        "gon": tile4(p["d_onorm_g"][l]),
        "hsum": jnp.asarray(bd, BF16), "bdmask": jnp.asarray(bd, F32),
        "a_ws": p["a_ws"][l], "a_bs_rows": jnp.repeat(p["a_bs"][l].T, HEAD_DIM, axis=1),
        "b_rel": p["b_rel_bias"][l],
        "c_dw": p["c_dw"][l], "c_dw_b": p["c_dw_b"][l][None, :],
        "c_ln_g": p["c_ln_g"][l][None, :], "c_ln_b": p["c_ln_b"][l][None, :],
        "w_out": p["w_out"][l].astype(BF16),
        "g2": p["norm2_g"][l][None, :],
        "wr_hi": wr_hi, "wr_lo": (wr - wr_hi.astype(F32)).astype(BF16), "br": br,
        "e_w_gate": p["e_w_gate"][l], "e_w_up": p["e_w_up"][l], "e_w_down": p["e_w_down"][l],
    }


def _states_to_bd(s):
    b = s.shape[0]
    st = jnp.swapaxes(s, 2, 3)
    eye = jnp.eye(HEADS, dtype=s.dtype)
    bd = st[:, :, :, None, :] * eye[None, :, None, :, None]
    return bd.reshape(b * GRP, GRP)


def _states_from_bd(sf, b):
    s = sf.reshape(b, HEADS, HEAD_DIM, HEADS, HEAD_DIM)
    diag = jnp.stack([s[:, h, :, h, :] for h in range(HEADS)], axis=1)
    return jnp.swapaxes(diag, 2, 3)


def _mix_and_route(x, lw, b, t, pending, caches):
    if pending is None:
        x, proj = _in_proj(x, lw)
    else:
        x, proj = _in_proj(x, lw, pending[0], pending[1])
    if caches is None:
        ya = _gmlp(proj, lw, t)
        yb = _attention(proj, lw, b, t)
        yc = _conv(proj, lw, b, t)
        yd, sf = _gla(proj, lw, b, t)
    else:
        ck, cv, cc, cs = caches
        ya = _gmlp(proj, lw, t)
        yb = _attention(proj, lw, b, t, ck.reshape(b * B_WINDOW, GRP), cv.reshape(b * B_WINDOW, GRP))
        halo = jnp.pad(cc, ((0, 0), (HALO - C_BUF, 0), (0, 0))).reshape(b * HALO, GRP)
        yc = _conv(proj, lw, b, t, halo)
        yd, sf = _gla(proj, lw, b, t, _states_to_bd(cs))
    x2, xn_packed, route_i, route_f, counts = _out_proj(ya, yb, yc, yd, x, lw)
    y = _moe(xn_packed, route_i, counts, lw)
    p3 = proj.reshape(b, t, N_PROJ, GRP)
    keep = min(B_WINDOW, t)
    new_k = p3[:, t - keep:, PK].astype(F32).reshape(b, keep, HEADS, HEAD_DIM)
    new_v = p3[:, t - keep:, PV].astype(F32).reshape(b, keep, HEADS, HEAD_DIM)
    new_buf = p3[:, t - C_BUF:, PGLU].astype(F32)
    a_v = p3[:, :, PAV].astype(F32)
    return x2, (y, route_f), (new_k, new_v, new_buf, _states_from_bd(sf, b), a_v)


def kernel(x_prompt, x_sample, cache_b_k, cache_b_v, state_c_conv, state_d_gla, norm1_g, w_in, a_vnorm_g, a_ws, a_bs, b_qnorm_g, b_knorm_g, b_rel_bias, c_dw, c_dw_b, c_ln_g, c_ln_b, d_wg2, d_bg, d_onorm_g, w_out, norm2_g, r_group_w, r_group_b, r_expert_w, r_expert_b, e_w_gate, e_w_up, e_w_down):
    params = dict(norm1_g=norm1_g, w_in=w_in, a_vnorm_g=a_vnorm_g, a_ws=a_ws, a_bs=a_bs, b_qnorm_g=b_qnorm_g,
                  b_knorm_g=b_knorm_g, b_rel_bias=b_rel_bias, c_dw=c_dw, c_dw_b=c_dw_b, c_ln_g=c_ln_g, c_ln_b=c_ln_b,
                  d_wg2=d_wg2, d_bg=d_bg, d_onorm_g=d_onorm_g, w_out=w_out, norm2_g=norm2_g, r_group_w=r_group_w,
                  r_group_b=r_group_b, r_expert_w=r_expert_w, r_expert_b=r_expert_b, e_w_gate=e_w_gate,
                  e_w_up=e_w_up, e_w_down=e_w_down)
    depth = w_in.shape[0]
    bp, tp, _ = x_prompt.shape
    bs, ts, _ = x_sample.shape
    xp = x_prompt.reshape(bp * tp, D_MODEL)
    xs = x_sample.reshape(bs * ts, D_MODEL)
    pend_p = pend_s = None
    st_p, st_s = [], []
    for l in range(depth):
        lw = _layer_weights(l, params)
        xp, pend_p, sp = _mix_and_route(xp, lw, bp, tp, pend_p, None)
        xs, pend_s, ss = _mix_and_route(xs, lw, bs, ts, pend_s,
                                        (cache_b_k[l], cache_b_v[l], state_c_conv[l], state_d_gla[l]))
        st_p.append(sp)
        st_s.append(ss)
    yp = _combine(xp, pend_p[0], pend_p[1]).reshape(bp, tp, D_MODEL)
    ys = _combine(xs, pend_s[0], pend_s[1]).reshape(bs, ts, D_MODEL)
    stack = lambda sts, i: jnp.stack([s[i] for s in sts])
    return (yp, ys, stack(st_p, 0), stack(st_p, 1), stack(st_p, 2), stack(st_p, 3),
            stack(st_s, 0), stack(st_s, 1), stack(st_s, 2), stack(st_s, 3), stack(st_s, 4))
```

```python
import functools

import numpy as np
import jax
import jax.numpy as jnp
from jax import lax
from jax.experimental import pallas as pl
from jax.experimental.pallas import tpu as pltpu
from jax.experimental.pallas import tpu_sc as plsc

F32 = jnp.float32
BF16 = jnp.bfloat16
I32 = jnp.int32
U32 = jnp.uint32

D_MODEL = 1024
GRP = 256
HEADS = 4
HEAD_DIM = 64
CHUNK = 64
A_CHUNK = 128
B_WINDOW = 512
REL_CLIP = 128
C_WIDTH = 31
C_BUF = C_WIDTH - 1
HALO = 32
GATE_RANK = 16
GLA_TAU = 16.0
GLA_SUB = 16
N_GROUPS = 4
PER_GROUP = 8
N_EXPERTS = 32
D_EXPERT = 512
EPS = 1e-6
NEG_INF = -1e30
LANES = 128
VMEM_LIMIT = 48 * 1024 * 1024

PK, PV, PQ, PAU, PAV, PGLU, PDQ, PDK, PDV, PDR, PLA = range(11)
N_PROJ = 11
_REF_GROUPS = (PAU, PAV, PQ, PK, PV, None, None, PDQ, PDK, PDV, PDR)

SC_WORKERS = 32
SC_WIN = 128


def _cparams(sem):
    return pltpu.CompilerParams(dimension_semantics=sem, vmem_limit_bytes=VMEM_LIMIT)


def _sigmoid(x):
    return 1.0 / (1.0 + jnp.exp(-x))


def _gelu_tanh(x):
    c = np.float32(np.sqrt(2.0 / np.pi))
    return 0.5 * x * (1.0 + jnp.tanh(c * (x + np.float32(0.044715) * (x * x * x))))


def _pack_halves(y):
    half = y.shape[1] // 2
    hi = pltpu.bitcast(y[:, :half].astype(BF16).astype(F32), U32)
    lo = pltpu.bitcast(y[:, half:].astype(BF16).astype(F32), U32)
    return hi | (lo >> np.uint32(16))


def _unpack_hi(w):
    return pltpu.bitcast(w & np.uint32(0xFFFF0000), F32)


def _unpack_lo(w):
    return pltpu.bitcast(w << np.uint32(16), F32)


def _head_id(shape, axis, size):
    return lax.broadcasted_iota(I32, shape, axis) // size


def _bd_stack(x, rows):
    x4 = jnp.concatenate([x] * HEADS, axis=0)
    shape = (HEADS * rows, GRP)
    keep = _head_id(shape, 0, rows) == _head_id(shape, 1, HEAD_DIM)
    return jnp.where(keep, x4, jnp.zeros_like(x4))


def _bd_unstack(o, rows):
    lane_h = _head_id((rows, GRP), 1, HEAD_DIM)
    out = o[(HEADS - 1) * rows:HEADS * rows]
    for h in range(HEADS - 2, -1, -1):
        out = jnp.where(lane_h == h, o[h * rows:(h + 1) * rows], out)
    return out


def _head_meansq(o, hsum_ref):
    sq = (o * o).astype(BF16)
    return jnp.dot(sq, hsum_ref[...], preferred_element_type=F32) * np.float32(1.0 / HEAD_DIM)


def _in_kernel(*refs, combine, emit_av, tiles_per_stream):
    refs = list(refs)
    x_ref = refs.pop(0)
    if combine:
        y_ref = refs.pop(0)
        gate_ref = refs.pop(0)
    g1_ref, w_ref, wcg_ref, wdg_ref, wg2_ref, bg_ref, gq_ref, gk_ref, gav_ref, hsum_ref = refs[:10]
    refs = refs[10:]
    if combine:
        xo_ref = refs.pop(0)
    p_ref = refs.pop(0)
    kv_ref = refs.pop(0)
    if emit_av:
        av_ref = refs.pop(0)
    x = x_ref[...]
    if combine:
        half = D_MODEL // 2
        g = gate_ref[...]
        g0 = g[:, 0:1]
        g1 = g[:, 1:2]
        w0 = y_ref[0]
        w1 = y_ref[1]
        xa = x[:, :half] + g0 * _unpack_hi(w0) + g1 * _unpack_hi(w1)
        xb = x[:, half:] + g0 * _unpack_lo(w0) + g1 * _unpack_lo(w1)
        xo_ref[:, :half] = xa
        xo_ref[:, half:] = xb
        x = jnp.concatenate([xa, xb], axis=1)
    ms = jnp.mean(x * x, axis=-1, keepdims=True)
    h = (x * lax.rsqrt(ms + EPS) * g1_ref[...]).astype(BF16)

    def proj(g):
        return jnp.dot(h, w_ref[:, g * GRP:(g + 1) * GRP], preferred_element_type=F32)

    def put(g, val):
        p_ref[:, g * GRP:(g + 1) * GRP] = val.astype(BF16)

    r = proj(PK)
    kn = r * lax.rsqrt(_head_meansq(r, hsum_ref) + EPS) * gk_ref[...]
    vv = proj(PV)
    put(PK, kn)
    put(PV, vv)

    @pl.when(pl.program_id(0) % tiles_per_stream == tiles_per_stream - 1)
    def _():
        kv_ref[:, :GRP] = kn
        kv_ref[:, GRP:] = vv

    r = proj(PQ)
    put(PQ, r * lax.rsqrt(_head_meansq(r, hsum_ref) + EPS) * (gq_ref[...] * np.float32(HEAD_DIM ** -0.5)))
    put(PAU, _gelu_tanh(proj(PAU)))
    r = _gelu_tanh(proj(PAV))
    av = r * lax.rsqrt(jnp.mean(r * r, axis=-1, keepdims=True) + EPS) * gav_ref[...]
    put(PAV, av)
    if emit_av:
        av_ref[...] = av
    ca = proj(PGLU)
    cg = jnp.dot(h, wcg_ref[...], preferred_element_type=F32)
    put(PGLU, ca * _sigmoid(cg))
    put(PDQ, proj(PDQ) * np.float32(HEAD_DIM ** -0.5))
    put(PDK, proj(PDK))
    put(PDV, proj(PDV))
    r = proj(PDR)
    put(PDR, r * _sigmoid(r))
    dg = jnp.dot(h, wdg_ref[...], preferred_element_type=F32)
    z = jnp.dot(dg.astype(BF16), wg2_ref[...], preferred_element_type=F32) + bg_ref[...]
    logsig = jnp.minimum(z, 0.0) - jnp.log(1.0 + jnp.exp(-jnp.abs(z)))
    put(PLA, logsig * np.float32(1.0 / GLA_TAU))


def _in_proj(x, lw, t, y=None, gates=None, emit_av=False):
    n = x.shape[0]
    tm = min(512, n)
    combine = y is not None
    keep = min(B_WINDOW, t)
    tps = max(t // tm, 1)
    assert tps == 1 or keep == tm
    row = lambda i: (i, 0)
    const = lambda i: (0, 0)
    ins, specs = [x], [pl.BlockSpec((tm, D_MODEL), row)]
    if combine:
        ins += [y, gates]
        specs += [pl.BlockSpec((2, tm, D_MODEL // 2), lambda i: (0, i, 0)), pl.BlockSpec((tm, LANES), row)]
    consts = [lw["g1"], lw["w_in"], lw["w_cg"], lw["w_dg"], lw["wg2"], lw["bg"], lw["gq"], lw["gk"], lw["gav"], lw["hsum"]]
    ins += consts
    specs += [pl.BlockSpec(c.shape, const) for c in consts]
    out_shape = [jax.ShapeDtypeStruct((n, N_PROJ * GRP), BF16), jax.ShapeDtypeStruct((n // tps, 2 * GRP), F32)]
    out_specs = [pl.BlockSpec((tm, N_PROJ * GRP), row), pl.BlockSpec((tm, 2 * GRP), lambda i: (i // tps, 0))]
    if combine:
        out_shape = [jax.ShapeDtypeStruct((n, D_MODEL), F32)] + out_shape
        out_specs = [pl.BlockSpec((tm, D_MODEL), row)] + out_specs
    if emit_av:
        out_shape.append(jax.ShapeDtypeStruct((n, GRP), F32))
        out_specs.append(pl.BlockSpec((tm, GRP), row))
    outs = list(pl.pallas_call(
        functools.partial(_in_kernel, combine=combine, emit_av=emit_av, tiles_per_stream=tps),
        grid=(n // tm,), in_specs=specs, out_specs=out_specs, out_shape=out_shape,
        compiler_params=_cparams(("arbitrary",)), name="in_proj",
    )(*ins))
    x_new = outs.pop(0) if combine else x
    proj, kv = outs[0], outs[1]
    return x_new, proj, kv, (outs[2] if emit_av else None)


def _gmlp_kernel(u_ref, v_ref, ws_ref, bs_ref, o_ref, *, chunk, n_chunks):
    lane_h = _head_id((chunk, GRP), 1, HEAD_DIM)
    ri = lax.broadcasted_iota(I32, (chunk, chunk), 0)
    ci = lax.broadcasted_iota(I32, (chunk, chunk), 1)
    wm = [jnp.where(ci <= ri, ws_ref[h], 0.0).astype(BF16) for h in range(HEADS)]
    for c in range(n_chunks):
        rows = slice(c * chunk, (c + 1) * chunk)
        v = v_ref[rows, :]
        sv = jnp.dot(wm[HEADS - 1], v, preferred_element_type=F32)
        for h in range(HEADS - 2, -1, -1):
            sv = jnp.where(lane_h == h, jnp.dot(wm[h], v, preferred_element_type=F32), sv)
        o_ref[rows, :] = (u_ref[rows, :].astype(F32) * (sv + bs_ref[...])).astype(BF16)


def _gmlp(proj, lw, t):
    n = proj.shape[0]
    chunk = min(t, A_CHUNK)
    ta = min(512, n)
    ws = lw["a_ws"][:, :chunk, :chunk]
    bs = lw["a_bs_rows"][:chunk]
    return pl.pallas_call(
        functools.partial(_gmlp_kernel, chunk=chunk, n_chunks=ta // chunk),
        grid=(n // ta,),
        in_specs=[pl.BlockSpec((ta, GRP), lambda i: (i, PAU)), pl.BlockSpec((ta, GRP), lambda i: (i, PAV)),
                  pl.BlockSpec(ws.shape, lambda i: (0, 0, 0)), pl.BlockSpec(bs.shape, lambda i: (0, 0))],
        out_specs=pl.BlockSpec((ta, GRP), lambda i: (i, 0)),
        out_shape=jax.ShapeDtypeStruct((n, GRP), BF16),
        compiler_params=_cparams(("arbitrary",)), name="gmlp",
    )(proj, proj, ws, bs)


def _attn_kernel(q_ref, kc_ref, vc_ref, kp_ref, vp_ref, bias_ref, o_ref, kbuf, vbuf, *, chunk, n_chunks, first_has_past):
    tq = chunk * n_chunks
    win = B_WINDOW + chunk
    kbuf[0:B_WINDOW, :] = kp_ref[...].astype(BF16)
    vbuf[0:B_WINDOW, :] = vp_ref[...].astype(BF16)
    kbuf[B_WINDOW:B_WINDOW + tq, :] = kc_ref[...]
    vbuf[B_WINDOW:B_WINDOW + tq, :] = vc_ref[...]
    has_past = jnp.logical_or(pl.program_id(1) > 0, first_has_past)
    col = lax.broadcasted_iota(I32, (HEADS * chunk, win), 1)
    for c in range(n_chunks):
        q = q_ref[c * chunk:(c + 1) * chunk, :]
        kk = kbuf[c * chunk:c * chunk + win, :]
        vv = vbuf[c * chunk:c * chunk + win, :]
        s = lax.dot_general(_bd_stack(q, chunk), kk, (((1,), (1,)), ((), ())), preferred_element_type=F32)
        s = s + bias_ref[...]
        visible = jnp.logical_or(has_past, col + c * chunk >= B_WINDOW)
        s = jnp.where(visible, s, NEG_INF)
        m = jnp.max(s, axis=-1, keepdims=True)
        p = jnp.exp(s - m)
        l = jnp.sum(p, axis=-1, keepdims=True)
        o = jnp.dot(p.astype(BF16), vv, preferred_element_type=F32) * (1.0 / l)
        o_ref[c * chunk:(c + 1) * chunk, :] = _bd_unstack(o, chunk).astype(BF16)


def _attention(proj, lw, b, t, cache_k=None, cache_v=None):
    n = proj.shape[0]
    step = cache_k is not None
    chunk = min(t, CHUNK)
    tq = min(t, B_WINDOW)
    nt = t // tq
    rel = lw["b_rel"]
    win = B_WINDOW + chunk
    lo = REL_CLIP - (chunk - 1)
    n_far = (chunk - 1) + win - (2 * REL_CLIP + 1 - lo)
    by_dist = jnp.concatenate([rel[:, lo:], jnp.broadcast_to(rel[:, -1:], (HEADS, n_far))], axis=1)
    by_key = by_dist[:, ::-1]
    bias = jnp.stack([by_key[:, chunk - 1 - i:chunk - 1 - i + win] for i in range(chunk)], axis=1)
    bias = bias.astype(F32).reshape(HEADS * chunk, win)
    cur = lambda g: pl.BlockSpec((tq, GRP), lambda bi, j: (bi * nt + j, g))
    if step:
        prev_k = pl.BlockSpec((B_WINDOW, GRP), lambda bi, j: (bi, 0))
        prev_v = prev_k
        pk_arr, pv_arr = cache_k, cache_v
    else:
        assert tq == B_WINDOW
        prev_k = pl.BlockSpec((B_WINDOW, GRP), lambda bi, j: (bi * nt + jnp.maximum(j - 1, 0), PK))
        prev_v = pl.BlockSpec((B_WINDOW, GRP), lambda bi, j: (bi * nt + jnp.maximum(j - 1, 0), PV))
        pk_arr, pv_arr = proj, proj
    return pl.pallas_call(
        functools.partial(_attn_kernel, chunk=chunk, n_chunks=tq // chunk, first_has_past=step),
        grid=(b, nt),
        in_specs=[cur(PQ), cur(PK), cur(PV), prev_k, prev_v, pl.BlockSpec(bias.shape, lambda bi, j: (0, 0))],
        out_specs=pl.BlockSpec((tq, GRP), lambda bi, j: (bi * nt + j, 0)),
        out_shape=jax.ShapeDtypeStruct((n, GRP), BF16),
        scratch_shapes=[pltpu.VMEM((B_WINDOW + tq, GRP), BF16), pltpu.VMEM((B_WINDOW + tq, GRP), BF16)],
        compiler_params=_cparams(("arbitrary", "arbitrary")), name="band_attn",
    )(proj, proj, proj, pk_arr, pv_arr, bias)


def _conv_kernel(g_ref, halo_ref, dw_ref, dwb_ref, lng_ref, lnb_ref, o_ref, tail_ref, xp, zbuf, *, tc, sub, first_has_past):
    halo = halo_ref[...].astype(F32)
    has_past = jnp.logical_or(pl.program_id(1) > 0, first_has_past)
    xp[0:HALO, :] = jnp.where(has_past, halo, 0.0)
    xp[HALO:HALO + tc, :] = g_ref[...].astype(F32)
    xp[HALO + tc:, :] = jnp.zeros((xp.shape[0] - HALO - tc, GRP), F32)

    @pl.when(pl.program_id(1) == pl.num_programs(1) - 1)
    def _():
        tail_ref[...] = xp[tc:tc + HALO, :]

    lead = HALO - C_BUF
    sl = 8
    for s in range(tc // sub):
        acc = None
        for r in range(sl):
            taps = [p for p in range(r, lead + C_WIDTH, sl) if p >= lead]
            z = None
            for p in taps:
                a0 = s * sub + p - r
                term = dw_ref[p - lead:p - lead + 1, :] * xp[a0:a0 + sub + sl, :]
                z = term if z is None else z + term
            zbuf[r] = z
            part = zbuf[r, r:r + sub, :]
            acc = part if acc is None else acc + part
        y = acc + dwb_ref[...]
        mu = jnp.mean(y, axis=-1, keepdims=True)
        yc = y - mu
        y = yc * lax.rsqrt(jnp.mean(yc * yc, axis=-1, keepdims=True) + EPS) * lng_ref[...] + lnb_ref[...]
        o_ref[s * sub:(s + 1) * sub, :] = (y * _sigmoid(y)).astype(BF16)


def _conv(proj, lw, b, t, state=None):
    n = proj.shape[0]
    step = state is not None
    tc = min(t, 512)
    nt = t // tc
    sub = min(tc, 64)
    if step:
        halo_arr = state
        halo_spec = pl.BlockSpec((HALO, GRP), lambda bi, j: (bi, 0))
    else:
        per = tc // HALO
        halo_arr = proj
        halo_spec = pl.BlockSpec((HALO, GRP), lambda bi, j: (jnp.maximum((bi * nt + j) * per - 1, 0), PGLU))
    vec = pl.BlockSpec((1, GRP), lambda bi, j: (0, 0))
    return pl.pallas_call(
        functools.partial(_conv_kernel, tc=tc, sub=sub, first_has_past=step),
        grid=(b, nt),
        in_specs=[pl.BlockSpec((tc, GRP), lambda bi, j: (bi * nt + j, PGLU)), halo_spec,
                  pl.BlockSpec((C_WIDTH, GRP), lambda bi, j: (0, 0)), vec, vec, vec],
        out_specs=[pl.BlockSpec((tc, GRP), lambda bi, j: (bi * nt + j, 0)),
                   pl.BlockSpec((HALO, GRP), lambda bi, j: (bi, 0))],
        out_shape=[jax.ShapeDtypeStruct((n, GRP), BF16), jax.ShapeDtypeStruct((b * HALO, GRP), F32)],
        scratch_shapes=[pltpu.VMEM((HALO + tc + 8, GRP), F32), pltpu.VMEM((8, sub + 8, GRP), F32)],
        compiler_params=_cparams(("arbitrary", "arbitrary")), name="conv_module",
    )(proj, halo_arr, lw["c_dw"], lw["c_dw_b"], lw["c_ln_g"], lw["c_ln_b"])


def _gla_tables(L):
    i = np.arange(L)[:, None]
    t = np.arange(L)[None, :]
    mats = [(t <= i), (t > i)]
    masks = []
    s = GLA_SUB
    same = (i // s) == (t // s)
    mats += [same & (t <= i), same & (t <= i)]
    masks.append(same & (t <= i))
    s *= 2
    while s <= L:
        h = s // 2
        same = (i // s) == (t // s)
        anchor = (i // s) * s + h - 1
        mats.append(same & (t > anchor) & (t <= i) & (i % s >= h))
        mats.append(same & (t > i) & (t <= anchor) & (i % s < h))
        jj = t
        masks.append(((i // s) == (jj // s)) & (i % s >= h) & (jj % s < h))
        s *= 2
    mat = np.concatenate([m.astype(np.float32) for m in mats], axis=0)
    mask = np.stack([np.tile(m.astype(np.float32), (HEADS, 1)) for m in masks], axis=0)
    return mat, mask


def _gla_kernel(q_ref, k_ref, v_ref, la_ref, dr_ref, s0_ref, emat_ref, lmask_ref, bdmask_ref, hsum_ref, gon_ref,
                o_ref, sf_ref, st, *, L, n_chunks, n_levels, first_has_state):
    j = pl.program_id(1)

    @pl.when(j == 0)
    def _():
        if first_has_state:
            st[...] = s0_ref[...]
        else:
            st[...] = jnp.zeros_like(st)

    row = lax.broadcasted_iota(I32, (L, GRP), 0)

    for c in range(n_chunks):
        rows = slice(c * L, (c + 1) * L)
        la = la_ref[rows, :]
        q = q_ref[rows, :].astype(F32)
        k = k_ref[rows, :].astype(F32)
        v = v_ref[rows, :]
        e = jnp.dot(emat_ref[...], la, preferred_element_type=F32)
        blk = lambda idx: e[idx * L:(idx + 1) * L]
        e_in = blk(0)
        s_t = st[...]
        qp = (q * jnp.exp(e_in)).astype(BF16)
        o = lax.dot_general(qp, s_t.astype(BF16), (((1,), (1,)), ((), ())), preferred_element_type=F32)
        att = None
        for lvl in range(n_levels):
            eq = blk(2 + 2 * lvl)
            ek = blk(3 + 2 * lvl)
            if lvl == 0:
                ql = q * jnp.exp(eq)
                kl = k * jnp.exp(-ek)
            else:
                size = GLA_SUB << lvl
                upper = (row & (size - 1)) >= (size // 2)
                ql = jnp.where(upper, q * jnp.exp(eq), 0.0)
                kl = jnp.where(upper, 0.0, k * jnp.exp(ek))
            a = lax.dot_general(_bd_stack(ql.astype(BF16), L), kl.astype(BF16), (((1,), (1,)), ((), ())),
                                preferred_element_type=F32) * lmask_ref[lvl]
            att = a if att is None else att + a
        o = o + _bd_unstack(jnp.dot(att.astype(BF16), v, preferred_element_type=F32), L)
        kst = (k * jnp.exp(blk(1))).astype(BF16)
        upd = lax.dot_general(v, kst, (((0,), (0,)), ((), ())), preferred_element_type=F32)
        st[...] = s_t * jnp.exp(e_in[L - 1:L, :]) + upd * bdmask_ref[...]
        y = o * lax.rsqrt(_head_meansq(o, hsum_ref) + EPS) * gon_ref[...] * dr_ref[rows, :].astype(F32)
        o_ref[rows, :] = y.astype(BF16)

    @pl.when(j == pl.num_programs(1) - 1)
    def _():
        sf_ref[...] = st[...]


def _gla(proj, lw, b, t, s0=None):
    n = proj.shape[0]
    step = s0 is not None
    L = min(t, 64)
    td = min(t, 512)
    nt = t // td
    n_levels = int(np.log2(L // GLA_SUB)) + 1
    emat, lmask = _gla_tables(L)
    emat = jnp.asarray(emat, BF16)
    lmask = jnp.asarray(lmask, F32)
    if not step:
        s0 = jnp.zeros((GRP, GRP), F32)
        s0_spec = pl.BlockSpec((GRP, GRP), lambda bi, j: (0, 0))
    else:
        s0_spec = pl.BlockSpec((GRP, GRP), lambda bi, j: (bi, 0))
    cur = lambda g: pl.BlockSpec((td, GRP), lambda bi, j: (bi * nt + j, g))
    c2 = lambda bi, j: (0, 0)
    return pl.pallas_call(
        functools.partial(_gla_kernel, L=L, n_chunks=td // L, n_levels=n_levels, first_has_state=step),
        grid=(b, nt),
        in_specs=[cur(PDQ), cur(PDK), cur(PDV), cur(PLA), cur(PDR), s0_spec,
                  pl.BlockSpec(emat.shape, c2), pl.BlockSpec(lmask.shape, lambda bi, j: (0, 0, 0)),
                  pl.BlockSpec((GRP, GRP), c2), pl.BlockSpec((GRP, GRP), c2), pl.BlockSpec((1, GRP), c2)],
        out_specs=[pl.BlockSpec((td, GRP), lambda bi, j: (bi * nt + j, 0)),
                   pl.BlockSpec((GRP, GRP), lambda bi, j: (bi, 0))],
        out_shape=[jax.ShapeDtypeStruct((n, GRP), BF16), jax.ShapeDtypeStruct((b * GRP, GRP), F32)],
        scratch_shapes=[pltpu.VMEM((GRP, GRP), F32)],
        compiler_params=_cparams(("arbitrary", "arbitrary")), name="gla",
    )(proj, proj, proj, proj, proj, s0, emat, lmask, lw["bdmask"], lw["hsum"], lw["gon"])


def _out_kernel(ya_ref, yb_ref, yc_ref, yd_ref, x_ref, wo_ref, g2_ref, wr_hi_ref, wr_lo_ref, br_ref, tri_ref,
                xo_ref, xn_ref, ri_ref, rf_ref, cnt_ref, *, tm):
    @pl.when(pl.program_id(0) == 0)
    def _():
        cnt_ref[...] = jnp.zeros_like(cnt_ref)

    y = jnp.dot(ya_ref[...], wo_ref[0 * GRP:1 * GRP, :], preferred_element_type=F32)
    y = y + jnp.dot(yb_ref[...], wo_ref[1 * GRP:2 * GRP, :], preferred_element_type=F32)
    y = y + jnp.dot(yc_ref[...], wo_ref[2 * GRP:3 * GRP, :], preferred_element_type=F32)
    y = y + jnp.dot(yd_ref[...], wo_ref[3 * GRP:4 * GRP, :], preferred_element_type=F32)
    x = x_ref[...] + y
    xo_ref[...] = x
    xn = x * lax.rsqrt(jnp.mean(x * x, axis=-1, keepdims=True) + EPS) * g2_ref[...]
    xn_ref[...] = _pack_halves(xn)

    xh = xn.astype(BF16)
    xl = (xn - xh.astype(F32)).astype(BF16)
    logits = (jnp.dot(xh, wr_hi_ref[...], preferred_element_type=F32)
              + jnp.dot(xl, wr_hi_ref[...], preferred_element_type=F32)
              + jnp.dot(xh, wr_lo_ref[...], preferred_element_type=F32)) + br_ref[...]
    lane = lax.broadcasted_iota(I32, (tm, LANES), 1)
    big = np.int32(1 << 20)

    def first_max(mask):
        v = jnp.max(jnp.where(mask, logits, -jnp.inf), axis=-1, keepdims=True)
        idx = jnp.min(jnp.where(jnp.logical_and(mask, logits == v), lane, big), axis=-1, keepdims=True)
        return v, idx

    is_grp = jnp.logical_and(lane >= N_EXPERTS, lane < N_EXPERTS + N_GROUPS)
    gmax, gidx = first_max(is_grp)
    p_grp = 1.0 / jnp.sum(jnp.where(is_grp, jnp.exp(logits - gmax), 0.0), axis=-1, keepdims=True)
    grp = gidx - N_EXPERTS
    in_grp = (lane // PER_GROUP) == grp
    v1, i1 = first_max(in_grp)
    v2, i2 = first_max(jnp.logical_and(in_grp, lane != i1))
    e21 = jnp.exp(v2 - v1)
    gate1 = p_grp / (1.0 + e21)
    gate2 = p_grp * e21 / (1.0 + e21)

    oh1 = lane == i1
    oh2 = lane == i2
    both = jnp.logical_or(oh1, oh2)
    ones = jnp.where(both, 1.0, 0.0).astype(BF16)
    before = jnp.dot(tri_ref[...], ones, preferred_element_type=F32) + cnt_ref[...].astype(F32)
    rank1 = jnp.sum(jnp.where(oh1, before, 0.0), axis=-1, keepdims=True)
    rank2 = jnp.sum(jnp.where(oh2, before, 0.0), axis=-1, keepdims=True)
    cnt_ref[...] = cnt_ref[...] + jnp.sum(jnp.where(both, 1.0, 0.0), axis=0, keepdims=True).astype(I32)

    ri = jnp.where(lane == 0, i1, jnp.where(lane == 1, i2, jnp.where(lane == 2, rank1.astype(I32),
                                                                      jnp.where(lane == 3, rank2.astype(I32), 0))))
    ri_ref[...] = ri
    rf_ref[...] = jnp.where(lane == 0, gate1, jnp.where(lane == 1, gate2, 0.0))


def _out_proj(ya, yb, yc, yd, x, lw):
    n = x.shape[0]
    tm = min(512, n)
    row = lambda i: (i, 0)
    const = lambda i: (0, 0)
    tri = jnp.asarray(np.tril(np.ones((tm, tm), np.float32), -1), BF16)
    consts = [lw["w_out"], lw["g2"], lw["wr_hi"], lw["wr_lo"], lw["br"], tri]
    yspec = pl.BlockSpec((tm, GRP), row)
    return pl.pallas_call(
        functools.partial(_out_kernel, tm=tm),
        grid=(n // tm,),
        in_specs=[yspec, yspec, yspec, yspec, pl.BlockSpec((tm, D_MODEL), row)] + [pl.BlockSpec(c.shape, const) for c in consts],
        out_specs=[pl.BlockSpec((tm, D_MODEL), row), pl.BlockSpec((tm, D_MODEL // 2), row),
                   pl.BlockSpec((tm, LANES), row), pl.BlockSpec((tm, LANES), row), pl.BlockSpec((1, LANES), const)],
        out_shape=[jax.ShapeDtypeStruct((n, D_MODEL), F32), jax.ShapeDtypeStruct((n, D_MODEL // 2), U32),
                   jax.ShapeDtypeStruct((n, LANES), I32), jax.ShapeDtypeStruct((n, LANES), F32),
                   jax.ShapeDtypeStruct((1, LANES), I32)],
        compiler_params=_cparams(("arbitrary",)), name="out_proj_router",
    )(ya, yb, yc, yd, x, *consts)


def _sc_scatter_rows(x, idx, n_out):
    n, d = x.shape
    kk = idx.shape[0]
    per_w = n // SC_WORKERS
    win = min(SC_WIN, per_w)
    n_win = per_w // win
    assert n_win * win * SC_WORKERS == n
    mesh = plsc.VectorSubcoreMesh(core_axis_name="c", subcore_axis_name="s")

    @functools.partial(
        pl.kernel, mesh=mesh, out_type=jax.ShapeDtypeStruct((n_out, d), x.dtype),
        scratch_types=[pltpu.VMEM((kk, win), I32), pltpu.VMEM((win, d), x.dtype)],
        name="sc_scatter_rows")
    def k(x_hbm, idx_hbm, o_hbm, idx_v, rows_v):
        wid = lax.axis_index("s") * 2 + lax.axis_index("c")
        base = wid * per_w

        @pl.loop(0, n_win)
        def _(w):
            off = base + w * win
            pltpu.sync_copy(x_hbm.at[pl.ds(off, win)], rows_v)
            for j in range(kk):
                pltpu.sync_copy(idx_hbm.at[j, pl.ds(off, win)], idx_v.at[j])
                pltpu.sync_copy(rows_v, o_hbm.at[idx_v.at[j]])

    return k(x, idx)


def _sc_gather_rows(y, idx):
    _, d = y.shape
    kk, n = idx.shape
    per_w = n // SC_WORKERS
    win = min(SC_WIN, per_w)
    n_win = per_w // win
    assert n_win * win * SC_WORKERS == n
    mesh = plsc.VectorSubcoreMesh(core_axis_name="c", subcore_axis_name="s")

    @functools.partial(
        pl.kernel, mesh=mesh, out_type=jax.ShapeDtypeStruct((kk, n, d), y.dtype),
        scratch_types=[pltpu.VMEM((kk, win), I32), pltpu.VMEM((win, d), y.dtype)],
        name="sc_gather_rows")
    def k(y_hbm, idx_hbm, o_hbm, idx_v, rows_v):
        wid = lax.axis_index("s") * 2 + lax.axis_index("c")
        base = wid * per_w

        @pl.loop(0, n_win)
        def _(w):
            off = base + w * win
            for j in range(kk):
                pltpu.sync_copy(idx_hbm.at[j, pl.ds(off, win)], idx_v.at[j])
                pltpu.sync_copy(y_hbm.at[idx_v.at[j]], rows_v)
                pltpu.sync_copy(rows_v, o_hbm.at[j, pl.ds(off, win)])

    return k(y, idx)


def _moe_kernel(bexp_ref, nused_ref, x_ref, wg_ref, wu_ref, wd_ref, o_ref, wg_s, wu_s, wd_s):
    i = pl.program_id(0)
    prev = bexp_ref[jnp.maximum(i - 1, 0)]
    fresh = jnp.logical_or(i == 0, bexp_ref[i] != prev)

    @pl.when(jnp.logical_and(fresh, i < nused_ref[0]))
    def _():
        wg_s[...] = wg_ref[...].astype(BF16)
        wu_s[...] = wu_ref[...].astype(BF16)
        wd_s[...] = wd_ref[...].astype(BF16)

    @pl.when(i < nused_ref[0])
    def _():
        half = D_MODEL // 2
        w = x_ref[...]
        xa = _unpack_hi(w).astype(BF16)
        xb = _unpack_lo(w).astype(BF16)
        hg = (jnp.dot(xa, wg_s[:half, :], preferred_element_type=F32)
              + jnp.dot(xb, wg_s[half:, :], preferred_element_type=F32))
        hu = (jnp.dot(xa, wu_s[:half, :], preferred_element_type=F32)
              + jnp.dot(xb, wu_s[half:, :], preferred_element_type=F32))
        h = (hg * _sigmoid(hg) * hu).astype(BF16)
        o_ref[...] = _pack_halves(jnp.dot(h, wd_s[...], preferred_element_type=F32))


def _moe_experts(xs, blk_exp, n_used, w_gate, w_up, w_down, layer, bm):
    p = xs.shape[0]
    n_blocks = p // bm
    live = lambda i, be, nu: jnp.minimum(i, jnp.maximum(nu[0] - 1, 0))
    wspec = lambda shape: pl.BlockSpec((None, None) + shape, lambda i, be, nu: (layer, be[live(i, be, nu)], 0, 0))
    grid_spec = pltpu.PrefetchScalarGridSpec(
        num_scalar_prefetch=2, grid=(n_blocks,),
        in_specs=[pl.BlockSpec((bm, D_MODEL // 2), lambda i, be, nu: (live(i, be, nu), 0)),
                  wspec((D_MODEL, D_EXPERT)), wspec((D_MODEL, D_EXPERT)), wspec((D_EXPERT, D_MODEL))],
        out_specs=pl.BlockSpec((bm, D_MODEL // 2), lambda i, be, nu: (live(i, be, nu), 0)),
        scratch_shapes=[pltpu.VMEM((D_MODEL, D_EXPERT), BF16), pltpu.VMEM((D_MODEL, D_EXPERT), BF16),
                        pltpu.VMEM((D_EXPERT, D_MODEL), BF16)])
    return pl.pallas_call(
        _moe_kernel, grid_spec=grid_spec, out_shape=jax.ShapeDtypeStruct((p, D_MODEL // 2), U32),
        compiler_params=_cparams(("arbitrary",)), name="moe_experts",
    )(blk_exp, n_used, xs, w_gate, w_up, w_down)


def _moe_block_rows(n):
    return 512 if n >= 16384 else 128


def _moe(xn_packed, route_i, counts, lw):
    n = xn_packed.shape[0]
    bm = _moe_block_rows(n)
    n_blocks = -(-(2 * n + N_EXPERTS * (bm - 1)) // bm)
    cnt = counts[0, :N_EXPERTS]
    padded = (cnt + bm - 1) // bm * bm
    pad_end = jnp.cumsum(padded)
    pad_start = pad_end - padded
    experts = jnp.arange(N_EXPERTS, dtype=I32)
    eid = route_i[:, 0:2].T
    start_of = jnp.sum(jnp.where(eid[:, :, None] == experts, pad_start, 0), axis=-1)
    dest = (start_of + route_i[:, 2:4].T).astype(I32)
    first_row = jnp.arange(n_blocks, dtype=I32) * bm
    blk_exp = jnp.minimum(jnp.sum((pad_end[None, :] <= first_row[:, None]).astype(I32), axis=1), N_EXPERTS - 1)
    n_used = (pad_end[-1:] // bm).astype(I32)
    xs = _sc_scatter_rows(xn_packed, dest, n_blocks * bm)
    ys = _moe_experts(xs, blk_exp, n_used, lw["e_w_gate"], lw["e_w_up"], lw["e_w_down"], lw["layer"], bm)
    return _sc_gather_rows(ys, dest)


def _combine_kernel(x_ref, y_ref, gate_ref, o_ref):
    half = D_MODEL // 2
    x = x_ref[...]
    g = gate_ref[...]
    g0 = g[:, 0:1]
    g1 = g[:, 1:2]
    w0 = y_ref[0]
    w1 = y_ref[1]
    o_ref[:, :half] = x[:, :half] + g0 * _unpack_hi(w0) + g1 * _unpack_hi(w1)
    o_ref[:, half:] = x[:, half:] + g0 * _unpack_lo(w0) + g1 * _unpack_lo(w1)


def _combine(x, y, gates):
    n = x.shape[0]
    tm = min(512, n)
    row = lambda i: (i, 0)
    return pl.pallas_call(
        _combine_kernel, grid=(n // tm,),
        in_specs=[pl.BlockSpec((tm, D_MODEL), row), pl.BlockSpec((2, tm, D_MODEL // 2), lambda i: (0, i, 0)),
                  pl.BlockSpec((tm, LANES), row)],
        out_specs=pl.BlockSpec((tm, D_MODEL), row), out_shape=jax.ShapeDtypeStruct((n, D_MODEL), F32),
        compiler_params=_cparams(("arbitrary",)), name="moe_combine",
    )(x, y, gates)


def _layer_weights(l, p):
    w_in = p["w_in"][l]
    cols = [w_in[:, i * GRP:(i + 1) * GRP] for i in range(11)]
    by_group = [None] * N_PROJ
    for ref_i, g in enumerate(_REF_GROUPS):
        if g is not None:
            by_group[g] = cols[ref_i]
    by_group[PGLU] = cols[5]
    w_dg = jnp.zeros((D_MODEL, LANES), F32).at[:, :GATE_RANK].set(w_in[:, 11 * GRP:])
    wg2 = jnp.zeros((LANES, GRP), F32).at[:GATE_RANK].set(p["d_wg2"][l])
    tile4 = lambda v: jnp.tile(v, HEADS)[None, :]
    hid = np.arange(GRP) // HEAD_DIM
    bd = (hid[:, None] == hid[None, :]).astype(np.float32)
    wr = jnp.zeros((D_MODEL, LANES), F32).at[:, :N_EXPERTS].set(p["r_expert_w"][l])
    wr = wr.at[:, N_EXPERTS:N_EXPERTS + N_GROUPS].set(p["r_group_w"][l])
    wr_hi = wr.astype(BF16)
    br = jnp.zeros((1, LANES), F32).at[0, :N_EXPERTS].set(p["r_expert_b"][l])
    br = br.at[0, N_EXPERTS:N_EXPERTS + N_GROUPS].set(p["r_group_b"][l])
    return {
        "g1": p["norm1_g"][l][None, :],
        "w_in": jnp.concatenate(by_group[:PLA], axis=1).astype(BF16),
        "w_cg": cols[6].astype(BF16),
        "w_dg": w_dg.astype(BF16),
        "wg2": wg2.astype(BF16),
        "bg": p["d_bg"][l][None, :],
        "gq": tile4(p["b_qnorm_g"][l]), "gk": tile4(p["b_knorm_g"][l]), "gav": p["a_vnorm_g"][l][None, :],
        "gon": tile4(p["d_onorm_g"][l]),
        "hsum": jnp.asarray(bd, BF16), "bdmask": jnp.asarray(bd, F32),
        "a_ws": p["a_ws"][l], "a_bs_rows": jnp.repeat(p["a_bs"][l].T, HEAD_DIM, axis=1),
        "b_rel": p["b_rel_bias"][l],
        "c_dw": p["c_dw"][l], "c_dw_b": p["c_dw_b"][l][None, :],
        "c_ln_g": p["c_ln_g"][l][None, :], "c_ln_b": p["c_ln_b"][l][None, :],
        "w_out": p["w_out"][l].astype(BF16),
        "g2": p["norm2_g"][l][None, :],
        "wr_hi": wr_hi, "wr_lo": (wr - wr_hi.astype(F32)).astype(BF16), "br": br,
        "e_w_gate": p["e_w_gate"], "e_w_up": p["e_w_up"], "e_w_down": p["e_w_down"], "layer": l,
    }


def _states_to_bd(s):
    b = s.shape[0]
    st = jnp.swapaxes(s, 2, 3)
    eye = jnp.eye(HEADS, dtype=s.dtype)
    bd = st[:, :, :, None, :] * eye[None, :, None, :, None]
    return bd.reshape(b * GRP, GRP)


def _states_from_bd(sf, b):
    s = sf.reshape(b, HEADS, HEAD_DIM, HEADS, HEAD_DIM)
    diag = jnp.stack([s[:, h, :, h, :] for h in range(HEADS)], axis=1)
    return jnp.swapaxes(diag, 2, 3)


def _mix_and_route(x, lw, b, t, pending, caches):
    step = caches is not None
    y_prev, gates_prev = pending if pending is not None else (None, None)
    x, proj, kv, a_v = _in_proj(x, lw, t, y_prev, gates_prev, emit_av=step)
    ya = _gmlp(proj, lw, t)
    if not step:
        yb = _attention(proj, lw, b, t)
        yc, tail = _conv(proj, lw, b, t)
        yd, sf = _gla(proj, lw, b, t)
    else:
        ck, cv, cc, cs = caches
        yb = _attention(proj, lw, b, t, ck.reshape(b * B_WINDOW, GRP), cv.reshape(b * B_WINDOW, GRP))
        halo = jnp.pad(cc, ((0, 0), (HALO - C_BUF, 0), (0, 0))).reshape(b * HALO, GRP)
        yc, tail = _conv(proj, lw, b, t, halo)
        yd, sf = _gla(proj, lw, b, t, _states_to_bd(cs))
        a_v = a_v.reshape(b, t, GRP)
    x2, xn_packed, route_i, route_f, counts = _out_proj(ya, yb, yc, yd, x, lw)
    y = _moe(xn_packed, route_i, counts, lw)
    keep = min(B_WINDOW, t)
    new_k = kv[:, :GRP].reshape(b, keep, HEADS, HEAD_DIM)
    new_v = kv[:, GRP:].reshape(b, keep, HEADS, HEAD_DIM)
    new_buf = tail.reshape(b, HALO, GRP)[:, HALO - C_BUF:]
    return x2, (y, route_f), (new_k, new_v, new_buf, _states_from_bd(sf, b), a_v)


def kernel(x_prompt, x_sample, cache_b_k, cache_b_v, state_c_conv, state_d_gla, norm1_g, w_in, a_vnorm_g, a_ws, a_bs, b_qnorm_g, b_knorm_g, b_rel_bias, c_dw, c_dw_b, c_ln_g, c_ln_b, d_wg2, d_bg, d_onorm_g, w_out, norm2_g, r_group_w, r_group_b, r_expert_w, r_expert_b, e_w_gate, e_w_up, e_w_down):
    params = dict(norm1_g=norm1_g, w_in=w_in, a_vnorm_g=a_vnorm_g, a_ws=a_ws, a_bs=a_bs, b_qnorm_g=b_qnorm_g,
                  b_knorm_g=b_knorm_g, b_rel_bias=b_rel_bias, c_dw=c_dw, c_dw_b=c_dw_b, c_ln_g=c_ln_g, c_ln_b=c_ln_b,
                  d_wg2=d_wg2, d_bg=d_bg, d_onorm_g=d_onorm_g, w_out=w_out, norm2_g=norm2_g, r_group_w=r_group_w,
                  r_group_b=r_group_b, r_expert_w=r_expert_w, r_expert_b=r_expert_b, e_w_gate=e_w_gate,
                  e_w_up=e_w_up, e_w_down=e_w_down)
    depth = w_in.shape[0]
    bp, tp, _ = x_prompt.shape
    bs, ts, _ = x_sample.shape
    xp = x_prompt.reshape(bp * tp, D_MODEL)
    xs = x_sample.reshape(bs * ts, D_MODEL)
    pend_p = pend_s = None
    st_p, st_s = [], []
    for l in range(depth):
        lw = _layer_weights(l, params)
        xp, pend_p, sp = _mix_and_route(xp, lw, bp, tp, pend_p, None)
        xs, pend_s, ss = _mix_and_route(xs, lw, bs, ts, pend_s,
                                        (cache_b_k[l], cache_b_v[l], state_c_conv[l], state_d_gla[l]))
        st_p.append(sp)
        st_s.append(ss)
    yp = _combine(xp, pend_p[0], pend_p[1]).reshape(bp, tp, D_MODEL)
    ys = _combine(xs, pend_s[0], pend_s[1]).reshape(bs, ts, D_MODEL)
    stack = lambda sts, i: jnp.stack([s[i] for s in sts])
    return (yp, ys, stack(st_p, 0), stack(st_p, 1), stack(st_p, 2), stack(st_p, 3),
            stack(st_s, 0), stack(st_s, 1), stack(st_s, 2), stack(st_s, 3), stack(st_s, 4))
```

```python
import functools

import numpy as np
import jax
import jax.numpy as jnp
from jax import lax
from jax.experimental import pallas as pl
from jax.experimental.pallas import tpu as pltpu
from jax.experimental.pallas import tpu_sc as plsc

F32 = jnp.float32
BF16 = jnp.bfloat16
I32 = jnp.int32
U32 = jnp.uint32

D_MODEL = 1024
GRP = 256
HEADS = 4
HEAD_DIM = 64
CHUNK = 64
A_CHUNK = 128
B_WINDOW = 512
REL_CLIP = 128
C_WIDTH = 31
C_BUF = C_WIDTH - 1
HALO = 32
GATE_RANK = 16
GLA_TAU = 16.0
GLA_SUB = 16
N_GROUPS = 4
PER_GROUP = 8
N_EXPERTS = 32
D_EXPERT = 512
EPS = 1e-6
NEG_INF = -1e30
LANES = 128
VMEM_LIMIT = 48 * 1024 * 1024

PK, PV, PQ, PAU, PAV, PGLU, PDQ, PDK, PDV, PDR, PLA = range(11)
N_PROJ = 11
_REF_GROUPS = (PAU, PAV, PQ, PK, PV, None, None, PDQ, PDK, PDV, PDR)

SC_WORKERS = 32
SC_WIN = 128


def _cparams(sem):
    return pltpu.CompilerParams(dimension_semantics=sem, vmem_limit_bytes=VMEM_LIMIT)


def _sigmoid(x):
    return 1.0 / (1.0 + jnp.exp(-x))


def _gelu_tanh(x):
    c = np.float32(np.sqrt(2.0 / np.pi))
    return 0.5 * x * (1.0 + jnp.tanh(c * (x + np.float32(0.044715) * (x * x * x))))


def _pack_halves(y):
    half = y.shape[1] // 2
    hi = pltpu.bitcast(y[:, :half].astype(BF16).astype(F32), U32)
    lo = pltpu.bitcast(y[:, half:].astype(BF16).astype(F32), U32)
    return hi | (lo >> np.uint32(16))


def _unpack_hi(w):
    return pltpu.bitcast(w & np.uint32(0xFFFF0000), F32)


def _unpack_lo(w):
    return pltpu.bitcast(w << np.uint32(16), F32)


def _head_id(shape, axis, size):
    return lax.broadcasted_iota(I32, shape, axis) // size


def _bd_stack(x, rows):
    x4 = jnp.concatenate([x] * HEADS, axis=0)
    shape = (HEADS * rows, GRP)
    keep = _head_id(shape, 0, rows) == _head_id(shape, 1, HEAD_DIM)
    return jnp.where(keep, x4, jnp.zeros_like(x4))


def _bd_unstack(o, rows):
    lane_h = _head_id((rows, GRP), 1, HEAD_DIM)
    out = o[(HEADS - 1) * rows:HEADS * rows]
    for h in range(HEADS - 2, -1, -1):
        out = jnp.where(lane_h == h, o[h * rows:(h + 1) * rows], out)
    return out


def _head_meansq(o, hsum_ref):
    sq = (o * o).astype(BF16)
    return jnp.dot(sq, hsum_ref[...], preferred_element_type=F32) * np.float32(1.0 / HEAD_DIM)


def _in_kernel(*refs, combine, emit_av, tiles_per_stream):
    refs = list(refs)
    x_ref = refs.pop(0)
    if combine:
        y_ref = refs.pop(0)
        gate_ref = refs.pop(0)
    g1_ref, w_ref, wcg_ref, wdg_ref, wg2_ref, bg_ref, gq_ref, gk_ref, gav_ref, hsum_ref = refs[:10]
    refs = refs[10:]
    if combine:
        xo_ref = refs.pop(0)
    p_ref = refs.pop(0)
    kv_ref = refs.pop(0)
    if emit_av:
        av_ref = refs.pop(0)
    raw = refs.pop(0)
    x = x_ref[...]
    if combine:
        half = D_MODEL // 2
        g = gate_ref[...]
        g0 = g[:, 0:1]
        g1 = g[:, 1:2]
        w0 = y_ref[0]
        w1 = y_ref[1]
        xa = x[:, :half] + g0 * _unpack_hi(w0) + g1 * _unpack_hi(w1)
        xb = x[:, half:] + g0 * _unpack_lo(w0) + g1 * _unpack_lo(w1)
        xo_ref[:, :half] = xa
        xo_ref[:, half:] = xb
        x = jnp.concatenate([xa, xb], axis=1)
    rs = lax.rsqrt(jnp.mean(x * x, axis=-1, keepdims=True) + EPS)
    h = (x * g1_ref[...]).astype(BF16)

    n_slots = PLA + 2

    def matmul(slot):
        if slot < PLA:
            raw[:, slot * GRP:(slot + 1) * GRP] = jnp.dot(h, w_ref[:, slot * GRP:(slot + 1) * GRP],
                                                          preferred_element_type=F32)
        elif slot == PLA:
            raw[:, PLA * GRP:(PLA + 1) * GRP] = jnp.dot(h, wcg_ref[...], preferred_element_type=F32)
        else:
            raw[:, (PLA + 1) * GRP:] = jnp.dot(h, wdg_ref[...], preferred_element_type=F32)

    def proj(g):
        return raw[:, g * GRP:(g + 1) * GRP] * rs

    def put(g, val):
        p_ref[:, g * GRP:(g + 1) * GRP] = val.astype(BF16)

    def epilogue(slot):
        if slot == PK:
            r = proj(PK)
            put(PK, r * lax.rsqrt(_head_meansq(r, hsum_ref) + EPS) * gk_ref[...])
        elif slot == PV:
            put(PV, proj(PV))

            @pl.when(pl.program_id(0) % tiles_per_stream == tiles_per_stream - 1)
            def _():
                r = proj(PK)
                kv_ref[:, :GRP] = r * lax.rsqrt(_head_meansq(r, hsum_ref) + EPS) * gk_ref[...]
                kv_ref[:, GRP:] = proj(PV)
        elif slot == PQ:
            r = proj(PQ)
            put(PQ, r * lax.rsqrt(_head_meansq(r, hsum_ref) + EPS) * (gq_ref[...] * np.float32(HEAD_DIM ** -0.5)))
        elif slot == PAU:
            put(PAU, _gelu_tanh(proj(PAU)))
        elif slot == PAV:
            r = _gelu_tanh(proj(PAV))
            av = r * lax.rsqrt(jnp.mean(r * r, axis=-1, keepdims=True) + EPS) * gav_ref[...]
            put(PAV, av)
            if emit_av:
                av_ref[...] = av
        elif slot == PGLU:
            pass
        elif slot == PDQ:
            put(PDQ, proj(PDQ) * np.float32(HEAD_DIM ** -0.5))
        elif slot in (PDK, PDV):
            put(slot, proj(slot))
        elif slot == PDR:
            r = proj(PDR)
            put(PDR, r * _sigmoid(r))
        elif slot == PLA:
            put(PGLU, proj(PGLU) * _sigmoid(proj(PLA)))
        else:
            dg = raw[:, (PLA + 1) * GRP:] * rs
            z = jnp.dot(dg.astype(BF16), wg2_ref[...], preferred_element_type=F32) + bg_ref[...]
            logsig = jnp.minimum(z, 0.0) - jnp.log(1.0 + jnp.exp(-jnp.abs(z)))
            put(PLA, logsig * np.float32(1.0 / GLA_TAU))

    lag = 2
    for i in range(n_slots + lag):
        if i < n_slots:
            matmul(i)
        if i >= lag:
            epilogue(i - lag)


def _in_proj(x, lw, t, y=None, gates=None, emit_av=False):
    n = x.shape[0]
    tm = min(512, n)
    combine = y is not None
    keep = min(B_WINDOW, t)
    tps = max(t // tm, 1)
    assert tps == 1 or keep == tm
    row = lambda i: (i, 0)
    const = lambda i: (0, 0)
    ins, specs = [x], [pl.BlockSpec((tm, D_MODEL), row)]
    if combine:
        ins += [y, gates]
        specs += [pl.BlockSpec((2, tm, D_MODEL // 2), lambda i: (0, i, 0)), pl.BlockSpec((tm, LANES), row)]
    consts = [lw["g1"], lw["w_in"], lw["w_cg"], lw["w_dg"], lw["wg2"], lw["bg"], lw["gq"], lw["gk"], lw["gav"], lw["hsum"]]
    ins += consts
    specs += [pl.BlockSpec(c.shape, const) for c in consts]
    out_shape = [jax.ShapeDtypeStruct((n, N_PROJ * GRP), BF16), jax.ShapeDtypeStruct((n // tps, 2 * GRP), F32)]
    out_specs = [pl.BlockSpec((tm, N_PROJ * GRP), row), pl.BlockSpec((tm, 2 * GRP), lambda i: (i // tps, 0))]
    if combine:
        out_shape = [jax.ShapeDtypeStruct((n, D_MODEL), F32)] + out_shape
        out_specs = [pl.BlockSpec((tm, D_MODEL), row)] + out_specs
    if emit_av:
        out_shape.append(jax.ShapeDtypeStruct((n, GRP), F32))
        out_specs.append(pl.BlockSpec((tm, GRP), row))
    outs = list(pl.pallas_call(
        functools.partial(_in_kernel, combine=combine, emit_av=emit_av, tiles_per_stream=tps),
        grid=(n // tm,), in_specs=specs, out_specs=out_specs, out_shape=out_shape,
        scratch_shapes=[pltpu.VMEM((tm, (PLA + 1) * GRP + LANES), F32)],
        compiler_params=_cparams(("arbitrary",)), name="in_proj",
    )(*ins))
    x_new = outs.pop(0) if combine else x
    proj, kv = outs[0], outs[1]
    return x_new, proj, kv, (outs[2] if emit_av else None)


def _gmlp_kernel(u_ref, v_ref, ws_ref, bs_ref, o_ref, *, chunk, n_chunks):
    lane_h = _head_id((chunk, GRP), 1, HEAD_DIM)
    ri = lax.broadcasted_iota(I32, (chunk, chunk), 0)
    ci = lax.broadcasted_iota(I32, (chunk, chunk), 1)
    wm = [jnp.where(ci <= ri, ws_ref[h], 0.0).astype(BF16) for h in range(HEADS)]
    for c in range(n_chunks):
        rows = slice(c * chunk, (c + 1) * chunk)
        v = v_ref[rows, :]
        sv = jnp.dot(wm[HEADS - 1], v, preferred_element_type=F32)
        for h in range(HEADS - 2, -1, -1):
            sv = jnp.where(lane_h == h, jnp.dot(wm[h], v, preferred_element_type=F32), sv)
        o_ref[rows, :] = (u_ref[rows, :].astype(F32) * (sv + bs_ref[...])).astype(BF16)


def _gmlp(proj, lw, t):
    n = proj.shape[0]
    chunk = min(t, A_CHUNK)
    ta = min(512, n)
    ws = lw["a_ws"][:, :chunk, :chunk]
    bs = lw["a_bs_rows"][:chunk]
    return pl.pallas_call(
        functools.partial(_gmlp_kernel, chunk=chunk, n_chunks=ta // chunk),
        grid=(n // ta,),
        in_specs=[pl.BlockSpec((ta, GRP), lambda i: (i, PAU)), pl.BlockSpec((ta, GRP), lambda i: (i, PAV)),
                  pl.BlockSpec(ws.shape, lambda i: (0, 0, 0)), pl.BlockSpec(bs.shape, lambda i: (0, 0))],
        out_specs=pl.BlockSpec((ta, GRP), lambda i: (i, 0)),
        out_shape=jax.ShapeDtypeStruct((n, GRP), BF16),
        compiler_params=_cparams(("arbitrary",)), name="gmlp",
    )(proj, proj, ws, bs)


def _attn_kernel(q_ref, kc_ref, vc_ref, kp_ref, vp_ref, bias_ref, o_ref, kbuf, vbuf, *, chunk, n_chunks, first_has_past):
    tq = chunk * n_chunks
    win = B_WINDOW + chunk
    kbuf[0:B_WINDOW, :] = kp_ref[...].astype(BF16)
    vbuf[0:B_WINDOW, :] = vp_ref[...].astype(BF16)
    kbuf[B_WINDOW:B_WINDOW + tq, :] = kc_ref[...]
    vbuf[B_WINDOW:B_WINDOW + tq, :] = vc_ref[...]
    has_past = jnp.logical_or(pl.program_id(1) > 0, first_has_past)
    col = lax.broadcasted_iota(I32, (HEADS * chunk, win), 1)
    for c in range(n_chunks):
        q = q_ref[c * chunk:(c + 1) * chunk, :]
        kk = kbuf[c * chunk:c * chunk + win, :]
        vv = vbuf[c * chunk:c * chunk + win, :]
        s = lax.dot_general(_bd_stack(q, chunk), kk, (((1,), (1,)), ((), ())), preferred_element_type=F32)
        s = s + bias_ref[...]
        visible = jnp.logical_or(has_past, col + c * chunk >= B_WINDOW)
        s = jnp.where(visible, s, NEG_INF)
        m = jnp.max(s, axis=-1, keepdims=True)
        p = jnp.exp(s - m)
        l = jnp.sum(p, axis=-1, keepdims=True)
        o = jnp.dot(p.astype(BF16), vv, preferred_element_type=F32) * (1.0 / l)
        o_ref[c * chunk:(c + 1) * chunk, :] = _bd_unstack(o, chunk).astype(BF16)


def _attention(proj, lw, b, t, cache_k=None, cache_v=None):
    n = proj.shape[0]
    step = cache_k is not None
    chunk = min(t, CHUNK)
    tq = min(t, B_WINDOW)
    nt = t // tq
    rel = lw["b_rel"]
    win = B_WINDOW + chunk
    lo = REL_CLIP - (chunk - 1)
    n_far = (chunk - 1) + win - (2 * REL_CLIP + 1 - lo)
    by_dist = jnp.concatenate([rel[:, lo:], jnp.broadcast_to(rel[:, -1:], (HEADS, n_far))], axis=1)
    by_key = by_dist[:, ::-1]
    n_k = chunk - 1 + win
    wrapped = jnp.tile(jnp.pad(by_key, ((0, 0), (0, 1))), (1, chunk))[:, :chunk * n_k].reshape(HEADS, chunk, n_k)
    bias = wrapped[:, :, chunk - 1:].astype(F32).reshape(HEADS * chunk, win)
    cur = lambda g: pl.BlockSpec((tq, GRP), lambda bi, j: (bi * nt + j, g))
    if step:
        prev_k = pl.BlockSpec((B_WINDOW, GRP), lambda bi, j: (bi, 0))
        prev_v = prev_k
        pk_arr, pv_arr = cache_k, cache_v
    else:
        assert tq == B_WINDOW
        prev_k = pl.BlockSpec((B_WINDOW, GRP), lambda bi, j: (bi * nt + jnp.maximum(j - 1, 0), PK))
        prev_v = pl.BlockSpec((B_WINDOW, GRP), lambda bi, j: (bi * nt + jnp.maximum(j - 1, 0), PV))
        pk_arr, pv_arr = proj, proj
    return pl.pallas_call(
        functools.partial(_attn_kernel, chunk=chunk, n_chunks=tq // chunk, first_has_past=step),
        grid=(b, nt),
        in_specs=[cur(PQ), cur(PK), cur(PV), prev_k, prev_v, pl.BlockSpec(bias.shape, lambda bi, j: (0, 0))],
        out_specs=pl.BlockSpec((tq, GRP), lambda bi, j: (bi * nt + j, 0)),
        out_shape=jax.ShapeDtypeStruct((n, GRP), BF16),
        scratch_shapes=[pltpu.VMEM((B_WINDOW + tq, GRP), BF16), pltpu.VMEM((B_WINDOW + tq, GRP), BF16)],
        compiler_params=_cparams(("arbitrary", "arbitrary")), name="band_attn",
    )(proj, proj, proj, pk_arr, pv_arr, bias)


def _conv_kernel(g_ref, halo_ref, dw_ref, dwb_ref, lng_ref, lnb_ref, o_ref, tail_ref, xp, zbuf, *, tc, sub, first_has_past):
    halo = halo_ref[...].astype(F32)
    has_past = jnp.logical_or(pl.program_id(1) > 0, first_has_past)
    xp[0:HALO, :] = jnp.where(has_past, halo, 0.0)
    xp[HALO:HALO + tc, :] = g_ref[...].astype(F32)
    xp[HALO + tc:, :] = jnp.zeros((xp.shape[0] - HALO - tc, GRP), F32)

    @pl.when(pl.program_id(1) == pl.num_programs(1) - 1)
    def _():
        tail_ref[...] = xp[tc:tc + HALO, :]

    lead = HALO - C_BUF
    sl = 8
    for s in range(tc // sub):
        acc = None
        for r in range(sl):
            taps = [p for p in range(r, lead + C_WIDTH, sl) if p >= lead]
            z = None
            for p in taps:
                a0 = s * sub + p - r
                term = dw_ref[p - lead:p - lead + 1, :] * xp[a0:a0 + sub + sl, :]
                z = term if z is None else z + term
            zbuf[r] = z
            part = zbuf[r, r:r + sub, :]
            acc = part if acc is None else acc + part
        y = acc + dwb_ref[...]
        mu = jnp.mean(y, axis=-1, keepdims=True)
        yc = y - mu
        y = yc * lax.rsqrt(jnp.mean(yc * yc, axis=-1, keepdims=True) + EPS) * lng_ref[...] + lnb_ref[...]
        o_ref[s * sub:(s + 1) * sub, :] = (y * _sigmoid(y)).astype(BF16)


def _conv(proj, lw, b, t, state=None):
    n = proj.shape[0]
    step = state is not None
    tc = min(t, 512)
    nt = t // tc
    sub = min(tc, 64)
    if step:
        halo_arr = state
        halo_spec = pl.BlockSpec((HALO, GRP), lambda bi, j: (bi, 0))
    else:
        per = tc // HALO
        halo_arr = proj
        halo_spec = pl.BlockSpec((HALO, GRP), lambda bi, j: (jnp.maximum((bi * nt + j) * per - 1, 0), PGLU))
    vec = pl.BlockSpec((1, GRP), lambda bi, j: (0, 0))
    return pl.pallas_call(
        functools.partial(_conv_kernel, tc=tc, sub=sub, first_has_past=step),
        grid=(b, nt),
        in_specs=[pl.BlockSpec((tc, GRP), lambda bi, j: (bi * nt + j, PGLU)), halo_spec,
                  pl.BlockSpec((C_WIDTH, GRP), lambda bi, j: (0, 0)), vec, vec, vec],
        out_specs=[pl.BlockSpec((tc, GRP), lambda bi, j: (bi * nt + j, 0)),
                   pl.BlockSpec((HALO, GRP), lambda bi, j: (bi, 0))],
        out_shape=[jax.ShapeDtypeStruct((n, GRP), BF16), jax.ShapeDtypeStruct((b * HALO, GRP), F32)],
        scratch_shapes=[pltpu.VMEM((HALO + tc + 8, GRP), F32), pltpu.VMEM((8, sub + 8, GRP), F32)],
        compiler_params=_cparams(("arbitrary", "arbitrary")), name="conv_module",
    )(proj, halo_arr, lw["c_dw"], lw["c_dw_b"], lw["c_ln_g"], lw["c_ln_b"])


def _gla_tables(L):
    i = np.arange(L)[:, None]
    t = np.arange(L)[None, :]
    masks = []
    s = GLA_SUB
    masks.append(((i // s) == (t // s)) & (t <= i))
    s *= 2
    while s <= L:
        h = s // 2
        masks.append(((i // s) == (t // s)) & (i % s >= h) & (t % s < h))
        s *= 2
    tri = (t <= i).astype(np.float32)
    mask = np.stack([np.tile(m.astype(np.float32), (1, HEADS)) for m in masks], axis=0)
    return tri, mask


def _gla_anchor(cum, row, L, size, first_half):
    out = None
    for start in range(0, L, size):
        ar = start + size // 2 - 1 if first_half else start - 1
        val = jnp.zeros((L, GRP), F32) if ar < 0 else jnp.broadcast_to(cum[ar:ar + 1, :], (L, GRP))
        out = val if out is None else jnp.where(row >= start, val, out)
    return out


def _gla_kernel(q_ref, k_ref, v_ref, la_ref, dr_ref, s0_ref, tri_ref, lmask_ref, bdmask_ref, hsum_ref, gon_ref,
                o_ref, sf_ref, st, o_all, *, L, n_chunks, n_levels, first_has_state):
    j = pl.program_id(1)

    @pl.when(j == 0)
    def _():
        if first_has_state:
            st[...] = s0_ref[...]
        else:
            st[...] = jnp.zeros_like(st)

    row = lax.broadcasted_iota(I32, (L, GRP), 0)
    dn_t = (((1,), (1,)), ((), ()))

    def prep(c):
        rows = slice(c * L, (c + 1) * L)
        q = q_ref[rows, :].astype(F32)
        k = k_ref[rows, :].astype(F32)
        v = v_ref[rows, :]
        cum = jnp.dot(tri_ref[...], la_ref[rows, :], preferred_element_type=F32)
        total = cum[L - 1:L, :]
        pairs = []
        for lvl in range(n_levels):
            size = GLA_SUB << lvl
            if lvl == 0:
                local = cum - _gla_anchor(cum, row, L, size, False)
                ql = q * jnp.exp(local)
                kl = k * jnp.exp(-local)
            else:
                upper = (row & (size - 1)) >= (size // 2)
                d = cum - _gla_anchor(cum, row, L, size, True)
                w = jnp.exp(jnp.where(upper, d, -d))
                ql = jnp.where(upper, q * w, 0.0)
                kl = jnp.where(upper, 0.0, k * w)
            pairs.append((ql.astype(BF16), _bd_stack(kl.astype(BF16), L)))
        return dict(rows=rows, v=v, qp=(q * jnp.exp(cum)).astype(BF16), kst=(k * jnp.exp(total - cum)).astype(BF16),
                    decay=jnp.exp(total), pairs=pairs)

    def intra(p):
        att = None
        for lvl, (ql, kbd) in enumerate(p["pairs"]):
            a = lax.dot_general(ql, kbd, dn_t, preferred_element_type=F32) * lmask_ref[lvl]
            att = a if att is None else att + a
        p["o_intra"] = jnp.dot(att.astype(BF16), _bd_stack(p["v"], L), preferred_element_type=F32)
        p["upd"] = lax.dot_general(p["v"], p["kst"], (((0,), (0,)), ((), ())),
                                   preferred_element_type=F32) * bdmask_ref[...]
        return p

    def finish(p, s_t):
        o_all[p["rows"], :] = lax.dot_general(p["qp"], s_t.astype(BF16), dn_t, preferred_element_type=F32) + p["o_intra"]
        return s_t * p["decay"] + p["upd"]

    s_t = st[...]
    stage1, stage2 = {}, {}
    for step in range(n_chunks + 2):
        if step < n_chunks:
            stage1[step] = prep(step)
        if 0 <= step - 1 < n_chunks:
            stage2[step - 1] = intra(stage1.pop(step - 1))
        if 0 <= step - 2 < n_chunks:
            s_t = finish(stage2.pop(step - 2), s_t)
    st[...] = s_t
    o = o_all[...]
    y = o * lax.rsqrt(_head_meansq(o, hsum_ref) + EPS) * gon_ref[...] * dr_ref[...].astype(F32)
    o_ref[...] = y.astype(BF16)

    @pl.when(j == pl.num_programs(1) - 1)
    def _():
        sf_ref[...] = st[...]


def _gla(proj, lw, b, t, s0=None):
    n = proj.shape[0]
    step = s0 is not None
    L = min(t, 64)
    td = min(t, 512)
    nt = t // td
    n_levels = int(np.log2(L // GLA_SUB)) + 1
    tri, lmask = _gla_tables(L)
    tri = jnp.asarray(tri, BF16)
    lmask = jnp.asarray(lmask, F32)
    if not step:
        s0 = jnp.zeros((GRP, GRP), F32)
        s0_spec = pl.BlockSpec((GRP, GRP), lambda bi, j: (0, 0))
    else:
        s0_spec = pl.BlockSpec((GRP, GRP), lambda bi, j: (bi, 0))
    cur = lambda g: pl.BlockSpec((td, GRP), lambda bi, j: (bi * nt + j, g))
    c2 = lambda bi, j: (0, 0)
    return pl.pallas_call(
        functools.partial(_gla_kernel, L=L, n_chunks=td // L, n_levels=n_levels, first_has_state=step),
        grid=(b, nt),
        in_specs=[cur(PDQ), cur(PDK), cur(PDV), cur(PLA), cur(PDR), s0_spec,
                  pl.BlockSpec(tri.shape, c2), pl.BlockSpec(lmask.shape, lambda bi, j: (0, 0, 0)),
                  pl.BlockSpec((GRP, GRP), c2), pl.BlockSpec((GRP, GRP), c2), pl.BlockSpec((1, GRP), c2)],
        out_specs=[pl.BlockSpec((td, GRP), lambda bi, j: (bi * nt + j, 0)),
                   pl.BlockSpec((GRP, GRP), lambda bi, j: (bi, 0))],
        out_shape=[jax.ShapeDtypeStruct((n, GRP), BF16), jax.ShapeDtypeStruct((b * GRP, GRP), F32)],
        scratch_shapes=[pltpu.VMEM((GRP, GRP), F32), pltpu.VMEM((td, GRP), F32)],
        compiler_params=_cparams(("arbitrary", "arbitrary")), name="gla",
    )(proj, proj, proj, proj, proj, s0, tri, lmask, lw["bdmask"], lw["hsum"], lw["gon"])


def _out_kernel(ya_ref, yb_ref, yc_ref, yd_ref, x_ref, wo_ref, g2_ref, wr_hi_ref, wr_lo_ref, br_ref, tri_ref,
                xo_ref, xn_ref, ri_ref, rf_ref, cnt_ref, *, tm, n_sub):
    @pl.when(pl.program_id(0) == 0)
    def _():
        cnt_ref[...] = jnp.zeros_like(cnt_ref)

    sub = tm // n_sub
    lane = lax.broadcasted_iota(I32, (sub, LANES), 1)
    big = np.int32(1 << 20)

    def project(s):
        rows = slice(s * sub, (s + 1) * sub)
        y = jnp.dot(ya_ref[rows, :], wo_ref[0 * GRP:1 * GRP, :], preferred_element_type=F32)
        y = y + jnp.dot(yb_ref[rows, :], wo_ref[1 * GRP:2 * GRP, :], preferred_element_type=F32)
        y = y + jnp.dot(yc_ref[rows, :], wo_ref[2 * GRP:3 * GRP, :], preferred_element_type=F32)
        y = y + jnp.dot(yd_ref[rows, :], wo_ref[3 * GRP:4 * GRP, :], preferred_element_type=F32)
        x = x_ref[rows, :] + y
        xo_ref[rows, :] = x
        xn = x * lax.rsqrt(jnp.mean(x * x, axis=-1, keepdims=True) + EPS) * g2_ref[...]
        xn_ref[rows, :] = _pack_halves(xn)
        xh = xn.astype(BF16)
        xl = (xn - xh.astype(F32)).astype(BF16)
        return (jnp.dot(xh, wr_hi_ref[...], preferred_element_type=F32)
                + jnp.dot(xl, wr_hi_ref[...], preferred_element_type=F32)
                + jnp.dot(xh, wr_lo_ref[...], preferred_element_type=F32)) + br_ref[...]

    def route(s, logits):
        rows = slice(s * sub, (s + 1) * sub)

        def first_max(mask):
            v = jnp.max(jnp.where(mask, logits, -jnp.inf), axis=-1, keepdims=True)
            idx = jnp.min(jnp.where(jnp.logical_and(mask, logits == v), lane, big), axis=-1, keepdims=True)
            return v, idx

        is_grp = jnp.logical_and(lane >= N_EXPERTS, lane < N_EXPERTS + N_GROUPS)
        gmax, gidx = first_max(is_grp)
        p_grp = 1.0 / jnp.sum(jnp.where(is_grp, jnp.exp(logits - gmax), 0.0), axis=-1, keepdims=True)
        grp = gidx - N_EXPERTS
        in_grp = (lane // PER_GROUP) == grp
        v1, i1 = first_max(in_grp)
        v2, i2 = first_max(jnp.logical_and(in_grp, lane != i1))
        e21 = jnp.exp(v2 - v1)
        gate1 = p_grp / (1.0 + e21)
        gate2 = p_grp * e21 / (1.0 + e21)

        oh1 = lane == i1
        oh2 = lane == i2
        both = jnp.logical_or(oh1, oh2)
        ones = jnp.where(both, 1.0, 0.0).astype(BF16)
        before = jnp.dot(tri_ref[...], ones, preferred_element_type=F32) + cnt_ref[...].astype(F32)
        rank1 = jnp.sum(jnp.where(oh1, before, 0.0), axis=-1, keepdims=True)
        rank2 = jnp.sum(jnp.where(oh2, before, 0.0), axis=-1, keepdims=True)
        cnt_ref[...] = cnt_ref[...] + jnp.sum(jnp.where(both, 1.0, 0.0), axis=0, keepdims=True).astype(I32)

        ri = jnp.where(lane == 0, i1, jnp.where(lane == 1, i2, jnp.where(lane == 2, rank1.astype(I32),
                                                                          jnp.where(lane == 3, rank2.astype(I32), 0))))
        ri_ref[rows, :] = ri
        rf_ref[rows, :] = jnp.where(lane == 0, gate1, jnp.where(lane == 1, gate2, 0.0))

    logits = project(0)
    for s in range(n_sub):
        nxt = project(s + 1) if s + 1 < n_sub else None
        route(s, logits)
        logits = nxt


def _out_proj(ya, yb, yc, yd, x, lw):
    n = x.shape[0]
    tm = min(1024, n)
    row = lambda i: (i, 0)
    const = lambda i: (0, 0)
    n_sub = 2 if tm >= 1024 else 1
    sub = tm // n_sub
    tri = jnp.asarray(np.tril(np.ones((sub, sub), np.float32), -1), BF16)
    consts = [lw["w_out"], lw["g2"], lw["wr_hi"], lw["wr_lo"], lw["br"], tri]
    yspec = pl.BlockSpec((tm, GRP), row)
    return pl.pallas_call(
        functools.partial(_out_kernel, tm=tm, n_sub=n_sub),
        grid=(n // tm,),
        in_specs=[yspec, yspec, yspec, yspec, pl.BlockSpec((tm, D_MODEL), row)] + [pl.BlockSpec(c.shape, const) for c in consts],
        out_specs=[pl.BlockSpec((tm, D_MODEL), row), pl.BlockSpec((tm, D_MODEL // 2), row),
                   pl.BlockSpec((tm, LANES), row), pl.BlockSpec((tm, LANES), row), pl.BlockSpec((1, LANES), const)],
        out_shape=[jax.ShapeDtypeStruct((n, D_MODEL), F32), jax.ShapeDtypeStruct((n, D_MODEL // 2), U32),
                   jax.ShapeDtypeStruct((n, LANES), I32), jax.ShapeDtypeStruct((n, LANES), F32),
                   jax.ShapeDtypeStruct((1, LANES), I32)],
        compiler_params=_cparams(("arbitrary",)), name="out_proj_router",
    )(ya, yb, yc, yd, x, *consts)


def _sc_scatter_rows(x, idx, n_out):
    n, d = x.shape
    kk = idx.shape[0]
    per_w = n // SC_WORKERS
    win = min(SC_WIN, per_w)
    n_win = per_w // win
    assert n_win * win * SC_WORKERS == n
    mesh = plsc.VectorSubcoreMesh(core_axis_name="c", subcore_axis_name="s")

    @functools.partial(
        pl.kernel, mesh=mesh, out_type=jax.ShapeDtypeStruct((n_out, d), x.dtype),
        scratch_types=[pltpu.VMEM((kk, win), I32), pltpu.VMEM((win, d), x.dtype)],
        name="sc_scatter_rows")
    def k(x_hbm, idx_hbm, o_hbm, idx_v, rows_v):
        wid = lax.axis_index("s") * 2 + lax.axis_index("c")
        base = wid * per_w

        @pl.loop(0, n_win)
        def _(w):
            off = base + w * win
            pltpu.sync_copy(x_hbm.at[pl.ds(off, win)], rows_v)
            for j in range(kk):
                pltpu.sync_copy(idx_hbm.at[j, pl.ds(off, win)], idx_v.at[j])
                pltpu.sync_copy(rows_v, o_hbm.at[idx_v.at[j]])

    return k(x, idx)


def _sc_gather_rows(y, idx):
    _, d = y.shape
    kk, n = idx.shape
    per_w = n // SC_WORKERS
    win = min(SC_WIN, per_w)
    n_win = per_w // win
    assert n_win * win * SC_WORKERS == n
    mesh = plsc.VectorSubcoreMesh(core_axis_name="c", subcore_axis_name="s")

    @functools.partial(
        pl.kernel, mesh=mesh, out_type=jax.ShapeDtypeStruct((kk, n, d), y.dtype),
        scratch_types=[pltpu.VMEM((kk, win), I32), pltpu.VMEM((win, d), y.dtype)],
        name="sc_gather_rows")
    def k(y_hbm, idx_hbm, o_hbm, idx_v, rows_v):
        wid = lax.axis_index("s") * 2 + lax.axis_index("c")
        base = wid * per_w

        @pl.loop(0, n_win)
        def _(w):
            off = base + w * win
            for j in range(kk):
                pltpu.sync_copy(idx_hbm.at[j, pl.ds(off, win)], idx_v.at[j])
                pltpu.sync_copy(y_hbm.at[idx_v.at[j]], rows_v)
                pltpu.sync_copy(rows_v, o_hbm.at[j, pl.ds(off, win)])

    return k(y, idx)


def _moe_kernel(bexp_ref, nused_ref, x_ref, wg_ref, wu_ref, wd_ref, o_ref, wg_s, wu_s, wd_s):
    i = pl.program_id(0)
    prev = bexp_ref[jnp.maximum(i - 1, 0)]
    fresh = jnp.logical_or(i == 0, bexp_ref[i] != prev)

    @pl.when(jnp.logical_and(fresh, i < nused_ref[0]))
    def _():
        wg_s[...] = wg_ref[...].astype(BF16)
        wu_s[...] = wu_ref[...].astype(BF16)
        wd_s[...] = wd_ref[...].astype(BF16)

    @pl.when(i < nused_ref[0])
    def _():
        half = D_MODEL // 2
        w = x_ref[...]
        xa = _unpack_hi(w).astype(BF16)
        xb = _unpack_lo(w).astype(BF16)
        hg = (jnp.dot(xa, wg_s[:half, :], preferred_element_type=F32)
              + jnp.dot(xb, wg_s[half:, :], preferred_element_type=F32))
        hu = (jnp.dot(xa, wu_s[:half, :], preferred_element_type=F32)
              + jnp.dot(xb, wu_s[half:, :], preferred_element_type=F32))
        h = (hg * _sigmoid(hg) * hu).astype(BF16)
        o_ref[...] = _pack_halves(jnp.dot(h, wd_s[...], preferred_element_type=F32))


def _moe_experts(xs, blk_exp, n_used, w_gate, w_up, w_down, layer, bm):
    p = xs.shape[0]
    n_blocks = p // bm
    live = lambda i, be, nu: jnp.minimum(i, jnp.maximum(nu[0] - 1, 0))
    wspec = lambda shape: pl.BlockSpec((None, None) + shape, lambda i, be, nu: (layer, be[live(i, be, nu)], 0, 0))
    grid_spec = pltpu.PrefetchScalarGridSpec(
        num_scalar_prefetch=2, grid=(n_blocks,),
        in_specs=[pl.BlockSpec((bm, D_MODEL // 2), lambda i, be, nu: (live(i, be, nu), 0)),
                  wspec((D_MODEL, D_EXPERT)), wspec((D_MODEL, D_EXPERT)), wspec((D_EXPERT, D_MODEL))],
        out_specs=pl.BlockSpec((bm, D_MODEL // 2), lambda i, be, nu: (live(i, be, nu), 0)),
        scratch_shapes=[pltpu.VMEM((D_MODEL, D_EXPERT), BF16), pltpu.VMEM((D_MODEL, D_EXPERT), BF16),
                        pltpu.VMEM((D_EXPERT, D_MODEL), BF16)])
    return pl.pallas_call(
        _moe_kernel, grid_spec=grid_spec, out_shape=jax.ShapeDtypeStruct((p, D_MODEL // 2), U32),
        compiler_params=_cparams(("arbitrary",)), name="moe_experts",
    )(blk_exp, n_used, xs, w_gate, w_up, w_down)


def _moe_block_rows(n):
    return 512 if n >= 16384 else 128


def _moe(xn_packed, route_i, counts, lw):
    n = xn_packed.shape[0]
    bm = _moe_block_rows(n)
    n_blocks = -(-(2 * n + N_EXPERTS * (bm - 1)) // bm)
    cnt = counts[0, :N_EXPERTS]
    padded = (cnt + bm - 1) // bm * bm
    pad_end = jnp.cumsum(padded)
    pad_start = pad_end - padded
    experts = jnp.arange(N_EXPERTS, dtype=I32)
    eid = route_i[:, 0:2].T
    start_of = jnp.sum(jnp.where(eid[:, :, None] == experts, pad_start, 0), axis=-1)
    dest = (start_of + route_i[:, 2:4].T).astype(I32)
    first_row = jnp.arange(n_blocks, dtype=I32) * bm
    blk_exp = jnp.minimum(jnp.sum((pad_end[None, :] <= first_row[:, None]).astype(I32), axis=1), N_EXPERTS - 1)
    n_used = (pad_end[-1:] // bm).astype(I32)
    xs = _sc_scatter_rows(xn_packed, dest, n_blocks * bm)
    ys = _moe_experts(xs, blk_exp, n_used, lw["e_w_gate"], lw["e_w_up"], lw["e_w_down"], lw["layer"], bm)
    return _sc_gather_rows(ys, dest)


def _combine_kernel(x_ref, y_ref, gate_ref, o_ref):
    half = D_MODEL // 2
    x = x_ref[...]
    g = gate_ref[...]
    g0 = g[:, 0:1]
    g1 = g[:, 1:2]
    w0 = y_ref[0]
    w1 = y_ref[1]
    o_ref[:, :half] = x[:, :half] + g0 * _unpack_hi(w0) + g1 * _unpack_hi(w1)
    o_ref[:, half:] = x[:, half:] + g0 * _unpack_lo(w0) + g1 * _unpack_lo(w1)


def _combine(x, y, gates):
    n = x.shape[0]
    tm = min(512, n)
    row = lambda i: (i, 0)
    return pl.pallas_call(
        _combine_kernel, grid=(n // tm,),
        in_specs=[pl.BlockSpec((tm, D_MODEL), row), pl.BlockSpec((2, tm, D_MODEL // 2), lambda i: (0, i, 0)),
                  pl.BlockSpec((tm, LANES), row)],
        out_specs=pl.BlockSpec((tm, D_MODEL), row), out_shape=jax.ShapeDtypeStruct((n, D_MODEL), F32),
        compiler_params=_cparams(("arbitrary",)), name="moe_combine",
    )(x, y, gates)


def _layer_weights(l, p):
    w_in = p["w_in"][l]
    cols = [w_in[:, i * GRP:(i + 1) * GRP] for i in range(11)]
    by_group = [None] * N_PROJ
    for ref_i, g in enumerate(_REF_GROUPS):
        if g is not None:
            by_group[g] = cols[ref_i]
    by_group[PGLU] = cols[5]
    w_dg = jnp.zeros((D_MODEL, LANES), F32).at[:, :GATE_RANK].set(w_in[:, 11 * GRP:])
    wg2 = jnp.zeros((LANES, GRP), F32).at[:GATE_RANK].set(p["d_wg2"][l])
    tile4 = lambda v: jnp.tile(v, HEADS)[None, :]
    hid = np.arange(GRP) // HEAD_DIM
    bd = (hid[:, None] == hid[None, :]).astype(np.float32)
    wr = jnp.zeros((D_MODEL, LANES), F32).at[:, :N_EXPERTS].set(p["r_expert_w"][l])
    wr = wr.at[:, N_EXPERTS:N_EXPERTS + N_GROUPS].set(p["r_group_w"][l])
    wr_hi = wr.astype(BF16)
    br = jnp.zeros((1, LANES), F32).at[0, :N_EXPERTS].set(p["r_expert_b"][l])
    br = br.at[0, N_EXPERTS:N_EXPERTS + N_GROUPS].set(p["r_group_b"][l])
    return {
        "g1": p["norm1_g"][l][None, :],
        "w_in": jnp.concatenate(by_group[:PLA], axis=1).astype(BF16),
        "w_cg": cols[6].astype(BF16),
        "w_dg": w_dg.astype(BF16),
        "wg2": wg2.astype(BF16),
        "bg": p["d_bg"][l][None, :],
        "gq": tile4(p["b_qnorm_g"][l]), "gk": tile4(p["b_knorm_g"][l]), "gav": p["a_vnorm_g"][l][None, :],
        "gon": tile4(p["d_onorm_g"][l]),
        "hsum": jnp.asarray(bd, BF16), "bdmask": jnp.asarray(bd, F32),
        "a_ws": p["a_ws"][l], "a_bs_rows": jnp.repeat(p["a_bs"][l].T, HEAD_DIM, axis=1),
        "b_rel": p["b_rel_bias"][l],
        "c_dw": p["c_dw"][l], "c_dw_b": p["c_dw_b"][l][None, :],
        "c_ln_g": p["c_ln_g"][l][None, :], "c_ln_b": p["c_ln_b"][l][None, :],
        "w_out": p["w_out"][l].astype(BF16),
        "g2": p["norm2_g"][l][None, :],
        "wr_hi": wr_hi, "wr_lo": (wr - wr_hi.astype(F32)).astype(BF16), "br": br,
        "e_w_gate": p["e_w_gate"], "e_w_up": p["e_w_up"], "e_w_down": p["e_w_down"], "layer": l,
    }


def _states_to_bd(s):
    b = s.shape[0]
    st = jnp.swapaxes(s, 2, 3)
    eye = jnp.eye(HEADS, dtype=s.dtype)
    bd = st[:, :, :, None, :] * eye[None, :, None, :, None]
    return bd.reshape(b * GRP, GRP)


def _states_from_bd(sf, b):
    s = sf.reshape(b, HEADS, HEAD_DIM, HEADS, HEAD_DIM)
    diag = jnp.stack([s[:, h, :, h, :] for h in range(HEADS)], axis=1)
    return jnp.swapaxes(diag, 2, 3)


def _mix_and_route(x, lw, b, t, pending, caches):
    step = caches is not None
    y_prev, gates_prev = pending if pending is not None else (None, None)
    x, proj, kv, a_v = _in_proj(x, lw, t, y_prev, gates_prev, emit_av=step)
    ya = _gmlp(proj, lw, t)
    if not step:
        yb = _attention(proj, lw, b, t)
        yc, tail = _conv(proj, lw, b, t)
        yd, sf = _gla(proj, lw, b, t)
    else:
        ck, cv, cc, cs = caches
        yb = _attention(proj, lw, b, t, ck.reshape(b * B_WINDOW, GRP), cv.reshape(b * B_WINDOW, GRP))
        halo = jnp.pad(cc, ((0, 0), (HALO - C_BUF, 0), (0, 0))).reshape(b * HALO, GRP)
        yc, tail = _conv(proj, lw, b, t, halo)
        yd, sf = _gla(proj, lw, b, t, _states_to_bd(cs))
        a_v = a_v.reshape(b, t, GRP)
    x2, xn_packed, route_i, route_f, counts = _out_proj(ya, yb, yc, yd, x, lw)
    y = _moe(xn_packed, route_i, counts, lw)
    keep = min(B_WINDOW, t)
    new_k = kv[:, :GRP].reshape(b, keep, HEADS, HEAD_DIM)
    new_v = kv[:, GRP:].reshape(b, keep, HEADS, HEAD_DIM)
    new_buf = tail.reshape(b, HALO, GRP)[:, HALO - C_BUF:]
    return x2, (y, route_f), (new_k, new_v, new_buf, _states_from_bd(sf, b), a_v)


def kernel(x_prompt, x_sample, cache_b_k, cache_b_v, state_c_conv, state_d_gla, norm1_g, w_in, a_vnorm_g, a_ws, a_bs, b_qnorm_g, b_knorm_g, b_rel_bias, c_dw, c_dw_b, c_ln_g, c_ln_b, d_wg2, d_bg, d_onorm_g, w_out, norm2_g, r_group_w, r_group_b, r_expert_w, r_expert_b, e_w_gate, e_w_up, e_w_down):
    params = dict(norm1_g=norm1_g, w_in=w_in, a_vnorm_g=a_vnorm_g, a_ws=a_ws, a_bs=a_bs, b_qnorm_g=b_qnorm_g,
                  b_knorm_g=b_knorm_g, b_rel_bias=b_rel_bias, c_dw=c_dw, c_dw_b=c_dw_b, c_ln_g=c_ln_g, c_ln_b=c_ln_b,
                  d_wg2=d_wg2, d_bg=d_bg, d_onorm_g=d_onorm_g, w_out=w_out, norm2_g=norm2_g, r_group_w=r_group_w,
                  r_group_b=r_group_b, r_expert_w=r_expert_w, r_expert_b=r_expert_b, e_w_gate=e_w_gate,
                  e_w_up=e_w_up, e_w_down=e_w_down)
    depth = w_in.shape[0]
    bp, tp, _ = x_prompt.shape
    bs, ts, _ = x_sample.shape
    xp = x_prompt.reshape(bp * tp, D_MODEL)
    xs = x_sample.reshape(bs * ts, D_MODEL)
    pend_p = pend_s = None
    st_p, st_s = [], []
    for l in range(depth):
        lw = _layer_weights(l, params)
        xp, pend_p, sp = _mix_and_route(xp, lw, bp, tp, pend_p, None)
        xs, pend_s, ss = _mix_and_route(xs, lw, bs, ts, pend_s,
                                        (cache_b_k[l], cache_b_v[l], state_c_conv[l], state_d_gla[l]))
        st_p.append(sp)
        st_s.append(ss)
    yp = _combine(xp, pend_p[0], pend_p[1]).reshape(bp, tp, D_MODEL)
    ys = _combine(xs, pend_s[0], pend_s[1]).reshape(bs, ts, D_MODEL)
    stack = lambda sts, i: jnp.stack([s[i] for s in sts])
    return (yp, ys, stack(st_p, 0), stack(st_p, 1), stack(st_p, 2), stack(st_p, 3),
            stack(st_s, 0), stack(st_s, 1), stack(st_s, 2), stack(st_s, 3), stack(st_s, 4))
```

```python
import functools

import numpy as np
import jax
import jax.numpy as jnp
from jax import lax
from jax.experimental import pallas as pl
from jax.experimental.pallas import tpu as pltpu
from jax.experimental.pallas import tpu_sc as plsc

F32 = jnp.float32
BF16 = jnp.bfloat16
I32 = jnp.int32
U32 = jnp.uint32

D_MODEL = 1024
GRP = 256
HEADS = 4
HEAD_DIM = 64
CHUNK = 64
A_CHUNK = 128
B_WINDOW = 512
REL_CLIP = 128
C_WIDTH = 31
C_BUF = C_WIDTH - 1
HALO = 32
GATE_RANK = 16
GLA_TAU = 16.0
GLA_SUB = 16
N_GROUPS = 4
PER_GROUP = 8
N_EXPERTS = 32
D_EXPERT = 512
EPS = 1e-6
NEG_INF = -1e30
LANES = 128
VMEM_LIMIT = 48 * 1024 * 1024

PK, PV, PQ, PAU, PAV, PGLU, PDQ, PDK, PDV, PDR, PLA = range(11)
N_PROJ = 11
_REF_GROUPS = (PAU, PAV, PQ, PK, PV, None, None, PDQ, PDK, PDV, PDR)

SC_WORKERS = 32
SC_WIN = 128


def _cparams(sem):
    return pltpu.CompilerParams(dimension_semantics=sem, vmem_limit_bytes=VMEM_LIMIT)


def _sigmoid(x):
    return 1.0 / (1.0 + jnp.exp(-x))


def _gelu_tanh(x):
    c = np.float32(np.sqrt(2.0 / np.pi))
    return 0.5 * x * (1.0 + jnp.tanh(c * (x + np.float32(0.044715) * (x * x * x))))


def _pack_halves(y):
    half = y.shape[1] // 2
    hi = pltpu.bitcast(y[:, :half].astype(BF16).astype(F32), U32)
    lo = pltpu.bitcast(y[:, half:].astype(BF16).astype(F32), U32)
    return hi | (lo >> np.uint32(16))


def _unpack_hi(w):
    return pltpu.bitcast(w & np.uint32(0xFFFF0000), F32)


def _unpack_lo(w):
    return pltpu.bitcast(w << np.uint32(16), F32)


def _head_id(shape, axis, size):
    return lax.broadcasted_iota(I32, shape, axis) // size


def _bd_stack(x, rows):
    x4 = jnp.concatenate([x] * HEADS, axis=0)
    shape = (HEADS * rows, GRP)
    keep = _head_id(shape, 0, rows) == _head_id(shape, 1, HEAD_DIM)
    return jnp.where(keep, x4, jnp.zeros_like(x4))


def _bd_unstack(o, rows):
    lane_h = _head_id((rows, GRP), 1, HEAD_DIM)
    out = o[(HEADS - 1) * rows:HEADS * rows]
    for h in range(HEADS - 2, -1, -1):
        out = jnp.where(lane_h == h, o[h * rows:(h + 1) * rows], out)
    return out


def _head_meansq(o, hsum_ref):
    sq = (o * o).astype(BF16)
    return jnp.dot(sq, hsum_ref[...], preferred_element_type=F32) * np.float32(1.0 / HEAD_DIM)


def _in_kernel(*refs, combine, emit_av, tiles_per_stream):
    refs = list(refs)
    x_ref = refs.pop(0)
    if combine:
        y_ref = refs.pop(0)
        gate_ref = refs.pop(0)
    g1_ref, w_ref, wcg_ref, wdg_ref, wg2_ref, bg_ref, gq_ref, gk_ref, gav_ref, hsum_ref = refs[:10]
    refs = refs[10:]
    if combine:
        xo_ref = refs.pop(0)
    p_ref = refs.pop(0)
    kc_ref = refs.pop(0)
    vc_ref = refs.pop(0)
    if emit_av:
        av_ref = refs.pop(0)
    raw = refs.pop(0)
    x = x_ref[...]
    if combine:
        half = D_MODEL // 2
        g = gate_ref[...]
        g0 = g[:, 0:1]
        g1 = g[:, 1:2]
        w0 = y_ref[0]
        w1 = y_ref[1]
        xa = x[:, :half] + g0 * _unpack_hi(w0) + g1 * _unpack_hi(w1)
        xb = x[:, half:] + g0 * _unpack_lo(w0) + g1 * _unpack_lo(w1)
        xo_ref[:, :half] = xa
        xo_ref[:, half:] = xb
        x = jnp.concatenate([xa, xb], axis=1)
    rs = lax.rsqrt(jnp.mean(x * x, axis=-1, keepdims=True) + EPS)
    h = (x * g1_ref[...]).astype(BF16)

    n_slots = PLA + 2

    def matmul(slot):
        if slot < PLA:
            raw[:, slot * GRP:(slot + 1) * GRP] = jnp.dot(h, w_ref[:, slot * GRP:(slot + 1) * GRP],
                                                          preferred_element_type=F32)
        elif slot == PLA:
            raw[:, PLA * GRP:(PLA + 1) * GRP] = jnp.dot(h, wcg_ref[...], preferred_element_type=F32)
        else:
            raw[:, (PLA + 1) * GRP:] = jnp.dot(h, wdg_ref[...], preferred_element_type=F32)

    def proj(g):
        return raw[:, g * GRP:(g + 1) * GRP] * rs

    def put(g, val):
        p_ref[:, g * GRP:(g + 1) * GRP] = val.astype(BF16)

    def epilogue(slot):
        if slot == PK:
            r = proj(PK)
            put(PK, r * lax.rsqrt(_head_meansq(r, hsum_ref) + EPS) * gk_ref[...])
        elif slot == PV:
            put(PV, proj(PV))

            @pl.when(pl.program_id(0) % tiles_per_stream == tiles_per_stream - 1)
            def _():
                r = proj(PK)
                kc_ref[...] = r * lax.rsqrt(_head_meansq(r, hsum_ref) + EPS) * gk_ref[...]
                vc_ref[...] = proj(PV)
        elif slot == PQ:
            r = proj(PQ)
            put(PQ, r * lax.rsqrt(_head_meansq(r, hsum_ref) + EPS) * (gq_ref[...] * np.float32(HEAD_DIM ** -0.5)))
        elif slot == PAU:
            put(PAU, _gelu_tanh(proj(PAU)))
        elif slot == PAV:
            r = _gelu_tanh(proj(PAV))
            av = r * lax.rsqrt(jnp.mean(r * r, axis=-1, keepdims=True) + EPS) * gav_ref[...]
            put(PAV, av)
            if emit_av:
                av_ref[...] = av
        elif slot == PGLU:
            pass
        elif slot == PDQ:
            put(PDQ, proj(PDQ) * np.float32(HEAD_DIM ** -0.5))
        elif slot in (PDK, PDV):
            put(slot, proj(slot))
        elif slot == PDR:
            r = proj(PDR)
            put(PDR, r * _sigmoid(r))
        elif slot == PLA:
            put(PGLU, proj(PGLU) * _sigmoid(proj(PLA)))
        else:
            dg = raw[:, (PLA + 1) * GRP:] * rs
            z = jnp.dot(dg.astype(BF16), wg2_ref[...], preferred_element_type=F32) + bg_ref[...]
            logsig = jnp.minimum(z, 0.0) - jnp.log(1.0 + jnp.exp(-jnp.abs(z)))
            put(PLA, logsig * np.float32(1.0 / GLA_TAU))

    order = (PLA + 1, PK, PQ, PAV, PAU, PGLU, PLA, PDR, PV, PDQ, PDK, PDV)
    assert sorted(order) == list(range(n_slots))
    lag = 2
    for i in range(n_slots + lag):
        if i < n_slots:
            matmul(order[i])
        if i >= lag:
            epilogue(order[i - lag])


def _in_proj(x, lw, t, y=None, gates=None, emit_av=False):
    n = x.shape[0]
    tm = min(512, n)
    combine = y is not None
    keep = min(B_WINDOW, t)
    tps = max(t // tm, 1)
    assert tps == 1 or keep == tm
    row = lambda i: (i, 0)
    const = lambda i: (0, 0)
    ins, specs = [x], [pl.BlockSpec((tm, D_MODEL), row)]
    if combine:
        ins += [y, gates]
        specs += [pl.BlockSpec((2, tm, D_MODEL // 2), lambda i: (0, i, 0)), pl.BlockSpec((tm, LANES), row)]
    consts = [lw["g1"], lw["w_in"], lw["w_cg"], lw["w_dg"], lw["wg2"], lw["bg"], lw["gq"], lw["gk"], lw["gav"], lw["hsum"]]
    ins += consts
    specs += [pl.BlockSpec(c.shape, const) for c in consts]
    newest = jax.ShapeDtypeStruct((n // tps, GRP), F32)
    newest_spec = pl.BlockSpec((tm, GRP), lambda i: (i // tps, 0))
    out_shape = [jax.ShapeDtypeStruct((n, N_PROJ * GRP), BF16), newest, newest]
    out_specs = [pl.BlockSpec((tm, N_PROJ * GRP), row), newest_spec, newest_spec]
    if combine:
        out_shape = [jax.ShapeDtypeStruct((n, D_MODEL), F32)] + out_shape
        out_specs = [pl.BlockSpec((tm, D_MODEL), row)] + out_specs
    if emit_av:
        out_shape.append(jax.ShapeDtypeStruct((n, GRP), F32))
        out_specs.append(pl.BlockSpec((tm, GRP), row))
    outs = list(pl.pallas_call(
        functools.partial(_in_kernel, combine=combine, emit_av=emit_av, tiles_per_stream=tps),
        grid=(n // tm,), in_specs=specs, out_specs=out_specs, out_shape=out_shape,
        scratch_shapes=[pltpu.VMEM((tm, (PLA + 1) * GRP + LANES), F32)],
        compiler_params=_cparams(("arbitrary",)), name="in_proj",
    )(*ins))
    x_new = outs.pop(0) if combine else x
    proj, k_new, v_new = outs[0], outs[1], outs[2]
    return x_new, proj, (k_new, v_new), (outs[3] if emit_av else None)


def _gmlp_kernel(u_ref, v_ref, ws_ref, bs_ref, o_ref, *, chunk, n_chunks):
    lane_h = _head_id((chunk, GRP), 1, HEAD_DIM)
    ri = lax.broadcasted_iota(I32, (chunk, chunk), 0)
    ci = lax.broadcasted_iota(I32, (chunk, chunk), 1)
    wm = [jnp.where(ci <= ri, ws_ref[h], 0.0).astype(BF16) for h in range(HEADS)]
    for c in range(n_chunks):
        rows = slice(c * chunk, (c + 1) * chunk)
        v = v_ref[rows, :]
        sv = jnp.dot(wm[HEADS - 1], v, preferred_element_type=F32)
        for h in range(HEADS - 2, -1, -1):
            sv = jnp.where(lane_h == h, jnp.dot(wm[h], v, preferred_element_type=F32), sv)
        o_ref[rows, :] = (u_ref[rows, :].astype(F32) * (sv + bs_ref[...])).astype(BF16)


def _gmlp(proj, lw, t):
    n = proj.shape[0]
    chunk = min(t, A_CHUNK)
    ta = min(512, n)
    ws = lw["a_ws"][:, :chunk, :chunk]
    bs = lw["a_bs_rows"][:chunk]
    return pl.pallas_call(
        functools.partial(_gmlp_kernel, chunk=chunk, n_chunks=ta // chunk),
        grid=(n // ta,),
        in_specs=[pl.BlockSpec((ta, GRP), lambda i: (i, PAU)), pl.BlockSpec((ta, GRP), lambda i: (i, PAV)),
                  pl.BlockSpec(ws.shape, lambda i: (0, 0, 0)), pl.BlockSpec(bs.shape, lambda i: (0, 0))],
        out_specs=pl.BlockSpec((ta, GRP), lambda i: (i, 0)),
        out_shape=jax.ShapeDtypeStruct((n, GRP), BF16),
        compiler_params=_cparams(("arbitrary",)), name="gmlp",
    )(proj, proj, ws, bs)


def _attn_kernel(q_ref, kc_ref, vc_ref, kp_ref, vp_ref, bias_ref, o_ref, kbuf, vbuf, *, chunk, n_chunks, first_has_past):
    tq = chunk * n_chunks
    win = B_WINDOW + chunk
    kbuf[0:B_WINDOW, :] = kp_ref[...].astype(BF16)
    vbuf[0:B_WINDOW, :] = vp_ref[...].astype(BF16)
    kbuf[B_WINDOW:B_WINDOW + tq, :] = kc_ref[...]
    vbuf[B_WINDOW:B_WINDOW + tq, :] = vc_ref[...]
    has_past = jnp.logical_or(pl.program_id(1) > 0, first_has_past)
    col = lax.broadcasted_iota(I32, (HEADS * chunk, win), 1)
    for c in range(n_chunks):
        q = q_ref[c * chunk:(c + 1) * chunk, :]
        kk = kbuf[c * chunk:c * chunk + win, :]
        vv = vbuf[c * chunk:c * chunk + win, :]
        s = lax.dot_general(_bd_stack(q, chunk), kk, (((1,), (1,)), ((), ())), preferred_element_type=F32)
        s = s + bias_ref[...]
        visible = jnp.logical_or(has_past, col + c * chunk >= B_WINDOW)
        s = jnp.where(visible, s, NEG_INF)
        m = jnp.max(s, axis=-1, keepdims=True)
        p = jnp.exp(s - m)
        l = jnp.sum(p, axis=-1, keepdims=True)
        o = jnp.dot(p.astype(BF16), vv, preferred_element_type=F32) * (1.0 / l)
        o_ref[c * chunk:(c + 1) * chunk, :] = _bd_unstack(o, chunk).astype(BF16)


def _attention(proj, lw, b, t, cache_k=None, cache_v=None):
    n = proj.shape[0]
    step = cache_k is not None
    chunk = min(t, CHUNK)
    tq = min(t, B_WINDOW)
    nt = t // tq
    rel = lw["b_rel"]
    win = B_WINDOW + chunk
    lo = REL_CLIP - (chunk - 1)
    n_far = (chunk - 1) + win - (2 * REL_CLIP + 1 - lo)
    by_dist = jnp.concatenate([rel[:, lo:], jnp.broadcast_to(rel[:, -1:], (HEADS, n_far))], axis=1)
    by_key = by_dist[:, ::-1]
    n_k = chunk - 1 + win
    wrapped = jnp.tile(jnp.pad(by_key, ((0, 0), (0, 1))), (1, chunk))[:, :chunk * n_k].reshape(HEADS, chunk, n_k)
    bias = wrapped[:, :, chunk - 1:].astype(F32).reshape(HEADS * chunk, win)
    cur = lambda g: pl.BlockSpec((tq, GRP), lambda bi, j: (bi * nt + j, g))
    if step:
        prev_k = pl.BlockSpec((B_WINDOW, GRP), lambda bi, j: (bi, 0))
        prev_v = prev_k
        pk_arr, pv_arr = cache_k, cache_v
    else:
        assert tq == B_WINDOW
        prev_k = pl.BlockSpec((B_WINDOW, GRP), lambda bi, j: (bi * nt + jnp.maximum(j - 1, 0), PK))
        prev_v = pl.BlockSpec((B_WINDOW, GRP), lambda bi, j: (bi * nt + jnp.maximum(j - 1, 0), PV))
        pk_arr, pv_arr = proj, proj
    return pl.pallas_call(
        functools.partial(_attn_kernel, chunk=chunk, n_chunks=tq // chunk, first_has_past=step),
        grid=(b, nt),
        in_specs=[cur(PQ), cur(PK), cur(PV), prev_k, prev_v, pl.BlockSpec(bias.shape, lambda bi, j: (0, 0))],
        out_specs=pl.BlockSpec((tq, GRP), lambda bi, j: (bi * nt + j, 0)),
        out_shape=jax.ShapeDtypeStruct((n, GRP), BF16),
        scratch_shapes=[pltpu.VMEM((B_WINDOW + tq, GRP), BF16), pltpu.VMEM((B_WINDOW + tq, GRP), BF16)],
        compiler_params=_cparams(("arbitrary", "arbitrary")), name="band_attn",
    )(proj, proj, proj, pk_arr, pv_arr, bias)


def _conv_kernel(g_ref, halo_ref, dw_ref, dwb_ref, lng_ref, lnb_ref, o_ref, tail_ref, xp, zbuf, *, tc, sub, first_has_past):
    halo = halo_ref[...].astype(F32)
    has_past = jnp.logical_or(pl.program_id(1) > 0, first_has_past)
    xp[0:HALO, :] = jnp.where(has_past, halo, 0.0)
    xp[HALO:HALO + tc, :] = g_ref[...].astype(F32)
    xp[HALO + tc:, :] = jnp.zeros((xp.shape[0] - HALO - tc, GRP), F32)

    @pl.when(pl.program_id(1) == pl.num_programs(1) - 1)
    def _():
        tail_ref[...] = xp[tc:tc + HALO, :]

    lead = HALO - C_BUF
    sl = 8
    for s in range(tc // sub):
        acc = None
        for r in range(sl):
            taps = [p for p in range(r, lead + C_WIDTH, sl) if p >= lead]
            z = None
            for p in taps:
                a0 = s * sub + p - r
                term = dw_ref[p - lead:p - lead + 1, :] * xp[a0:a0 + sub + sl, :]
                z = term if z is None else z + term
            zbuf[r] = z
            part = zbuf[r, r:r + sub, :]
            acc = part if acc is None else acc + part
        y = acc + dwb_ref[...]
        mu = jnp.mean(y, axis=-1, keepdims=True)
        yc = y - mu
        y = yc * lax.rsqrt(jnp.mean(yc * yc, axis=-1, keepdims=True) + EPS) * lng_ref[...] + lnb_ref[...]
        o_ref[s * sub:(s + 1) * sub, :] = (y * _sigmoid(y)).astype(BF16)


def _conv(proj, lw, b, t, state=None):
    n = proj.shape[0]
    step = state is not None
    tc = min(t, 512)
    nt = t // tc
    sub = min(tc, 64)
    if step:
        halo_arr = state
        halo_spec = pl.BlockSpec((HALO, GRP), lambda bi, j: (bi, 0))
    else:
        per = tc // HALO
        halo_arr = proj
        halo_spec = pl.BlockSpec((HALO, GRP), lambda bi, j: (jnp.maximum((bi * nt + j) * per - 1, 0), PGLU))
    vec = pl.BlockSpec((1, GRP), lambda bi, j: (0, 0))
    return pl.pallas_call(
        functools.partial(_conv_kernel, tc=tc, sub=sub, first_has_past=step),
        grid=(b, nt),
        in_specs=[pl.BlockSpec((tc, GRP), lambda bi, j: (bi * nt + j, PGLU)), halo_spec,
                  pl.BlockSpec((C_WIDTH, GRP), lambda bi, j: (0, 0)), vec, vec, vec],
        out_specs=[pl.BlockSpec((tc, GRP), lambda bi, j: (bi * nt + j, 0)),
                   pl.BlockSpec((HALO, GRP), lambda bi, j: (bi, 0))],
        out_shape=[jax.ShapeDtypeStruct((n, GRP), BF16), jax.ShapeDtypeStruct((b * HALO, GRP), F32)],
        scratch_shapes=[pltpu.VMEM((HALO + tc + 8, GRP), F32), pltpu.VMEM((8, sub + 8, GRP), F32)],
        compiler_params=_cparams(("arbitrary", "arbitrary")), name="conv_module",
    )(proj, halo_arr, lw["c_dw"], lw["c_dw_b"], lw["c_ln_g"], lw["c_ln_b"])


def _gla_tables(L):
    i = np.arange(L)[:, None]
    t = np.arange(L)[None, :]
    masks = []
    s = GLA_SUB
    masks.append(((i // s) == (t // s)) & (t <= i))
    s *= 2
    while s <= L:
        h = s // 2
        masks.append(((i // s) == (t // s)) & (i % s >= h) & (t % s < h))
        s *= 2
    tri = (t <= i).astype(np.float32)
    mask = np.stack([np.tile(m.astype(np.float32), (1, HEADS)) for m in masks], axis=0)
    return tri, mask


def _gla_anchor(cum, row, L, size, first_half):
    out = None
    for start in range(0, L, size):
        ar = start + size // 2 - 1 if first_half else start - 1
        val = jnp.zeros((L, GRP), F32) if ar < 0 else jnp.broadcast_to(cum[ar:ar + 1, :], (L, GRP))
        out = val if out is None else jnp.where(row >= start, val, out)
    return out


def _gla_kernel(q_ref, k_ref, v_ref, la_ref, dr_ref, s0_ref, tri_ref, lmask_ref, bdmask_ref, hsum_ref, gon_ref,
                o_ref, sf_ref, st, o_all, *, L, n_chunks, n_levels, first_has_state):
    j = pl.program_id(1)

    @pl.when(j == 0)
    def _():
        st[...] = jnp.zeros_like(st)
        if first_has_state:
            for h in range(HEADS):
                blk = slice(h * HEAD_DIM, (h + 1) * HEAD_DIM)
                st[blk, blk] = s0_ref[blk, :].T

    row = lax.broadcasted_iota(I32, (L, GRP), 0)
    dn_t = (((1,), (1,)), ((), ()))

    def prep(c):
        rows = slice(c * L, (c + 1) * L)
        q = q_ref[rows, :].astype(F32)
        k = k_ref[rows, :].astype(F32)
        v = v_ref[rows, :]
        cum = jnp.dot(tri_ref[...], la_ref[rows, :], preferred_element_type=F32)
        total = cum[L - 1:L, :]
        pairs = []
        for lvl in range(n_levels):
            size = GLA_SUB << lvl
            if lvl == 0:
                local = cum - _gla_anchor(cum, row, L, size, False)
                ql = q * jnp.exp(local)
                kl = k * jnp.exp(-local)
            else:
                upper = (row & (size - 1)) >= (size // 2)
                d = cum - _gla_anchor(cum, row, L, size, True)
                w = jnp.exp(jnp.where(upper, d, -d))
                ql = jnp.where(upper, q * w, 0.0)
                kl = jnp.where(upper, 0.0, k * w)
            pairs.append((ql.astype(BF16), _bd_stack(kl.astype(BF16), L)))
        return dict(rows=rows, v=v, qp=(q * jnp.exp(cum)).astype(BF16), kst=(k * jnp.exp(total - cum)).astype(BF16),
                    decay=jnp.exp(total), pairs=pairs)

    def intra(p):
        att = None
        for lvl, (ql, kbd) in enumerate(p["pairs"]):
            a = lax.dot_general(ql, kbd, dn_t, preferred_element_type=F32) * lmask_ref[lvl]
            att = a if att is None else att + a
        p["o_intra"] = jnp.dot(att.astype(BF16), _bd_stack(p["v"], L), preferred_element_type=F32)
        p["upd"] = lax.dot_general(p["v"], p["kst"], (((0,), (0,)), ((), ())),
                                   preferred_element_type=F32) * bdmask_ref[...]
        return p

    def finish(p, s_t):
        o_all[p["rows"], :] = lax.dot_general(p["qp"], s_t.astype(BF16), dn_t, preferred_element_type=F32) + p["o_intra"]
        return s_t * p["decay"] + p["upd"]

    s_t = st[...]
    stage1, stage2 = {}, {}
    for step in range(n_chunks + 2):
        if step < n_chunks:
            stage1[step] = prep(step)
        if 0 <= step - 1 < n_chunks:
            stage2[step - 1] = intra(stage1.pop(step - 1))
        if 0 <= step - 2 < n_chunks:
            s_t = finish(stage2.pop(step - 2), s_t)
    st[...] = s_t
    o = o_all[...]
    y = o * lax.rsqrt(_head_meansq(o, hsum_ref) + EPS) * gon_ref[...] * dr_ref[...].astype(F32)
    o_ref[...] = y.astype(BF16)

    @pl.when(j == pl.num_programs(1) - 1)
    def _():
        for h in range(HEADS):
            blk = slice(h * HEAD_DIM, (h + 1) * HEAD_DIM)
            sf_ref[blk, :] = st[blk, blk].T


def _gla(proj, lw, b, t, s0=None):
    n = proj.shape[0]
    step = s0 is not None
    L = min(t, 64)
    td = min(t, 512)
    nt = t // td
    n_levels = int(np.log2(L // GLA_SUB)) + 1
    tri, lmask = _gla_tables(L)
    tri = jnp.asarray(tri, BF16)
    lmask = jnp.asarray(lmask, F32)
    if not step:
        s0 = jnp.zeros((GRP, HEAD_DIM), F32)
        s0_spec = pl.BlockSpec((GRP, HEAD_DIM), lambda bi, j: (0, 0))
    else:
        s0_spec = pl.BlockSpec((GRP, HEAD_DIM), lambda bi, j: (bi, 0))
    cur = lambda g: pl.BlockSpec((td, GRP), lambda bi, j: (bi * nt + j, g))
    c2 = lambda bi, j: (0, 0)
    return pl.pallas_call(
        functools.partial(_gla_kernel, L=L, n_chunks=td // L, n_levels=n_levels, first_has_state=step),
        grid=(b, nt),
        in_specs=[cur(PDQ), cur(PDK), cur(PDV), cur(PLA), cur(PDR), s0_spec,
                  pl.BlockSpec(tri.shape, c2), pl.BlockSpec(lmask.shape, lambda bi, j: (0, 0, 0)),
                  pl.BlockSpec((GRP, GRP), c2), pl.BlockSpec((GRP, GRP), c2), pl.BlockSpec((1, GRP), c2)],
        out_specs=[pl.BlockSpec((td, GRP), lambda bi, j: (bi * nt + j, 0)),
                   pl.BlockSpec((GRP, HEAD_DIM), lambda bi, j: (bi, 0))],
        out_shape=[jax.ShapeDtypeStruct((n, GRP), BF16), jax.ShapeDtypeStruct((b * GRP, HEAD_DIM), F32)],
        scratch_shapes=[pltpu.VMEM((GRP, GRP), F32), pltpu.VMEM((td, GRP), F32)],
        compiler_params=_cparams(("arbitrary", "arbitrary")), name="gla",
    )(proj, proj, proj, proj, proj, s0, tri, lmask, lw["bdmask"], lw["hsum"], lw["gon"])


def _out_kernel(ya_ref, yb_ref, yc_ref, yd_ref, x_ref, wo_ref, g2_ref, wr_hi_ref, wr_lo_ref, br_ref, tri_ref,
                xo_ref, xn_ref, ri_ref, rf_ref, cnt_ref, *, tm, n_sub):
    @pl.when(pl.program_id(0) == 0)
    def _():
        cnt_ref[...] = jnp.zeros_like(cnt_ref)

    sub = tm // n_sub
    lane = lax.broadcasted_iota(I32, (sub, LANES), 1)
    big = np.int32(1 << 20)

    def project(s):
        rows = slice(s * sub, (s + 1) * sub)
        ycat = jnp.concatenate([ya_ref[rows, :], yb_ref[rows, :], yc_ref[rows, :], yd_ref[rows, :]], axis=1)
        x = x_ref[rows, :] + jnp.dot(ycat, wo_ref[...], preferred_element_type=F32)
        xo_ref[rows, :] = x
        xn = x * lax.rsqrt(jnp.mean(x * x, axis=-1, keepdims=True) + EPS) * g2_ref[...]
        xn_ref[rows, :] = _pack_halves(xn)
        xh = xn.astype(BF16)
        xl = (xn - xh.astype(F32)).astype(BF16)
        return (jnp.dot(xh, wr_hi_ref[...], preferred_element_type=F32)
                + jnp.dot(xl, wr_hi_ref[...], preferred_element_type=F32)
                + jnp.dot(xh, wr_lo_ref[...], preferred_element_type=F32)) + br_ref[...]

    def route(s, logits):
        rows = slice(s * sub, (s + 1) * sub)

        def first_max(mask):
            v = jnp.max(jnp.where(mask, logits, -jnp.inf), axis=-1, keepdims=True)
            idx = jnp.min(jnp.where(jnp.logical_and(mask, logits == v), lane, big), axis=-1, keepdims=True)
            return v, idx

        is_grp = jnp.logical_and(lane >= N_EXPERTS, lane < N_EXPERTS + N_GROUPS)
        gmax, gidx = first_max(is_grp)
        p_grp = 1.0 / jnp.sum(jnp.where(is_grp, jnp.exp(logits - gmax), 0.0), axis=-1, keepdims=True)
        grp = gidx - N_EXPERTS
        in_grp = (lane // PER_GROUP) == grp
        v1, i1 = first_max(in_grp)
        v2, i2 = first_max(jnp.logical_and(in_grp, lane != i1))
        e21 = jnp.exp(v2 - v1)
        gate1 = p_grp / (1.0 + e21)
        gate2 = p_grp * e21 / (1.0 + e21)

        oh1 = lane == i1
        oh2 = lane == i2
        both = jnp.logical_or(oh1, oh2)
        ones = jnp.where(both, 1.0, 0.0).astype(BF16)
        before = jnp.dot(tri_ref[...], ones, preferred_element_type=F32) + cnt_ref[...].astype(F32)
        rank1 = jnp.sum(jnp.where(oh1, before, 0.0), axis=-1, keepdims=True)
        rank2 = jnp.sum(jnp.where(oh2, before, 0.0), axis=-1, keepdims=True)
        cnt_ref[...] = cnt_ref[...] + jnp.sum(jnp.where(both, 1.0, 0.0), axis=0, keepdims=True).astype(I32)

        ri = jnp.where(lane == 0, i1, jnp.where(lane == 1, i2, jnp.where(lane == 2, rank1.astype(I32),
                                                                          jnp.where(lane == 3, rank2.astype(I32), 0))))
        ri_ref[rows, :] = ri
        rf_ref[rows, :] = jnp.where(lane == 0, gate1, jnp.where(lane == 1, gate2, 0.0))

    logits = project(0)
    for s in range(n_sub):
        nxt = project(s + 1) if s + 1 < n_sub else None
        route(s, logits)
        logits = nxt


def _out_proj(ya, yb, yc, yd, x, lw):
    n = x.shape[0]
    tm = min(1024, n)
    row = lambda i: (i, 0)
    const = lambda i: (0, 0)
    n_sub = 2 if tm >= 1024 else 1
    sub = tm // n_sub
    tri = jnp.asarray(np.tril(np.ones((sub, sub), np.float32), -1), BF16)
    consts = [lw["w_out"], lw["g2"], lw["wr_hi"], lw["wr_lo"], lw["br"], tri]
    yspec = pl.BlockSpec((tm, GRP), row)
    return pl.pallas_call(
        functools.partial(_out_kernel, tm=tm, n_sub=n_sub),
        grid=(n // tm,),
        in_specs=[yspec, yspec, yspec, yspec, pl.BlockSpec((tm, D_MODEL), row)] + [pl.BlockSpec(c.shape, const) for c in consts],
        out_specs=[pl.BlockSpec((tm, D_MODEL), row), pl.BlockSpec((tm, D_MODEL // 2), row),
                   pl.BlockSpec((tm, LANES), row), pl.BlockSpec((tm, LANES), row), pl.BlockSpec((1, LANES), const)],
        out_shape=[jax.ShapeDtypeStruct((n, D_MODEL), F32), jax.ShapeDtypeStruct((n, D_MODEL // 2), U32),
                   jax.ShapeDtypeStruct((n, LANES), I32), jax.ShapeDtypeStruct((n, LANES), F32),
                   jax.ShapeDtypeStruct((1, LANES), I32)],
        compiler_params=_cparams(("arbitrary",)), name="out_proj_router",
    )(ya, yb, yc, yd, x, *consts)


def _sc_scatter_rows(x, idx, n_out):
    n, d = x.shape
    kk = idx.shape[0]
    per_w = n // SC_WORKERS
    win = min(SC_WIN, per_w)
    n_win = per_w // win
    assert n_win * win * SC_WORKERS == n
    mesh = plsc.VectorSubcoreMesh(core_axis_name="c", subcore_axis_name="s")

    @functools.partial(
        pl.kernel, mesh=mesh, out_type=jax.ShapeDtypeStruct((n_out, d), x.dtype),
        scratch_types=[pltpu.VMEM((kk, win), I32), pltpu.VMEM((win, d), x.dtype)],
        name="sc_scatter_rows")
    def k(x_hbm, idx_hbm, o_hbm, idx_v, rows_v):
        wid = lax.axis_index("s") * 2 + lax.axis_index("c")
        base = wid * per_w

        @pl.loop(0, n_win)
        def _(w):
            off = base + w * win
            pltpu.sync_copy(x_hbm.at[pl.ds(off, win)], rows_v)
            for j in range(kk):
                pltpu.sync_copy(idx_hbm.at[j, pl.ds(off, win)], idx_v.at[j])
                pltpu.sync_copy(rows_v, o_hbm.at[idx_v.at[j]])

    return k(x, idx)


def _sc_gather_rows(y, idx):
    _, d = y.shape
    kk, n = idx.shape
    per_w = n // SC_WORKERS
    win = min(SC_WIN, per_w)
    n_win = per_w // win
    assert n_win * win * SC_WORKERS == n
    mesh = plsc.VectorSubcoreMesh(core_axis_name="c", subcore_axis_name="s")

    @functools.partial(
        pl.kernel, mesh=mesh, out_type=jax.ShapeDtypeStruct((kk, n, d), y.dtype),
        scratch_types=[pltpu.VMEM((kk, win), I32), pltpu.VMEM((win, d), y.dtype)],
        name="sc_gather_rows")
    def k(y_hbm, idx_hbm, o_hbm, idx_v, rows_v):
        wid = lax.axis_index("s") * 2 + lax.axis_index("c")
        base = wid * per_w

        @pl.loop(0, n_win)
        def _(w):
            off = base + w * win
            for j in range(kk):
                pltpu.sync_copy(idx_hbm.at[j, pl.ds(off, win)], idx_v.at[j])
                pltpu.sync_copy(y_hbm.at[idx_v.at[j]], rows_v)
                pltpu.sync_copy(rows_v, o_hbm.at[j, pl.ds(off, win)])

    return k(y, idx)


def _moe_kernel(bexp_ref, nused_ref, x_ref, wg_ref, wu_ref, wd_ref, o_ref, wg_s, wu_s, wd_s, *, n_sub):
    i = pl.program_id(0)
    prev = bexp_ref[jnp.maximum(i - 1, 0)]
    fresh = jnp.logical_or(i == 0, bexp_ref[i] != prev)

    @pl.when(jnp.logical_and(fresh, i < nused_ref[0]))
    def _():
        wg_s[...] = wg_ref[...].astype(BF16)
        wu_s[...] = wu_ref[...].astype(BF16)
        wd_s[...] = wd_ref[...].astype(BF16)

    @pl.when(i < nused_ref[0])
    def _():
        half = D_MODEL // 2
        sub = x_ref.shape[0] // n_sub

        def up(s):
            w = x_ref[s * sub:(s + 1) * sub, :]
            x = jnp.concatenate([_unpack_hi(w).astype(BF16), _unpack_lo(w).astype(BF16)], axis=1)
            hg = jnp.dot(x, wg_s[...], preferred_element_type=F32)
            hu = jnp.dot(x, wu_s[...], preferred_element_type=F32)
            return (hg * _sigmoid(hg) * hu).astype(BF16)

        def down(s, h):
            o_ref[s * sub:(s + 1) * sub, :] = _pack_halves(jnp.dot(h, wd_s[...], preferred_element_type=F32))

        h = up(0)
        for s in range(n_sub):
            nxt = up(s + 1) if s + 1 < n_sub else None
            down(s, h)
            h = nxt


def _moe_experts(xs, blk_exp, n_used, w_gate, w_up, w_down, layer, bm):
    p = xs.shape[0]
    n_blocks = p // bm
    live = lambda i, be, nu: jnp.minimum(i, jnp.maximum(nu[0] - 1, 0))
    wspec = lambda shape: pl.BlockSpec((None, None) + shape, lambda i, be, nu: (layer, be[live(i, be, nu)], 0, 0))
    grid_spec = pltpu.PrefetchScalarGridSpec(
        num_scalar_prefetch=2, grid=(n_blocks,),
        in_specs=[pl.BlockSpec((bm, D_MODEL // 2), lambda i, be, nu: (live(i, be, nu), 0)),
                  wspec((D_MODEL, D_EXPERT)), wspec((D_MODEL, D_EXPERT)), wspec((D_EXPERT, D_MODEL))],
        out_specs=pl.BlockSpec((bm, D_MODEL // 2), lambda i, be, nu: (live(i, be, nu), 0)),
        scratch_shapes=[pltpu.VMEM((D_MODEL, D_EXPERT), BF16), pltpu.VMEM((D_MODEL, D_EXPERT), BF16),
                        pltpu.VMEM((D_EXPERT, D_MODEL), BF16)])
    return pl.pallas_call(
        functools.partial(_moe_kernel, n_sub=2 if bm >= 512 else 1),
        grid_spec=grid_spec, out_shape=jax.ShapeDtypeStruct((p, D_MODEL // 2), U32),
        compiler_params=_cparams(("arbitrary",)), name="moe_experts",
    )(blk_exp, n_used, xs, w_gate, w_up, w_down)


def _moe_block_rows(n):
    return 512 if n >= 16384 else 128


def _moe(xn_packed, route_i, counts, lw):
    n = xn_packed.shape[0]
    bm = _moe_block_rows(n)
    n_blocks = -(-(2 * n + N_EXPERTS * (bm - 1)) // bm)
    cnt = counts[0, :N_EXPERTS]
    padded = (cnt + bm - 1) // bm * bm
    pad_end = jnp.cumsum(padded)
    pad_start = pad_end - padded
    experts = jnp.arange(N_EXPERTS, dtype=I32)
    eid = route_i[:, 0:2].T
    start_of = jnp.sum(jnp.where(eid[:, :, None] == experts, pad_start, 0), axis=-1)
    dest = (start_of + route_i[:, 2:4].T).astype(I32)
    first_row = jnp.arange(n_blocks, dtype=I32) * bm
    blk_exp = jnp.minimum(jnp.sum((pad_end[None, :] <= first_row[:, None]).astype(I32), axis=1), N_EXPERTS - 1)
    n_used = (pad_end[-1:] // bm).astype(I32)
    xs = _sc_scatter_rows(xn_packed, dest, n_blocks * bm)
    ys = _moe_experts(xs, blk_exp, n_used, lw["e_w_gate"], lw["e_w_up"], lw["e_w_down"], lw["layer"], bm)
    return _sc_gather_rows(ys, dest)


def _combine_kernel(x_ref, y_ref, gate_ref, o_ref):
    half = D_MODEL // 2
    x = x_ref[...]
    g = gate_ref[...]
    g0 = g[:, 0:1]
    g1 = g[:, 1:2]
    w0 = y_ref[0]
    w1 = y_ref[1]
    o_ref[:, :half] = x[:, :half] + g0 * _unpack_hi(w0) + g1 * _unpack_hi(w1)
    o_ref[:, half:] = x[:, half:] + g0 * _unpack_lo(w0) + g1 * _unpack_lo(w1)


def _combine(x, y, gates):
    n = x.shape[0]
    tm = min(512, n)
    row = lambda i: (i, 0)
    return pl.pallas_call(
        _combine_kernel, grid=(n // tm,),
        in_specs=[pl.BlockSpec((tm, D_MODEL), row), pl.BlockSpec((2, tm, D_MODEL // 2), lambda i: (0, i, 0)),
                  pl.BlockSpec((tm, LANES), row)],
        out_specs=pl.BlockSpec((tm, D_MODEL), row), out_shape=jax.ShapeDtypeStruct((n, D_MODEL), F32),
        compiler_params=_cparams(("arbitrary",)), name="moe_combine",
    )(x, y, gates)


def _layer_weights(l, p):
    w_in = p["w_in"][l]
    cols = [w_in[:, i * GRP:(i + 1) * GRP] for i in range(11)]
    by_group = [None] * N_PROJ
    for ref_i, g in enumerate(_REF_GROUPS):
        if g is not None:
            by_group[g] = cols[ref_i]
    by_group[PGLU] = cols[5]
    w_dg = jnp.zeros((D_MODEL, LANES), F32).at[:, :GATE_RANK].set(w_in[:, 11 * GRP:])
    wg2 = jnp.zeros((LANES, GRP), F32).at[:GATE_RANK].set(p["d_wg2"][l])
    tile4 = lambda v: jnp.tile(v, HEADS)[None, :]
    hid = np.arange(GRP) // HEAD_DIM
    bd = (hid[:, None] == hid[None, :]).astype(np.float32)
    wr = jnp.zeros((D_MODEL, LANES), F32).at[:, :N_EXPERTS].set(p["r_expert_w"][l])
    wr = wr.at[:, N_EXPERTS:N_EXPERTS + N_GROUPS].set(p["r_group_w"][l])
    wr_hi = wr.astype(BF16)
    br = jnp.zeros((1, LANES), F32).at[0, :N_EXPERTS].set(p["r_expert_b"][l])
    br = br.at[0, N_EXPERTS:N_EXPERTS + N_GROUPS].set(p["r_group_b"][l])
    return {
        "g1": p["norm1_g"][l][None, :],
        "w_in": jnp.concatenate(by_group[:PLA], axis=1).astype(BF16),
        "w_cg": cols[6].astype(BF16),
        "w_dg": w_dg.astype(BF16),
        "wg2": wg2.astype(BF16),
        "bg": p["d_bg"][l][None, :],
        "gq": tile4(p["b_qnorm_g"][l]), "gk": tile4(p["b_knorm_g"][l]), "gav": p["a_vnorm_g"][l][None, :],
        "gon": tile4(p["d_onorm_g"][l]),
        "hsum": jnp.asarray(bd, BF16), "bdmask": jnp.asarray(bd, F32),
        "a_ws": p["a_ws"][l], "a_bs_rows": jnp.repeat(p["a_bs"][l].T, HEAD_DIM, axis=1),
        "b_rel": p["b_rel_bias"][l],
        "c_dw": p["c_dw"][l], "c_dw_b": p["c_dw_b"][l][None, :],
        "c_ln_g": p["c_ln_g"][l][None, :], "c_ln_b": p["c_ln_b"][l][None, :],
        "w_out": p["w_out"][l].astype(BF16),
        "g2": p["norm2_g"][l][None, :],
        "wr_hi": wr_hi, "wr_lo": (wr - wr_hi.astype(F32)).astype(BF16), "br": br,
        "e_w_gate": p["e_w_gate"], "e_w_up": p["e_w_up"], "e_w_down": p["e_w_down"], "layer": l,
    }


def _mix_and_route(x, lw, b, t, pending, caches):
    step = caches is not None
    y_prev, gates_prev = pending if pending is not None else (None, None)
    x, proj, kv, a_v = _in_proj(x, lw, t, y_prev, gates_prev, emit_av=step)
    ya = _gmlp(proj, lw, t)
    if not step:
        yb = _attention(proj, lw, b, t)
        yc, tail = _conv(proj, lw, b, t)
        yd, sf = _gla(proj, lw, b, t)
    else:
        ck, cv, cc, cs = caches
        yb = _attention(proj, lw, b, t, ck.reshape(b * B_WINDOW, GRP), cv.reshape(b * B_WINDOW, GRP))
        halo = jnp.pad(cc, ((0, 0), (HALO - C_BUF, 0), (0, 0))).reshape(b * HALO, GRP)
        yc, tail = _conv(proj, lw, b, t, halo)
        yd, sf = _gla(proj, lw, b, t, cs.reshape(b * GRP, HEAD_DIM))
        a_v = a_v.reshape(b, t, GRP)
    x2, xn_packed, route_i, route_f, counts = _out_proj(ya, yb, yc, yd, x, lw)
    y = _moe(xn_packed, route_i, counts, lw)
    keep = min(B_WINDOW, t)
    new_k = kv[0].reshape(b, keep, HEADS, HEAD_DIM)
    new_v = kv[1].reshape(b, keep, HEADS, HEAD_DIM)
    new_buf = tail.reshape(b, HALO, GRP)[:, HALO - C_BUF:]
    return x2, (y, route_f), (new_k, new_v, new_buf, sf.reshape(b, HEADS, HEAD_DIM, HEAD_DIM), a_v)


def kernel(x_prompt, x_sample, cache_b_k, cache_b_v, state_c_conv, state_d_gla, norm1_g, w_in, a_vnorm_g, a_ws, a_bs, b_qnorm_g, b_knorm_g, b_rel_bias, c_dw, c_dw_b, c_ln_g, c_ln_b, d_wg2, d_bg, d_onorm_g, w_out, norm2_g, r_group_w, r_group_b, r_expert_w, r_expert_b, e_w_gate, e_w_up, e_w_down):
    params = dict(norm1_g=norm1_g, w_in=w_in, a_vnorm_g=a_vnorm_g, a_ws=a_ws, a_bs=a_bs, b_qnorm_g=b_qnorm_g,
                  b_knorm_g=b_knorm_g, b_rel_bias=b_rel_bias, c_dw=c_dw, c_dw_b=c_dw_b, c_ln_g=c_ln_g, c_ln_b=c_ln_b,
                  d_wg2=d_wg2, d_bg=d_bg, d_onorm_g=d_onorm_g, w_out=w_out, norm2_g=norm2_g, r_group_w=r_group_w,
                  r_group_b=r_group_b, r_expert_w=r_expert_w, r_expert_b=r_expert_b, e_w_gate=e_w_gate,
                  e_w_up=e_w_up, e_w_down=e_w_down)
    depth = w_in.shape[0]
    bp, tp, _ = x_prompt.shape
    bs, ts, _ = x_sample.shape
    xp = x_prompt.reshape(bp * tp, D_MODEL)
    xs = x_sample.reshape(bs * ts, D_MODEL)
    pend_p = pend_s = None
    st_p, st_s = [], []
    for l in range(depth):
        lw = _layer_weights(l, params)
        xp, pend_p, sp = _mix_and_route(xp, lw, bp, tp, pend_p, None)
        xs, pend_s, ss = _mix_and_route(xs, lw, bs, ts, pend_s,
                                        (cache_b_k[l], cache_b_v[l], state_c_conv[l], state_d_gla[l]))
        st_p.append(sp)
        st_s.append(ss)
    yp = _combine(xp, pend_p[0], pend_p[1]).reshape(bp, tp, D_MODEL)
    ys = _combine(xs, pend_s[0], pend_s[1]).reshape(bs, ts, D_MODEL)
    stack = lambda sts, i: jnp.stack([s[i] for s in sts])
    return (yp, ys, stack(st_p, 0), stack(st_p, 1), stack(st_p, 2), stack(st_p, 3),
            stack(st_s, 0), stack(st_s, 1), stack(st_s, 2), stack(st_s, 3), stack(st_s, 4))
```

```python
import functools

import numpy as np
import jax
import jax.numpy as jnp
from jax import lax
from jax.experimental import pallas as pl
from jax.experimental.pallas import tpu as pltpu
from jax.experimental.pallas import tpu_sc as plsc

F32 = jnp.float32
BF16 = jnp.bfloat16
I32 = jnp.int32
U32 = jnp.uint32

D_MODEL = 1024
GRP = 256
HEADS = 4
HEAD_DIM = 64
CHUNK = 64
A_CHUNK = 128
B_WINDOW = 512
REL_CLIP = 128
C_WIDTH = 31
C_BUF = C_WIDTH - 1
HALO = 32
GATE_RANK = 16
GLA_TAU = 16.0
GLA_SUB = 16
N_GROUPS = 4
PER_GROUP = 8
N_EXPERTS = 32
D_EXPERT = 512
EPS = 1e-6
NEG_INF = -1e30
LANES = 128
VMEM_LIMIT = 48 * 1024 * 1024

PK, PV, PQ, PAU, PAV, PGLU, PDQ, PDK, PDV, PDR, PLA = range(11)
N_PROJ = 11
_REF_GROUPS = (PAU, PAV, PQ, PK, PV, None, None, PDQ, PDK, PDV, PDR)

SC_WORKERS = 32
SC_WIN = 128


def _cparams(sem):
    return pltpu.CompilerParams(dimension_semantics=sem, vmem_limit_bytes=VMEM_LIMIT)


def _sigmoid(x):
    return 1.0 / (1.0 + jnp.exp(-x))


def _gelu_tanh(x):
    c = np.float32(np.sqrt(2.0 / np.pi))
    return 0.5 * x * (1.0 + jnp.tanh(c * (x + np.float32(0.044715) * (x * x * x))))


def _pack_halves(y):
    half = y.shape[1] // 2
    hi = pltpu.bitcast(y[:, :half].astype(BF16).astype(F32), U32)
    lo = pltpu.bitcast(y[:, half:].astype(BF16).astype(F32), U32)
    return hi | (lo >> np.uint32(16))


def _unpack_hi(w):
    return pltpu.bitcast(w & np.uint32(0xFFFF0000), F32)


def _unpack_lo(w):
    return pltpu.bitcast(w << np.uint32(16), F32)


def _head_id(shape, axis, size):
    return lax.broadcasted_iota(I32, shape, axis) // size


def _bd_stack(x, rows):
    x4 = jnp.concatenate([x] * HEADS, axis=0)
    shape = (HEADS * rows, GRP)
    keep = _head_id(shape, 0, rows) == _head_id(shape, 1, HEAD_DIM)
    return jnp.where(keep, x4, jnp.zeros_like(x4))


def _bd_unstack(o, rows):
    lane_h = _head_id((rows, GRP), 1, HEAD_DIM)
    out = o[(HEADS - 1) * rows:HEADS * rows]
    for h in range(HEADS - 2, -1, -1):
        out = jnp.where(lane_h == h, o[h * rows:(h + 1) * rows], out)
    return out


def _head_meansq(o, hsum_ref):
    sq = (o * o).astype(BF16)
    return jnp.dot(sq, hsum_ref[...], preferred_element_type=F32) * np.float32(1.0 / HEAD_DIM)


def _in_kernel(*refs, combine, emit_av, tiles_per_stream):
    refs = list(refs)
    x_ref = refs.pop(0)
    if combine:
        y_ref = refs.pop(0)
        gate_ref = refs.pop(0)
    g1_ref, w_ref, wcg_ref, wdg_ref, wg2_ref, bg_ref, gq_ref, gk_ref, gav_ref, hsum_ref = refs[:10]
    refs = refs[10:]
    if combine:
        xo_ref = refs.pop(0)
    p_ref = refs.pop(0)
    kc_ref = refs.pop(0)
    vc_ref = refs.pop(0)
    if emit_av:
        av_ref = refs.pop(0)
    raw = refs.pop(0)
    x = x_ref[...]
    if combine:
        half = D_MODEL // 2
        g = gate_ref[...]
        g0 = g[:, 0:1]
        g1 = g[:, 1:2]
        w0 = y_ref[0]
        w1 = y_ref[1]
        xa = x[:, :half] + g0 * _unpack_hi(w0) + g1 * _unpack_hi(w1)
        xb = x[:, half:] + g0 * _unpack_lo(w0) + g1 * _unpack_lo(w1)
        xo_ref[:, :half] = xa
        xo_ref[:, half:] = xb
        x = jnp.concatenate([xa, xb], axis=1)
    rs = lax.rsqrt(jnp.mean(x * x, axis=-1, keepdims=True) + EPS)
    h = (x * g1_ref[...]).astype(BF16)

    n_slots = PLA + 2

    def matmul(slot):
        if slot < PLA:
            raw[:, slot * GRP:(slot + 1) * GRP] = jnp.dot(h, w_ref[:, slot * GRP:(slot + 1) * GRP],
                                                          preferred_element_type=F32)
        elif slot == PLA:
            raw[:, PLA * GRP:(PLA + 1) * GRP] = jnp.dot(h, wcg_ref[...], preferred_element_type=F32)
        else:
            raw[:, (PLA + 1) * GRP:] = jnp.dot(h, wdg_ref[...], preferred_element_type=F32)

    def proj(g):
        return raw[:, g * GRP:(g + 1) * GRP] * rs

    def put(g, val):
        p_ref[:, g * GRP:(g + 1) * GRP] = val.astype(BF16)

    def epilogue(slot):
        if slot == PK:
            r = proj(PK)
            put(PK, r * lax.rsqrt(_head_meansq(r, hsum_ref) + EPS) * gk_ref[...])
        elif slot == PV:
            put(PV, proj(PV))

            @pl.when(pl.program_id(0) % tiles_per_stream == tiles_per_stream - 1)
            def _():
                r = proj(PK)
                kc_ref[...] = r * lax.rsqrt(_head_meansq(r, hsum_ref) + EPS) * gk_ref[...]
                vc_ref[...] = proj(PV)
        elif slot == PQ:
            r = proj(PQ)
            put(PQ, r * lax.rsqrt(_head_meansq(r, hsum_ref) + EPS) * (gq_ref[...] * np.float32(HEAD_DIM ** -0.5)))
        elif slot == PAU:
            put(PAU, _gelu_tanh(proj(PAU)))
        elif slot == PAV:
            r = _gelu_tanh(proj(PAV))
            av = r * lax.rsqrt(jnp.mean(r * r, axis=-1, keepdims=True) + EPS) * gav_ref[...]
            put(PAV, av)
            if emit_av:
                av_ref[...] = av
        elif slot == PGLU:
            pass
        elif slot == PDQ:
            put(PDQ, proj(PDQ) * np.float32(HEAD_DIM ** -0.5))
        elif slot in (PDK, PDV):
            put(slot, proj(slot))
        elif slot == PDR:
            r = proj(PDR)
            put(PDR, r * _sigmoid(r))
        elif slot == PLA:
            put(PGLU, proj(PGLU) * _sigmoid(proj(PLA)))
        else:
            dg = raw[:, (PLA + 1) * GRP:] * rs
            z = jnp.dot(dg.astype(BF16), wg2_ref[...], preferred_element_type=F32) + bg_ref[...]
            logsig = jnp.minimum(z, 0.0) - jnp.log(1.0 + jnp.exp(-jnp.abs(z)))
            put(PLA, logsig * np.float32(1.0 / GLA_TAU))

    order = (PLA + 1, PK, PQ, PAV, PAU, PGLU, PLA, PDR, PV, PDQ, PDK, PDV)
    assert sorted(order) == list(range(n_slots))
    lag = 2
    for i in range(n_slots + lag):
        if i < n_slots:
            matmul(order[i])
        if i >= lag:
            epilogue(order[i - lag])


def _in_proj(x, lw, t, y=None, gates=None, emit_av=False):
    n = x.shape[0]
    tm = min(512, n)
    combine = y is not None
    keep = min(B_WINDOW, t)
    tps = max(t // tm, 1)
    assert tps == 1 or keep == tm
    row = lambda i: (i, 0)
    const = lambda i: (0, 0)
    ins, specs = [x], [pl.BlockSpec((tm, D_MODEL), row)]
    if combine:
        ins += [y, gates]
        specs += [pl.BlockSpec((2, tm, D_MODEL // 2), lambda i: (0, i, 0)), pl.BlockSpec((tm, LANES), row)]
    consts = [lw["g1"], lw["w_in"], lw["w_cg"], lw["w_dg"], lw["wg2"], lw["bg"], lw["gq"], lw["gk"], lw["gav"], lw["hsum"]]
    ins += consts
    specs += [pl.BlockSpec(c.shape, const) for c in consts]
    newest = jax.ShapeDtypeStruct((n // tps, GRP), F32)
    newest_spec = pl.BlockSpec((tm, GRP), lambda i: (i // tps, 0))
    out_shape = [jax.ShapeDtypeStruct((n, N_PROJ * GRP), BF16), newest, newest]
    out_specs = [pl.BlockSpec((tm, N_PROJ * GRP), row), newest_spec, newest_spec]
    if combine:
        out_shape = [jax.ShapeDtypeStruct((n, D_MODEL), F32)] + out_shape
        out_specs = [pl.BlockSpec((tm, D_MODEL), row)] + out_specs
    if emit_av:
        out_shape.append(jax.ShapeDtypeStruct((n, GRP), F32))
        out_specs.append(pl.BlockSpec((tm, GRP), row))
    outs = list(pl.pallas_call(
        functools.partial(_in_kernel, combine=combine, emit_av=emit_av, tiles_per_stream=tps),
        grid=(n // tm,), in_specs=specs, out_specs=out_specs, out_shape=out_shape,
        scratch_shapes=[pltpu.VMEM((tm, (PLA + 1) * GRP + LANES), F32)],
        compiler_params=_cparams(("arbitrary",)), name="in_proj",
    )(*ins))
    x_new = outs.pop(0) if combine else x
    proj, k_new, v_new = outs[0], outs[1], outs[2]
    return x_new, proj, (k_new, v_new), (outs[3] if emit_av else None)


def _gmlp_body(u_ref, v_ref, ws_ref, bs_ref, o_ref, chunk, n_chunks):
    lane_h = _head_id((chunk, GRP), 1, HEAD_DIM)
    ri = lax.broadcasted_iota(I32, (chunk, chunk), 0)
    ci = lax.broadcasted_iota(I32, (chunk, chunk), 1)
    wm = [jnp.where(ci <= ri, ws_ref[h], 0.0).astype(BF16) for h in range(HEADS)]
    for c in range(n_chunks):
        rows = slice(c * chunk, (c + 1) * chunk)
        v = v_ref[rows, :]
        sv = jnp.dot(wm[HEADS - 1], v, preferred_element_type=F32)
        for h in range(HEADS - 2, -1, -1):
            sv = jnp.where(lane_h == h, jnp.dot(wm[h], v, preferred_element_type=F32), sv)
        o_ref[rows, :] = (u_ref[rows, :].astype(F32) * (sv + bs_ref[...])).astype(BF16)


def _attn_kernel(q_ref, kc_ref, vc_ref, kp_ref, vp_ref, bias_ref, o_ref, kbuf, vbuf, *, chunk, n_chunks, first_has_past):
    tq = chunk * n_chunks
    win = B_WINDOW + chunk
    kbuf[0:B_WINDOW, :] = kp_ref[...].astype(BF16)
    vbuf[0:B_WINDOW, :] = vp_ref[...].astype(BF16)
    kbuf[B_WINDOW:B_WINDOW + tq, :] = kc_ref[...]
    vbuf[B_WINDOW:B_WINDOW + tq, :] = vc_ref[...]
    col = lax.broadcasted_iota(I32, (HEADS * chunk, win), 1)

    def chunks(no_past):
        for c in range(n_chunks):
            q = q_ref[c * chunk:(c + 1) * chunk, :]
            kk = kbuf[c * chunk:c * chunk + win, :]
            vv = vbuf[c * chunk:c * chunk + win, :]
            s = lax.dot_general(_bd_stack(q, chunk), kk, (((1,), (1,)), ((), ())), preferred_element_type=F32)
            s = s + bias_ref[...]
            if no_past:
                s = jnp.where(col + c * chunk >= B_WINDOW, s, NEG_INF)
            m = jnp.max(s, axis=-1, keepdims=True)
            p = jnp.exp(s - m)
            l = jnp.sum(p, axis=-1, keepdims=True)
            o = jnp.dot(p.astype(BF16), vv, preferred_element_type=F32) * (1.0 / l)
            o_ref[c * chunk:(c + 1) * chunk, :] = _bd_unstack(o, chunk).astype(BF16)

    if first_has_past:
        chunks(False)
    else:
        pl.when(pl.program_id(1) == 0)(functools.partial(chunks, True))
        pl.when(pl.program_id(1) > 0)(functools.partial(chunks, False))


def _attention(proj, lw, b, t, cache_k=None, cache_v=None):
    n = proj.shape[0]
    step = cache_k is not None
    chunk = min(t, CHUNK)
    tq = min(t, B_WINDOW)
    nt = t // tq
    rel = lw["b_rel"]
    win = B_WINDOW + chunk
    lo = REL_CLIP - (chunk - 1)
    n_far = (chunk - 1) + win - (2 * REL_CLIP + 1 - lo)
    by_dist = jnp.concatenate([rel[:, lo:], jnp.broadcast_to(rel[:, -1:], (HEADS, n_far))], axis=1)
    by_key = by_dist[:, ::-1]
    n_k = chunk - 1 + win
    wrapped = jnp.tile(jnp.pad(by_key, ((0, 0), (0, 1))), (1, chunk))[:, :chunk * n_k].reshape(HEADS, chunk, n_k)
    bias = wrapped[:, :, chunk - 1:].astype(F32).reshape(HEADS * chunk, win)
    cur = lambda g: pl.BlockSpec((tq, GRP), lambda bi, j: (bi * nt + j, g))
    if step:
        prev_k = pl.BlockSpec((B_WINDOW, GRP), lambda bi, j: (bi, 0))
        prev_v = prev_k
        pk_arr, pv_arr = cache_k, cache_v
    else:
        assert tq == B_WINDOW
        prev_k = pl.BlockSpec((B_WINDOW, GRP), lambda bi, j: (bi * nt + jnp.maximum(j - 1, 0), PK))
        prev_v = pl.BlockSpec((B_WINDOW, GRP), lambda bi, j: (bi * nt + jnp.maximum(j - 1, 0), PV))
        pk_arr, pv_arr = proj, proj
    return pl.pallas_call(
        functools.partial(_attn_kernel, chunk=chunk, n_chunks=tq // chunk, first_has_past=step),
        grid=(b, nt),
        in_specs=[cur(PQ), cur(PK), cur(PV), prev_k, prev_v, pl.BlockSpec(bias.shape, lambda bi, j: (0, 0))],
        out_specs=pl.BlockSpec((tq, GRP), lambda bi, j: (bi * nt + j, 0)),
        out_shape=jax.ShapeDtypeStruct((n, GRP), BF16),
        scratch_shapes=[pltpu.VMEM((B_WINDOW + tq, GRP), BF16), pltpu.VMEM((B_WINDOW + tq, GRP), BF16)],
        compiler_params=_cparams(("arbitrary", "arbitrary")), name="band_attn",
    )(proj, proj, proj, pk_arr, pv_arr, bias)


def _conv_kernel(g_ref, halo_ref, dw_ref, dwb_ref, lng_ref, lnb_ref, u_ref, v_ref, ws_ref, bs_ref,
                 o_ref, tail_ref, ya_ref, xp, zbuf, *, tc, sub, first_has_past, a_chunk):
    _gmlp_body(u_ref, v_ref, ws_ref, bs_ref, ya_ref, a_chunk, tc // a_chunk)
    halo = halo_ref[...].astype(F32)
    has_past = jnp.logical_or(pl.program_id(1) > 0, first_has_past)
    xp[0:HALO, :] = jnp.where(has_past, halo, 0.0)
    xp[HALO:HALO + tc, :] = g_ref[...].astype(F32)
    xp[HALO + tc:, :] = jnp.zeros((xp.shape[0] - HALO - tc, GRP), F32)

    @pl.when(pl.program_id(1) == pl.num_programs(1) - 1)
    def _():
        tail_ref[...] = xp[tc:tc + HALO, :]

    lead = HALO - C_BUF
    sl = 8
    for s in range(tc // sub):
        acc = None
        for r in range(sl):
            taps = [p for p in range(r, lead + C_WIDTH, sl) if p >= lead]
            z = None
            for p in taps:
                a0 = s * sub + p - r
                term = dw_ref[p - lead:p - lead + 1, :] * xp[a0:a0 + sub + sl, :]
                z = term if z is None else z + term
            zbuf[r] = z
            part = zbuf[r, r:r + sub, :]
            acc = part if acc is None else acc + part
        y = acc + dwb_ref[...]
        mu = jnp.mean(y, axis=-1, keepdims=True)
        yc = y - mu
        y = yc * lax.rsqrt(jnp.mean(yc * yc, axis=-1, keepdims=True) + EPS) * lng_ref[...] + lnb_ref[...]
        o_ref[s * sub:(s + 1) * sub, :] = (y * _sigmoid(y)).astype(BF16)


def _conv_gmlp(proj, lw, b, t, state=None):
    n = proj.shape[0]
    step = state is not None
    tc = min(t, 512)
    nt = t // tc
    sub = min(tc, 64)
    a_chunk = min(t, A_CHUNK)
    ws = lw["a_ws"][:, :a_chunk, :a_chunk]
    bs = lw["a_bs_rows"][:a_chunk]
    if step:
        halo_arr = state
        halo_spec = pl.BlockSpec((HALO, GRP), lambda bi, j: (bi, 0))
    else:
        per = tc // HALO
        halo_arr = proj
        halo_spec = pl.BlockSpec((HALO, GRP), lambda bi, j: (jnp.maximum((bi * nt + j) * per - 1, 0), PGLU))
    vec = pl.BlockSpec((1, GRP), lambda bi, j: (0, 0))
    cur = lambda g: pl.BlockSpec((tc, GRP), lambda bi, j: (bi * nt + j, g))
    return pl.pallas_call(
        functools.partial(_conv_kernel, tc=tc, sub=sub, first_has_past=step, a_chunk=a_chunk),
        grid=(b, nt),
        in_specs=[cur(PGLU), halo_spec, pl.BlockSpec((C_WIDTH, GRP), lambda bi, j: (0, 0)), vec, vec, vec,
                  cur(PAU), cur(PAV), pl.BlockSpec(ws.shape, lambda bi, j: (0, 0, 0)),
                  pl.BlockSpec(bs.shape, lambda bi, j: (0, 0))],
        out_specs=[cur(0), pl.BlockSpec((HALO, GRP), lambda bi, j: (bi, 0)), cur(0)],
        out_shape=[jax.ShapeDtypeStruct((n, GRP), BF16), jax.ShapeDtypeStruct((b * HALO, GRP), F32),
                   jax.ShapeDtypeStruct((n, GRP), BF16)],
        scratch_shapes=[pltpu.VMEM((HALO + tc + 8, GRP), F32), pltpu.VMEM((8, sub + 8, GRP), F32)],
        compiler_params=_cparams(("arbitrary", "arbitrary")), name="conv_gmlp",
    )(proj, halo_arr, lw["c_dw"], lw["c_dw_b"], lw["c_ln_g"], lw["c_ln_b"], proj, proj, ws, bs)


def _gla_tables(L):
    i = np.arange(L)[:, None]
    t = np.arange(L)[None, :]
    masks = []
    s = GLA_SUB
    masks.append(((i // s) == (t // s)) & (t <= i))
    s *= 2
    while s <= L:
        h = s // 2
        masks.append(((i // s) == (t // s)) & (i % s >= h) & (t % s < h))
        s *= 2
    tri = (t <= i).astype(np.float32)
    mask = np.stack([np.tile(m.astype(np.float32), (1, HEADS)) for m in masks], axis=0)
    return tri, mask


def _gla_anchor(cum, row, L, size, first_half):
    out = None
    for start in range(0, L, size):
        ar = start + size // 2 - 1 if first_half else start - 1
        val = jnp.zeros((L, GRP), F32) if ar < 0 else jnp.broadcast_to(cum[ar:ar + 1, :], (L, GRP))
        out = val if out is None else jnp.where(row >= start, val, out)
    return out


def _gla_kernel(q_ref, k_ref, v_ref, la_ref, dr_ref, s0_ref, tri_ref, lmask_ref, bdmask_ref, hsum_ref, gon_ref,
                o_ref, sf_ref, st, o_all, *, L, n_chunks, n_levels, first_has_state):
    j = pl.program_id(1)

    @pl.when(j == 0)
    def _():
        st[...] = jnp.zeros_like(st)
        if first_has_state:
            for h in range(HEADS):
                blk = slice(h * HEAD_DIM, (h + 1) * HEAD_DIM)
                st[blk, blk] = s0_ref[blk, :].T

    row = lax.broadcasted_iota(I32, (L, GRP), 0)
    dn_t = (((1,), (1,)), ((), ()))

    def prep(c):
        rows = slice(c * L, (c + 1) * L)
        q = q_ref[rows, :].astype(F32)
        k = k_ref[rows, :].astype(F32)
        v = v_ref[rows, :]
        cum = jnp.dot(tri_ref[...], la_ref[rows, :], preferred_element_type=F32)
        total = cum[L - 1:L, :]
        pairs = []
        for lvl in range(n_levels):
            size = GLA_SUB << lvl
            if lvl == 0:
                local = cum - _gla_anchor(cum, row, L, size, False)
                ql = q * jnp.exp(local)
                kl = k * jnp.exp(-local)
            else:
                upper = (row & (size - 1)) >= (size // 2)
                d = cum - _gla_anchor(cum, row, L, size, True)
                w = jnp.exp(jnp.where(upper, d, -d))
                ql = jnp.where(upper, q * w, 0.0)
                kl = jnp.where(upper, 0.0, k * w)
            pairs.append((ql.astype(BF16), _bd_stack(kl.astype(BF16), L)))
        return dict(rows=rows, v=v, qp=(q * jnp.exp(cum)).astype(BF16), kst=(k * jnp.exp(total - cum)).astype(BF16),
                    decay=jnp.exp(total), pairs=pairs)

    def intra(p):
        att = None
        for lvl, (ql, kbd) in enumerate(p["pairs"]):
            a = lax.dot_general(ql, kbd, dn_t, preferred_element_type=F32) * lmask_ref[lvl]
            att = a if att is None else att + a
        p["o_intra"] = jnp.dot(att.astype(BF16), _bd_stack(p["v"], L), preferred_element_type=F32)
        p["upd"] = lax.dot_general(p["v"], p["kst"], (((0,), (0,)), ((), ())),
                                   preferred_element_type=F32) * bdmask_ref[...]
        return p

    def finish(p, s_t):
        o_all[p["rows"], :] = lax.dot_general(p["qp"], s_t.astype(BF16), dn_t, preferred_element_type=F32) + p["o_intra"]
        return s_t * p["decay"] + p["upd"]

    s_t = st[...]
    stage1, stage2 = {}, {}
    for step in range(n_chunks + 2):
        if step < n_chunks:
            stage1[step] = prep(step)
        if 0 <= step - 1 < n_chunks:
            stage2[step - 1] = intra(stage1.pop(step - 1))
        if 0 <= step - 2 < n_chunks:
            s_t = finish(stage2.pop(step - 2), s_t)
    st[...] = s_t
    o = o_all[...]
    y = o * lax.rsqrt(_head_meansq(o, hsum_ref) + EPS) * gon_ref[...] * dr_ref[...].astype(F32)
    o_ref[...] = y.astype(BF16)

    @pl.when(j == pl.num_programs(1) - 1)
    def _():
        for h in range(HEADS):
            blk = slice(h * HEAD_DIM, (h + 1) * HEAD_DIM)
            sf_ref[blk, :] = st[blk, blk].T


def _gla(proj, lw, b, t, s0=None):
    n = proj.shape[0]
    step = s0 is not None
    L = min(t, 64)
    td = min(t, 512)
    nt = t // td
    n_levels = int(np.log2(L // GLA_SUB)) + 1
    tri, lmask = _gla_tables(L)
    tri = jnp.asarray(tri, BF16)
    lmask = jnp.asarray(lmask, F32)
    if not step:
        s0 = jnp.zeros((GRP, HEAD_DIM), F32)
        s0_spec = pl.BlockSpec((GRP, HEAD_DIM), lambda bi, j: (0, 0))
    else:
        s0_spec = pl.BlockSpec((GRP, HEAD_DIM), lambda bi, j: (bi, 0))
    cur = lambda g: pl.BlockSpec((td, GRP), lambda bi, j: (bi * nt + j, g))
    c2 = lambda bi, j: (0, 0)
    return pl.pallas_call(
        functools.partial(_gla_kernel, L=L, n_chunks=td // L, n_levels=n_levels, first_has_state=step),
        grid=(b, nt),
        in_specs=[cur(PDQ), cur(PDK), cur(PDV), cur(PLA), cur(PDR), s0_spec,
                  pl.BlockSpec(tri.shape, c2), pl.BlockSpec(lmask.shape, lambda bi, j: (0, 0, 0)),
                  pl.BlockSpec((GRP, GRP), c2), pl.BlockSpec((GRP, GRP), c2), pl.BlockSpec((1, GRP), c2)],
        out_specs=[pl.BlockSpec((td, GRP), lambda bi, j: (bi * nt + j, 0)),
                   pl.BlockSpec((GRP, HEAD_DIM), lambda bi, j: (bi, 0))],
        out_shape=[jax.ShapeDtypeStruct((n, GRP), BF16), jax.ShapeDtypeStruct((b * GRP, HEAD_DIM), F32)],
        scratch_shapes=[pltpu.VMEM((GRP, GRP), F32), pltpu.VMEM((td, GRP), F32)],
        compiler_params=_cparams(("arbitrary", "arbitrary")), name="gla",
    )(proj, proj, proj, proj, proj, s0, tri, lmask, lw["bdmask"], lw["hsum"], lw["gon"])


def _out_kernel(ya_ref, yb_ref, yc_ref, yd_ref, x_ref, wo_ref, g2_ref, wr_ref, br_ref, tri_ref,
                xo_ref, xn_ref, ri_ref, rf_ref, cnt_ref, *, tm, n_sub):
    @pl.when(pl.program_id(0) == 0)
    def _():
        cnt_ref[...] = jnp.zeros_like(cnt_ref)

    sub = tm // n_sub
    lane = lax.broadcasted_iota(I32, (sub, LANES), 1)
    big = np.int32(1 << 20)

    def project(s):
        rows = slice(s * sub, (s + 1) * sub)
        ycat = jnp.concatenate([ya_ref[rows, :], yb_ref[rows, :], yc_ref[rows, :], yd_ref[rows, :]], axis=1)
        x = x_ref[rows, :] + jnp.dot(ycat, wo_ref[...], preferred_element_type=F32)
        xo_ref[rows, :] = x
        xn = x * lax.rsqrt(jnp.mean(x * x, axis=-1, keepdims=True) + EPS) * g2_ref[...]
        xn_ref[rows, :] = _pack_halves(xn)
        both = jnp.dot(xn.astype(BF16), wr_ref[...], preferred_element_type=F32)
        return both[:, :LANES] + both[:, LANES:] + br_ref[...]

    def route(s, logits):
        rows = slice(s * sub, (s + 1) * sub)

        def first_max(mask):
            v = jnp.max(jnp.where(mask, logits, -jnp.inf), axis=-1, keepdims=True)
            idx = jnp.min(jnp.where(jnp.logical_and(mask, logits == v), lane, big), axis=-1, keepdims=True)
            return v, idx

        is_grp = jnp.logical_and(lane >= N_EXPERTS, lane < N_EXPERTS + N_GROUPS)
        gmax, gidx = first_max(is_grp)
        p_grp = 1.0 / jnp.sum(jnp.where(is_grp, jnp.exp(logits - gmax), 0.0), axis=-1, keepdims=True)
        grp = gidx - N_EXPERTS
        in_grp = (lane // PER_GROUP) == grp
        v1, i1 = first_max(in_grp)
        v2, i2 = first_max(jnp.logical_and(in_grp, lane != i1))
        e21 = jnp.exp(v2 - v1)
        gate1 = p_grp / (1.0 + e21)
        gate2 = p_grp * e21 / (1.0 + e21)

        oh1 = lane == i1
        oh2 = lane == i2
        both = jnp.logical_or(oh1, oh2)
        ones = jnp.where(both, 1.0, 0.0).astype(BF16)
        before = jnp.dot(tri_ref[...], ones, preferred_element_type=F32) + cnt_ref[...].astype(F32)
        rank1 = jnp.sum(jnp.where(oh1, before, 0.0), axis=-1, keepdims=True)
        rank2 = jnp.sum(jnp.where(oh2, before, 0.0), axis=-1, keepdims=True)
        cnt_ref[...] = cnt_ref[...] + jnp.sum(jnp.where(both, 1.0, 0.0), axis=0, keepdims=True).astype(I32)

        ri = jnp.where(lane == 0, i1, jnp.where(lane == 1, i2, jnp.where(lane == 2, rank1.astype(I32),
                                                                          jnp.where(lane == 3, rank2.astype(I32), 0))))
        ri_ref[rows, :] = ri
        rf_ref[rows, :] = jnp.where(lane == 0, gate1, jnp.where(lane == 1, gate2, 0.0))

    logits = project(0)
    for s in range(n_sub):
        nxt = project(s + 1) if s + 1 < n_sub else None
        route(s, logits)
        logits = nxt


def _out_proj(ya, yb, yc, yd, x, lw):
    n = x.shape[0]
    tm = min(1024, n)
    row = lambda i: (i, 0)
    const = lambda i: (0, 0)
    n_sub = 2 if tm >= 1024 else 1
    sub = tm // n_sub
    tri = jnp.asarray(np.tril(np.ones((sub, sub), np.float32), -1), BF16)
    consts = [lw["w_out"], lw["g2"], lw["wr"], lw["br"], tri]
    yspec = pl.BlockSpec((tm, GRP), row)
    return pl.pallas_call(
        functools.partial(_out_kernel, tm=tm, n_sub=n_sub),
        grid=(n // tm,),
        in_specs=[yspec, yspec, yspec, yspec, pl.BlockSpec((tm, D_MODEL), row)] + [pl.BlockSpec(c.shape, const) for c in consts],
        out_specs=[pl.BlockSpec((tm, D_MODEL), row), pl.BlockSpec((tm, D_MODEL // 2), row),
                   pl.BlockSpec((tm, LANES), row), pl.BlockSpec((tm, LANES), row), pl.BlockSpec((1, LANES), const)],
        out_shape=[jax.ShapeDtypeStruct((n, D_MODEL), F32), jax.ShapeDtypeStruct((n, D_MODEL // 2), U32),
                   jax.ShapeDtypeStruct((n, LANES), I32), jax.ShapeDtypeStruct((n, LANES), F32),
                   jax.ShapeDtypeStruct((1, LANES), I32)],
        compiler_params=_cparams(("arbitrary",)), name="out_proj_router",
    )(ya, yb, yc, yd, x, *consts)


def _sc_scatter_rows(x, idx, n_out):
    n, d = x.shape
    kk = idx.shape[0]
    per_w = n // SC_WORKERS
    win = min(SC_WIN, per_w)
    n_win = per_w // win
    assert n_win * win * SC_WORKERS == n
    mesh = plsc.VectorSubcoreMesh(core_axis_name="c", subcore_axis_name="s")

    @functools.partial(
        pl.kernel, mesh=mesh, out_type=jax.ShapeDtypeStruct((n_out, d), x.dtype),
        scratch_types=[pltpu.VMEM((kk, win), I32), pltpu.VMEM((win, d), x.dtype)],
        name="sc_scatter_rows")
    def k(x_hbm, idx_hbm, o_hbm, idx_v, rows_v):
        wid = lax.axis_index("s") * 2 + lax.axis_index("c")
        base = wid * per_w

        @pl.loop(0, n_win)
        def _(w):
            off = base + w * win
            pltpu.sync_copy(x_hbm.at[pl.ds(off, win)], rows_v)
            for j in range(kk):
                pltpu.sync_copy(idx_hbm.at[j, pl.ds(off, win)], idx_v.at[j])
                pltpu.sync_copy(rows_v, o_hbm.at[idx_v.at[j]])

    return k(x, idx)


def _sc_gather_rows(y, idx):
    _, d = y.shape
    kk, n = idx.shape
    per_w = n // SC_WORKERS
    win = min(SC_WIN, per_w)
    n_win = per_w // win
    assert n_win * win * SC_WORKERS == n
    mesh = plsc.VectorSubcoreMesh(core_axis_name="c", subcore_axis_name="s")

    @functools.partial(
        pl.kernel, mesh=mesh, out_type=jax.ShapeDtypeStruct((kk, n, d), y.dtype),
        scratch_types=[pltpu.VMEM((kk, win), I32), pltpu.VMEM((win, d), y.dtype)],
        name="sc_gather_rows")
    def k(y_hbm, idx_hbm, o_hbm, idx_v, rows_v):
        wid = lax.axis_index("s") * 2 + lax.axis_index("c")
        base = wid * per_w

        @pl.loop(0, n_win)
        def _(w):
            off = base + w * win
            for j in range(kk):
                pltpu.sync_copy(idx_hbm.at[j, pl.ds(off, win)], idx_v.at[j])
                pltpu.sync_copy(y_hbm.at[idx_v.at[j]], rows_v)
                pltpu.sync_copy(rows_v, o_hbm.at[j, pl.ds(off, win)])

    return k(y, idx)


def _moe_kernel(bexp_ref, nused_ref, x_ref, wg_ref, wu_ref, wd_ref, o_ref, wg_s, wu_s, wd_s, *, n_sub):
    i = pl.program_id(0)
    prev = bexp_ref[jnp.maximum(i - 1, 0)]
    fresh = jnp.logical_or(i == 0, bexp_ref[i] != prev)

    @pl.when(jnp.logical_and(fresh, i < nused_ref[0]))
    def _():
        wg_s[...] = wg_ref[...].astype(BF16)
        wu_s[...] = wu_ref[...].astype(BF16)
        wd_s[...] = wd_ref[...].astype(BF16)

    @pl.when(i < nused_ref[0])
    def _():
        half = D_MODEL // 2
        sub = x_ref.shape[0] // n_sub

        def up(s):
            w = x_ref[s * sub:(s + 1) * sub, :]
            x = jnp.concatenate([_unpack_hi(w).astype(BF16), _unpack_lo(w).astype(BF16)], axis=1)
            hg = jnp.dot(x, wg_s[...], preferred_element_type=F32)
            hu = jnp.dot(x, wu_s[...], preferred_element_type=F32)
            return (hg * _sigmoid(hg) * hu).astype(BF16)

        def down(s, h):
            o_ref[s * sub:(s + 1) * sub, :] = _pack_halves(jnp.dot(h, wd_s[...], preferred_element_type=F32))

        h = up(0)
        for s in range(n_sub):
            nxt = up(s + 1) if s + 1 < n_sub else None
            down(s, h)
            h = nxt


def _moe_experts(xs, blk_exp, n_used, w_gate, w_up, w_down, layer, bm):
    p = xs.shape[0]
    n_blocks = p // bm
    live = lambda i, be, nu: jnp.minimum(i, jnp.maximum(nu[0] - 1, 0))
    wspec = lambda shape: pl.BlockSpec((None, None) + shape, lambda i, be, nu: (layer, be[live(i, be, nu)], 0, 0))
    grid_spec = pltpu.PrefetchScalarGridSpec(
        num_scalar_prefetch=2, grid=(n_blocks,),
        in_specs=[pl.BlockSpec((bm, D_MODEL // 2), lambda i, be, nu: (live(i, be, nu), 0)),
                  wspec((D_MODEL, D_EXPERT)), wspec((D_MODEL, D_EXPERT)), wspec((D_EXPERT, D_MODEL))],
        out_specs=pl.BlockSpec((bm, D_MODEL // 2), lambda i, be, nu: (live(i, be, nu), 0)),
        scratch_shapes=[pltpu.VMEM((D_MODEL, D_EXPERT), BF16), pltpu.VMEM((D_MODEL, D_EXPERT), BF16),
                        pltpu.VMEM((D_EXPERT, D_MODEL), BF16)])
    return pl.pallas_call(
        functools.partial(_moe_kernel, n_sub=2 if bm >= 512 else 1),
        grid_spec=grid_spec, out_shape=jax.ShapeDtypeStruct((p, D_MODEL // 2), U32),
        compiler_params=_cparams(("arbitrary",)), name="moe_experts",
    )(blk_exp, n_used, xs, w_gate, w_up, w_down)


def _moe_block_rows(n):
    return 512 if n >= 16384 else 128


def _moe(xn_packed, route_i, counts, lw):
    n = xn_packed.shape[0]
    bm = _moe_block_rows(n)
    n_blocks = -(-(2 * n + N_EXPERTS * (bm - 1)) // bm)
    cnt = counts[0, :N_EXPERTS]
    padded = (cnt + bm - 1) // bm * bm
    pad_end = jnp.cumsum(padded)
    pad_start = pad_end - padded
    experts = jnp.arange(N_EXPERTS, dtype=I32)
    eid = route_i[:, 0:2].T
    start_of = jnp.sum(jnp.where(eid[:, :, None] == experts, pad_start, 0), axis=-1)
    dest = (start_of + route_i[:, 2:4].T).astype(I32)
    first_row = jnp.arange(n_blocks, dtype=I32) * bm
    blk_exp = jnp.minimum(jnp.sum((pad_end[None, :] <= first_row[:, None]).astype(I32), axis=1), N_EXPERTS - 1)
    n_used = (pad_end[-1:] // bm).astype(I32)
    xs = _sc_scatter_rows(xn_packed, dest, n_blocks * bm)
    ys = _moe_experts(xs, blk_exp, n_used, lw["e_w_gate"], lw["e_w_up"], lw["e_w_down"], lw["layer"], bm)
    return _sc_gather_rows(ys, dest)


def _combine_kernel(x_ref, y_ref, gate_ref, o_ref):
    half = D_MODEL // 2
    x = x_ref[...]
    g = gate_ref[...]
    g0 = g[:, 0:1]
    g1 = g[:, 1:2]
    w0 = y_ref[0]
    w1 = y_ref[1]
    o_ref[:, :half] = x[:, :half] + g0 * _unpack_hi(w0) + g1 * _unpack_hi(w1)
    o_ref[:, half:] = x[:, half:] + g0 * _unpack_lo(w0) + g1 * _unpack_lo(w1)


def _combine(x, y, gates):
    n = x.shape[0]
    tm = min(512, n)
    row = lambda i: (i, 0)
    return pl.pallas_call(
        _combine_kernel, grid=(n // tm,),
        in_specs=[pl.BlockSpec((tm, D_MODEL), row), pl.BlockSpec((2, tm, D_MODEL // 2), lambda i: (0, i, 0)),
                  pl.BlockSpec((tm, LANES), row)],
        out_specs=pl.BlockSpec((tm, D_MODEL), row), out_shape=jax.ShapeDtypeStruct((n, D_MODEL), F32),
        compiler_params=_cparams(("arbitrary",)), name="moe_combine",
    )(x, y, gates)


def _layer_weights(l, p):
    w_in = p["w_in"][l]
    cols = [w_in[:, i * GRP:(i + 1) * GRP] for i in range(11)]
    by_group = [None] * N_PROJ
    for ref_i, g in enumerate(_REF_GROUPS):
        if g is not None:
            by_group[g] = cols[ref_i]
    by_group[PGLU] = cols[5]
    w_dg = jnp.zeros((D_MODEL, LANES), F32).at[:, :GATE_RANK].set(w_in[:, 11 * GRP:])
    wg2 = jnp.zeros((LANES, GRP), F32).at[:GATE_RANK].set(p["d_wg2"][l])
    tile4 = lambda v: jnp.tile(v, HEADS)[None, :]
    hid = np.arange(GRP) // HEAD_DIM
    bd = (hid[:, None] == hid[None, :]).astype(np.float32)
    wr = jnp.zeros((D_MODEL, LANES), F32).at[:, :N_EXPERTS].set(p["r_expert_w"][l])
    wr = wr.at[:, N_EXPERTS:N_EXPERTS + N_GROUPS].set(p["r_group_w"][l])
    wr_hi = wr.astype(BF16)
    br = jnp.zeros((1, LANES), F32).at[0, :N_EXPERTS].set(p["r_expert_b"][l])
    br = br.at[0, N_EXPERTS:N_EXPERTS + N_GROUPS].set(p["r_group_b"][l])
    return {
        "g1": p["norm1_g"][l][None, :],
        "w_in": jnp.concatenate(by_group[:PLA], axis=1).astype(BF16),
        "w_cg": cols[6].astype(BF16),
        "w_dg": w_dg.astype(BF16),
        "wg2": wg2.astype(BF16),
        "bg": p["d_bg"][l][None, :],
        "gq": tile4(p["b_qnorm_g"][l]), "gk": tile4(p["b_knorm_g"][l]), "gav": p["a_vnorm_g"][l][None, :],
        "gon": tile4(p["d_onorm_g"][l]),
        "hsum": jnp.asarray(bd, BF16), "bdmask": jnp.asarray(bd, F32),
        "a_ws": p["a_ws"][l], "a_bs_rows": jnp.repeat(p["a_bs"][l].T, HEAD_DIM, axis=1),
        "b_rel": p["b_rel_bias"][l],
        "c_dw": p["c_dw"][l], "c_dw_b": p["c_dw_b"][l][None, :],
        "c_ln_g": p["c_ln_g"][l][None, :], "c_ln_b": p["c_ln_b"][l][None, :],
        "w_out": p["w_out"][l].astype(BF16),
        "g2": p["norm2_g"][l][None, :],
        "wr": jnp.concatenate([wr_hi, (wr - wr_hi.astype(F32)).astype(BF16)], axis=1), "br": br,
        "e_w_gate": p["e_w_gate"], "e_w_up": p["e_w_up"], "e_w_down": p["e_w_down"], "layer": l,
    }


def _mix_and_route(x, lw, b, t, pending, caches):
    step = caches is not None
    y_prev, gates_prev = pending if pending is not None else (None, None)
    x, proj, kv, a_v = _in_proj(x, lw, t, y_prev, gates_prev, emit_av=step)
    if not step:
        yb = _attention(proj, lw, b, t)
        yc, tail, ya = _conv_gmlp(proj, lw, b, t)
        yd, sf = _gla(proj, lw, b, t)
    else:
        ck, cv, cc, cs = caches
        yb = _attention(proj, lw, b, t, ck.reshape(b * B_WINDOW, GRP), cv.reshape(b * B_WINDOW, GRP))
        halo = jnp.pad(cc, ((0, 0), (HALO - C_BUF, 0), (0, 0))).reshape(b * HALO, GRP)
        yc, tail, ya = _conv_gmlp(proj, lw, b, t, halo)
        yd, sf = _gla(proj, lw, b, t, cs.reshape(b * GRP, HEAD_DIM))
        a_v = a_v.reshape(b, t, GRP)
    x2, xn_packed, route_i, route_f, counts = _out_proj(ya, yb, yc, yd, x, lw)
    y = _moe(xn_packed, route_i, counts, lw)
    keep = min(B_WINDOW, t)
    new_k = kv[0].reshape(b, keep, HEADS, HEAD_DIM)
    new_v = kv[1].reshape(b, keep, HEADS, HEAD_DIM)
    new_buf = tail.reshape(b, HALO, GRP)[:, HALO - C_BUF:]
    return x2, (y, route_f), (new_k, new_v, new_buf, sf.reshape(b, HEADS, HEAD_DIM, HEAD_DIM), a_v)


def kernel(x_prompt, x_sample, cache_b_k, cache_b_v, state_c_conv, state_d_gla, norm1_g, w_in, a_vnorm_g, a_ws, a_bs, b_qnorm_g, b_knorm_g, b_rel_bias, c_dw, c_dw_b, c_ln_g, c_ln_b, d_wg2, d_bg, d_onorm_g, w_out, norm2_g, r_group_w, r_group_b, r_expert_w, r_expert_b, e_w_gate, e_w_up, e_w_down):
    params = dict(norm1_g=norm1_g, w_in=w_in, a_vnorm_g=a_vnorm_g, a_ws=a_ws, a_bs=a_bs, b_qnorm_g=b_qnorm_g,
                  b_knorm_g=b_knorm_g, b_rel_bias=b_rel_bias, c_dw=c_dw, c_dw_b=c_dw_b, c_ln_g=c_ln_g, c_ln_b=c_ln_b,
                  d_wg2=d_wg2, d_bg=d_bg, d_onorm_g=d_onorm_g, w_out=w_out, norm2_g=norm2_g, r_group_w=r_group_w,
                  r_group_b=r_group_b, r_expert_w=r_expert_w, r_expert_b=r_expert_b, e_w_gate=e_w_gate,
                  e_w_up=e_w_up, e_w_down=e_w_down)
    depth = w_in.shape[0]
    bp, tp, _ = x_prompt.shape
    bs, ts, _ = x_sample.shape
    xp = x_prompt.reshape(bp * tp, D_MODEL)
    xs = x_sample.reshape(bs * ts, D_MODEL)
    pend_p = pend_s = None
    st_p, st_s = [], []
    for l in range(depth):
        lw = _layer_weights(l, params)
        xp, pend_p, sp = _mix_and_route(xp, lw, bp, tp, pend_p, None)
        xs, pend_s, ss = _mix_and_route(xs, lw, bs, ts, pend_s,
                                        (cache_b_k[l], cache_b_v[l], state_c_conv[l], state_d_gla[l]))
        st_p.append(sp)
        st_s.append(ss)
    yp = _combine(xp, pend_p[0], pend_p[1]).reshape(bp, tp, D_MODEL)
    ys = _combine(xs, pend_s[0], pend_s[1]).reshape(bs, ts, D_MODEL)
    stack = lambda sts, i: jnp.stack([s[i] for s in sts])
    return (yp, ys, stack(st_p, 0), stack(st_p, 1), stack(st_p, 2), stack(st_p, 3),
            stack(st_s, 0), stack(st_s, 1), stack(st_s, 2), stack(st_s, 3), stack(st_s, 4))
```

```python
import functools

import numpy as np
import jax
import jax.numpy as jnp
from jax import lax
from jax.experimental import pallas as pl
from jax.experimental.pallas import tpu as pltpu
from jax.experimental.pallas import tpu_sc as plsc

F32 = jnp.float32
BF16 = jnp.bfloat16
I32 = jnp.int32
U32 = jnp.uint32

D_MODEL = 1024
GRP = 256
HEADS = 4
HEAD_DIM = 64
CHUNK = 64
A_CHUNK = 128
B_WINDOW = 512
REL_CLIP = 128
C_WIDTH = 31
C_BUF = C_WIDTH - 1
HALO = 32
GATE_RANK = 16
GLA_TAU = 16.0
GLA_SUB = 16
N_GROUPS = 4
PER_GROUP = 8
N_EXPERTS = 32
D_EXPERT = 512
EPS = 1e-6
NEG_INF = -1e30
LANES = 128
VMEM_LIMIT = 48 * 1024 * 1024

PK, PV, PQ, PAU, PAV, PGLU, PDQ, PDK, PDV, PDR, PLA = range(11)
N_PROJ = 11
_REF_GROUPS = (PAU, PAV, PQ, PK, PV, None, None, PDQ, PDK, PDV, PDR)

SC_WORKERS = 32
SC_WIN = 128


def _cparams(sem):
    return pltpu.CompilerParams(dimension_semantics=sem, vmem_limit_bytes=VMEM_LIMIT)


def _sigmoid(x):
    return 1.0 / (1.0 + jnp.exp(-x))


def _gelu_tanh(x):
    c = np.float32(np.sqrt(2.0 / np.pi))
    return 0.5 * x * (1.0 + jnp.tanh(c * (x + np.float32(0.044715) * (x * x * x))))


def _pack_halves(y):
    half = y.shape[1] // 2
    hi = pltpu.bitcast(y[:, :half].astype(BF16).astype(F32), U32)
    lo = pltpu.bitcast(y[:, half:].astype(BF16).astype(F32), U32)
    return hi | (lo >> np.uint32(16))


def _unpack_hi(w):
    return pltpu.bitcast(w & np.uint32(0xFFFF0000), F32)


def _unpack_lo(w):
    return pltpu.bitcast(w << np.uint32(16), F32)


def _head_id(shape, axis, size):
    return lax.broadcasted_iota(I32, shape, axis) // size


def _bd_stack(x, rows):
    x4 = jnp.concatenate([x] * HEADS, axis=0)
    shape = (HEADS * rows, GRP)
    keep = _head_id(shape, 0, rows) == _head_id(shape, 1, HEAD_DIM)
    return jnp.where(keep, x4, jnp.zeros_like(x4))


def _bd_unstack(o, rows):
    lane_h = _head_id((rows, GRP), 1, HEAD_DIM)
    out = o[(HEADS - 1) * rows:HEADS * rows]
    for h in range(HEADS - 2, -1, -1):
        out = jnp.where(lane_h == h, o[h * rows:(h + 1) * rows], out)
    return out


def _head_meansq(o, hsum_ref):
    sq = (o * o).astype(BF16)
    return jnp.dot(sq, hsum_ref[...], preferred_element_type=F32) * np.float32(1.0 / HEAD_DIM)


def _in_kernel(*refs, combine, emit_av, tiles_per_stream):
    refs = list(refs)
    x_ref = refs.pop(0)
    if combine:
        y_ref = refs.pop(0)
        gate_ref = refs.pop(0)
    g1_ref, w_ref, wcg_ref, wdg_ref, wg2_ref, bg_ref, gq_ref, gk_ref, gav_ref, hsum_ref = refs[:10]
    refs = refs[12:]
    if combine:
        xo_ref = refs.pop(0)
    p_ref = refs.pop(0)
    kc_ref = refs.pop(0)
    vc_ref = refs.pop(0)
    if emit_av:
        av_ref = refs.pop(0)
    raw = refs.pop(0)
    x = x_ref[...]
    if combine:
        half = D_MODEL // 2
        g = gate_ref[...]
        g0 = g[:, 0:1]
        g1 = g[:, 1:2]
        w0 = y_ref[0]
        w1 = y_ref[1]
        xa = x[:, :half] + g0 * _unpack_hi(w0) + g1 * _unpack_hi(w1)
        xb = x[:, half:] + g0 * _unpack_lo(w0) + g1 * _unpack_lo(w1)
        xo_ref[:, :half] = xa
        xo_ref[:, half:] = xb
        x = jnp.concatenate([xa, xb], axis=1)
    rs = lax.rsqrt(jnp.mean(x * x, axis=-1, keepdims=True) + EPS)
    h = (x * g1_ref[...]).astype(BF16)

    n_slots = PLA + 2

    def matmul(slot):
        if slot < PLA:
            raw[:, slot * GRP:(slot + 1) * GRP] = jnp.dot(h, w_ref[:, slot * GRP:(slot + 1) * GRP],
                                                          preferred_element_type=F32)
        elif slot == PLA:
            raw[:, PLA * GRP:(PLA + 1) * GRP] = jnp.dot(h, wcg_ref[...], preferred_element_type=F32)
        else:
            raw[:, (PLA + 1) * GRP:] = jnp.dot(h, wdg_ref[...], preferred_element_type=F32)

    def proj(g):
        return raw[:, g * GRP:(g + 1) * GRP] * rs

    def put(g, val):
        p_ref[:, g * GRP:(g + 1) * GRP] = val.astype(BF16)

    def epilogue(slot):
        if slot == PK:
            r = proj(PK)
            put(PK, r * lax.rsqrt(_head_meansq(r, hsum_ref) + EPS) * gk_ref[...])
        elif slot == PV:
            put(PV, proj(PV))

            @pl.when(pl.program_id(0) % tiles_per_stream == tiles_per_stream - 1)
            def _():
                r = proj(PK)
                kc_ref[...] = r * lax.rsqrt(_head_meansq(r, hsum_ref) + EPS) * gk_ref[...]
                vc_ref[...] = proj(PV)
        elif slot == PQ:
            r = proj(PQ)
            put(PQ, r * lax.rsqrt(_head_meansq(r, hsum_ref) + EPS) * (gq_ref[...] * np.float32(HEAD_DIM ** -0.5)))
        elif slot == PAU:
            put(PAU, _gelu_tanh(proj(PAU)))
        elif slot == PAV:
            r = _gelu_tanh(proj(PAV))
            av = r * lax.rsqrt(jnp.mean(r * r, axis=-1, keepdims=True) + EPS) * gav_ref[...]
            put(PAV, av)
            if emit_av:
                av_ref[...] = av
        elif slot == PGLU:
            pass
        elif slot == PDQ:
            put(PDQ, proj(PDQ) * np.float32(HEAD_DIM ** -0.5))
        elif slot in (PDK, PDV):
            put(slot, proj(slot))
        elif slot == PDR:
            r = proj(PDR)
            put(PDR, r * _sigmoid(r))
        elif slot == PLA:
            put(PGLU, proj(PGLU) * _sigmoid(proj(PLA)))
        else:
            dg = raw[:, (PLA + 1) * GRP:] * rs
            z = jnp.dot(dg.astype(BF16), wg2_ref[...], preferred_element_type=F32) + bg_ref[...]
            logsig = jnp.minimum(z, 0.0) - jnp.log(1.0 + jnp.exp(-jnp.abs(z)))
            put(PLA, logsig * np.float32(1.0 / GLA_TAU))

    order = (PLA + 1, PK, PQ, PAV, PAU, PGLU, PLA, PDR, PV, PDQ, PDK, PDV)
    assert sorted(order) == list(range(n_slots))
    lag = 2
    for i in range(n_slots + lag):
        if i < n_slots:
            matmul(order[i])
        if i >= lag:
            epilogue(order[i - lag])


def _in_proj(x, lw, t, stacks, layer, y=None, gates=None, emit_av=False):
    n = x.shape[0]
    tm = min(512, n)
    combine = y is not None
    keep = min(B_WINDOW, t)
    tps = max(t // tm, 1)
    assert tps == 1 or keep == tm
    row = lambda i: (i, 0)
    const = lambda i: (0, 0)
    ins, specs = [x], [pl.BlockSpec((tm, D_MODEL), row)]
    if combine:
        ins += [y, gates]
        specs += [pl.BlockSpec((2, tm, D_MODEL // 2), lambda i: (0, i, 0)), pl.BlockSpec((tm, LANES), row)]
    consts = [lw["g1"], lw["w_in"], lw["w_cg"], lw["w_dg"], lw["wg2"], lw["bg"], lw["gq"], lw["gk"], lw["gav"], lw["hsum"]]
    ins += consts
    specs += [pl.BlockSpec(c.shape, const) for c in consts]
    k_stack, v_stack = stacks
    per_layer = (n // tps) // tm
    assert k_stack.shape[0] % (per_layer * tm) == 0
    ins += [k_stack, v_stack]
    specs += [pl.BlockSpec(memory_space=pl.ANY)] * 2
    aliases = {len(ins) - 2: 1 + int(combine), len(ins) - 1: 2 + int(combine)}
    newest = jax.ShapeDtypeStruct(k_stack.shape, F32)
    newest_spec = pl.BlockSpec((tm, GRP), lambda i: (layer * per_layer + i // tps, 0))
    out_shape = [jax.ShapeDtypeStruct((n, N_PROJ * GRP), BF16), newest, newest]
    out_specs = [pl.BlockSpec((tm, N_PROJ * GRP), row), newest_spec, newest_spec]
    if combine:
        out_shape = [jax.ShapeDtypeStruct((n, D_MODEL), F32)] + out_shape
        out_specs = [pl.BlockSpec((tm, D_MODEL), row)] + out_specs
    if emit_av:
        out_shape.append(jax.ShapeDtypeStruct((n, GRP), F32))
        out_specs.append(pl.BlockSpec((tm, GRP), row))
    outs = list(pl.pallas_call(
        functools.partial(_in_kernel, combine=combine, emit_av=emit_av, tiles_per_stream=tps),
        grid=(n // tm,), in_specs=specs, out_specs=out_specs, out_shape=out_shape,
        input_output_aliases=aliases,
        scratch_shapes=[pltpu.VMEM((tm, (PLA + 1) * GRP + LANES), F32)],
        compiler_params=_cparams(("arbitrary",)), name="in_proj",
    )(*ins))
    x_new = outs.pop(0) if combine else x
    proj, k_new, v_new = outs[0], outs[1], outs[2]
    return x_new, proj, (k_new, v_new), (outs[3] if emit_av else None)


def _gmlp_body(u_ref, v_ref, ws_ref, bs_ref, o_ref, chunk, n_chunks):
    lane_h = _head_id((chunk, GRP), 1, HEAD_DIM)
    ri = lax.broadcasted_iota(I32, (chunk, chunk), 0)
    ci = lax.broadcasted_iota(I32, (chunk, chunk), 1)
    wm = [jnp.where(ci <= ri, ws_ref[h], 0.0).astype(BF16) for h in range(HEADS)]
    for c in range(n_chunks):
        rows = slice(c * chunk, (c + 1) * chunk)
        v = v_ref[rows, :]
        sv = jnp.dot(wm[HEADS - 1], v, preferred_element_type=F32)
        for h in range(HEADS - 2, -1, -1):
            sv = jnp.where(lane_h == h, jnp.dot(wm[h], v, preferred_element_type=F32), sv)
        o_ref[rows, :] = (u_ref[rows, :].astype(F32) * (sv + bs_ref[...])).astype(BF16)


def _attn_kernel(q_ref, kc_ref, vc_ref, kp_ref, vp_ref, bias_ref, o_ref, kbuf, vbuf, *, chunk, n_chunks, first_has_past):
    tq = chunk * n_chunks
    win = B_WINDOW + chunk
    if first_has_past:
        for h in range(HEADS):
            lanes = slice(h * HEAD_DIM, (h + 1) * HEAD_DIM)
            kbuf[0:B_WINDOW, lanes] = kp_ref[:, h, :].astype(BF16)
            vbuf[0:B_WINDOW, lanes] = vp_ref[:, h, :].astype(BF16)
    else:
        kbuf[0:B_WINDOW, :] = kp_ref[...]
        vbuf[0:B_WINDOW, :] = vp_ref[...]
    kbuf[B_WINDOW:B_WINDOW + tq, :] = kc_ref[...]
    vbuf[B_WINDOW:B_WINDOW + tq, :] = vc_ref[...]
    col = lax.broadcasted_iota(I32, (HEADS * chunk, win), 1)

    def chunks(no_past):
        for c in range(n_chunks):
            q = q_ref[c * chunk:(c + 1) * chunk, :]
            kk = kbuf[c * chunk:c * chunk + win, :]
            vv = vbuf[c * chunk:c * chunk + win, :]
            s = lax.dot_general(_bd_stack(q, chunk), kk, (((1,), (1,)), ((), ())), preferred_element_type=F32)
            s = s + bias_ref[...]
            if no_past:
                s = jnp.where(col + c * chunk >= B_WINDOW, s, NEG_INF)
            m = jnp.max(s, axis=-1, keepdims=True)
            p = jnp.exp(s - m)
            l = jnp.sum(p, axis=-1, keepdims=True)
            o = jnp.dot(p.astype(BF16), vv, preferred_element_type=F32) * (1.0 / l)
            o_ref[c * chunk:(c + 1) * chunk, :] = _bd_unstack(o, chunk).astype(BF16)

    if first_has_past:
        chunks(False)
    else:
        pl.when(pl.program_id(1) == 0)(functools.partial(chunks, True))
        pl.when(pl.program_id(1) > 0)(functools.partial(chunks, False))


def _attention(proj, lw, b, t, cache_k=None, cache_v=None):
    n = proj.shape[0]
    step = cache_k is not None
    chunk = min(t, CHUNK)
    tq = min(t, B_WINDOW)
    nt = t // tq
    rel = lw["b_rel"]
    win = B_WINDOW + chunk
    lo = REL_CLIP - (chunk - 1)
    n_far = (chunk - 1) + win - (2 * REL_CLIP + 1 - lo)
    by_dist = jnp.concatenate([rel[:, lo:], jnp.broadcast_to(rel[:, -1:], (HEADS, n_far))], axis=1)
    by_key = by_dist[:, ::-1]
    n_k = chunk - 1 + win
    wrapped = jnp.tile(jnp.pad(by_key, ((0, 0), (0, 1))), (1, chunk))[:, :chunk * n_k].reshape(HEADS, chunk, n_k)
    bias = wrapped[:, :, chunk - 1:].astype(F32).reshape(HEADS * chunk, win)
    cur = lambda g: pl.BlockSpec((tq, GRP), lambda bi, j: (bi * nt + j, g))
    if step:
        layer = lw["layer"]
        prev_k = pl.BlockSpec((None, None, B_WINDOW, HEADS, HEAD_DIM), lambda bi, j: (layer, bi, 0, 0, 0))
        prev_v = prev_k
        pk_arr, pv_arr = cache_k, cache_v
    else:
        assert tq == B_WINDOW
        prev_k = pl.BlockSpec((B_WINDOW, GRP), lambda bi, j: (bi * nt + jnp.maximum(j - 1, 0), PK))
        prev_v = pl.BlockSpec((B_WINDOW, GRP), lambda bi, j: (bi * nt + jnp.maximum(j - 1, 0), PV))
        pk_arr, pv_arr = proj, proj
    return pl.pallas_call(
        functools.partial(_attn_kernel, chunk=chunk, n_chunks=tq // chunk, first_has_past=step),
        grid=(b, nt),
        in_specs=[cur(PQ), cur(PK), cur(PV), prev_k, prev_v, pl.BlockSpec(bias.shape, lambda bi, j: (0, 0))],
        out_specs=pl.BlockSpec((tq, GRP), lambda bi, j: (bi * nt + j, 0)),
        out_shape=jax.ShapeDtypeStruct((n, GRP), BF16),
        scratch_shapes=[pltpu.VMEM((B_WINDOW + tq, GRP), BF16), pltpu.VMEM((B_WINDOW + tq, GRP), BF16)],
        compiler_params=_cparams(("arbitrary", "arbitrary")), name="band_attn",
    )(proj, proj, proj, pk_arr, pv_arr, bias)


def _conv_kernel(g_ref, halo_ref, dw_ref, dwb_ref, lng_ref, lnb_ref, u_ref, v_ref, ws_ref, bs_ref,
                 o_ref, tail_ref, ya_ref, xp, zbuf, *, tc, sub, first_has_past, a_chunk):
    _gmlp_body(u_ref, v_ref, ws_ref, bs_ref, ya_ref, a_chunk, tc // a_chunk)
    halo = halo_ref[...].astype(F32)
    has_past = jnp.logical_or(pl.program_id(1) > 0, first_has_past)
    xp[0:HALO, :] = jnp.where(has_past, halo, 0.0)
    xp[HALO:HALO + tc, :] = g_ref[...].astype(F32)
    xp[HALO + tc:, :] = jnp.zeros((xp.shape[0] - HALO - tc, GRP), F32)

    @pl.when(pl.program_id(1) == pl.num_programs(1) - 1)
    def _():
        tail_ref[...] = xp[tc:tc + HALO, :]

    lead = HALO - C_BUF
    sl = 8
    for s in range(tc // sub):
        acc = None
        for r in range(sl):
            taps = [p for p in range(r, lead + C_WIDTH, sl) if p >= lead]
            z = None
            for p in taps:
                a0 = s * sub + p - r
                term = dw_ref[p - lead:p - lead + 1, :] * xp[a0:a0 + sub + sl, :]
                z = term if z is None else z + term
            zbuf[r] = z
            part = zbuf[r, r:r + sub, :]
            acc = part if acc is None else acc + part
        y = acc + dwb_ref[...]
        mu = jnp.mean(y, axis=-1, keepdims=True)
        yc = y - mu
        y = yc * lax.rsqrt(jnp.mean(yc * yc, axis=-1, keepdims=True) + EPS) * lng_ref[...] + lnb_ref[...]
        o_ref[s * sub:(s + 1) * sub, :] = (y * _sigmoid(y)).astype(BF16)


def _conv_gmlp(proj, lw, b, t, state=None):
    n = proj.shape[0]
    step = state is not None
    tc = min(t, 512)
    nt = t // tc
    sub = min(tc, 64)
    a_chunk = min(t, A_CHUNK)
    ws = lw["a_ws"][:, :a_chunk, :a_chunk]
    bs = lw["a_bs_rows"][:a_chunk]
    if step:
        halo_arr = state
        halo_spec = pl.BlockSpec((HALO, GRP), lambda bi, j: (bi, 0))
    else:
        per = tc // HALO
        halo_arr = proj
        halo_spec = pl.BlockSpec((HALO, GRP), lambda bi, j: (jnp.maximum((bi * nt + j) * per - 1, 0), PGLU))
    vec = pl.BlockSpec((1, GRP), lambda bi, j: (0, 0))
    cur = lambda g: pl.BlockSpec((tc, GRP), lambda bi, j: (bi * nt + j, g))
    return pl.pallas_call(
        functools.partial(_conv_kernel, tc=tc, sub=sub, first_has_past=step, a_chunk=a_chunk),
        grid=(b, nt),
        in_specs=[cur(PGLU), halo_spec, pl.BlockSpec((C_WIDTH, GRP), lambda bi, j: (0, 0)), vec, vec, vec,
                  cur(PAU), cur(PAV), pl.BlockSpec(ws.shape, lambda bi, j: (0, 0, 0)),
                  pl.BlockSpec(bs.shape, lambda bi, j: (0, 0))],
        out_specs=[cur(0), pl.BlockSpec((HALO, GRP), lambda bi, j: (bi, 0)), cur(0)],
        out_shape=[jax.ShapeDtypeStruct((n, GRP), BF16), jax.ShapeDtypeStruct((b * HALO, GRP), F32),
                   jax.ShapeDtypeStruct((n, GRP), BF16)],
        scratch_shapes=[pltpu.VMEM((HALO + tc + 8, GRP), F32), pltpu.VMEM((8, sub + 8, GRP), F32)],
        compiler_params=_cparams(("arbitrary", "arbitrary")), name="conv_gmlp",
    )(proj, halo_arr, lw["c_dw"], lw["c_dw_b"], lw["c_ln_g"], lw["c_ln_b"], proj, proj, ws, bs)


def _gla_tables(L):
    i = np.arange(L)[:, None]
    t = np.arange(L)[None, :]
    masks = []
    s = GLA_SUB
    masks.append(((i // s) == (t // s)) & (t <= i))
    s *= 2
    while s <= L:
        h = s // 2
        masks.append(((i // s) == (t // s)) & (i % s >= h) & (t % s < h))
        s *= 2
    tri = (t <= i).astype(np.float32)
    mask = np.stack([np.tile(m.astype(np.float32), (1, HEADS)) for m in masks], axis=0)
    return tri, mask


def _gla_anchor(cum, row, L, size, first_half):
    out = None
    for start in range(0, L, size):
        ar = start + size // 2 - 1 if first_half else start - 1
        val = jnp.zeros((L, GRP), F32) if ar < 0 else jnp.broadcast_to(cum[ar:ar + 1, :], (L, GRP))
        out = val if out is None else jnp.where(row >= start, val, out)
    return out


def _gla_kernel(q_ref, k_ref, v_ref, la_ref, dr_ref, s0_ref, tri_ref, lmask_ref, bdmask_ref, hsum_ref, gon_ref,
                o_ref, sf_ref, st, o_all, *, L, n_chunks, n_levels, first_has_state):
    j = pl.program_id(1)

    @pl.when(j == 0)
    def _():
        st[...] = jnp.zeros_like(st)
        if first_has_state:
            for h in range(HEADS):
                blk = slice(h * HEAD_DIM, (h + 1) * HEAD_DIM)
                st[blk, blk] = s0_ref[blk, :].T

    row = lax.broadcasted_iota(I32, (L, GRP), 0)
    dn_t = (((1,), (1,)), ((), ()))

    def prep(c):
        rows = slice(c * L, (c + 1) * L)
        q = q_ref[rows, :].astype(F32)
        k = k_ref[rows, :].astype(F32)
        v = v_ref[rows, :]
        cum = jnp.dot(tri_ref[...], la_ref[rows, :], preferred_element_type=F32)
        total = cum[L - 1:L, :]
        pairs = []
        for lvl in range(n_levels):
            size = GLA_SUB << lvl
            if lvl == 0:
                local = cum - _gla_anchor(cum, row, L, size, False)
                ql = q * jnp.exp(local)
                kl = k * jnp.exp(-local)
            else:
                upper = (row & (size - 1)) >= (size // 2)
                d = cum - _gla_anchor(cum, row, L, size, True)
                w = jnp.exp(jnp.where(upper, d, -d))
                ql = jnp.where(upper, q * w, 0.0)
                kl = jnp.where(upper, 0.0, k * w)
            pairs.append((ql.astype(BF16), _bd_stack(kl.astype(BF16), L)))
        return dict(rows=rows, v=v, qp=(q * jnp.exp(cum)).astype(BF16), kst=(k * jnp.exp(total - cum)).astype(BF16),
                    decay=jnp.exp(total), pairs=pairs)

    def intra(p):
        att = None
        for lvl, (ql, kbd) in enumerate(p["pairs"]):
            a = lax.dot_general(ql, kbd, dn_t, preferred_element_type=F32) * lmask_ref[lvl]
            att = a if att is None else att + a
        p["o_intra"] = jnp.dot(att.astype(BF16), _bd_stack(p["v"], L), preferred_element_type=F32)
        p["upd"] = lax.dot_general(p["v"], p["kst"], (((0,), (0,)), ((), ())),
                                   preferred_element_type=F32) * bdmask_ref[...]
        return p

    def finish(p, s_t):
        o_all[p["rows"], :] = lax.dot_general(p["qp"], s_t.astype(BF16), dn_t, preferred_element_type=F32) + p["o_intra"]
        return s_t * p["decay"] + p["upd"]

    s_t = st[...]
    stage1, stage2 = {}, {}
    for step in range(n_chunks + 2):
        if step < n_chunks:
            stage1[step] = prep(step)
        if 0 <= step - 1 < n_chunks:
            stage2[step - 1] = intra(stage1.pop(step - 1))
        if 0 <= step - 2 < n_chunks:
            s_t = finish(stage2.pop(step - 2), s_t)
    st[...] = s_t
    o = o_all[...]
    y = o * lax.rsqrt(_head_meansq(o, hsum_ref) + EPS) * gon_ref[...] * dr_ref[...].astype(F32)
    o_ref[...] = y.astype(BF16)

    @pl.when(j == pl.num_programs(1) - 1)
    def _():
        for h in range(HEADS):
            blk = slice(h * HEAD_DIM, (h + 1) * HEAD_DIM)
            sf_ref[blk, :] = st[blk, blk].T


def _gla(proj, lw, b, t, s0=None):
    n = proj.shape[0]
    step = s0 is not None
    L = min(t, 64)
    td = min(t, 512)
    nt = t // td
    n_levels = int(np.log2(L // GLA_SUB)) + 1
    tri, lmask = _gla_tables(L)
    tri = jnp.asarray(tri, BF16)
    lmask = jnp.asarray(lmask, F32)
    if not step:
        s0 = jnp.zeros((GRP, HEAD_DIM), F32)
        s0_spec = pl.BlockSpec((GRP, HEAD_DIM), lambda bi, j: (0, 0))
    else:
        s0_spec = pl.BlockSpec((GRP, HEAD_DIM), lambda bi, j: (bi, 0))
    cur = lambda g: pl.BlockSpec((td, GRP), lambda bi, j: (bi * nt + j, g))
    c2 = lambda bi, j: (0, 0)
    return pl.pallas_call(
        functools.partial(_gla_kernel, L=L, n_chunks=td // L, n_levels=n_levels, first_has_state=step),
        grid=(b, nt),
        in_specs=[cur(PDQ), cur(PDK), cur(PDV), cur(PLA), cur(PDR), s0_spec,
                  pl.BlockSpec(tri.shape, c2), pl.BlockSpec(lmask.shape, lambda bi, j: (0, 0, 0)),
                  pl.BlockSpec((GRP, GRP), c2), pl.BlockSpec((GRP, GRP), c2), pl.BlockSpec((1, GRP), c2)],
        out_specs=[pl.BlockSpec((td, GRP), lambda bi, j: (bi * nt + j, 0)),
                   pl.BlockSpec((GRP, HEAD_DIM), lambda bi, j: (bi, 0))],
        out_shape=[jax.ShapeDtypeStruct((n, GRP), BF16), jax.ShapeDtypeStruct((b * GRP, HEAD_DIM), F32)],
        scratch_shapes=[pltpu.VMEM((GRP, GRP), F32), pltpu.VMEM((td, GRP), F32)],
        compiler_params=_cparams(("arbitrary", "arbitrary")), name="gla",
    )(proj, proj, proj, proj, proj, s0, tri, lmask, lw["bdmask"], lw["hsum"], lw["gon"])


def _out_kernel(ya_ref, yb_ref, yc_ref, yd_ref, x_ref, wo_ref, g2_ref, wr_ref, br_ref, tri_ref,
                xo_ref, xn_ref, ri_ref, rf_ref, cnt_ref, *, tm, n_sub):
    @pl.when(pl.program_id(0) == 0)
    def _():
        cnt_ref[...] = jnp.zeros_like(cnt_ref)

    sub = tm // n_sub
    lane = lax.broadcasted_iota(I32, (sub, LANES), 1)
    big = np.int32(1 << 20)

    def project(s):
        rows = slice(s * sub, (s + 1) * sub)
        ycat = jnp.concatenate([ya_ref[rows, :], yb_ref[rows, :], yc_ref[rows, :], yd_ref[rows, :]], axis=1)
        x = x_ref[rows, :] + jnp.dot(ycat, wo_ref[...], preferred_element_type=F32)
        xo_ref[rows, :] = x
        xn = x * lax.rsqrt(jnp.mean(x * x, axis=-1, keepdims=True) + EPS) * g2_ref[...]
        xn_ref[rows, :] = _pack_halves(xn)
        both = jnp.dot(xn.astype(BF16), wr_ref[...], preferred_element_type=F32)
        return both[:, :LANES] + both[:, LANES:] + br_ref[...]

    def route(s, logits):
        rows = slice(s * sub, (s + 1) * sub)

        def first_max(mask):
            v = jnp.max(jnp.where(mask, logits, -jnp.inf), axis=-1, keepdims=True)
            idx = jnp.min(jnp.where(jnp.logical_and(mask, logits == v), lane, big), axis=-1, keepdims=True)
            return v, idx

        is_grp = jnp.logical_and(lane >= N_EXPERTS, lane < N_EXPERTS + N_GROUPS)
        gmax, gidx = first_max(is_grp)
        p_grp = 1.0 / jnp.sum(jnp.where(is_grp, jnp.exp(logits - gmax), 0.0), axis=-1, keepdims=True)
        grp = gidx - N_EXPERTS
        in_grp = (lane // PER_GROUP) == grp
        v1, i1 = first_max(in_grp)
        v2, i2 = first_max(jnp.logical_and(in_grp, lane != i1))
        e21 = jnp.exp(v2 - v1)
        gate1 = p_grp / (1.0 + e21)
        gate2 = p_grp * e21 / (1.0 + e21)

        oh1 = lane == i1
        oh2 = lane == i2
        both = jnp.logical_or(oh1, oh2)
        ones = jnp.where(both, 1.0, 0.0).astype(BF16)
        before = jnp.dot(tri_ref[...], ones, preferred_element_type=F32) + cnt_ref[...].astype(F32)
        rank1 = jnp.sum(jnp.where(oh1, before, 0.0), axis=-1, keepdims=True)
        rank2 = jnp.sum(jnp.where(oh2, before, 0.0), axis=-1, keepdims=True)
        cnt_ref[...] = cnt_ref[...] + jnp.sum(jnp.where(both, 1.0, 0.0), axis=0, keepdims=True).astype(I32)

        ri = jnp.where(lane == 0, i1, jnp.where(lane == 1, i2, jnp.where(lane == 2, rank1.astype(I32),
                                                                          jnp.where(lane == 3, rank2.astype(I32), 0))))
        ri_ref[rows, :] = ri
        rf_ref[rows, :] = jnp.where(lane == 0, gate1, jnp.where(lane == 1, gate2, 0.0))

    logits = project(0)
    for s in range(n_sub):
        nxt = project(s + 1) if s + 1 < n_sub else None
        route(s, logits)
        logits = nxt


def _out_proj(ya, yb, yc, yd, x, lw):
    n = x.shape[0]
    tm = min(1024, n)
    row = lambda i: (i, 0)
    const = lambda i: (0, 0)
    n_sub = 2 if tm >= 1024 else 1
    sub = tm // n_sub
    tri = jnp.asarray(np.tril(np.ones((sub, sub), np.float32), -1), BF16)
    consts = [lw["w_out"], lw["g2"], lw["wr"], lw["br"], tri]
    yspec = pl.BlockSpec((tm, GRP), row)
    return pl.pallas_call(
        functools.partial(_out_kernel, tm=tm, n_sub=n_sub),
        grid=(n // tm,),
        in_specs=[yspec, yspec, yspec, yspec, pl.BlockSpec((tm, D_MODEL), row)] + [pl.BlockSpec(c.shape, const) for c in consts],
        out_specs=[pl.BlockSpec((tm, D_MODEL), row), pl.BlockSpec((tm, D_MODEL // 2), row),
                   pl.BlockSpec((tm, LANES), row), pl.BlockSpec((tm, LANES), row), pl.BlockSpec((1, LANES), const)],
        out_shape=[jax.ShapeDtypeStruct((n, D_MODEL), F32), jax.ShapeDtypeStruct((n, D_MODEL // 2), U32),
                   jax.ShapeDtypeStruct((n, LANES), I32), jax.ShapeDtypeStruct((n, LANES), F32),
                   jax.ShapeDtypeStruct((1, LANES), I32)],
        compiler_params=_cparams(("arbitrary",)), name="out_proj_router",
    )(ya, yb, yc, yd, x, *consts)


def _sc_scatter_rows(x, idx, n_out):
    n, d = x.shape
    kk = idx.shape[0]
    per_w = n // SC_WORKERS
    win = min(SC_WIN, per_w)
    n_win = per_w // win
    assert n_win * win * SC_WORKERS == n
    mesh = plsc.VectorSubcoreMesh(core_axis_name="c", subcore_axis_name="s")

    @functools.partial(
        pl.kernel, mesh=mesh, out_type=jax.ShapeDtypeStruct((n_out, d), x.dtype),
        scratch_types=[pltpu.VMEM((kk, win), I32), pltpu.VMEM((win, d), x.dtype)],
        name="sc_scatter_rows")
    def k(x_hbm, idx_hbm, o_hbm, idx_v, rows_v):
        wid = lax.axis_index("s") * 2 + lax.axis_index("c")
        base = wid * per_w

        @pl.loop(0, n_win)
        def _(w):
            off = base + w * win
            pltpu.sync_copy(x_hbm.at[pl.ds(off, win)], rows_v)
            for j in range(kk):
                pltpu.sync_copy(idx_hbm.at[j, pl.ds(off, win)], idx_v.at[j])
                pltpu.sync_copy(rows_v, o_hbm.at[idx_v.at[j]])

    return k(x, idx)


def _sc_gather_rows(y, idx):
    _, d = y.shape
    kk, n = idx.shape
    per_w = n // SC_WORKERS
    win = min(SC_WIN, per_w)
    n_win = per_w // win
    assert n_win * win * SC_WORKERS == n
    mesh = plsc.VectorSubcoreMesh(core_axis_name="c", subcore_axis_name="s")

    @functools.partial(
        pl.kernel, mesh=mesh, out_type=jax.ShapeDtypeStruct((kk, n, d), y.dtype),
        scratch_types=[pltpu.VMEM((kk, win), I32), pltpu.VMEM((win, d), y.dtype)],
        name="sc_gather_rows")
    def k(y_hbm, idx_hbm, o_hbm, idx_v, rows_v):
        wid = lax.axis_index("s") * 2 + lax.axis_index("c")
        base = wid * per_w

        @pl.loop(0, n_win)
        def _(w):
            off = base + w * win
            for j in range(kk):
                pltpu.sync_copy(idx_hbm.at[j, pl.ds(off, win)], idx_v.at[j])
                pltpu.sync_copy(y_hbm.at[idx_v.at[j]], rows_v)
                pltpu.sync_copy(rows_v, o_hbm.at[j, pl.ds(off, win)])

    return k(y, idx)


def _moe_kernel(bexp_ref, nused_ref, x_ref, wg_ref, wu_ref, wd_ref, o_ref, *, n_sub):
    del bexp_ref

    @pl.when(pl.program_id(0) < nused_ref[0])
    def _():
        sub = x_ref.shape[0] // n_sub

        def up(s):
            w = x_ref[s * sub:(s + 1) * sub, :]
            x = jnp.concatenate([_unpack_hi(w).astype(BF16), _unpack_lo(w).astype(BF16)], axis=1)
            hg = jnp.dot(x, wg_ref[...], preferred_element_type=F32)
            hu = jnp.dot(x, wu_ref[...], preferred_element_type=F32)
            return (hg * _sigmoid(hg) * hu).astype(BF16)

        def down(s, h):
            o_ref[s * sub:(s + 1) * sub, :] = _pack_halves(jnp.dot(h, wd_ref[...], preferred_element_type=F32))

        h = up(0)
        for s in range(n_sub):
            nxt = up(s + 1) if s + 1 < n_sub else None
            down(s, h)
            h = nxt


def _moe_experts(xs, blk_exp, n_used, w_gate, w_up, w_down, layer, bm):
    p = xs.shape[0]
    n_blocks = p // bm
    live = lambda i, be, nu: jnp.minimum(i, jnp.maximum(nu[0] - 1, 0))
    wspec = lambda shape: pl.BlockSpec((None, None) + shape, lambda i, be, nu: (layer, be[live(i, be, nu)], 0, 0))
    grid_spec = pltpu.PrefetchScalarGridSpec(
        num_scalar_prefetch=2, grid=(n_blocks,),
        in_specs=[pl.BlockSpec((bm, D_MODEL // 2), lambda i, be, nu: (live(i, be, nu), 0)),
                  wspec((D_MODEL, D_EXPERT)), wspec((D_MODEL, D_EXPERT)), wspec((D_EXPERT, D_MODEL))],
        out_specs=pl.BlockSpec((bm, D_MODEL // 2), lambda i, be, nu: (live(i, be, nu), 0)))
    return pl.pallas_call(
        functools.partial(_moe_kernel, n_sub=2 if bm >= 512 else 1),
        grid_spec=grid_spec, out_shape=jax.ShapeDtypeStruct((p, D_MODEL // 2), U32),
        compiler_params=_cparams(("arbitrary",)), name="moe_experts",
    )(blk_exp, n_used, xs, w_gate, w_up, w_down)


def _moe_block_rows(n):
    return 512 if n >= 16384 else 128


def _moe(xn_packed, route_i, counts, lw):
    n = xn_packed.shape[0]
    bm = _moe_block_rows(n)
    n_blocks = -(-(2 * n + N_EXPERTS * (bm - 1)) // bm)
    cnt = counts[0, :N_EXPERTS]
    padded = (cnt + bm - 1) // bm * bm
    pad_end = jnp.cumsum(padded)
    pad_start = pad_end - padded
    experts = jnp.arange(N_EXPERTS, dtype=I32)
    eid = route_i[:, 0:2].T
    start_of = jnp.sum(jnp.where(eid[:, :, None] == experts, pad_start, 0), axis=-1)
    dest = (start_of + route_i[:, 2:4].T).astype(I32)
    first_row = jnp.arange(n_blocks, dtype=I32) * bm
    blk_exp = jnp.minimum(jnp.sum((pad_end[None, :] <= first_row[:, None]).astype(I32), axis=1), N_EXPERTS - 1)
    n_used = (pad_end[-1:] // bm).astype(I32)
    xs = _sc_scatter_rows(xn_packed, dest, n_blocks * bm)
    ys = _moe_experts(xs, blk_exp, n_used, lw["e_w_gate"], lw["e_w_up"], lw["e_w_down"], lw["layer"], bm)
    return _sc_gather_rows(ys, dest)


def _combine_kernel(x_ref, y_ref, gate_ref, o_ref):
    half = D_MODEL // 2
    x = x_ref[...]
    g = gate_ref[...]
    g0 = g[:, 0:1]
    g1 = g[:, 1:2]
    w0 = y_ref[0]
    w1 = y_ref[1]
    o_ref[:, :half] = x[:, :half] + g0 * _unpack_hi(w0) + g1 * _unpack_hi(w1)
    o_ref[:, half:] = x[:, half:] + g0 * _unpack_lo(w0) + g1 * _unpack_lo(w1)


def _combine(x, y, gates):
    n = x.shape[0]
    tm = min(512, n)
    row = lambda i: (i, 0)
    return pl.pallas_call(
        _combine_kernel, grid=(n // tm,),
        in_specs=[pl.BlockSpec((tm, D_MODEL), row), pl.BlockSpec((2, tm, D_MODEL // 2), lambda i: (0, i, 0)),
                  pl.BlockSpec((tm, LANES), row)],
        out_specs=pl.BlockSpec((tm, D_MODEL), row), out_shape=jax.ShapeDtypeStruct((n, D_MODEL), F32),
        compiler_params=_cparams(("arbitrary",)), name="moe_combine",
    )(x, y, gates)


def _layer_weights(l, p):
    w_in = p["w_in"][l]
    cols = [w_in[:, i * GRP:(i + 1) * GRP] for i in range(11)]
    by_group = [None] * N_PROJ
    for ref_i, g in enumerate(_REF_GROUPS):
        if g is not None:
            by_group[g] = cols[ref_i]
    by_group[PGLU] = cols[5]
    w_dg = jnp.zeros((D_MODEL, LANES), F32).at[:, :GATE_RANK].set(w_in[:, 11 * GRP:])
    wg2 = jnp.zeros((LANES, GRP), F32).at[:GATE_RANK].set(p["d_wg2"][l])
    tile4 = lambda v: jnp.tile(v, HEADS)[None, :]
    hid = np.arange(GRP) // HEAD_DIM
    bd = (hid[:, None] == hid[None, :]).astype(np.float32)
    wr = jnp.zeros((D_MODEL, LANES), F32).at[:, :N_EXPERTS].set(p["r_expert_w"][l])
    wr = wr.at[:, N_EXPERTS:N_EXPERTS + N_GROUPS].set(p["r_group_w"][l])
    wr_hi = wr.astype(BF16)
    br = jnp.zeros((1, LANES), F32).at[0, :N_EXPERTS].set(p["r_expert_b"][l])
    br = br.at[0, N_EXPERTS:N_EXPERTS + N_GROUPS].set(p["r_group_b"][l])
    return {
        "g1": p["norm1_g"][l][None, :],
        "w_in": jnp.concatenate(by_group[:PLA], axis=1).astype(BF16),
        "w_cg": cols[6].astype(BF16),
        "w_dg": w_dg.astype(BF16),
        "wg2": wg2.astype(BF16),
        "bg": p["d_bg"][l][None, :],
        "gq": tile4(p["b_qnorm_g"][l]), "gk": tile4(p["b_knorm_g"][l]), "gav": p["a_vnorm_g"][l][None, :],
        "gon": tile4(p["d_onorm_g"][l]),
        "hsum": jnp.asarray(bd, BF16), "bdmask": jnp.asarray(bd, F32),
        "a_ws": p["a_ws"][l], "a_bs_rows": jnp.repeat(p["a_bs"][l].T, HEAD_DIM, axis=1),
        "b_rel": p["b_rel_bias"][l],
        "c_dw": p["c_dw"][l], "c_dw_b": p["c_dw_b"][l][None, :],
        "c_ln_g": p["c_ln_g"][l][None, :], "c_ln_b": p["c_ln_b"][l][None, :],
        "w_out": p["w_out"][l].astype(BF16),
        "g2": p["norm2_g"][l][None, :],
        "wr": jnp.concatenate([wr_hi, (wr - wr_hi.astype(F32)).astype(BF16)], axis=1), "br": br,
        "e_w_gate": p["e_w_gate"], "e_w_up": p["e_w_up"], "e_w_down": p["e_w_down"], "layer": l,
    }


def _mix_and_route(x, lw, b, t, pending, caches, kv_stacks):
    step = caches is not None
    y_prev, gates_prev = pending if pending is not None else (None, None)
    x, proj, kv_stacks, a_v = _in_proj(x, lw, t, kv_stacks, lw["layer"], y_prev, gates_prev, emit_av=step)
    if not step:
        yb = _attention(proj, lw, b, t)
        yc, tail, ya = _conv_gmlp(proj, lw, b, t)
        yd, sf = _gla(proj, lw, b, t)
    else:
        ck, cv, cc, cs = caches
        yb = _attention(proj, lw, b, t, ck, cv)
        halo = jnp.pad(cc, ((0, 0), (HALO - C_BUF, 0), (0, 0))).reshape(b * HALO, GRP)
        yc, tail, ya = _conv_gmlp(proj, lw, b, t, halo)
        yd, sf = _gla(proj, lw, b, t, cs.reshape(b * GRP, HEAD_DIM))
        a_v = a_v.reshape(b, t, GRP)
    x2, xn_packed, route_i, route_f, counts = _out_proj(ya, yb, yc, yd, x, lw)
    y = _moe(xn_packed, route_i, counts, lw)
    new_buf = tail.reshape(b, HALO, GRP)[:, HALO - C_BUF:]
    return x2, (y, route_f), kv_stacks, (new_buf, sf.reshape(b, HEADS, HEAD_DIM, HEAD_DIM), a_v)


def kernel(x_prompt, x_sample, cache_b_k, cache_b_v, state_c_conv, state_d_gla, norm1_g, w_in, a_vnorm_g, a_ws, a_bs, b_qnorm_g, b_knorm_g, b_rel_bias, c_dw, c_dw_b, c_ln_g, c_ln_b, d_wg2, d_bg, d_onorm_g, w_out, norm2_g, r_group_w, r_group_b, r_expert_w, r_expert_b, e_w_gate, e_w_up, e_w_down):
    params = dict(norm1_g=norm1_g, w_in=w_in, a_vnorm_g=a_vnorm_g, a_ws=a_ws, a_bs=a_bs, b_qnorm_g=b_qnorm_g,
                  b_knorm_g=b_knorm_g, b_rel_bias=b_rel_bias, c_dw=c_dw, c_dw_b=c_dw_b, c_ln_g=c_ln_g, c_ln_b=c_ln_b,
                  d_wg2=d_wg2, d_bg=d_bg, d_onorm_g=d_onorm_g, w_out=w_out, norm2_g=norm2_g, r_group_w=r_group_w,
                  r_group_b=r_group_b, r_expert_w=r_expert_w, r_expert_b=r_expert_b, e_w_gate=e_w_gate,
                  e_w_up=e_w_up, e_w_down=e_w_down)
    for name in ("e_w_gate", "e_w_up", "e_w_down"):
        params[name] = params[name].astype(BF16)
    depth = w_in.shape[0]
    bp, tp, _ = x_prompt.shape
    bs, ts, _ = x_sample.shape
    xp = x_prompt.reshape(bp * tp, D_MODEL)
    xs = x_sample.reshape(bs * ts, D_MODEL)
    pend_p = pend_s = None
    st_p, st_s = [], []
    keep_p, keep_s = min(B_WINDOW, tp), min(B_WINDOW, ts)
    newest = lambda b, keep: tuple(jnp.zeros((depth * b * keep, GRP), F32) for _ in range(2))
    kv_p = newest(bp, keep_p)
    kv_s = newest(bs, keep_s)
    for l in range(depth):
        lw = _layer_weights(l, params)
        xp, pend_p, kv_p, sp = _mix_and_route(xp, lw, bp, tp, pend_p, None, kv_p)
        xs, pend_s, kv_s, ss = _mix_and_route(xs, lw, bs, ts, pend_s,
                                              (cache_b_k, cache_b_v, state_c_conv[l], state_d_gla[l]), kv_s)
        st_p.append(sp)
        st_s.append(ss)
    yp = _combine(xp, pend_p[0], pend_p[1]).reshape(bp, tp, D_MODEL)
    ys = _combine(xs, pend_s[0], pend_s[1]).reshape(bs, ts, D_MODEL)
    stack = lambda sts, i: jnp.stack([s[i] for s in sts])
    heads = lambda arr, b, keep: arr.reshape(depth, b, keep, HEADS, HEAD_DIM)
    return (yp, ys, heads(kv_p[0], bp, keep_p), heads(kv_p[1], bp, keep_p), stack(st_p, 0), stack(st_p, 1),
            heads(kv_s[0], bs, keep_s), heads(kv_s[1], bs, keep_s), stack(st_s, 0), stack(st_s, 1), stack(st_s, 2))
```

```python
import functools

import numpy as np
import jax
import jax.numpy as jnp
from jax import lax
from jax.experimental import pallas as pl
from jax.experimental.pallas import tpu as pltpu
from jax.experimental.pallas import tpu_sc as plsc

F32 = jnp.float32
BF16 = jnp.bfloat16
I32 = jnp.int32
U32 = jnp.uint32

D_MODEL = 1024
GRP = 256
HEADS = 4
HEAD_DIM = 64
CHUNK = 64
A_CHUNK = 128
B_WINDOW = 512
REL_CLIP = 128
C_WIDTH = 31
C_BUF = C_WIDTH - 1
HALO = 32
GATE_RANK = 16
GLA_TAU = 16.0
GLA_SUB = 16
N_GROUPS = 4
PER_GROUP = 8
N_EXPERTS = 32
D_EXPERT = 512
EPS = 1e-6
NEG_INF = -1e30
LANES = 128
VMEM_LIMIT = 48 * 1024 * 1024

PK, PV, PQ, PAU, PAV, PGLU, PDQ, PDK, PDV, PDR, PLA = range(11)
N_PROJ = 11
_REF_GROUPS = (PAU, PAV, PQ, PK, PV, None, None, PDQ, PDK, PDV, PDR)

SC_WORKERS = 32
SC_WIN = 128


def _cparams(sem):
    return pltpu.CompilerParams(dimension_semantics=sem, vmem_limit_bytes=VMEM_LIMIT)


def _sigmoid(x):
    return 1.0 / (1.0 + jnp.exp(-x))


def _gelu_tanh(x):
    c = np.float32(np.sqrt(2.0 / np.pi))
    return 0.5 * x * (1.0 + jnp.tanh(c * (x + np.float32(0.044715) * (x * x * x))))


def _pack_halves(y):
    half = y.shape[1] // 2
    hi = pltpu.bitcast(y[:, :half].astype(BF16).astype(F32), U32)
    lo = pltpu.bitcast(y[:, half:].astype(BF16).astype(F32), U32)
    return hi | (lo >> np.uint32(16))


def _unpack_hi(w):
    return pltpu.bitcast(w & np.uint32(0xFFFF0000), F32)


def _unpack_lo(w):
    return pltpu.bitcast(w << np.uint32(16), F32)


def _head_id(shape, axis, size):
    return lax.broadcasted_iota(I32, shape, axis) // size


def _bd_stack(x, rows):
    x4 = jnp.concatenate([x] * HEADS, axis=0)
    shape = (HEADS * rows, GRP)
    keep = _head_id(shape, 0, rows) == _head_id(shape, 1, HEAD_DIM)
    return jnp.where(keep, x4, jnp.zeros_like(x4))


def _bd_unstack(o, rows):
    lane_h = _head_id((rows, GRP), 1, HEAD_DIM)
    out = o[(HEADS - 1) * rows:HEADS * rows]
    for h in range(HEADS - 2, -1, -1):
        out = jnp.where(lane_h == h, o[h * rows:(h + 1) * rows], out)
    return out


def _head_meansq(o, hsum_ref):
    sq = (o * o).astype(BF16)
    return jnp.dot(sq, hsum_ref[...], preferred_element_type=F32) * np.float32(1.0 / HEAD_DIM)


def _in_kernel(*refs, combine, emit_av, tiles_per_stream, n_cast):
    refs = list(refs)
    x_ref = refs.pop(0)
    if combine:
        y_ref = refs.pop(0)
        gate_ref = refs.pop(0)
    g1_ref, w_ref, wcg_ref, wdg_ref, wg2_ref, bg_ref, gq_ref, gk_ref, gav_ref, hsum_ref = refs[:10]
    cast_in = refs[10:10 + n_cast]
    refs = refs[10 + n_cast:]
    if combine:
        xo_ref = refs.pop(0)
    p_ref = refs.pop(0)
    kc_ref = refs.pop(0)
    vc_ref = refs.pop(0)
    if emit_av:
        av_ref = refs.pop(0)
    cast_out = refs[:n_cast]
    raw = refs[n_cast]
    x = x_ref[...]
    if combine:
        half = D_MODEL // 2
        g = gate_ref[...]
        g0 = g[:, 0:1]
        g1 = g[:, 1:2]
        w0 = y_ref[0]
        w1 = y_ref[1]
        xa = x[:, :half] + g0 * _unpack_hi(w0) + g1 * _unpack_hi(w1)
        xb = x[:, half:] + g0 * _unpack_lo(w0) + g1 * _unpack_lo(w1)
        xo_ref[:, :half] = xa
        xo_ref[:, half:] = xb
        x = jnp.concatenate([xa, xb], axis=1)
    rs = lax.rsqrt(jnp.mean(x * x, axis=-1, keepdims=True) + EPS)
    h = (x * g1_ref[...]).astype(BF16)

    n_slots = PLA + 2

    def matmul(slot):
        if slot < PLA:
            raw[:, slot * GRP:(slot + 1) * GRP] = jnp.dot(h, w_ref[:, slot * GRP:(slot + 1) * GRP],
                                                          preferred_element_type=F32)
        elif slot == PLA:
            raw[:, PLA * GRP:(PLA + 1) * GRP] = jnp.dot(h, wcg_ref[...], preferred_element_type=F32)
        else:
            raw[:, (PLA + 1) * GRP:] = jnp.dot(h, wdg_ref[...], preferred_element_type=F32)

    def proj(g):
        return raw[:, g * GRP:(g + 1) * GRP] * rs

    def put(g, val):
        p_ref[:, g * GRP:(g + 1) * GRP] = val.astype(BF16)

    def epilogue(slot):
        if slot == PK:
            r = proj(PK)
            put(PK, r * lax.rsqrt(_head_meansq(r, hsum_ref) + EPS) * gk_ref[...])
        elif slot == PV:
            put(PV, proj(PV))

            @pl.when(pl.program_id(0) % tiles_per_stream == tiles_per_stream - 1)
            def _():
                r = proj(PK)
                kc_ref[...] = r * lax.rsqrt(_head_meansq(r, hsum_ref) + EPS) * gk_ref[...]
                vc_ref[...] = proj(PV)
        elif slot == PQ:
            r = proj(PQ)
            put(PQ, r * lax.rsqrt(_head_meansq(r, hsum_ref) + EPS) * (gq_ref[...] * np.float32(HEAD_DIM ** -0.5)))
        elif slot == PAU:
            put(PAU, _gelu_tanh(proj(PAU)))
        elif slot == PAV:
            r = _gelu_tanh(proj(PAV))
            av = r * lax.rsqrt(jnp.mean(r * r, axis=-1, keepdims=True) + EPS) * gav_ref[...]
            put(PAV, av)
            if emit_av:
                av_ref[...] = av
        elif slot == PGLU:
            pass
        elif slot == PDQ:
            put(PDQ, proj(PDQ) * np.float32(HEAD_DIM ** -0.5))
        elif slot in (PDK, PDV):
            put(slot, proj(slot))
        elif slot == PDR:
            r = proj(PDR)
            put(PDR, r * _sigmoid(r))
        elif slot == PLA:
            put(PGLU, proj(PGLU) * _sigmoid(proj(PLA)))
        else:
            dg = raw[:, (PLA + 1) * GRP:] * rs
            z = jnp.dot(dg.astype(BF16), wg2_ref[...], preferred_element_type=F32) + bg_ref[...]
            logsig = jnp.minimum(z, 0.0) - jnp.log(1.0 + jnp.exp(-jnp.abs(z)))
            put(PLA, logsig * np.float32(1.0 / GLA_TAU))

    order = (PLA + 1, PK, PQ, PAV, PAU, PGLU, PLA, PDR, PV, PDQ, PDK, PDV)
    assert sorted(order) == list(range(n_slots))
    lag = 2
    for i in range(n_slots + lag):
        if i < n_slots:
            matmul(order[i])
        if i >= lag:
            epilogue(order[i - lag])
        if i % 3 == 2 and i // 3 < n_cast:
            cast_out[i // 3][...] = cast_in[i // 3][...].astype(BF16)


def _in_proj(x, lw, t, y=None, gates=None, emit_av=False, cast=()):
    n = x.shape[0]
    tm = min(512, n)
    steps = n // tm
    combine = y is not None
    keep = min(B_WINDOW, t)
    tps = max(t // tm, 1)
    assert tps == 1 or keep == tm
    row = lambda i: (i, 0)
    const = lambda i: (0, 0)
    ins, specs = [x], [pl.BlockSpec((tm, D_MODEL), row)]
    if combine:
        ins += [y, gates]
        specs += [pl.BlockSpec((2, tm, D_MODEL // 2), lambda i: (0, i, 0)), pl.BlockSpec((tm, LANES), row)]
    consts = [lw["g1"], lw["w_in"], lw["w_cg"], lw["w_dg"], lw["wg2"], lw["bg"], lw["gq"], lw["gk"], lw["gav"], lw["hsum"]]
    ins += consts
    specs += [pl.BlockSpec(c.shape, const) for c in consts]
    cast_shapes, cast_specs = [], []
    layer = lw["layer"]
    for arr in cast:
        depth, rows, cols = arr.shape[0], arr.shape[1] * arr.shape[2], arr.shape[3]
        slab = rows // steps
        assert slab * steps == rows and slab % 16 == 0
        ins.append(arr.reshape(depth * rows, cols))
        specs.append(pl.BlockSpec((slab, cols), lambda i: (layer * steps + i, 0)))
        cast_shapes.append(jax.ShapeDtypeStruct((rows, cols), BF16))
        cast_specs.append(pl.BlockSpec((slab, cols), row))
    newest = jax.ShapeDtypeStruct((n // tps, GRP), F32)
    newest_spec = pl.BlockSpec((tm, GRP), lambda i: (i // tps, 0))
    out_shape = [jax.ShapeDtypeStruct((n, N_PROJ * GRP), BF16), newest, newest]
    out_specs = [pl.BlockSpec((tm, N_PROJ * GRP), row), newest_spec, newest_spec]
    if combine:
        out_shape = [jax.ShapeDtypeStruct((n, D_MODEL), F32)] + out_shape
        out_specs = [pl.BlockSpec((tm, D_MODEL), row)] + out_specs
    if emit_av:
        out_shape.append(jax.ShapeDtypeStruct((n, GRP), F32))
        out_specs.append(pl.BlockSpec((tm, GRP), row))
    out_shape += cast_shapes
    out_specs += cast_specs
    outs = list(pl.pallas_call(
        functools.partial(_in_kernel, combine=combine, emit_av=emit_av, tiles_per_stream=tps, n_cast=len(cast)),
        grid=(steps,), in_specs=specs, out_specs=out_specs, out_shape=out_shape,
        scratch_shapes=[pltpu.VMEM((tm, (PLA + 1) * GRP + LANES), F32)],
        compiler_params=_cparams(("arbitrary",)), name="in_proj",
    )(*ins))
    x_new = outs.pop(0) if combine else x
    proj, k_new, v_new = outs[0], outs[1], outs[2]
    a_v = outs[3] if emit_av else None
    casted = [o.reshape(a.shape[1:]) for o, a in zip(outs[3 + int(emit_av):], cast)]
    return x_new, proj, (k_new, v_new), a_v, casted


def _gmlp_body(u_ref, v_ref, ws_ref, bs_ref, o_ref, chunk, n_chunks):
    lane_h = _head_id((chunk, GRP), 1, HEAD_DIM)
    ri = lax.broadcasted_iota(I32, (chunk, chunk), 0)
    ci = lax.broadcasted_iota(I32, (chunk, chunk), 1)
    wm = [jnp.where(ci <= ri, ws_ref[h], 0.0).astype(BF16) for h in range(HEADS)]
    for c in range(n_chunks):
        rows = slice(c * chunk, (c + 1) * chunk)
        v = v_ref[rows, :]
        sv = jnp.dot(wm[HEADS - 1], v, preferred_element_type=F32)
        for h in range(HEADS - 2, -1, -1):
            sv = jnp.where(lane_h == h, jnp.dot(wm[h], v, preferred_element_type=F32), sv)
        o_ref[rows, :] = (u_ref[rows, :].astype(F32) * (sv + bs_ref[...])).astype(BF16)


def _attn_kernel(q_ref, kc_ref, vc_ref, kp_ref, vp_ref, bias_ref, o_ref, kbuf, vbuf, *, chunk, n_chunks, first_has_past):
    tq = chunk * n_chunks
    win = B_WINDOW + chunk
    kbuf[0:B_WINDOW, :] = kp_ref[...].astype(BF16)
    vbuf[0:B_WINDOW, :] = vp_ref[...].astype(BF16)
    kbuf[B_WINDOW:B_WINDOW + tq, :] = kc_ref[...]
    vbuf[B_WINDOW:B_WINDOW + tq, :] = vc_ref[...]
    col = lax.broadcasted_iota(I32, (HEADS * chunk, win), 1)

    def chunks(no_past):
        for c in range(n_chunks):
            q = q_ref[c * chunk:(c + 1) * chunk, :]
            kk = kbuf[c * chunk:c * chunk + win, :]
            vv = vbuf[c * chunk:c * chunk + win, :]
            s = lax.dot_general(_bd_stack(q, chunk), kk, (((1,), (1,)), ((), ())), preferred_element_type=F32)
            s = s + bias_ref[...]
            if no_past:
                s = jnp.where(col + c * chunk >= B_WINDOW, s, NEG_INF)
            m = jnp.max(s, axis=-1, keepdims=True)
            p = jnp.exp(s - m)
            l = jnp.sum(p, axis=-1, keepdims=True)
            o = jnp.dot(p.astype(BF16), vv, preferred_element_type=F32) * (1.0 / l)
            o_ref[c * chunk:(c + 1) * chunk, :] = _bd_unstack(o, chunk).astype(BF16)

    if first_has_past:
        chunks(False)
    else:
        pl.when(pl.program_id(1) == 0)(functools.partial(chunks, True))
        pl.when(pl.program_id(1) > 0)(functools.partial(chunks, False))


def _attention(proj, lw, b, t, cache_k=None, cache_v=None):
    n = proj.shape[0]
    step = cache_k is not None
    chunk = min(t, CHUNK)
    tq = min(t, B_WINDOW)
    nt = t // tq
    rel = lw["b_rel"]
    win = B_WINDOW + chunk
    lo = REL_CLIP - (chunk - 1)
    n_far = (chunk - 1) + win - (2 * REL_CLIP + 1 - lo)
    by_dist = jnp.concatenate([rel[:, lo:], jnp.broadcast_to(rel[:, -1:], (HEADS, n_far))], axis=1)
    by_key = by_dist[:, ::-1]
    n_k = chunk - 1 + win
    wrapped = jnp.tile(jnp.pad(by_key, ((0, 0), (0, 1))), (1, chunk))[:, :chunk * n_k].reshape(HEADS, chunk, n_k)
    bias = wrapped[:, :, chunk - 1:].astype(F32).reshape(HEADS * chunk, win)
    cur = lambda g: pl.BlockSpec((tq, GRP), lambda bi, j: (bi * nt + j, g))
    if step:
        prev_k = pl.BlockSpec((B_WINDOW, GRP), lambda bi, j: (bi, 0))
        prev_v = prev_k
        pk_arr, pv_arr = cache_k, cache_v
    else:
        assert tq == B_WINDOW
        prev_k = pl.BlockSpec((B_WINDOW, GRP), lambda bi, j: (bi * nt + jnp.maximum(j - 1, 0), PK))
        prev_v = pl.BlockSpec((B_WINDOW, GRP), lambda bi, j: (bi * nt + jnp.maximum(j - 1, 0), PV))
        pk_arr, pv_arr = proj, proj
    return pl.pallas_call(
        functools.partial(_attn_kernel, chunk=chunk, n_chunks=tq // chunk, first_has_past=step),
        grid=(b, nt),
        in_specs=[cur(PQ), cur(PK), cur(PV), prev_k, prev_v, pl.BlockSpec(bias.shape, lambda bi, j: (0, 0))],
        out_specs=pl.BlockSpec((tq, GRP), lambda bi, j: (bi * nt + j, 0)),
        out_shape=jax.ShapeDtypeStruct((n, GRP), BF16),
        scratch_shapes=[pltpu.VMEM((B_WINDOW + tq, GRP), BF16), pltpu.VMEM((B_WINDOW + tq, GRP), BF16)],
        compiler_params=_cparams(("arbitrary", "arbitrary")), name="band_attn",
    )(proj, proj, proj, pk_arr, pv_arr, bias)


def _conv_kernel(g_ref, halo_ref, dw_ref, dwb_ref, lng_ref, lnb_ref, u_ref, v_ref, ws_ref, bs_ref,
                 o_ref, tail_ref, ya_ref, xp, zbuf, *, tc, sub, first_has_past, a_chunk):
    _gmlp_body(u_ref, v_ref, ws_ref, bs_ref, ya_ref, a_chunk, tc // a_chunk)
    halo = halo_ref[...].astype(F32)
    has_past = jnp.logical_or(pl.program_id(1) > 0, first_has_past)
    xp[0:HALO, :] = jnp.where(has_past, halo, 0.0)
    xp[HALO:HALO + tc, :] = g_ref[...].astype(F32)
    xp[HALO + tc:, :] = jnp.zeros((xp.shape[0] - HALO - tc, GRP), F32)

    @pl.when(pl.program_id(1) == pl.num_programs(1) - 1)
    def _():
        tail_ref[...] = xp[tc:tc + HALO, :]

    lead = HALO - C_BUF
    sl = 8
    for s in range(tc // sub):
        acc = None
        for r in range(sl):
            taps = [p for p in range(r, lead + C_WIDTH, sl) if p >= lead]
            z = None
            for p in taps:
                a0 = s * sub + p - r
                term = dw_ref[p - lead:p - lead + 1, :] * xp[a0:a0 + sub + sl, :]
                z = term if z is None else z + term
            zbuf[r] = z
            part = zbuf[r, r:r + sub, :]
            acc = part if acc is None else acc + part
        y = acc + dwb_ref[...]
        mu = jnp.mean(y, axis=-1, keepdims=True)
        yc = y - mu
        y = yc * lax.rsqrt(jnp.mean(yc * yc, axis=-1, keepdims=True) + EPS) * lng_ref[...] + lnb_ref[...]
        o_ref[s * sub:(s + 1) * sub, :] = (y * _sigmoid(y)).astype(BF16)


def _conv_gmlp(proj, lw, b, t, state=None):
    n = proj.shape[0]
    step = state is not None
    tc = min(t, 512)
    nt = t // tc
    sub = min(tc, 64)
    a_chunk = min(t, A_CHUNK)
    ws = lw["a_ws"][:, :a_chunk, :a_chunk]
    bs = lw["a_bs_rows"][:a_chunk]
    if step:
        halo_arr = state
        halo_spec = pl.BlockSpec((HALO, GRP), lambda bi, j: (bi, 0))
    else:
        per = tc // HALO
        halo_arr = proj
        halo_spec = pl.BlockSpec((HALO, GRP), lambda bi, j: (jnp.maximum((bi * nt + j) * per - 1, 0), PGLU))
    vec = pl.BlockSpec((1, GRP), lambda bi, j: (0, 0))
    cur = lambda g: pl.BlockSpec((tc, GRP), lambda bi, j: (bi * nt + j, g))
    return pl.pallas_call(
        functools.partial(_conv_kernel, tc=tc, sub=sub, first_has_past=step, a_chunk=a_chunk),
        grid=(b, nt),
        in_specs=[cur(PGLU), halo_spec, pl.BlockSpec((C_WIDTH, GRP), lambda bi, j: (0, 0)), vec, vec, vec,
                  cur(PAU), cur(PAV), pl.BlockSpec(ws.shape, lambda bi, j: (0, 0, 0)),
                  pl.BlockSpec(bs.shape, lambda bi, j: (0, 0))],
        out_specs=[cur(0), pl.BlockSpec((HALO, GRP), lambda bi, j: (bi, 0)), cur(0)],
        out_shape=[jax.ShapeDtypeStruct((n, GRP), BF16), jax.ShapeDtypeStruct((b * HALO, GRP), F32),
                   jax.ShapeDtypeStruct((n, GRP), BF16)],
        scratch_shapes=[pltpu.VMEM((HALO + tc + 8, GRP), F32), pltpu.VMEM((8, sub + 8, GRP), F32)],
        compiler_params=_cparams(("arbitrary", "arbitrary")), name="conv_gmlp",
    )(proj, halo_arr, lw["c_dw"], lw["c_dw_b"], lw["c_ln_g"], lw["c_ln_b"], proj, proj, ws, bs)


def _gla_tables(L):
    i = np.arange(L)[:, None]
    t = np.arange(L)[None, :]
    masks = []
    s = GLA_SUB
    masks.append(((i // s) == (t // s)) & (t <= i))
    s *= 2
    while s <= L:
        h = s // 2
        masks.append(((i // s) == (t // s)) & (i % s >= h) & (t % s < h))
        s *= 2
    tri = (t <= i).astype(np.float32)
    mask = np.stack([np.tile(m.astype(np.float32), (1, HEADS)) for m in masks], axis=0)
    return tri, mask


def _gla_anchor(cum, row, L, size, first_half):
    out = None
    for start in range(0, L, size):
        ar = start + size // 2 - 1 if first_half else start - 1
        val = jnp.zeros((L, GRP), F32) if ar < 0 else jnp.broadcast_to(cum[ar:ar + 1, :], (L, GRP))
        out = val if out is None else jnp.where(row >= start, val, out)
    return out


def _gla_kernel(q_ref, k_ref, v_ref, la_ref, dr_ref, s0_ref, tri_ref, lmask_ref, bdmask_ref, hsum_ref, gon_ref,
                o_ref, sf_ref, st, o_all, *, L, n_chunks, n_levels, first_has_state):
    j = pl.program_id(1)

    @pl.when(j == 0)
    def _():
        st[...] = jnp.zeros_like(st)
        if first_has_state:
            for h in range(HEADS):
                blk = slice(h * HEAD_DIM, (h + 1) * HEAD_DIM)
                st[blk, blk] = s0_ref[blk, :].T

    row = lax.broadcasted_iota(I32, (L, GRP), 0)
    dn_t = (((1,), (1,)), ((), ()))

    def prep(c):
        rows = slice(c * L, (c + 1) * L)
        q = q_ref[rows, :].astype(F32)
        k = k_ref[rows, :].astype(F32)
        v = v_ref[rows, :]
        cum = jnp.dot(tri_ref[...], la_ref[rows, :], preferred_element_type=F32)
        total = cum[L - 1:L, :]
        pairs = []
        for lvl in range(n_levels):
            size = GLA_SUB << lvl
            if lvl == 0:
                local = cum - _gla_anchor(cum, row, L, size, False)
                ql = q * jnp.exp(local)
                kl = k * jnp.exp(-local)
            else:
                upper = (row & (size - 1)) >= (size // 2)
                d = cum - _gla_anchor(cum, row, L, size, True)
                w = jnp.exp(jnp.where(upper, d, -d))
                ql = jnp.where(upper, q * w, 0.0)
                kl = jnp.where(upper, 0.0, k * w)
            pairs.append((ql.astype(BF16), _bd_stack(kl.astype(BF16), L)))
        return dict(rows=rows, v=v, qp=(q * jnp.exp(cum)).astype(BF16), kst=(k * jnp.exp(total - cum)).astype(BF16),
                    decay=jnp.exp(total), pairs=pairs)

    def intra(p):
        att = None
        for lvl, (ql, kbd) in enumerate(p["pairs"]):
            a = lax.dot_general(ql, kbd, dn_t, preferred_element_type=F32) * lmask_ref[lvl]
            att = a if att is None else att + a
        p["o_intra"] = jnp.dot(att.astype(BF16), _bd_stack(p["v"], L), preferred_element_type=F32)
        p["upd"] = lax.dot_general(p["v"], p["kst"], (((0,), (0,)), ((), ())),
                                   preferred_element_type=F32) * bdmask_ref[...]
        return p

    def finish(p, s_t):
        o_all[p["rows"], :] = lax.dot_general(p["qp"], s_t.astype(BF16), dn_t, preferred_element_type=F32) + p["o_intra"]
        return s_t * p["decay"] + p["upd"]

    s_t = st[...]
    stage1, stage2 = {}, {}
    for step in range(n_chunks + 2):
        if step < n_chunks:
            stage1[step] = prep(step)
        if 0 <= step - 1 < n_chunks:
            stage2[step - 1] = intra(stage1.pop(step - 1))
        if 0 <= step - 2 < n_chunks:
            s_t = finish(stage2.pop(step - 2), s_t)
    st[...] = s_t
    o = o_all[...]
    y = o * lax.rsqrt(_head_meansq(o, hsum_ref) + EPS) * gon_ref[...] * dr_ref[...].astype(F32)
    o_ref[...] = y.astype(BF16)

    @pl.when(j == pl.num_programs(1) - 1)
    def _():
        for h in range(HEADS):
            blk = slice(h * HEAD_DIM, (h + 1) * HEAD_DIM)
            sf_ref[blk, :] = st[blk, blk].T


def _gla(proj, lw, b, t, s0=None):
    n = proj.shape[0]
    step = s0 is not None
    L = min(t, 64)
    td = min(t, 512)
    nt = t // td
    n_levels = int(np.log2(L // GLA_SUB)) + 1
    tri, lmask = _gla_tables(L)
    tri = jnp.asarray(tri, BF16)
    lmask = jnp.asarray(lmask, F32)
    if not step:
        s0 = jnp.zeros((GRP, HEAD_DIM), F32)
        s0_spec = pl.BlockSpec((GRP, HEAD_DIM), lambda bi, j: (0, 0))
    else:
        s0_spec = pl.BlockSpec((GRP, HEAD_DIM), lambda bi, j: (bi, 0))
    cur = lambda g: pl.BlockSpec((td, GRP), lambda bi, j: (bi * nt + j, g))
    c2 = lambda bi, j: (0, 0)
    return pl.pallas_call(
        functools.partial(_gla_kernel, L=L, n_chunks=td // L, n_levels=n_levels, first_has_state=step),
        grid=(b, nt),
        in_specs=[cur(PDQ), cur(PDK), cur(PDV), cur(PLA), cur(PDR), s0_spec,
                  pl.BlockSpec(tri.shape, c2), pl.BlockSpec(lmask.shape, lambda bi, j: (0, 0, 0)),
                  pl.BlockSpec((GRP, GRP), c2), pl.BlockSpec((GRP, GRP), c2), pl.BlockSpec((1, GRP), c2)],
        out_specs=[pl.BlockSpec((td, GRP), lambda bi, j: (bi * nt + j, 0)),
                   pl.BlockSpec((GRP, HEAD_DIM), lambda bi, j: (bi, 0))],
        out_shape=[jax.ShapeDtypeStruct((n, GRP), BF16), jax.ShapeDtypeStruct((b * GRP, HEAD_DIM), F32)],
        scratch_shapes=[pltpu.VMEM((GRP, GRP), F32), pltpu.VMEM((td, GRP), F32)],
        compiler_params=_cparams(("arbitrary", "arbitrary")), name="gla",
    )(proj, proj, proj, proj, proj, s0, tri, lmask, lw["bdmask"], lw["hsum"], lw["gon"])


def _out_kernel(ya_ref, yb_ref, yc_ref, yd_ref, x_ref, wo_ref, g2_ref, wr_ref, br_ref, tri_ref,
                xo_ref, xn_ref, ri_ref, rf_ref, cnt_ref, *, tm, n_sub):
    @pl.when(pl.program_id(0) == 0)
    def _():
        cnt_ref[...] = jnp.zeros_like(cnt_ref)

    sub = tm // n_sub
    lane = lax.broadcasted_iota(I32, (sub, LANES), 1)
    big = np.int32(1 << 20)

    def project(s):
        rows = slice(s * sub, (s + 1) * sub)
        ycat = jnp.concatenate([ya_ref[rows, :], yb_ref[rows, :], yc_ref[rows, :], yd_ref[rows, :]], axis=1)
        x = x_ref[rows, :] + jnp.dot(ycat, wo_ref[...], preferred_element_type=F32)
        xo_ref[rows, :] = x
        xn = x * lax.rsqrt(jnp.mean(x * x, axis=-1, keepdims=True) + EPS) * g2_ref[...]
        xn_ref[rows, :] = _pack_halves(xn)
        both = jnp.dot(xn.astype(BF16), wr_ref[...], preferred_element_type=F32)
        return both[:, :LANES] + both[:, LANES:] + br_ref[...]

    def route(s, logits):
        rows = slice(s * sub, (s + 1) * sub)

        def first_max(mask):
            v = jnp.max(jnp.where(mask, logits, -jnp.inf), axis=-1, keepdims=True)
            idx = jnp.min(jnp.where(jnp.logical_and(mask, logits == v), lane, big), axis=-1, keepdims=True)
            return v, idx

        is_grp = jnp.logical_and(lane >= N_EXPERTS, lane < N_EXPERTS + N_GROUPS)
        gmax, gidx = first_max(is_grp)
        p_grp = 1.0 / jnp.sum(jnp.where(is_grp, jnp.exp(logits - gmax), 0.0), axis=-1, keepdims=True)
        grp = gidx - N_EXPERTS
        in_grp = (lane // PER_GROUP) == grp
        v1, i1 = first_max(in_grp)
        v2, i2 = first_max(jnp.logical_and(in_grp, lane != i1))
        e21 = jnp.exp(v2 - v1)
        gate1 = p_grp / (1.0 + e21)
        gate2 = p_grp * e21 / (1.0 + e21)

        oh1 = lane == i1
        oh2 = lane == i2
        both = jnp.logical_or(oh1, oh2)
        ones = jnp.where(both, 1.0, 0.0).astype(BF16)
        before = jnp.dot(tri_ref[...], ones, preferred_element_type=F32) + cnt_ref[...].astype(F32)
        rank1 = jnp.sum(jnp.where(oh1, before, 0.0), axis=-1, keepdims=True)
        rank2 = jnp.sum(jnp.where(oh2, before, 0.0), axis=-1, keepdims=True)
        cnt_ref[...] = cnt_ref[...] + jnp.sum(jnp.where(both, 1.0, 0.0), axis=0, keepdims=True).astype(I32)

        ri = jnp.where(lane == 0, i1, jnp.where(lane == 1, i2, jnp.where(lane == 2, rank1.astype(I32),
                                                                          jnp.where(lane == 3, rank2.astype(I32), 0))))
        ri_ref[rows, :] = ri
        rf_ref[rows, :] = jnp.where(lane == 0, gate1, jnp.where(lane == 1, gate2, 0.0))

    logits = project(0)
    for s in range(n_sub):
        nxt = project(s + 1) if s + 1 < n_sub else None
        route(s, logits)
        logits = nxt


def _out_proj(ya, yb, yc, yd, x, lw):
    n = x.shape[0]
    tm = min(1024, n)
    row = lambda i: (i, 0)
    const = lambda i: (0, 0)
    n_sub = 2 if tm >= 1024 else 1
    sub = tm // n_sub
    tri = jnp.asarray(np.tril(np.ones((sub, sub), np.float32), -1), BF16)
    consts = [lw["w_out"], lw["g2"], lw["wr"], lw["br"], tri]
    yspec = pl.BlockSpec((tm, GRP), row)
    return pl.pallas_call(
        functools.partial(_out_kernel, tm=tm, n_sub=n_sub),
        grid=(n // tm,),
        in_specs=[yspec, yspec, yspec, yspec, pl.BlockSpec((tm, D_MODEL), row)] + [pl.BlockSpec(c.shape, const) for c in consts],
        out_specs=[pl.BlockSpec((tm, D_MODEL), row), pl.BlockSpec((tm, D_MODEL // 2), row),
                   pl.BlockSpec((tm, LANES), row), pl.BlockSpec((tm, LANES), row), pl.BlockSpec((1, LANES), const)],
        out_shape=[jax.ShapeDtypeStruct((n, D_MODEL), F32), jax.ShapeDtypeStruct((n, D_MODEL // 2), U32),
                   jax.ShapeDtypeStruct((n, LANES), I32), jax.ShapeDtypeStruct((n, LANES), F32),
                   jax.ShapeDtypeStruct((1, LANES), I32)],
        compiler_params=_cparams(("arbitrary",)), name="out_proj_router",
    )(ya, yb, yc, yd, x, *consts)


def _sc_scatter_rows(x, idx, n_out):
    n, d = x.shape
    kk = idx.shape[0]
    per_w = n // SC_WORKERS
    win = min(SC_WIN, per_w)
    n_win = per_w // win
    assert n_win * win * SC_WORKERS == n
    mesh = plsc.VectorSubcoreMesh(core_axis_name="c", subcore_axis_name="s")

    @functools.partial(
        pl.kernel, mesh=mesh, out_type=jax.ShapeDtypeStruct((n_out, d), x.dtype),
        scratch_types=[pltpu.VMEM((kk, win), I32), pltpu.VMEM((win, d), x.dtype)],
        name="sc_scatter_rows")
    def k(x_hbm, idx_hbm, o_hbm, idx_v, rows_v):
        wid = lax.axis_index("s") * 2 + lax.axis_index("c")
        base = wid * per_w

        @pl.loop(0, n_win)
        def _(w):
            off = base + w * win
            pltpu.sync_copy(x_hbm.at[pl.ds(off, win)], rows_v)
            for j in range(kk):
                pltpu.sync_copy(idx_hbm.at[j, pl.ds(off, win)], idx_v.at[j])
                pltpu.sync_copy(rows_v, o_hbm.at[idx_v.at[j]])

    return k(x, idx)


def _sc_gather_rows(y, idx):
    _, d = y.shape
    kk, n = idx.shape
    per_w = n // SC_WORKERS
    win = min(SC_WIN, per_w)
    n_win = per_w // win
    assert n_win * win * SC_WORKERS == n
    mesh = plsc.VectorSubcoreMesh(core_axis_name="c", subcore_axis_name="s")

    @functools.partial(
        pl.kernel, mesh=mesh, out_type=jax.ShapeDtypeStruct((kk, n, d), y.dtype),
        scratch_types=[pltpu.VMEM((kk, win), I32), pltpu.VMEM((win, d), y.dtype)],
        name="sc_gather_rows")
    def k(y_hbm, idx_hbm, o_hbm, idx_v, rows_v):
        wid = lax.axis_index("s") * 2 + lax.axis_index("c")
        base = wid * per_w

        @pl.loop(0, n_win)
        def _(w):
            off = base + w * win
            for j in range(kk):
                pltpu.sync_copy(idx_hbm.at[j, pl.ds(off, win)], idx_v.at[j])
                pltpu.sync_copy(y_hbm.at[idx_v.at[j]], rows_v)
                pltpu.sync_copy(rows_v, o_hbm.at[j, pl.ds(off, win)])

    return k(y, idx)


def _moe_kernel(bexp_ref, nused_ref, x_ref, wg_ref, wu_ref, wd_ref, o_ref, *, n_sub):
    del bexp_ref

    @pl.when(pl.program_id(0) < nused_ref[0])
    def _():
        sub = x_ref.shape[0] // n_sub

        def up(s):
            w = x_ref[s * sub:(s + 1) * sub, :]
            x = jnp.concatenate([_unpack_hi(w).astype(BF16), _unpack_lo(w).astype(BF16)], axis=1)
            hg = jnp.dot(x, wg_ref[...], preferred_element_type=F32)
            hu = jnp.dot(x, wu_ref[...], preferred_element_type=F32)
            return (hg * _sigmoid(hg) * hu).astype(BF16)

        def down(s, h):
            o_ref[s * sub:(s + 1) * sub, :] = _pack_halves(jnp.dot(h, wd_ref[...], preferred_element_type=F32))

        h = up(0)
        for s in range(n_sub):
            nxt = up(s + 1) if s + 1 < n_sub else None
            down(s, h)
            h = nxt


def _moe_experts(xs, blk_exp, n_used, w_gate, w_up, w_down, bm):
    p = xs.shape[0]
    n_blocks = p // bm
    live = lambda i, be, nu: jnp.minimum(i, jnp.maximum(nu[0] - 1, 0))
    wspec = lambda shape: pl.BlockSpec((None,) + shape, lambda i, be, nu: (be[live(i, be, nu)], 0, 0))
    grid_spec = pltpu.PrefetchScalarGridSpec(
        num_scalar_prefetch=2, grid=(n_blocks,),
        in_specs=[pl.BlockSpec((bm, D_MODEL // 2), lambda i, be, nu: (live(i, be, nu), 0)),
                  wspec((D_MODEL, D_EXPERT)), wspec((D_MODEL, D_EXPERT)), wspec((D_EXPERT, D_MODEL))],
        out_specs=pl.BlockSpec((bm, D_MODEL // 2), lambda i, be, nu: (live(i, be, nu), 0)))
    return pl.pallas_call(
        functools.partial(_moe_kernel, n_sub=2 if bm >= 512 else 1),
        grid_spec=grid_spec, out_shape=jax.ShapeDtypeStruct((p, D_MODEL // 2), U32),
        compiler_params=_cparams(("arbitrary",)), name="moe_experts",
    )(blk_exp, n_used, xs, w_gate, w_up, w_down)


def _moe_block_rows(n):
    return 512 if n >= 16384 else 128


def _moe(xn_packed, route_i, counts, experts_bf16):
    n = xn_packed.shape[0]
    bm = _moe_block_rows(n)
    n_blocks = -(-(2 * n + N_EXPERTS * (bm - 1)) // bm)
    cnt = counts[0, :N_EXPERTS]
    padded = (cnt + bm - 1) // bm * bm
    pad_end = jnp.cumsum(padded)
    pad_start = pad_end - padded
    experts = jnp.arange(N_EXPERTS, dtype=I32)
    eid = route_i[:, 0:2].T
    start_of = jnp.sum(jnp.where(eid[:, :, None] == experts, pad_start, 0), axis=-1)
    dest = (start_of + route_i[:, 2:4].T).astype(I32)
    first_row = jnp.arange(n_blocks, dtype=I32) * bm
    blk_exp = jnp.minimum(jnp.sum((pad_end[None, :] <= first_row[:, None]).astype(I32), axis=1), N_EXPERTS - 1)
    n_used = (pad_end[-1:] // bm).astype(I32)
    xs = _sc_scatter_rows(xn_packed, dest, n_blocks * bm)
    ys = _moe_experts(xs, blk_exp, n_used, *experts_bf16, bm)
    return _sc_gather_rows(ys, dest)


def _combine_kernel(x_ref, y_ref, gate_ref, o_ref):
    half = D_MODEL // 2
    x = x_ref[...]
    g = gate_ref[...]
    g0 = g[:, 0:1]
    g1 = g[:, 1:2]
    w0 = y_ref[0]
    w1 = y_ref[1]
    o_ref[:, :half] = x[:, :half] + g0 * _unpack_hi(w0) + g1 * _unpack_hi(w1)
    o_ref[:, half:] = x[:, half:] + g0 * _unpack_lo(w0) + g1 * _unpack_lo(w1)


def _combine(x, y, gates):
    n = x.shape[0]
    tm = min(1024, n)
    row = lambda i: (i, 0)
    return pl.pallas_call(
        _combine_kernel, grid=(n // tm,),
        in_specs=[pl.BlockSpec((tm, D_MODEL), row), pl.BlockSpec((2, tm, D_MODEL // 2), lambda i: (0, i, 0)),
                  pl.BlockSpec((tm, LANES), row)],
        out_specs=pl.BlockSpec((tm, D_MODEL), row), out_shape=jax.ShapeDtypeStruct((n, D_MODEL), F32),
        compiler_params=_cparams(("arbitrary",)), name="moe_combine",
    )(x, y, gates)


def _layer_weights(l, p):
    w_in = p["w_in"][l]
    cols = [w_in[:, i * GRP:(i + 1) * GRP] for i in range(11)]
    by_group = [None] * N_PROJ
    for ref_i, g in enumerate(_REF_GROUPS):
        if g is not None:
            by_group[g] = cols[ref_i]
    by_group[PGLU] = cols[5]
    w_dg = jnp.zeros((D_MODEL, LANES), F32).at[:, :GATE_RANK].set(w_in[:, 11 * GRP:])
    wg2 = jnp.zeros((LANES, GRP), F32).at[:GATE_RANK].set(p["d_wg2"][l])
    tile4 = lambda v: jnp.tile(v, HEADS)[None, :]
    hid = np.arange(GRP) // HEAD_DIM
    bd = (hid[:, None] == hid[None, :]).astype(np.float32)
    wr = jnp.zeros((D_MODEL, LANES), F32).at[:, :N_EXPERTS].set(p["r_expert_w"][l])
    wr = wr.at[:, N_EXPERTS:N_EXPERTS + N_GROUPS].set(p["r_group_w"][l])
    wr_hi = wr.astype(BF16)
    br = jnp.zeros((1, LANES), F32).at[0, :N_EXPERTS].set(p["r_expert_b"][l])
    br = br.at[0, N_EXPERTS:N_EXPERTS + N_GROUPS].set(p["r_group_b"][l])
    return {
        "g1": p["norm1_g"][l][None, :],
        "w_in": jnp.concatenate(by_group[:PLA], axis=1).astype(BF16),
        "w_cg": cols[6].astype(BF16),
        "w_dg": w_dg.astype(BF16),
        "wg2": wg2.astype(BF16),
        "bg": p["d_bg"][l][None, :],
        "gq": tile4(p["b_qnorm_g"][l]), "gk": tile4(p["b_knorm_g"][l]), "gav": p["a_vnorm_g"][l][None, :],
        "gon": tile4(p["d_onorm_g"][l]),
        "hsum": jnp.asarray(bd, BF16), "bdmask": jnp.asarray(bd, F32),
        "a_ws": p["a_ws"][l], "a_bs_rows": jnp.repeat(p["a_bs"][l].T, HEAD_DIM, axis=1),
        "b_rel": p["b_rel_bias"][l],
        "c_dw": p["c_dw"][l], "c_dw_b": p["c_dw_b"][l][None, :],
        "c_ln_g": p["c_ln_g"][l][None, :], "c_ln_b": p["c_ln_b"][l][None, :],
        "w_out": p["w_out"][l].astype(BF16),
        "g2": p["norm2_g"][l][None, :],
        "wr": jnp.concatenate([wr_hi, (wr - wr_hi.astype(F32)).astype(BF16)], axis=1), "br": br,
        "e_w_gate": p["e_w_gate"], "e_w_up": p["e_w_up"], "e_w_down": p["e_w_down"], "layer": l,
    }


def _mix_and_route(x, lw, b, t, pending, caches, experts_bf16):
    step = caches is not None
    y_prev, gates_prev = pending if pending is not None else (None, None)
    cast = () if experts_bf16 is not None else (lw["e_w_gate"], lw["e_w_up"], lw["e_w_down"])
    x, proj, kv, a_v, casted = _in_proj(x, lw, t, y_prev, gates_prev, emit_av=step, cast=cast)
    if experts_bf16 is None:
        experts_bf16 = casted
    if not step:
        yb = _attention(proj, lw, b, t)
        yc, tail, ya = _conv_gmlp(proj, lw, b, t)
        yd, sf = _gla(proj, lw, b, t)
    else:
        ck, cv, cc, cs = caches
        yb = _attention(proj, lw, b, t, ck.reshape(b * B_WINDOW, GRP), cv.reshape(b * B_WINDOW, GRP))
        halo = jnp.pad(cc, ((0, 0), (HALO - C_BUF, 0), (0, 0))).reshape(b * HALO, GRP)
        yc, tail, ya = _conv_gmlp(proj, lw, b, t, halo)
        yd, sf = _gla(proj, lw, b, t, cs.reshape(b * GRP, HEAD_DIM))
        a_v = a_v.reshape(b, t, GRP)
    x2, xn_packed, route_i, route_f, counts = _out_proj(ya, yb, yc, yd, x, lw)
    y = _moe(xn_packed, route_i, counts, experts_bf16)
    keep = min(B_WINDOW, t)
    new_k = kv[0].reshape(b, keep, HEADS, HEAD_DIM)
    new_v = kv[1].reshape(b, keep, HEADS, HEAD_DIM)
    new_buf = tail.reshape(b, HALO, GRP)[:, HALO - C_BUF:]
    states = (new_k, new_v, new_buf, sf.reshape(b, HEADS, HEAD_DIM, HEAD_DIM), a_v)
    return x2, (y, route_f), states, experts_bf16


def kernel(x_prompt, x_sample, cache_b_k, cache_b_v, state_c_conv, state_d_gla, norm1_g, w_in, a_vnorm_g, a_ws, a_bs, b_qnorm_g, b_knorm_g, b_rel_bias, c_dw, c_dw_b, c_ln_g, c_ln_b, d_wg2, d_bg, d_onorm_g, w_out, norm2_g, r_group_w, r_group_b, r_expert_w, r_expert_b, e_w_gate, e_w_up, e_w_down):
    params = dict(norm1_g=norm1_g, w_in=w_in, a_vnorm_g=a_vnorm_g, a_ws=a_ws, a_bs=a_bs, b_qnorm_g=b_qnorm_g,
                  b_knorm_g=b_knorm_g, b_rel_bias=b_rel_bias, c_dw=c_dw, c_dw_b=c_dw_b, c_ln_g=c_ln_g, c_ln_b=c_ln_b,
                  d_wg2=d_wg2, d_bg=d_bg, d_onorm_g=d_onorm_g, w_out=w_out, norm2_g=norm2_g, r_group_w=r_group_w,
                  r_group_b=r_group_b, r_expert_w=r_expert_w, r_expert_b=r_expert_b, e_w_gate=e_w_gate,
                  e_w_up=e_w_up, e_w_down=e_w_down)
    depth = w_in.shape[0]
    bp, tp, _ = x_prompt.shape
    bs, ts, _ = x_sample.shape
    xp = x_prompt.reshape(bp * tp, D_MODEL)
    xs = x_sample.reshape(bs * ts, D_MODEL)
    pend_p = pend_s = None
    st_p, st_s = [], []
    for l in range(depth):
        lw = _layer_weights(l, params)
        xp, pend_p, sp, experts = _mix_and_route(xp, lw, bp, tp, pend_p, None, None)
        xs, pend_s, ss, _ = _mix_and_route(xs, lw, bs, ts, pend_s,
                                           (cache_b_k[l], cache_b_v[l], state_c_conv[l], state_d_gla[l]), experts)
        st_p.append(sp)
        st_s.append(ss)
    yp = _combine(xp, pend_p[0], pend_p[1]).reshape(bp, tp, D_MODEL)
    ys = _combine(xs, pend_s[0], pend_s[1]).reshape(bs, ts, D_MODEL)
    stack = lambda sts, i: jnp.stack([s[i] for s in sts])
    return (yp, ys, stack(st_p, 0), stack(st_p, 1), stack(st_p, 2), stack(st_p, 3),
            stack(st_s, 0), stack(st_s, 1), stack(st_s, 2), stack(st_s, 3), stack(st_s, 4))
```

```python
import functools

import numpy as np
import jax
import jax.numpy as jnp
from jax import lax
from jax.experimental import pallas as pl
from jax.experimental.pallas import tpu as pltpu
from jax.experimental.pallas import tpu_sc as plsc

F32 = jnp.float32
BF16 = jnp.bfloat16
I32 = jnp.int32
U32 = jnp.uint32

D_MODEL = 1024
GRP = 256
HEADS = 4
HEAD_DIM = 64
CHUNK = 64
A_CHUNK = 128
B_WINDOW = 512
REL_CLIP = 128
C_WIDTH = 31
C_BUF = C_WIDTH - 1
HALO = 32
GATE_RANK = 16
GLA_TAU = 16.0
GLA_SUB = 16
N_GROUPS = 4
PER_GROUP = 8
N_EXPERTS = 32
D_EXPERT = 512
EPS = 1e-6
NEG_INF = -1e30
LANES = 128
VMEM_LIMIT = 48 * 1024 * 1024

PK, PV, PQ, PAU, PAV, PGLU, PDQ, PDK, PDV, PDR, PLA = range(11)
N_PROJ = 11
_REF_GROUPS = (PAU, PAV, PQ, PK, PV, None, None, PDQ, PDK, PDV, PDR)

SC_WORKERS = 32
SC_WIN = 64


def _cparams(sem):
    return pltpu.CompilerParams(dimension_semantics=sem, vmem_limit_bytes=VMEM_LIMIT)


def _sigmoid(x):
    return 1.0 / (1.0 + jnp.exp(-x))


def _gelu_tanh(x):
    c = np.float32(np.sqrt(2.0 / np.pi))
    return 0.5 * x * (1.0 + jnp.tanh(c * (x + np.float32(0.044715) * (x * x * x))))


def _pack_halves(y):
    half = y.shape[1] // 2
    hi = pltpu.bitcast(y[:, :half].astype(BF16).astype(F32), U32)
    lo = pltpu.bitcast(y[:, half:].astype(BF16).astype(F32), U32)
    return hi | (lo >> np.uint32(16))


def _unpack_hi(w):
    return pltpu.bitcast(w & np.uint32(0xFFFF0000), F32)


def _unpack_lo(w):
    return pltpu.bitcast(w << np.uint32(16), F32)


def _head_id(shape, axis, size):
    return lax.broadcasted_iota(I32, shape, axis) // size


def _bd_stack(x, rows):
    x4 = jnp.concatenate([x] * HEADS, axis=0)
    shape = (HEADS * rows, GRP)
    keep = _head_id(shape, 0, rows) == _head_id(shape, 1, HEAD_DIM)
    return jnp.where(keep, x4, jnp.zeros_like(x4))


def _bd_unstack(o, rows):
    lane_h = _head_id((rows, GRP), 1, HEAD_DIM)
    out = o[(HEADS - 1) * rows:HEADS * rows]
    for h in range(HEADS - 2, -1, -1):
        out = jnp.where(lane_h == h, o[h * rows:(h + 1) * rows], out)
    return out


def _head_meansq(o, hsum_ref):
    sq = (o * o).astype(BF16)
    return jnp.dot(sq, hsum_ref[...], preferred_element_type=F32) * np.float32(1.0 / HEAD_DIM)


def _in_kernel(*refs, combine, emit_av, tiles_per_stream, n_cast):
    refs = list(refs)
    x_ref = refs.pop(0)
    if combine:
        y_ref = refs.pop(0)
        gate_ref = refs.pop(0)
    g1_ref, w_ref, wcg_ref, wdg_ref, wg2_ref, bg_ref, gq_ref, gk_ref, gav_ref, hsum_ref = refs[:10]
    cast_in = refs[10:10 + n_cast]
    refs = refs[10 + n_cast:]
    if combine:
        xo_ref = refs.pop(0)
    p_ref = refs.pop(0)
    kc_ref = refs.pop(0)
    vc_ref = refs.pop(0)
    if emit_av:
        av_ref = refs.pop(0)
    cast_out = refs[:n_cast]
    raw = refs[n_cast]
    x = x_ref[...]
    if combine:
        half = D_MODEL // 2
        g = gate_ref[...]
        g0 = g[:, 0:1]
        g1 = g[:, 1:2]
        w0 = y_ref[0]
        w1 = y_ref[1]
        xa = x[:, :half] + g0 * _unpack_hi(w0) + g1 * _unpack_hi(w1)
        xb = x[:, half:] + g0 * _unpack_lo(w0) + g1 * _unpack_lo(w1)
        xo_ref[:, :half] = xa
        xo_ref[:, half:] = xb
        x = jnp.concatenate([xa, xb], axis=1)
    rs = lax.rsqrt(jnp.mean(x * x, axis=-1, keepdims=True) + EPS)
    h = (x * g1_ref[...]).astype(BF16)

    n_slots = PLA + 2

    def matmul(slot):
        if slot < PLA:
            raw[:, slot * GRP:(slot + 1) * GRP] = jnp.dot(h, w_ref[:, slot * GRP:(slot + 1) * GRP],
                                                          preferred_element_type=F32)
        elif slot == PLA:
            raw[:, PLA * GRP:(PLA + 1) * GRP] = jnp.dot(h, wcg_ref[...], preferred_element_type=F32)
        else:
            raw[:, (PLA + 1) * GRP:] = jnp.dot(h, wdg_ref[...], preferred_element_type=F32)

    def proj(g):
        return raw[:, g * GRP:(g + 1) * GRP] * rs

    def put(g, val):
        p_ref[:, g * GRP:(g + 1) * GRP] = val.astype(BF16)

    def epilogue(slot):
        if slot == PK:
            r = proj(PK)
            put(PK, r * lax.rsqrt(_head_meansq(r, hsum_ref) + EPS) * gk_ref[...])
        elif slot == PV:
            put(PV, proj(PV))

            @pl.when(pl.program_id(0) % tiles_per_stream == tiles_per_stream - 1)
            def _():
                r = proj(PK)
                kc_ref[...] = r * lax.rsqrt(_head_meansq(r, hsum_ref) + EPS) * gk_ref[...]
                vc_ref[...] = proj(PV)
        elif slot == PQ:
            r = proj(PQ)
            put(PQ, r * lax.rsqrt(_head_meansq(r, hsum_ref) + EPS) * (gq_ref[...] * np.float32(HEAD_DIM ** -0.5)))
        elif slot == PAU:
            put(PAU, _gelu_tanh(proj(PAU)))
        elif slot == PAV:
            r = _gelu_tanh(proj(PAV))
            av = r * lax.rsqrt(jnp.mean(r * r, axis=-1, keepdims=True) + EPS) * gav_ref[...]
            put(PAV, av)
            if emit_av:
                av_ref[...] = av
        elif slot == PGLU:
            pass
        elif slot == PDQ:
            put(PDQ, proj(PDQ) * np.float32(HEAD_DIM ** -0.5))
        elif slot in (PDK, PDV):
            put(slot, proj(slot))
        elif slot == PDR:
            r = proj(PDR)
            put(PDR, r * _sigmoid(r))
        elif slot == PLA:
            put(PGLU, proj(PGLU) * _sigmoid(proj(PLA)))
        else:
            dg = raw[:, (PLA + 1) * GRP:] * rs
            z = jnp.dot(dg.astype(BF16), wg2_ref[...], preferred_element_type=F32) + bg_ref[...]
            logsig = jnp.minimum(z, 0.0) - jnp.log(1.0 + jnp.exp(-jnp.abs(z)))
            put(PLA, logsig * np.float32(1.0 / GLA_TAU))

    order = (PLA + 1, PK, PQ, PAV, PAU, PGLU, PLA, PDR, PV, PDQ, PDK, PDV)
    assert sorted(order) == list(range(n_slots))
    lag = 2
    for i in range(n_slots + lag):
        if i < n_slots:
            matmul(order[i])
        if i >= lag:
            epilogue(order[i - lag])
        if i % 3 == 2 and i // 3 < n_cast:
            cast_out[i // 3][...] = cast_in[i // 3][...].astype(BF16)


def _in_proj(x, lw, t, y=None, gates=None, emit_av=False, cast=()):
    n = x.shape[0]
    tm = min(512, n)
    steps = n // tm
    combine = y is not None
    keep = min(B_WINDOW, t)
    tps = max(t // tm, 1)
    assert tps == 1 or keep == tm
    row = lambda i: (i, 0)
    const = lambda i: (0, 0)
    ins, specs = [x], [pl.BlockSpec((tm, D_MODEL), row)]
    if combine:
        ins += [y, gates]
        specs += [pl.BlockSpec((2, tm, D_MODEL // 2), lambda i: (0, i, 0)), pl.BlockSpec((tm, LANES), row)]
    consts = [lw["g1"], lw["w_in"], lw["w_cg"], lw["w_dg"], lw["wg2"], lw["bg"], lw["gq"], lw["gk"], lw["gav"], lw["hsum"]]
    ins += consts
    specs += [pl.BlockSpec(c.shape, const) for c in consts]
    cast_shapes, cast_specs = [], []
    layer = lw["layer"]
    for arr in cast:
        depth, rows, cols = arr.shape[0], arr.shape[1] * arr.shape[2], arr.shape[3]
        slab = rows // steps
        assert slab * steps == rows and slab % 16 == 0
        ins.append(arr.reshape(depth * rows, cols))
        specs.append(pl.BlockSpec((slab, cols), lambda i: (layer * steps + i, 0)))
        cast_shapes.append(jax.ShapeDtypeStruct((rows, cols), BF16))
        cast_specs.append(pl.BlockSpec((slab, cols), row))
    newest = jax.ShapeDtypeStruct((n // tps, GRP), F32)
    newest_spec = pl.BlockSpec((tm, GRP), lambda i: (i // tps, 0))
    out_shape = [jax.ShapeDtypeStruct((n, N_PROJ * GRP), BF16), newest, newest]
    out_specs = [pl.BlockSpec((tm, N_PROJ * GRP), row), newest_spec, newest_spec]
    if combine:
        out_shape = [jax.ShapeDtypeStruct((n, D_MODEL), F32)] + out_shape
        out_specs = [pl.BlockSpec((tm, D_MODEL), row)] + out_specs
    if emit_av:
        out_shape.append(jax.ShapeDtypeStruct((n, GRP), F32))
        out_specs.append(pl.BlockSpec((tm, GRP), row))
    out_shape += cast_shapes
    out_specs += cast_specs
    outs = list(pl.pallas_call(
        functools.partial(_in_kernel, combine=combine, emit_av=emit_av, tiles_per_stream=tps, n_cast=len(cast)),
        grid=(steps,), in_specs=specs, out_specs=out_specs, out_shape=out_shape,
        scratch_shapes=[pltpu.VMEM((tm, (PLA + 1) * GRP + LANES), F32)],
        compiler_params=_cparams(("arbitrary",)), name="in_proj",
    )(*ins))
    x_new = outs.pop(0) if combine else x
    proj, k_new, v_new = outs[0], outs[1], outs[2]
    a_v = outs[3] if emit_av else None
    casted = [o.reshape(a.shape[1:]) for o, a in zip(outs[3 + int(emit_av):], cast)]
    return x_new, proj, (k_new, v_new), a_v, casted


def _gmlp_body(u_ref, v_ref, ws_ref, bs_ref, o_ref, chunk, n_chunks):
    lane_h = _head_id((chunk, GRP), 1, HEAD_DIM)
    ri = lax.broadcasted_iota(I32, (chunk, chunk), 0)
    ci = lax.broadcasted_iota(I32, (chunk, chunk), 1)
    wm = [jnp.where(ci <= ri, ws_ref[h], 0.0).astype(BF16) for h in range(HEADS)]
    for c in range(n_chunks):
        rows = slice(c * chunk, (c + 1) * chunk)
        v = v_ref[rows, :]
        sv = jnp.dot(wm[HEADS - 1], v, preferred_element_type=F32)
        for h in range(HEADS - 2, -1, -1):
            sv = jnp.where(lane_h == h, jnp.dot(wm[h], v, preferred_element_type=F32), sv)
        o_ref[rows, :] = (u_ref[rows, :].astype(F32) * (sv + bs_ref[...])).astype(BF16)


def _attn_kernel(q_ref, kc_ref, vc_ref, kp_ref, vp_ref, bias_ref, o_ref, kbuf, vbuf, *, chunk, n_chunks, first_has_past):
    tq = chunk * n_chunks
    win = B_WINDOW + chunk
    kbuf[0:B_WINDOW, :] = kp_ref[...].astype(BF16)
    vbuf[0:B_WINDOW, :] = vp_ref[...].astype(BF16)
    kbuf[B_WINDOW:B_WINDOW + tq, :] = kc_ref[...]
    vbuf[B_WINDOW:B_WINDOW + tq, :] = vc_ref[...]
    col = lax.broadcasted_iota(I32, (HEADS * chunk, win), 1)

    def chunks(no_past):
        for c in range(n_chunks):
            q = q_ref[c * chunk:(c + 1) * chunk, :]
            kk = kbuf[c * chunk:c * chunk + win, :]
            vv = vbuf[c * chunk:c * chunk + win, :]
            s = lax.dot_general(_bd_stack(q, chunk), kk, (((1,), (1,)), ((), ())), preferred_element_type=F32)
            s = s + bias_ref[...]
            if no_past:
                s = jnp.where(col + c * chunk >= B_WINDOW, s, NEG_INF)
            m = jnp.max(s, axis=-1, keepdims=True)
            p = jnp.exp(s - m)
            l = jnp.sum(p, axis=-1, keepdims=True)
            o = jnp.dot(p.astype(BF16), vv, preferred_element_type=F32) * (1.0 / l)
            o_ref[c * chunk:(c + 1) * chunk, :] = _bd_unstack(o, chunk).astype(BF16)

    if first_has_past:
        chunks(False)
    else:
        pl.when(pl.program_id(1) == 0)(functools.partial(chunks, True))
        pl.when(pl.program_id(1) > 0)(functools.partial(chunks, False))


def _attention(proj, lw, b, t, cache_k=None, cache_v=None):
    n = proj.shape[0]
    step = cache_k is not None
    chunk = min(t, CHUNK)
    tq = min(t, B_WINDOW)
    nt = t // tq
    rel = lw["b_rel"]
    win = B_WINDOW + chunk
    lo = REL_CLIP - (chunk - 1)
    n_far = (chunk - 1) + win - (2 * REL_CLIP + 1 - lo)
    by_dist = jnp.concatenate([rel[:, lo:], jnp.broadcast_to(rel[:, -1:], (HEADS, n_far))], axis=1)
    by_key = by_dist[:, ::-1]
    n_k = chunk - 1 + win
    wrapped = jnp.tile(jnp.pad(by_key, ((0, 0), (0, 1))), (1, chunk))[:, :chunk * n_k].reshape(HEADS, chunk, n_k)
    bias = wrapped[:, :, chunk - 1:].astype(F32).reshape(HEADS * chunk, win)
    cur = lambda g: pl.BlockSpec((tq, GRP), lambda bi, j: (bi * nt + j, g))
    if step:
        prev_k = pl.BlockSpec((B_WINDOW, GRP), lambda bi, j: (bi, 0))
        prev_v = prev_k
        pk_arr, pv_arr = cache_k, cache_v
    else:
        assert tq == B_WINDOW
        prev_k = pl.BlockSpec((B_WINDOW, GRP), lambda bi, j: (bi * nt + jnp.maximum(j - 1, 0), PK))
        prev_v = pl.BlockSpec((B_WINDOW, GRP), lambda bi, j: (bi * nt + jnp.maximum(j - 1, 0), PV))
        pk_arr, pv_arr = proj, proj
    return pl.pallas_call(
        functools.partial(_attn_kernel, chunk=chunk, n_chunks=tq // chunk, first_has_past=step),
        grid=(b, nt),
        in_specs=[cur(PQ), cur(PK), cur(PV), prev_k, prev_v, pl.BlockSpec(bias.shape, lambda bi, j: (0, 0))],
        out_specs=pl.BlockSpec((tq, GRP), lambda bi, j: (bi * nt + j, 0)),
        out_shape=jax.ShapeDtypeStruct((n, GRP), BF16),
        scratch_shapes=[pltpu.VMEM((B_WINDOW + tq, GRP), BF16), pltpu.VMEM((B_WINDOW + tq, GRP), BF16)],
        compiler_params=_cparams(("arbitrary", "arbitrary")), name="band_attn",
    )(proj, proj, proj, pk_arr, pv_arr, bias)


def _conv_kernel(g_ref, halo_ref, dw_ref, dwb_ref, lng_ref, lnb_ref, u_ref, v_ref, ws_ref, bs_ref,
                 o_ref, tail_ref, ya_ref, xp, zbuf, *, tc, sub, first_has_past, a_chunk):
    _gmlp_body(u_ref, v_ref, ws_ref, bs_ref, ya_ref, a_chunk, tc // a_chunk)
    halo = halo_ref[...].astype(F32)
    has_past = jnp.logical_or(pl.program_id(1) > 0, first_has_past)
    xp[0:HALO, :] = jnp.where(has_past, halo, 0.0)
    xp[HALO:HALO + tc, :] = g_ref[...].astype(F32)
    xp[HALO + tc:, :] = jnp.zeros((xp.shape[0] - HALO - tc, GRP), F32)

    @pl.when(pl.program_id(1) == pl.num_programs(1) - 1)
    def _():
        tail_ref[...] = xp[tc:tc + HALO, :]

    lead = HALO - C_BUF
    sl = 8
    for s in range(tc // sub):
        acc = None
        for r in range(sl):
            taps = [p for p in range(r, lead + C_WIDTH, sl) if p >= lead]
            z = None
            for p in taps:
                a0 = s * sub + p - r
                term = dw_ref[p - lead:p - lead + 1, :] * xp[a0:a0 + sub + sl, :]
                z = term if z is None else z + term
            zbuf[r] = z
            part = zbuf[r, r:r + sub, :]
            acc = part if acc is None else acc + part
        y = acc + dwb_ref[...]
        mu = jnp.mean(y, axis=-1, keepdims=True)
        yc = y - mu
        y = yc * lax.rsqrt(jnp.mean(yc * yc, axis=-1, keepdims=True) + EPS) * lng_ref[...] + lnb_ref[...]
        o_ref[s * sub:(s + 1) * sub, :] = (y * _sigmoid(y)).astype(BF16)


def _conv_gmlp(proj, lw, b, t, state=None):
    n = proj.shape[0]
    step = state is not None
    tc = min(t, 512)
    nt = t // tc
    sub = min(tc, 64)
    a_chunk = min(t, A_CHUNK)
    ws = lw["a_ws"][:, :a_chunk, :a_chunk]
    bs = lw["a_bs_rows"][:a_chunk]
    if step:
        halo_arr = state
        halo_spec = pl.BlockSpec((HALO, GRP), lambda bi, j: (bi, 0))
    else:
        per = tc // HALO
        halo_arr = proj
        halo_spec = pl.BlockSpec((HALO, GRP), lambda bi, j: (jnp.maximum((bi * nt + j) * per - 1, 0), PGLU))
    vec = pl.BlockSpec((1, GRP), lambda bi, j: (0, 0))
    cur = lambda g: pl.BlockSpec((tc, GRP), lambda bi, j: (bi * nt + j, g))
    return pl.pallas_call(
        functools.partial(_conv_kernel, tc=tc, sub=sub, first_has_past=step, a_chunk=a_chunk),
        grid=(b, nt),
        in_specs=[cur(PGLU), halo_spec, pl.BlockSpec((C_WIDTH, GRP), lambda bi, j: (0, 0)), vec, vec, vec,
                  cur(PAU), cur(PAV), pl.BlockSpec(ws.shape, lambda bi, j: (0, 0, 0)),
                  pl.BlockSpec(bs.shape, lambda bi, j: (0, 0))],
        out_specs=[cur(0), pl.BlockSpec((HALO, GRP), lambda bi, j: (bi, 0)), cur(0)],
        out_shape=[jax.ShapeDtypeStruct((n, GRP), BF16), jax.ShapeDtypeStruct((b * HALO, GRP), F32),
                   jax.ShapeDtypeStruct((n, GRP), BF16)],
        scratch_shapes=[pltpu.VMEM((HALO + tc + 8, GRP), F32), pltpu.VMEM((8, sub + 8, GRP), F32)],
        compiler_params=_cparams(("arbitrary", "arbitrary")), name="conv_gmlp",
    )(proj, halo_arr, lw["c_dw"], lw["c_dw_b"], lw["c_ln_g"], lw["c_ln_b"], proj, proj, ws, bs)


def _gla_tables(L):
    i = np.arange(L)[:, None]
    t = np.arange(L)[None, :]
    masks = []
    s = GLA_SUB
    masks.append(((i // s) == (t // s)) & (t <= i))
    s *= 2
    while s <= L:
        h = s // 2
        masks.append(((i // s) == (t // s)) & (i % s >= h) & (t % s < h))
        s *= 2
    tri = (t <= i).astype(np.float32)
    mask = np.stack([np.tile(m.astype(np.float32), (1, HEADS)) for m in masks], axis=0)
    return tri, mask


def _gla_anchor(cum, row, L, size, first_half):
    out = None
    for start in range(0, L, size):
        ar = start + size // 2 - 1 if first_half else start - 1
        val = jnp.zeros((L, GRP), F32) if ar < 0 else jnp.broadcast_to(cum[ar:ar + 1, :], (L, GRP))
        out = val if out is None else jnp.where(row >= start, val, out)
    return out


def _gla_kernel(q_ref, k_ref, v_ref, la_ref, dr_ref, s0_ref, tri_ref, lmask_ref, bdmask_ref, hsum_ref, gon_ref,
                o_ref, sf_ref, st, o_all, *, L, n_chunks, n_levels, first_has_state):
    j = pl.program_id(1)

    @pl.when(j == 0)
    def _():
        st[...] = jnp.zeros_like(st)
        if first_has_state:
            for h in range(HEADS):
                blk = slice(h * HEAD_DIM, (h + 1) * HEAD_DIM)
                st[blk, blk] = s0_ref[blk, :].T

    row = lax.broadcasted_iota(I32, (L, GRP), 0)
    dn_t = (((1,), (1,)), ((), ()))

    def prep(c):
        rows = slice(c * L, (c + 1) * L)
        q = q_ref[rows, :].astype(F32)
        k = k_ref[rows, :].astype(F32)
        v = v_ref[rows, :]
        cum = jnp.dot(tri_ref[...], la_ref[rows, :], preferred_element_type=F32)
        total = cum[L - 1:L, :]
        pairs = []
        for lvl in range(n_levels):
            size = GLA_SUB << lvl
            if lvl == 0:
                local = cum - _gla_anchor(cum, row, L, size, False)
                ql = q * jnp.exp(local)
                kl = k * jnp.exp(-local)
            else:
                upper = (row & (size - 1)) >= (size // 2)
                d = cum - _gla_anchor(cum, row, L, size, True)
                w = jnp.exp(jnp.where(upper, d, -d))
                ql = jnp.where(upper, q * w, 0.0)
                kl = jnp.where(upper, 0.0, k * w)
            pairs.append((ql.astype(BF16), _bd_stack(kl.astype(BF16), L)))
        return dict(rows=rows, v=v, qp=(q * jnp.exp(cum)).astype(BF16), kst=(k * jnp.exp(total - cum)).astype(BF16),
                    decay=jnp.exp(total), pairs=pairs)

    def intra(p):
        att = None
        for lvl, (ql, kbd) in enumerate(p["pairs"]):
            a = lax.dot_general(ql, kbd, dn_t, preferred_element_type=F32) * lmask_ref[lvl]
            att = a if att is None else att + a
        p["o_intra"] = jnp.dot(att.astype(BF16), _bd_stack(p["v"], L), preferred_element_type=F32)
        p["upd"] = lax.dot_general(p["v"], p["kst"], (((0,), (0,)), ((), ())),
                                   preferred_element_type=F32) * bdmask_ref[...]
        return p

    def finish(p, s_t):
        o_all[p["rows"], :] = lax.dot_general(p["qp"], s_t.astype(BF16), dn_t, preferred_element_type=F32) + p["o_intra"]
        return s_t * p["decay"] + p["upd"]

    s_t = st[...]
    stage1, stage2 = {}, {}
    for step in range(n_chunks + 2):
        if step < n_chunks:
            stage1[step] = prep(step)
        if 0 <= step - 1 < n_chunks:
            stage2[step - 1] = intra(stage1.pop(step - 1))
        if 0 <= step - 2 < n_chunks:
            s_t = finish(stage2.pop(step - 2), s_t)
    st[...] = s_t
    o = o_all[...]
    y = o * lax.rsqrt(_head_meansq(o, hsum_ref) + EPS) * gon_ref[...] * dr_ref[...].astype(F32)
    o_ref[...] = y.astype(BF16)

    @pl.when(j == pl.num_programs(1) - 1)
    def _():
        for h in range(HEADS):
            blk = slice(h * HEAD_DIM, (h + 1) * HEAD_DIM)
            sf_ref[blk, :] = st[blk, blk].T


def _gla(proj, lw, b, t, s0=None):
    n = proj.shape[0]
    step = s0 is not None
    L = min(t, 64)
    td = min(t, 512)
    nt = t // td
    n_levels = int(np.log2(L // GLA_SUB)) + 1
    tri, lmask = _gla_tables(L)
    tri = jnp.asarray(tri, BF16)
    lmask = jnp.asarray(lmask, F32)
    if not step:
        s0 = jnp.zeros((GRP, HEAD_DIM), F32)
        s0_spec = pl.BlockSpec((GRP, HEAD_DIM), lambda bi, j: (0, 0))
    else:
        s0_spec = pl.BlockSpec((GRP, HEAD_DIM), lambda bi, j: (bi, 0))
    cur = lambda g: pl.BlockSpec((td, GRP), lambda bi, j: (bi * nt + j, g))
    c2 = lambda bi, j: (0, 0)
    return pl.pallas_call(
        functools.partial(_gla_kernel, L=L, n_chunks=td // L, n_levels=n_levels, first_has_state=step),
        grid=(b, nt),
        in_specs=[cur(PDQ), cur(PDK), cur(PDV), cur(PLA), cur(PDR), s0_spec,
                  pl.BlockSpec(tri.shape, c2), pl.BlockSpec(lmask.shape, lambda bi, j: (0, 0, 0)),
                  pl.BlockSpec((GRP, GRP), c2), pl.BlockSpec((GRP, GRP), c2), pl.BlockSpec((1, GRP), c2)],
        out_specs=[pl.BlockSpec((td, GRP), lambda bi, j: (bi * nt + j, 0)),
                   pl.BlockSpec((GRP, HEAD_DIM), lambda bi, j: (bi, 0))],
        out_shape=[jax.ShapeDtypeStruct((n, GRP), BF16), jax.ShapeDtypeStruct((b * GRP, HEAD_DIM), F32)],
        scratch_shapes=[pltpu.VMEM((GRP, GRP), F32), pltpu.VMEM((td, GRP), F32)],
        compiler_params=_cparams(("arbitrary", "arbitrary")), name="gla",
    )(proj, proj, proj, proj, proj, s0, tri, lmask, lw["bdmask"], lw["hsum"], lw["gon"])


def _out_kernel(ya_ref, yb_ref, yc_ref, yd_ref, x_ref, wo_ref, g2_ref, wr_ref, br_ref, tri_ref,
                xo_ref, xn_ref, ri_ref, rf_ref, cnt_ref, *, tm, n_sub):
    @pl.when(pl.program_id(0) == 0)
    def _():
        cnt_ref[...] = jnp.zeros_like(cnt_ref)

    sub = tm // n_sub
    lane = lax.broadcasted_iota(I32, (sub, LANES), 1)
    big = np.int32(1 << 20)

    def project(s):
        rows = slice(s * sub, (s + 1) * sub)
        ycat = jnp.concatenate([ya_ref[rows, :], yb_ref[rows, :], yc_ref[rows, :], yd_ref[rows, :]], axis=1)
        x = x_ref[rows, :] + jnp.dot(ycat, wo_ref[...], preferred_element_type=F32)
        xo_ref[rows, :] = x
        xn = x * lax.rsqrt(jnp.mean(x * x, axis=-1, keepdims=True) + EPS) * g2_ref[...]
        xn_ref[rows, :] = _pack_halves(xn)
        both = jnp.dot(xn.astype(BF16), wr_ref[...], preferred_element_type=F32)
        return both[:, :LANES] + both[:, LANES:] + br_ref[...]

    def route(s, logits):
        rows = slice(s * sub, (s + 1) * sub)

        def first_max(mask):
            v = jnp.max(jnp.where(mask, logits, -jnp.inf), axis=-1, keepdims=True)
            idx = jnp.min(jnp.where(jnp.logical_and(mask, logits == v), lane, big), axis=-1, keepdims=True)
            return v, idx

        is_grp = jnp.logical_and(lane >= N_EXPERTS, lane < N_EXPERTS + N_GROUPS)
        gmax, gidx = first_max(is_grp)
        p_grp = 1.0 / jnp.sum(jnp.where(is_grp, jnp.exp(logits - gmax), 0.0), axis=-1, keepdims=True)
        grp = gidx - N_EXPERTS
        in_grp = (lane // PER_GROUP) == grp
        v1, i1 = first_max(in_grp)
        v2, i2 = first_max(jnp.logical_and(in_grp, lane != i1))
        e21 = jnp.exp(v2 - v1)
        gate1 = p_grp / (1.0 + e21)
        gate2 = p_grp * e21 / (1.0 + e21)

        oh1 = lane == i1
        oh2 = lane == i2
        both = jnp.logical_or(oh1, oh2)
        ones = jnp.where(both, 1.0, 0.0).astype(BF16)
        before = jnp.dot(tri_ref[...], ones, preferred_element_type=F32) + cnt_ref[...].astype(F32)
        rank1 = jnp.sum(jnp.where(oh1, before, 0.0), axis=-1, keepdims=True)
        rank2 = jnp.sum(jnp.where(oh2, before, 0.0), axis=-1, keepdims=True)
        cnt_ref[...] = cnt_ref[...] + jnp.sum(jnp.where(both, 1.0, 0.0), axis=0, keepdims=True).astype(I32)

        ri = jnp.where(lane == 0, i1, jnp.where(lane == 1, i2, jnp.where(lane == 2, rank1.astype(I32),
                                                                          jnp.where(lane == 3, rank2.astype(I32), 0))))
        ri_ref[rows, :] = ri
        rf_ref[rows, :] = jnp.where(lane == 0, gate1, jnp.where(lane == 1, gate2, 0.0))

    logits = project(0)
    for s in range(n_sub):
        nxt = project(s + 1) if s + 1 < n_sub else None
        route(s, logits)
        logits = nxt


def _out_proj(ya, yb, yc, yd, x, lw):
    n = x.shape[0]
    tm = min(1024, n)
    row = lambda i: (i, 0)
    const = lambda i: (0, 0)
    n_sub = 2 if tm >= 1024 else 1
    sub = tm // n_sub
    tri = jnp.asarray(np.tril(np.ones((sub, sub), np.float32), -1), BF16)
    consts = [lw["w_out"], lw["g2"], lw["wr"], lw["br"], tri]
    yspec = pl.BlockSpec((tm, GRP), row)
    return pl.pallas_call(
        functools.partial(_out_kernel, tm=tm, n_sub=n_sub),
        grid=(n // tm,),
        in_specs=[yspec, yspec, yspec, yspec, pl.BlockSpec((tm, D_MODEL), row)] + [pl.BlockSpec(c.shape, const) for c in consts],
        out_specs=[pl.BlockSpec((tm, D_MODEL), row), pl.BlockSpec((tm, D_MODEL // 2), row),
                   pl.BlockSpec((tm, LANES), row), pl.BlockSpec((tm, LANES), row), pl.BlockSpec((1, LANES), const)],
        out_shape=[jax.ShapeDtypeStruct((n, D_MODEL), F32), jax.ShapeDtypeStruct((n, D_MODEL // 2), U32),
                   jax.ShapeDtypeStruct((n, LANES), I32), jax.ShapeDtypeStruct((n, LANES), F32),
                   jax.ShapeDtypeStruct((1, LANES), I32)],
        compiler_params=_cparams(("arbitrary",)), name="out_proj_router",
    )(ya, yb, yc, yd, x, *consts)


def _sc_scatter_rows(x, idx, n_out):
    n, d = x.shape
    kk = idx.shape[0]
    per_w = n // SC_WORKERS
    win = min(SC_WIN, per_w)
    n_win = per_w // win
    assert n_win * win * SC_WORKERS == n
    mesh = plsc.VectorSubcoreMesh(core_axis_name="c", subcore_axis_name="s")

    wpi = 2 if n_win % 2 == 0 else 1

    @functools.partial(
        pl.kernel, mesh=mesh, out_type=jax.ShapeDtypeStruct((n_out, d), x.dtype),
        scratch_types=[pltpu.VMEM((wpi * kk, win), I32), pltpu.VMEM((wpi, win, d), x.dtype),
                       pltpu.SemaphoreType.DMA((wpi,)), pltpu.SemaphoreType.DMA((wpi * kk,))],
        name="sc_scatter_rows")
    def k(x_hbm, idx_hbm, o_hbm, idx_v, rows_v, read_sem, write_sem):
        wid = lax.axis_index("s") * 2 + lax.axis_index("c")
        base = wid * per_w

        @pl.loop(0, n_win // wpi)
        def _(g):
            offs = [base + (g * wpi + b) * win for b in range(wpi)]
            reads = [pltpu.async_copy(x_hbm.at[pl.ds(offs[b], win)], rows_v.at[b], read_sem.at[b]) for b in range(wpi)]
            for b in range(wpi):
                for j in range(kk):
                    pltpu.sync_copy(idx_hbm.at[j, pl.ds(offs[b], win)], idx_v.at[b * kk + j])
            writes = []
            for b in range(wpi):
                reads[b].wait()
                for j in range(kk):
                    writes.append(pltpu.async_copy(rows_v.at[b], o_hbm.at[idx_v.at[b * kk + j]],
                                                   write_sem.at[b * kk + j]))
            for wr in writes:
                wr.wait()

    return k(x, idx)


def _sc_gather_rows(y, idx):
    _, d = y.shape
    kk, n = idx.shape
    per_w = n // SC_WORKERS
    win = min(SC_WIN, per_w)
    n_win = per_w // win
    assert n_win * win * SC_WORKERS == n
    mesh = plsc.VectorSubcoreMesh(core_axis_name="c", subcore_axis_name="s")

    @functools.partial(
        pl.kernel, mesh=mesh, out_type=jax.ShapeDtypeStruct((kk, n, d), y.dtype),
        scratch_types=[pltpu.VMEM((kk, win), I32), pltpu.VMEM((kk, win, d), y.dtype),
                       pltpu.SemaphoreType.DMA((kk,)), pltpu.SemaphoreType.DMA((kk,))],
        name="sc_gather_rows")
    def k(y_hbm, idx_hbm, o_hbm, idx_v, rows_v, gather_sem, write_sem):
        wid = lax.axis_index("s") * 2 + lax.axis_index("c")
        base = wid * per_w

        @pl.loop(0, n_win)
        def _(w):
            off = base + w * win
            for j in range(kk):
                pltpu.sync_copy(idx_hbm.at[j, pl.ds(off, win)], idx_v.at[j])
            gathers = [pltpu.async_copy(y_hbm.at[idx_v.at[j]], rows_v.at[j], gather_sem.at[j]) for j in range(kk)]
            writes = []
            for j in range(kk):
                gathers[j].wait()
                writes.append(pltpu.async_copy(rows_v.at[j], o_hbm.at[j, pl.ds(off, win)], write_sem.at[j]))
            for wr in writes:
                wr.wait()

    return k(y, idx)


def _moe_kernel(bexp_ref, nused_ref, x_ref, wg_ref, wu_ref, wd_ref, o_ref, *, n_sub):
    del bexp_ref

    @pl.when(pl.program_id(0) < nused_ref[0])
    def _():
        sub = x_ref.shape[0] // n_sub

        def up(s):
            w = x_ref[s * sub:(s + 1) * sub, :]
            x = jnp.concatenate([_unpack_hi(w).astype(BF16), _unpack_lo(w).astype(BF16)], axis=1)
            hg = jnp.dot(x, wg_ref[...], preferred_element_type=F32)
            hu = jnp.dot(x, wu_ref[...], preferred_element_type=F32)
            return (hg * _sigmoid(hg) * hu).astype(BF16)

        def down(s, h):
            o_ref[s * sub:(s + 1) * sub, :] = _pack_halves(jnp.dot(h, wd_ref[...], preferred_element_type=F32))

        h = up(0)
        for s in range(n_sub):
            nxt = up(s + 1) if s + 1 < n_sub else None
            down(s, h)
            h = nxt


def _moe_experts(xs, blk_exp, n_used, w_gate, w_up, w_down, bm):
    p = xs.shape[0]
    n_blocks = p // bm
    live = lambda i, be, nu: jnp.minimum(i, jnp.maximum(nu[0] - 1, 0))
    wspec = lambda shape: pl.BlockSpec((None,) + shape, lambda i, be, nu: (be[live(i, be, nu)], 0, 0))
    grid_spec = pltpu.PrefetchScalarGridSpec(
        num_scalar_prefetch=2, grid=(n_blocks,),
        in_specs=[pl.BlockSpec((bm, D_MODEL // 2), lambda i, be, nu: (live(i, be, nu), 0)),
                  wspec((D_MODEL, D_EXPERT)), wspec((D_MODEL, D_EXPERT)), wspec((D_EXPERT, D_MODEL))],
        out_specs=pl.BlockSpec((bm, D_MODEL // 2), lambda i, be, nu: (live(i, be, nu), 0)))
    return pl.pallas_call(
        functools.partial(_moe_kernel, n_sub=2 if bm >= 512 else 1),
        grid_spec=grid_spec, out_shape=jax.ShapeDtypeStruct((p, D_MODEL // 2), U32),
        compiler_params=_cparams(("arbitrary",)), name="moe_experts",
    )(blk_exp, n_used, xs, w_gate, w_up, w_down)


def _moe_block_rows(n):
    return 512 if n >= 16384 else 128


def _moe(xn_packed, route_i, counts, experts_bf16):
    n = xn_packed.shape[0]
    bm = _moe_block_rows(n)
    n_blocks = -(-(2 * n + N_EXPERTS * (bm - 1)) // bm)
    cnt = counts[0, :N_EXPERTS]
    padded = (cnt + bm - 1) // bm * bm
    pad_end = jnp.cumsum(padded)
    pad_start = pad_end - padded
    experts = jnp.arange(N_EXPERTS, dtype=I32)
    eid = route_i[:, 0:2].T
    start_of = jnp.sum(jnp.where(eid[:, :, None] == experts, pad_start, 0), axis=-1)
    dest = (start_of + route_i[:, 2:4].T).astype(I32)
    first_row = jnp.arange(n_blocks, dtype=I32) * bm
    blk_exp = jnp.minimum(jnp.sum((pad_end[None, :] <= first_row[:, None]).astype(I32), axis=1), N_EXPERTS - 1)
    n_used = (pad_end[-1:] // bm).astype(I32)
    xs = _sc_scatter_rows(xn_packed, dest, n_blocks * bm)
    ys = _moe_experts(xs, blk_exp, n_used, *experts_bf16, bm)
    return _sc_gather_rows(ys, dest)


def _combine_kernel(x_ref, y_ref, gate_ref, o_ref):
    half = D_MODEL // 2
    x = x_ref[...]
    g = gate_ref[...]
    g0 = g[:, 0:1]
    g1 = g[:, 1:2]
    w0 = y_ref[0]
    w1 = y_ref[1]
    o_ref[:, :half] = x[:, :half] + g0 * _unpack_hi(w0) + g1 * _unpack_hi(w1)
    o_ref[:, half:] = x[:, half:] + g0 * _unpack_lo(w0) + g1 * _unpack_lo(w1)


def _combine(x, y, gates):
    n = x.shape[0]
    tm = min(1024, n)
    row = lambda i: (i, 0)
    return pl.pallas_call(
        _combine_kernel, grid=(n // tm,),
        in_specs=[pl.BlockSpec((tm, D_MODEL), row), pl.BlockSpec((2, tm, D_MODEL // 2), lambda i: (0, i, 0)),
                  pl.BlockSpec((tm, LANES), row)],
        out_specs=pl.BlockSpec((tm, D_MODEL), row), out_shape=jax.ShapeDtypeStruct((n, D_MODEL), F32),
        compiler_params=_cparams(("arbitrary",)), name="moe_combine",
    )(x, y, gates)


def _layer_weights(l, p):
    w_in = p["w_in"][l]
    cols = [w_in[:, i * GRP:(i + 1) * GRP] for i in range(11)]
    by_group = [None] * N_PROJ
    for ref_i, g in enumerate(_REF_GROUPS):
        if g is not None:
            by_group[g] = cols[ref_i]
    by_group[PGLU] = cols[5]
    w_dg = jnp.zeros((D_MODEL, LANES), F32).at[:, :GATE_RANK].set(w_in[:, 11 * GRP:])
    wg2 = jnp.zeros((LANES, GRP), F32).at[:GATE_RANK].set(p["d_wg2"][l])
    tile4 = lambda v: jnp.tile(v, HEADS)[None, :]
    hid = np.arange(GRP) // HEAD_DIM
    bd = (hid[:, None] == hid[None, :]).astype(np.float32)
    wr = jnp.zeros((D_MODEL, LANES), F32).at[:, :N_EXPERTS].set(p["r_expert_w"][l])
    wr = wr.at[:, N_EXPERTS:N_EXPERTS + N_GROUPS].set(p["r_group_w"][l])
    wr_hi = wr.astype(BF16)
    br = jnp.zeros((1, LANES), F32).at[0, :N_EXPERTS].set(p["r_expert_b"][l])
    br = br.at[0, N_EXPERTS:N_EXPERTS + N_GROUPS].set(p["r_group_b"][l])
    return {
        "g1": p["norm1_g"][l][None, :],
        "w_in": jnp.concatenate(by_group[:PLA], axis=1).astype(BF16),
        "w_cg": cols[6].astype(BF16),
        "w_dg": w_dg.astype(BF16),
        "wg2": wg2.astype(BF16),
        "bg": p["d_bg"][l][None, :],
        "gq": tile4(p["b_qnorm_g"][l]), "gk": tile4(p["b_knorm_g"][l]), "gav": p["a_vnorm_g"][l][None, :],
        "gon": tile4(p["d_onorm_g"][l]),
        "hsum": jnp.asarray(bd, BF16), "bdmask": jnp.asarray(bd, F32),
        "a_ws": p["a_ws"][l], "a_bs_rows": jnp.repeat(p["a_bs"][l].T, HEAD_DIM, axis=1),
        "b_rel": p["b_rel_bias"][l],
        "c_dw": p["c_dw"][l], "c_dw_b": p["c_dw_b"][l][None, :],
        "c_ln_g": p["c_ln_g"][l][None, :], "c_ln_b": p["c_ln_b"][l][None, :],
        "w_out": p["w_out"][l].astype(BF16),
        "g2": p["norm2_g"][l][None, :],
        "wr": jnp.concatenate([wr_hi, (wr - wr_hi.astype(F32)).astype(BF16)], axis=1), "br": br,
        "e_w_gate": p["e_w_gate"], "e_w_up": p["e_w_up"], "e_w_down": p["e_w_down"], "layer": l,
    }


def _mix_and_route(x, lw, b, t, pending, caches, experts_bf16):
    step = caches is not None
    y_prev, gates_prev = pending if pending is not None else (None, None)
    cast = () if experts_bf16 is not None else (lw["e_w_gate"], lw["e_w_up"], lw["e_w_down"])
    x, proj, kv, a_v, casted = _in_proj(x, lw, t, y_prev, gates_prev, emit_av=step, cast=cast)
    if experts_bf16 is None:
        experts_bf16 = casted
    if not step:
        yb = _attention(proj, lw, b, t)
        yc, tail, ya = _conv_gmlp(proj, lw, b, t)
        yd, sf = _gla(proj, lw, b, t)
    else:
        ck, cv, cc, cs = caches
        yb = _attention(proj, lw, b, t, ck.reshape(b * B_WINDOW, GRP), cv.reshape(b * B_WINDOW, GRP))
        halo = jnp.pad(cc, ((0, 0), (HALO - C_BUF, 0), (0, 0))).reshape(b * HALO, GRP)
        yc, tail, ya = _conv_gmlp(proj, lw, b, t, halo)
        yd, sf = _gla(proj, lw, b, t, cs.reshape(b * GRP, HEAD_DIM))
        a_v = a_v.reshape(b, t, GRP)
    x2, xn_packed, route_i, route_f, counts = _out_proj(ya, yb, yc, yd, x, lw)
    y = _moe(xn_packed, route_i, counts, experts_bf16)
    keep = min(B_WINDOW, t)
    new_k = kv[0].reshape(b, keep, HEADS, HEAD_DIM)
    new_v = kv[1].reshape(b, keep, HEADS, HEAD_DIM)
    new_buf = tail.reshape(b, HALO, GRP)[:, HALO - C_BUF:]
    states = (new_k, new_v, new_buf, sf.reshape(b, HEADS, HEAD_DIM, HEAD_DIM), a_v)
    return x2, (y, route_f), states, experts_bf16


def kernel(x_prompt, x_sample, cache_b_k, cache_b_v, state_c_conv, state_d_gla, norm1_g, w_in, a_vnorm_g, a_ws, a_bs, b_qnorm_g, b_knorm_g, b_rel_bias, c_dw, c_dw_b, c_ln_g, c_ln_b, d_wg2, d_bg, d_onorm_g, w_out, norm2_g, r_group_w, r_group_b, r_expert_w, r_expert_b, e_w_gate, e_w_up, e_w_down):
    params = dict(norm1_g=norm1_g, w_in=w_in, a_vnorm_g=a_vnorm_g, a_ws=a_ws, a_bs=a_bs, b_qnorm_g=b_qnorm_g,
                  b_knorm_g=b_knorm_g, b_rel_bias=b_rel_bias, c_dw=c_dw, c_dw_b=c_dw_b, c_ln_g=c_ln_g, c_ln_b=c_ln_b,
                  d_wg2=d_wg2, d_bg=d_bg, d_onorm_g=d_onorm_g, w_out=w_out, norm2_g=norm2_g, r_group_w=r_group_w,
                  r_group_b=r_group_b, r_expert_w=r_expert_w, r_expert_b=r_expert_b, e_w_gate=e_w_gate,
                  e_w_up=e_w_up, e_w_down=e_w_down)
    depth = w_in.shape[0]
    bp, tp, _ = x_prompt.shape
    bs, ts, _ = x_sample.shape
    xp = x_prompt.reshape(bp * tp, D_MODEL)
    xs = x_sample.reshape(bs * ts, D_MODEL)
    pend_p = pend_s = None
    st_p, st_s = [], []
    for l in range(depth):
        lw = _layer_weights(l, params)
        xp, pend_p, sp, experts = _mix_and_route(xp, lw, bp, tp, pend_p, None, None)
        xs, pend_s, ss, _ = _mix_and_route(xs, lw, bs, ts, pend_s,
                                           (cache_b_k[l], cache_b_v[l], state_c_conv[l], state_d_gla[l]), experts)
        st_p.append(sp)
        st_s.append(ss)
    yp = _combine(xp, pend_p[0], pend_p[1]).reshape(bp, tp, D_MODEL)
    ys = _combine(xs, pend_s[0], pend_s[1]).reshape(bs, ts, D_MODEL)
    stack = lambda sts, i: jnp.stack([s[i] for s in sts])
    return (yp, ys, stack(st_p, 0), stack(st_p, 1), stack(st_p, 2), stack(st_p, 3),
            stack(st_s, 0), stack(st_s, 1), stack(st_s, 2), stack(st_s, 3), stack(st_s, 4))
```

```python
import functools

import numpy as np
import jax
import jax.numpy as jnp
from jax import lax
from jax.experimental import pallas as pl
from jax.experimental.pallas import tpu as pltpu
from jax.experimental.pallas import tpu_sc as plsc

F32 = jnp.float32
BF16 = jnp.bfloat16
I32 = jnp.int32
U32 = jnp.uint32

D_MODEL = 1024
GRP = 256
HEADS = 4
HEAD_DIM = 64
CHUNK = 64
A_CHUNK = 128
B_WINDOW = 512
REL_CLIP = 128
C_WIDTH = 31
C_BUF = C_WIDTH - 1
HALO = 32
GATE_RANK = 16
GLA_TAU = 16.0
GLA_SUB = 16
N_GROUPS = 4
PER_GROUP = 8
N_EXPERTS = 32
D_EXPERT = 512
EPS = 1e-6
NEG_INF = -1e30
LANES = 128
VMEM_LIMIT = 48 * 1024 * 1024

PK, PV, PQ, PAU, PAV, PGLU, PDQ, PDK, PDV, PDR, PLA = range(11)
N_PROJ = 11
_REF_GROUPS = (PAU, PAV, PQ, PK, PV, None, None, PDQ, PDK, PDV, PDR)

SC_WORKERS = 32
SC_WIN = 128


def _cparams(sem):
    return pltpu.CompilerParams(dimension_semantics=sem, vmem_limit_bytes=VMEM_LIMIT)


def _sigmoid(x):
    return 1.0 / (1.0 + jnp.exp(-x))


def _gelu_tanh(x):
    c = np.float32(np.sqrt(2.0 / np.pi))
    return 0.5 * x * (1.0 + jnp.tanh(c * (x + np.float32(0.044715) * (x * x * x))))


def _pack_halves(y):
    half = y.shape[1] // 2
    hi = pltpu.bitcast(y[:, :half].astype(BF16).astype(F32), U32)
    lo = pltpu.bitcast(y[:, half:].astype(BF16).astype(F32), U32)
    return hi | (lo >> np.uint32(16))


def _unpack_hi(w):
    return pltpu.bitcast(w & np.uint32(0xFFFF0000), F32)


def _unpack_lo(w):
    return pltpu.bitcast(w << np.uint32(16), F32)


def _head_id(shape, axis, size):
    return lax.broadcasted_iota(I32, shape, axis) // size


def _bd_stack(x, rows):
    x4 = jnp.concatenate([x] * HEADS, axis=0)
    shape = (HEADS * rows, GRP)
    keep = _head_id(shape, 0, rows) == _head_id(shape, 1, HEAD_DIM)
    return jnp.where(keep, x4, jnp.zeros_like(x4))


def _bd_unstack(o, rows):
    lane_h = _head_id((rows, GRP), 1, HEAD_DIM)
    out = o[(HEADS - 1) * rows:HEADS * rows]
    for h in range(HEADS - 2, -1, -1):
        out = jnp.where(lane_h == h, o[h * rows:(h + 1) * rows], out)
    return out


def _head_meansq(o, hsum_ref):
    sq = (o * o).astype(BF16)
    return jnp.dot(sq, hsum_ref[...], preferred_element_type=F32) * np.float32(1.0 / HEAD_DIM)


def _in_kernel(*refs, combine, emit_av, tiles_per_stream, n_cast):
    refs = list(refs)
    x_ref = refs.pop(0)
    if combine:
        y_ref = refs.pop(0)
        gate_ref = refs.pop(0)
    g1_ref, w_ref, wcg_ref, wdg_ref, wg2_ref, bg_ref, gq_ref, gk_ref, gav_ref, hsum_ref = refs[:10]
    cast_in = refs[10:10 + n_cast]
    refs = refs[10 + n_cast:]
    if combine:
        xo_ref = refs.pop(0)
    p_ref = refs.pop(0)
    kc_ref = refs.pop(0)
    vc_ref = refs.pop(0)
    if emit_av:
        av_ref = refs.pop(0)
    cast_out = refs[:n_cast]
    raw = refs[n_cast]
    n_sub = 1
    sub = x_ref.shape[0] // n_sub

    def prologue(s):
        rows = slice(s * sub, (s + 1) * sub)
        x = x_ref[rows, :]
        if combine:
            half = D_MODEL // 2
            g = gate_ref[rows, :]
            g0 = g[:, 0:1]
            g1 = g[:, 1:2]
            w0 = y_ref[0, rows, :]
            w1 = y_ref[1, rows, :]
            xa = x[:, :half] + g0 * _unpack_hi(w0) + g1 * _unpack_hi(w1)
            xb = x[:, half:] + g0 * _unpack_lo(w0) + g1 * _unpack_lo(w1)
            xo_ref[rows, :half] = xa
            xo_ref[rows, half:] = xb
            x = jnp.concatenate([xa, xb], axis=1)
        rs = lax.rsqrt(jnp.mean(x * x, axis=-1, keepdims=True) + EPS)
        return rows, rs, (x * g1_ref[...]).astype(BF16)

    n_slots = PLA + 2

    def matmul(tile, slot):
        rows, _, h = tile
        if slot < PLA:
            raw[rows, slot * GRP:(slot + 1) * GRP] = jnp.dot(h, w_ref[:, slot * GRP:(slot + 1) * GRP],
                                                             preferred_element_type=F32)
        elif slot == PLA:
            raw[rows, PLA * GRP:(PLA + 1) * GRP] = jnp.dot(h, wcg_ref[...], preferred_element_type=F32)
        else:
            raw[rows, (PLA + 1) * GRP:] = jnp.dot(h, wdg_ref[...], preferred_element_type=F32)

    def epilogue(tile, slot):
        rows, rs, _ = tile

        def proj(g):
            return raw[rows, g * GRP:(g + 1) * GRP] * rs

        def put(g, val):
            p_ref[rows, g * GRP:(g + 1) * GRP] = val.astype(BF16)

        if slot == PK:
            r = proj(PK)
            put(PK, r * lax.rsqrt(_head_meansq(r, hsum_ref) + EPS) * gk_ref[...])
        elif slot == PV:
            put(PV, proj(PV))

            @pl.when(pl.program_id(0) % tiles_per_stream == tiles_per_stream - 1)
            def _():
                r = proj(PK)
                kc_ref[rows, :] = r * lax.rsqrt(_head_meansq(r, hsum_ref) + EPS) * gk_ref[...]
                vc_ref[rows, :] = proj(PV)
        elif slot == PQ:
            r = proj(PQ)
            put(PQ, r * lax.rsqrt(_head_meansq(r, hsum_ref) + EPS) * (gq_ref[...] * np.float32(HEAD_DIM ** -0.5)))
        elif slot == PAU:
            put(PAU, _gelu_tanh(proj(PAU)))
        elif slot == PAV:
            r = _gelu_tanh(proj(PAV))
            av = r * lax.rsqrt(jnp.mean(r * r, axis=-1, keepdims=True) + EPS) * gav_ref[...]
            put(PAV, av)
            if emit_av:
                av_ref[rows, :] = av
        elif slot == PGLU:
            pass
        elif slot == PDQ:
            put(PDQ, proj(PDQ) * np.float32(HEAD_DIM ** -0.5))
        elif slot in (PDK, PDV):
            put(slot, proj(slot))
        elif slot == PDR:
            r = proj(PDR)
            put(PDR, r * _sigmoid(r))
        elif slot == PLA:
            put(PGLU, proj(PGLU) * _sigmoid(proj(PLA)))
        else:
            dg = raw[rows, (PLA + 1) * GRP:] * rs
            z = jnp.dot(dg.astype(BF16), wg2_ref[...], preferred_element_type=F32) + bg_ref[...]
            logsig = jnp.minimum(z, 0.0) - jnp.log(1.0 + jnp.exp(-jnp.abs(z)))
            put(PLA, logsig * np.float32(1.0 / GLA_TAU))

    order = (PLA + 1, PK, PQ, PAV, PAU, PGLU, PLA, PDR, PV, PDQ, PDK, PDV)
    assert sorted(order) == list(range(n_slots))
    pairs = [(s, slot) for s in range(n_sub) for slot in order]
    lag = 2
    tiles = {0: prologue(0)}
    for i in range(len(pairs) + lag):
        if i < len(pairs):
            matmul(tiles[pairs[i][0]], pairs[i][1])
        if i == lag and n_sub > 1:
            tiles[1] = prologue(1)
        if i >= lag:
            epilogue(tiles[pairs[i - lag][0]], pairs[i - lag][1])
        if i % 3 == 2 and i // 3 < n_cast:
            cast_out[i // 3][...] = cast_in[i // 3][...].astype(BF16)


def _in_proj(x, lw, t, y=None, gates=None, emit_av=False, cast=()):
    n = x.shape[0]
    tm = min(512, n)
    steps = n // tm
    combine = y is not None
    keep = min(B_WINDOW, t)
    tps = max(t // tm, 1)
    assert tps == 1 or keep == tm
    row = lambda i: (i, 0)
    const = lambda i: (0, 0)
    ins, specs = [x], [pl.BlockSpec((tm, D_MODEL), row)]
    if combine:
        ins += [y, gates]
        specs += [pl.BlockSpec((2, tm, D_MODEL // 2), lambda i: (0, i, 0)), pl.BlockSpec((tm, LANES), row)]
    consts = [lw["g1"], lw["w_in"], lw["w_cg"], lw["w_dg"], lw["wg2"], lw["bg"], lw["gq"], lw["gk"], lw["gav"], lw["hsum"]]
    ins += consts
    specs += [pl.BlockSpec(c.shape, const) for c in consts]
    cast_shapes, cast_specs = [], []
    layer = lw["layer"]
    for arr in cast:
        depth, rows, cols = arr.shape[0], arr.shape[1] * arr.shape[2], arr.shape[3]
        slab = rows // steps
        assert slab * steps == rows and slab % 16 == 0
        ins.append(arr.reshape(depth * rows, cols))
        specs.append(pl.BlockSpec((slab, cols), lambda i: (layer * steps + i, 0)))
        cast_shapes.append(jax.ShapeDtypeStruct((rows, cols), BF16))
        cast_specs.append(pl.BlockSpec((slab, cols), row))
    newest = jax.ShapeDtypeStruct((n // tps, GRP), F32)
    newest_spec = pl.BlockSpec((tm, GRP), lambda i: (i // tps, 0))
    out_shape = [jax.ShapeDtypeStruct((n, N_PROJ * GRP), BF16), newest, newest]
    out_specs = [pl.BlockSpec((tm, N_PROJ * GRP), row), newest_spec, newest_spec]
    if combine:
        out_shape = [jax.ShapeDtypeStruct((n, D_MODEL), F32)] + out_shape
        out_specs = [pl.BlockSpec((tm, D_MODEL), row)] + out_specs
    if emit_av:
        out_shape.append(jax.ShapeDtypeStruct((n, GRP), F32))
        out_specs.append(pl.BlockSpec((tm, GRP), row))
    out_shape += cast_shapes
    out_specs += cast_specs
    outs = list(pl.pallas_call(
        functools.partial(_in_kernel, combine=combine, emit_av=emit_av, tiles_per_stream=tps, n_cast=len(cast)),
        grid=(steps,), in_specs=specs, out_specs=out_specs, out_shape=out_shape,
        scratch_shapes=[pltpu.VMEM((tm, (PLA + 1) * GRP + LANES), F32)],
        compiler_params=_cparams(("arbitrary",)), name="in_proj",
    )(*ins))
    x_new = outs.pop(0) if combine else x
    proj, k_new, v_new = outs[0], outs[1], outs[2]
    a_v = outs[3] if emit_av else None
    casted = [o.reshape(a.shape[1:]) for o, a in zip(outs[3 + int(emit_av):], cast)]
    return x_new, proj, (k_new, v_new), a_v, casted


def _gmlp_body(u_ref, v_ref, ws_ref, bs_ref, o_ref, chunk, n_chunks):
    lane_h = _head_id((chunk, GRP), 1, HEAD_DIM)
    ri = lax.broadcasted_iota(I32, (chunk, chunk), 0)
    ci = lax.broadcasted_iota(I32, (chunk, chunk), 1)
    wm = [jnp.where(ci <= ri, ws_ref[h], 0.0).astype(BF16) for h in range(HEADS)]
    for c in range(n_chunks):
        rows = slice(c * chunk, (c + 1) * chunk)
        v = v_ref[rows, :]
        sv = jnp.dot(wm[HEADS - 1], v, preferred_element_type=F32)
        for h in range(HEADS - 2, -1, -1):
            sv = jnp.where(lane_h == h, jnp.dot(wm[h], v, preferred_element_type=F32), sv)
        o_ref[rows, :] = (u_ref[rows, :].astype(F32) * (sv + bs_ref[...])).astype(BF16)


def _attn_kernel(q_ref, kc_ref, vc_ref, kp_ref, vp_ref, bias_ref, o_ref, kbuf, vbuf, *, chunk, n_chunks, first_has_past):
    tq = chunk * n_chunks
    win = B_WINDOW + chunk
    kbuf[0:B_WINDOW, :] = kp_ref[...].astype(BF16)
    vbuf[0:B_WINDOW, :] = vp_ref[...].astype(BF16)
    kbuf[B_WINDOW:B_WINDOW + tq, :] = kc_ref[...]
    vbuf[B_WINDOW:B_WINDOW + tq, :] = vc_ref[...]
    col = lax.broadcasted_iota(I32, (HEADS * chunk, win), 1)

    def chunks(no_past):
        for c in range(n_chunks):
            q = q_ref[c * chunk:(c + 1) * chunk, :]
            kk = kbuf[c * chunk:c * chunk + win, :]
            vv = vbuf[c * chunk:c * chunk + win, :]
            s = lax.dot_general(_bd_stack(q, chunk), kk, (((1,), (1,)), ((), ())), preferred_element_type=F32)
            s = s + bias_ref[...]
            if no_past:
                s = jnp.where(col + c * chunk >= B_WINDOW, s, NEG_INF)
            m = jnp.max(s, axis=-1, keepdims=True)
            p = jnp.exp(s - m)
            l = jnp.sum(p, axis=-1, keepdims=True)
            o = jnp.dot(p.astype(BF16), vv, preferred_element_type=F32) * (1.0 / l)
            o_ref[c * chunk:(c + 1) * chunk, :] = _bd_unstack(o, chunk).astype(BF16)

    if first_has_past:
        chunks(False)
    else:
        pl.when(pl.program_id(1) == 0)(functools.partial(chunks, True))
        pl.when(pl.program_id(1) > 0)(functools.partial(chunks, False))


def _attention(proj, lw, b, t, cache_k=None, cache_v=None):
    n = proj.shape[0]
    step = cache_k is not None
    chunk = min(t, CHUNK)
    tq = min(t, B_WINDOW)
    nt = t // tq
    rel = lw["b_rel"]
    win = B_WINDOW + chunk
    lo = REL_CLIP - (chunk - 1)
    n_far = (chunk - 1) + win - (2 * REL_CLIP + 1 - lo)
    by_dist = jnp.concatenate([rel[:, lo:], jnp.broadcast_to(rel[:, -1:], (HEADS, n_far))], axis=1)
    by_key = by_dist[:, ::-1]
    n_k = chunk - 1 + win
    wrapped = jnp.tile(jnp.pad(by_key, ((0, 0), (0, 1))), (1, chunk))[:, :chunk * n_k].reshape(HEADS, chunk, n_k)
    bias = wrapped[:, :, chunk - 1:].astype(F32).reshape(HEADS * chunk, win)
    cur = lambda g: pl.BlockSpec((tq, GRP), lambda bi, j: (bi * nt + j, g))
    if step:
        prev_k = pl.BlockSpec((B_WINDOW, GRP), lambda bi, j: (bi, 0))
        prev_v = prev_k
        pk_arr, pv_arr = cache_k, cache_v
    else:
        assert tq == B_WINDOW
        prev_k = pl.BlockSpec((B_WINDOW, GRP), lambda bi, j: (bi * nt + jnp.maximum(j - 1, 0), PK))
        prev_v = pl.BlockSpec((B_WINDOW, GRP), lambda bi, j: (bi * nt + jnp.maximum(j - 1, 0), PV))
        pk_arr, pv_arr = proj, proj
    return pl.pallas_call(
        functools.partial(_attn_kernel, chunk=chunk, n_chunks=tq // chunk, first_has_past=step),
        grid=(b, nt),
        in_specs=[cur(PQ), cur(PK), cur(PV), prev_k, prev_v, pl.BlockSpec(bias.shape, lambda bi, j: (0, 0))],
        out_specs=pl.BlockSpec((tq, GRP), lambda bi, j: (bi * nt + j, 0)),
        out_shape=jax.ShapeDtypeStruct((n, GRP), BF16),
        scratch_shapes=[pltpu.VMEM((B_WINDOW + tq, GRP), BF16), pltpu.VMEM((B_WINDOW + tq, GRP), BF16)],
        compiler_params=_cparams(("arbitrary", "arbitrary")), name="band_attn",
    )(proj, proj, proj, pk_arr, pv_arr, bias)


def _conv_kernel(g_ref, halo_ref, dw_ref, dwb_ref, lng_ref, lnb_ref, u_ref, v_ref, ws_ref, bs_ref,
                 o_ref, tail_ref, ya_ref, xp, zbuf, *, tc, sub, first_has_past, a_chunk):
    _gmlp_body(u_ref, v_ref, ws_ref, bs_ref, ya_ref, a_chunk, tc // a_chunk)
    halo = halo_ref[...].astype(F32)
    has_past = jnp.logical_or(pl.program_id(1) > 0, first_has_past)
    xp[0:HALO, :] = jnp.where(has_past, halo, 0.0)
    xp[HALO:HALO + tc, :] = g_ref[...].astype(F32)
    xp[HALO + tc:, :] = jnp.zeros((xp.shape[0] - HALO - tc, GRP), F32)

    @pl.when(pl.program_id(1) == pl.num_programs(1) - 1)
    def _():
        tail_ref[...] = xp[tc:tc + HALO, :]

    lead = HALO - C_BUF
    sl = 8
    for s in range(tc // sub):
        acc = None
        for r in range(sl):
            taps = [p for p in range(r, lead + C_WIDTH, sl) if p >= lead]
            z = None
            for p in taps:
                a0 = s * sub + p - r
                term = dw_ref[p - lead:p - lead + 1, :] * xp[a0:a0 + sub + sl, :]
                z = term if z is None else z + term
            zbuf[r] = z
            part = zbuf[r, r:r + sub, :]
            acc = part if acc is None else acc + part
        y = acc + dwb_ref[...]
        mu = jnp.mean(y, axis=-1, keepdims=True)
        yc = y - mu
        y = yc * lax.rsqrt(jnp.mean(yc * yc, axis=-1, keepdims=True) + EPS) * lng_ref[...] + lnb_ref[...]
        o_ref[s * sub:(s + 1) * sub, :] = (y * _sigmoid(y)).astype(BF16)


def _conv_gmlp(proj, lw, b, t, state=None):
    n = proj.shape[0]
    step = state is not None
    tc = min(t, 512)
    nt = t // tc
    sub = min(tc, 64)
    a_chunk = min(t, A_CHUNK)
    ws = lw["a_ws"][:, :a_chunk, :a_chunk]
    bs = lw["a_bs_rows"][:a_chunk]
    if step:
        halo_arr = state
        halo_spec = pl.BlockSpec((HALO, GRP), lambda bi, j: (bi, 0))
    else:
        per = tc // HALO
        halo_arr = proj
        halo_spec = pl.BlockSpec((HALO, GRP), lambda bi, j: (jnp.maximum((bi * nt + j) * per - 1, 0), PGLU))
    vec = pl.BlockSpec((1, GRP), lambda bi, j: (0, 0))
    cur = lambda g: pl.BlockSpec((tc, GRP), lambda bi, j: (bi * nt + j, g))
    return pl.pallas_call(
        functools.partial(_conv_kernel, tc=tc, sub=sub, first_has_past=step, a_chunk=a_chunk),
        grid=(b, nt),
        in_specs=[cur(PGLU), halo_spec, pl.BlockSpec((C_WIDTH, GRP), lambda bi, j: (0, 0)), vec, vec, vec,
                  cur(PAU), cur(PAV), pl.BlockSpec(ws.shape, lambda bi, j: (0, 0, 0)),
                  pl.BlockSpec(bs.shape, lambda bi, j: (0, 0))],
        out_specs=[cur(0), pl.BlockSpec((HALO, GRP), lambda bi, j: (bi, 0)), cur(0)],
        out_shape=[jax.ShapeDtypeStruct((n, GRP), BF16), jax.ShapeDtypeStruct((b * HALO, GRP), F32),
                   jax.ShapeDtypeStruct((n, GRP), BF16)],
        scratch_shapes=[pltpu.VMEM((HALO + tc + 8, GRP), F32), pltpu.VMEM((8, sub + 8, GRP), F32)],
        compiler_params=_cparams(("arbitrary", "arbitrary")), name="conv_gmlp",
    )(proj, halo_arr, lw["c_dw"], lw["c_dw_b"], lw["c_ln_g"], lw["c_ln_b"], proj, proj, ws, bs)


def _gla_tables(L):
    i = np.arange(L)[:, None]
    t = np.arange(L)[None, :]
    masks = []
    s = GLA_SUB
    masks.append(((i // s) == (t // s)) & (t <= i))
    s *= 2
    while s <= L:
        h = s // 2
        masks.append(((i // s) == (t // s)) & (i % s >= h) & (t % s < h))
        s *= 2
    tri = (t <= i).astype(np.float32)
    mask = np.stack([np.tile(m.astype(np.float32), (1, HEADS)) for m in masks], axis=0)
    return tri, mask


def _gla_anchor(cum, row, L, size, first_half):
    out = None
    for start in range(0, L, size):
        ar = start + size // 2 - 1 if first_half else start - 1
        val = jnp.zeros((L, GRP), F32) if ar < 0 else jnp.broadcast_to(cum[ar:ar + 1, :], (L, GRP))
        out = val if out is None else jnp.where(row >= start, val, out)
    return out


def _gla_kernel(q_ref, k_ref, v_ref, la_ref, dr_ref, s0_ref, tri_ref, lmask_ref, bdmask_ref, hsum_ref, gon_ref,
                o_ref, sf_ref, st, o_all, *, L, n_chunks, n_levels, first_has_state):
    j = pl.program_id(1)

    @pl.when(j == 0)
    def _():
        st[...] = jnp.zeros_like(st)
        if first_has_state:
            for h in range(HEADS):
                blk = slice(h * HEAD_DIM, (h + 1) * HEAD_DIM)
                st[blk, blk] = s0_ref[blk, :].T

    row = lax.broadcasted_iota(I32, (L, GRP), 0)
    dn_t = (((1,), (1,)), ((), ()))

    def prep(c):
        rows = slice(c * L, (c + 1) * L)
        q = q_ref[rows, :].astype(F32)
        k = k_ref[rows, :].astype(F32)
        v = v_ref[rows, :]
        cum = jnp.dot(tri_ref[...], la_ref[rows, :], preferred_element_type=F32)
        total = cum[L - 1:L, :]
        pairs = []
        for lvl in range(n_levels):
            size = GLA_SUB << lvl
            if lvl == 0:
                local = cum - _gla_anchor(cum, row, L, size, False)
                ql = q * jnp.exp(local)
                kl = k * jnp.exp(-local)
            else:
                upper = (row & (size - 1)) >= (size // 2)
                d = cum - _gla_anchor(cum, row, L, size, True)
                w = jnp.exp(jnp.where(upper, d, -d))
                ql = jnp.where(upper, q * w, 0.0)
                kl = jnp.where(upper, 0.0, k * w)
            pairs.append((ql.astype(BF16), _bd_stack(kl.astype(BF16), L)))
        return dict(rows=rows, v=v, qp=(q * jnp.exp(cum)).astype(BF16), kst=(k * jnp.exp(total - cum)).astype(BF16),
                    decay=jnp.exp(total), pairs=pairs)

    def intra(p):
        att = None
        for lvl, (ql, kbd) in enumerate(p["pairs"]):
            a = lax.dot_general(ql, kbd, dn_t, preferred_element_type=F32) * lmask_ref[lvl]
            att = a if att is None else att + a
        p["o_intra"] = jnp.dot(att.astype(BF16), _bd_stack(p["v"], L), preferred_element_type=F32)
        p["upd"] = lax.dot_general(p["v"], p["kst"], (((0,), (0,)), ((), ())),
                                   preferred_element_type=F32) * bdmask_ref[...]
        return p

    def finish(p, s_t):
        o_all[p["rows"], :] = lax.dot_general(p["qp"], s_t.astype(BF16), dn_t, preferred_element_type=F32) + p["o_intra"]
        return s_t * p["decay"] + p["upd"]

    s_t = st[...]
    stage1, stage2 = {}, {}
    for step in range(n_chunks + 2):
        if step < n_chunks:
            stage1[step] = prep(step)
        if 0 <= step - 1 < n_chunks:
            stage2[step - 1] = intra(stage1.pop(step - 1))
        if 0 <= step - 2 < n_chunks:
            s_t = finish(stage2.pop(step - 2), s_t)
    st[...] = s_t
    o = o_all[...]
    y = o * lax.rsqrt(_head_meansq(o, hsum_ref) + EPS) * gon_ref[...] * dr_ref[...].astype(F32)
    o_ref[...] = y.astype(BF16)

    @pl.when(j == pl.num_programs(1) - 1)
    def _():
        for h in range(HEADS):
            blk = slice(h * HEAD_DIM, (h + 1) * HEAD_DIM)
            sf_ref[blk, :] = st[blk, blk].T


def _gla(proj, lw, b, t, s0=None):
    n = proj.shape[0]
    step = s0 is not None
    L = min(t, 64)
    td = min(t, 512)
    nt = t // td
    n_levels = int(np.log2(L // GLA_SUB)) + 1
    tri, lmask = _gla_tables(L)
    tri = jnp.asarray(tri, BF16)
    lmask = jnp.asarray(lmask, F32)
    if not step:
        s0 = jnp.zeros((GRP, HEAD_DIM), F32)
        s0_spec = pl.BlockSpec((GRP, HEAD_DIM), lambda bi, j: (0, 0))
    else:
        s0_spec = pl.BlockSpec((GRP, HEAD_DIM), lambda bi, j: (bi, 0))
    cur = lambda g: pl.BlockSpec((td, GRP), lambda bi, j: (bi * nt + j, g))
    c2 = lambda bi, j: (0, 0)
    return pl.pallas_call(
        functools.partial(_gla_kernel, L=L, n_chunks=td // L, n_levels=n_levels, first_has_state=step),
        grid=(b, nt),
        in_specs=[cur(PDQ), cur(PDK), cur(PDV), cur(PLA), cur(PDR), s0_spec,
                  pl.BlockSpec(tri.shape, c2), pl.BlockSpec(lmask.shape, lambda bi, j: (0, 0, 0)),
                  pl.BlockSpec((GRP, GRP), c2), pl.BlockSpec((GRP, GRP), c2), pl.BlockSpec((1, GRP), c2)],
        out_specs=[pl.BlockSpec((td, GRP), lambda bi, j: (bi * nt + j, 0)),
                   pl.BlockSpec((GRP, HEAD_DIM), lambda bi, j: (bi, 0))],
        out_shape=[jax.ShapeDtypeStruct((n, GRP), BF16), jax.ShapeDtypeStruct((b * GRP, HEAD_DIM), F32)],
        scratch_shapes=[pltpu.VMEM((GRP, GRP), F32), pltpu.VMEM((td, GRP), F32)],
        compiler_params=_cparams(("arbitrary", "arbitrary")), name="gla",
    )(proj, proj, proj, proj, proj, s0, tri, lmask, lw["bdmask"], lw["hsum"], lw["gon"])


def _out_kernel(ya_ref, yb_ref, yc_ref, yd_ref, x_ref, wo_ref, g2_ref, wr_ref, br_ref, tri_ref,
                xo_ref, xn_ref, ri_ref, rf_ref, cnt_ref, *, tm, n_sub):
    @pl.when(pl.program_id(0) == 0)
    def _():
        cnt_ref[...] = jnp.zeros_like(cnt_ref)

    sub = tm // n_sub
    lane = lax.broadcasted_iota(I32, (sub, LANES), 1)
    lane_f = lane.astype(F32)
    lane_grp_f = (lane // PER_GROUP).astype(F32)
    is_grp = jnp.logical_and(lane >= N_EXPERTS, lane < N_EXPERTS + N_GROUPS)
    far = np.float32(1 << 20)

    def mix(s):
        rows = slice(s * sub, (s + 1) * sub)
        ycat = jnp.concatenate([ya_ref[rows, :], yb_ref[rows, :], yc_ref[rows, :], yd_ref[rows, :]], axis=1)
        x = x_ref[rows, :] + jnp.dot(ycat, wo_ref[...], preferred_element_type=F32)
        xo_ref[rows, :] = x
        return x

    def norm_logits(s, x):
        rows = slice(s * sub, (s + 1) * sub)
        xn = x * lax.rsqrt(jnp.mean(x * x, axis=-1, keepdims=True) + EPS) * g2_ref[...]
        xn_ref[rows, :] = _pack_halves(xn)
        both = jnp.dot(xn.astype(BF16), wr_ref[...], preferred_element_type=F32)
        return both[:, :LANES] + both[:, LANES:] + br_ref[...]

    def route(s, logits):
        rows = slice(s * sub, (s + 1) * sub)

        def first_max(masked):
            v = jnp.max(masked, axis=-1, keepdims=True)
            return v, jnp.min(jnp.where(masked == v, lane_f, far), axis=-1, keepdims=True)

        grp_logits = jnp.where(is_grp, logits, -jnp.inf)
        gmax, gidx = first_max(grp_logits)
        p_grp = 1.0 / jnp.sum(jnp.exp(grp_logits - gmax), axis=-1, keepdims=True)
        in_grp = lane_grp_f == gidx - np.float32(N_EXPERTS)
        exp_logits = jnp.where(in_grp, logits, -jnp.inf)
        v1, i1 = first_max(exp_logits)
        v2, i2 = first_max(jnp.where(lane_f == i1, -jnp.inf, exp_logits))
        e21 = jnp.exp(v2 - v1)
        gate1 = p_grp / (1.0 + e21)
        gate2 = p_grp * e21 / (1.0 + e21)

        oh1 = lane_f == i1
        oh2 = lane_f == i2
        ones = jnp.where(oh1, 1.0, jnp.where(oh2, 1.0, 0.0))
        before = jnp.dot(tri_ref[...], ones.astype(BF16), preferred_element_type=F32) + cnt_ref[...].astype(F32)
        rank1 = jnp.sum(jnp.where(oh1, before, 0.0), axis=-1, keepdims=True)
        rank2 = jnp.sum(jnp.where(oh2, before, 0.0), axis=-1, keepdims=True)
        cnt_ref[...] = cnt_ref[...] + jnp.sum(ones, axis=0, keepdims=True).astype(I32)

        ri = jnp.where(lane == 0, i1, jnp.where(lane == 1, i2, jnp.where(lane == 2, rank1,
                                                                          jnp.where(lane == 3, rank2, 0.0))))
        ri_ref[rows, :] = ri.astype(I32)
        rf_ref[rows, :] = jnp.where(lane == 0, gate1, jnp.where(lane == 1, gate2, 0.0))

    xs, lg = {}, {}
    for step in range(n_sub + 2):
        if step < n_sub:
            xs[step] = mix(step)
        if 0 <= step - 1 < n_sub:
            lg[step - 1] = norm_logits(step - 1, xs.pop(step - 1))
        if 0 <= step - 2 < n_sub:
            route(step - 2, lg.pop(step - 2))


def _out_proj(ya, yb, yc, yd, x, lw):
    n = x.shape[0]
    tm = min(1024, n)
    row = lambda i: (i, 0)
    const = lambda i: (0, 0)
    n_sub = 4 if tm >= 1024 else 1
    sub = tm // n_sub
    tri = jnp.asarray(np.tril(np.ones((sub, sub), np.float32), -1), BF16)
    consts = [lw["w_out"], lw["g2"], lw["wr"], lw["br"], tri]
    yspec = pl.BlockSpec((tm, GRP), row)
    return pl.pallas_call(
        functools.partial(_out_kernel, tm=tm, n_sub=n_sub),
        grid=(n // tm,),
        in_specs=[yspec, yspec, yspec, yspec, pl.BlockSpec((tm, D_MODEL), row)] + [pl.BlockSpec(c.shape, const) for c in consts],
        out_specs=[pl.BlockSpec((tm, D_MODEL), row), pl.BlockSpec((tm, D_MODEL // 2), row),
                   pl.BlockSpec((tm, LANES), row), pl.BlockSpec((tm, LANES), row), pl.BlockSpec((1, LANES), const)],
        out_shape=[jax.ShapeDtypeStruct((n, D_MODEL), F32), jax.ShapeDtypeStruct((n, D_MODEL // 2), U32),
                   jax.ShapeDtypeStruct((n, LANES), I32), jax.ShapeDtypeStruct((n, LANES), F32),
                   jax.ShapeDtypeStruct((1, LANES), I32)],
        compiler_params=_cparams(("arbitrary",)), name="out_proj_router",
    )(ya, yb, yc, yd, x, *consts)


def _sc_scatter_rows(x, idx, n_out):
    n, d = x.shape
    kk = idx.shape[0]
    per_w = n // SC_WORKERS
    win = min(SC_WIN, per_w)
    n_win = per_w // win
    assert n_win * win * SC_WORKERS == n
    mesh = plsc.VectorSubcoreMesh(core_axis_name="c", subcore_axis_name="s")

    @functools.partial(
        pl.kernel, mesh=mesh, out_type=jax.ShapeDtypeStruct((n_out, d), x.dtype),
        scratch_types=[pltpu.VMEM((kk, win), I32), pltpu.VMEM((win, d), x.dtype)],
        name="sc_scatter_rows")
    def k(x_hbm, idx_hbm, o_hbm, idx_v, rows_v):
        wid = lax.axis_index("s") * 2 + lax.axis_index("c")
        base = wid * per_w

        @pl.loop(0, n_win)
        def _(w):
            off = base + w * win
            pltpu.sync_copy(x_hbm.at[pl.ds(off, win)], rows_v)
            for j in range(kk):
                pltpu.sync_copy(idx_hbm.at[j, pl.ds(off, win)], idx_v.at[j])
                pltpu.sync_copy(rows_v, o_hbm.at[idx_v.at[j]])

    return k(x, idx)


def _sc_gather_rows(y, idx):
    _, d = y.shape
    kk, n = idx.shape
    per_w = n // SC_WORKERS
    win = min(SC_WIN, per_w)
    n_win = per_w // win
    assert n_win * win * SC_WORKERS == n
    mesh = plsc.VectorSubcoreMesh(core_axis_name="c", subcore_axis_name="s")

    @functools.partial(
        pl.kernel, mesh=mesh, out_type=jax.ShapeDtypeStruct((kk, n, d), y.dtype),
        scratch_types=[pltpu.VMEM((kk, win), I32), pltpu.VMEM((win, d), y.dtype)],
        name="sc_gather_rows")
    def k(y_hbm, idx_hbm, o_hbm, idx_v, rows_v):
        wid = lax.axis_index("s") * 2 + lax.axis_index("c")
        base = wid * per_w

        @pl.loop(0, n_win)
        def _(w):
            off = base + w * win
            for j in range(kk):
                pltpu.sync_copy(idx_hbm.at[j, pl.ds(off, win)], idx_v.at[j])
                pltpu.sync_copy(y_hbm.at[idx_v.at[j]], rows_v)
                pltpu.sync_copy(rows_v, o_hbm.at[j, pl.ds(off, win)])

    return k(y, idx)


def _moe_kernel(bexp_ref, nused_ref, x_ref, wg_ref, wu_ref, wd_ref, o_ref, *, n_sub):
    del bexp_ref

    @pl.when(pl.program_id(0) < nused_ref[0])
    def _():
        sub = x_ref.shape[0] // n_sub

        def up(s):
            w = x_ref[s * sub:(s + 1) * sub, :]
            x = jnp.concatenate([_unpack_hi(w).astype(BF16), _unpack_lo(w).astype(BF16)], axis=1)
            hg = jnp.dot(x, wg_ref[...], preferred_element_type=F32)
            hu = jnp.dot(x, wu_ref[...], preferred_element_type=F32)
            return (hg * _sigmoid(hg) * hu).astype(BF16)

        def down(s, h):
            o_ref[s * sub:(s + 1) * sub, :] = _pack_halves(jnp.dot(h, wd_ref[...], preferred_element_type=F32))

        h = up(0)
        for s in range(n_sub):
            nxt = up(s + 1) if s + 1 < n_sub else None
            down(s, h)
            h = nxt


def _moe_experts(xs, blk_exp, n_used, w_gate, w_up, w_down, bm):
    p = xs.shape[0]
    n_blocks = p // bm
    live = lambda i, be, nu: jnp.minimum(i, jnp.maximum(nu[0] - 1, 0))
    wspec = lambda shape: pl.BlockSpec((None,) + shape, lambda i, be, nu: (be[live(i, be, nu)], 0, 0))
    grid_spec = pltpu.PrefetchScalarGridSpec(
        num_scalar_prefetch=2, grid=(n_blocks,),
        in_specs=[pl.BlockSpec((bm, D_MODEL // 2), lambda i, be, nu: (live(i, be, nu), 0)),
                  wspec((D_MODEL, D_EXPERT)), wspec((D_MODEL, D_EXPERT)), wspec((D_EXPERT, D_MODEL))],
        out_specs=pl.BlockSpec((bm, D_MODEL // 2), lambda i, be, nu: (live(i, be, nu), 0)))
    return pl.pallas_call(
        functools.partial(_moe_kernel, n_sub=2 if bm >= 512 else 1),
        grid_spec=grid_spec, out_shape=jax.ShapeDtypeStruct((p, D_MODEL // 2), U32),
        compiler_params=_cparams(("arbitrary",)), name="moe_experts",
    )(blk_exp, n_used, xs, w_gate, w_up, w_down)


def _moe_block_rows(n):
    return 512 if n >= 16384 else 128


def _moe(xn_packed, route_i, counts, experts_bf16):
    n = xn_packed.shape[0]
    bm = _moe_block_rows(n)
    n_blocks = -(-(2 * n + N_EXPERTS * (bm - 1)) // bm)
    cnt = counts[0, :N_EXPERTS]
    padded = (cnt + bm - 1) // bm * bm
    pad_end = jnp.cumsum(padded)
    pad_start = pad_end - padded
    experts = jnp.arange(N_EXPERTS, dtype=I32)
    eid = route_i[:, 0:2].T
    start_of = jnp.sum(jnp.where(eid[:, :, None] == experts, pad_start, 0), axis=-1)
    dest = (start_of + route_i[:, 2:4].T).astype(I32)
    first_row = jnp.arange(n_blocks, dtype=I32) * bm
    blk_exp = jnp.minimum(jnp.sum((pad_end[None, :] <= first_row[:, None]).astype(I32), axis=1), N_EXPERTS - 1)
    n_used = (pad_end[-1:] // bm).astype(I32)
    xs = _sc_scatter_rows(xn_packed, dest, n_blocks * bm)
    ys = _moe_experts(xs, blk_exp, n_used, *experts_bf16, bm)
    return _sc_gather_rows(ys, dest)


def _combine_kernel(x_ref, y_ref, gate_ref, o_ref):
    half = D_MODEL // 2
    x = x_ref[...]
    g = gate_ref[...]
    g0 = g[:, 0:1]
    g1 = g[:, 1:2]
    w0 = y_ref[0]
    w1 = y_ref[1]
    o_ref[:, :half] = x[:, :half] + g0 * _unpack_hi(w0) + g1 * _unpack_hi(w1)
    o_ref[:, half:] = x[:, half:] + g0 * _unpack_lo(w0) + g1 * _unpack_lo(w1)


def _combine(x, y, gates):
    n = x.shape[0]
    tm = min(1024, n)
    row = lambda i: (i, 0)
    return pl.pallas_call(
        _combine_kernel, grid=(n // tm,),
        in_specs=[pl.BlockSpec((tm, D_MODEL), row), pl.BlockSpec((2, tm, D_MODEL // 2), lambda i: (0, i, 0)),
                  pl.BlockSpec((tm, LANES), row)],
        out_specs=pl.BlockSpec((tm, D_MODEL), row), out_shape=jax.ShapeDtypeStruct((n, D_MODEL), F32),
        compiler_params=_cparams(("arbitrary",)), name="moe_combine",
    )(x, y, gates)


def _layer_weights(l, p):
    w_in = p["w_in"][l]
    cols = [w_in[:, i * GRP:(i + 1) * GRP] for i in range(11)]
    by_group = [None] * N_PROJ
    for ref_i, g in enumerate(_REF_GROUPS):
        if g is not None:
            by_group[g] = cols[ref_i]
    by_group[PGLU] = cols[5]
    w_dg = jnp.zeros((D_MODEL, LANES), F32).at[:, :GATE_RANK].set(w_in[:, 11 * GRP:])
    wg2 = jnp.zeros((LANES, GRP), F32).at[:GATE_RANK].set(p["d_wg2"][l])
    tile4 = lambda v: jnp.tile(v, HEADS)[None, :]
    hid = np.arange(GRP) // HEAD_DIM
    bd = (hid[:, None] == hid[None, :]).astype(np.float32)
    wr = jnp.zeros((D_MODEL, LANES), F32).at[:, :N_EXPERTS].set(p["r_expert_w"][l])
    wr = wr.at[:, N_EXPERTS:N_EXPERTS + N_GROUPS].set(p["r_group_w"][l])
    wr_hi = wr.astype(BF16)
    br = jnp.zeros((1, LANES), F32).at[0, :N_EXPERTS].set(p["r_expert_b"][l])
    br = br.at[0, N_EXPERTS:N_EXPERTS + N_GROUPS].set(p["r_group_b"][l])
    return {
        "g1": p["norm1_g"][l][None, :],
        "w_in": jnp.concatenate(by_group[:PLA], axis=1).astype(BF16),
        "w_cg": cols[6].astype(BF16),
        "w_dg": w_dg.astype(BF16),
        "wg2": wg2.astype(BF16),
        "bg": p["d_bg"][l][None, :],
        "gq": tile4(p["b_qnorm_g"][l]), "gk": tile4(p["b_knorm_g"][l]), "gav": p["a_vnorm_g"][l][None, :],
        "gon": tile4(p["d_onorm_g"][l]),
        "hsum": jnp.asarray(bd, BF16), "bdmask": jnp.asarray(bd, F32),
        "a_ws": p["a_ws"][l], "a_bs_rows": jnp.repeat(p["a_bs"][l].T, HEAD_DIM, axis=1),
        "b_rel": p["b_rel_bias"][l],
        "c_dw": p["c_dw"][l], "c_dw_b": p["c_dw_b"][l][None, :],
        "c_ln_g": p["c_ln_g"][l][None, :], "c_ln_b": p["c_ln_b"][l][None, :],
        "w_out": p["w_out"][l].astype(BF16),
        "g2": p["norm2_g"][l][None, :],
        "wr": jnp.concatenate([wr_hi, (wr - wr_hi.astype(F32)).astype(BF16)], axis=1), "br": br,
        "e_w_gate": p["e_w_gate"], "e_w_up": p["e_w_up"], "e_w_down": p["e_w_down"], "layer": l,
    }


def _mix_and_route(x, lw, b, t, pending, caches, experts_bf16):
    step = caches is not None
    y_prev, gates_prev = pending if pending is not None else (None, None)
    cast = () if experts_bf16 is not None else (lw["e_w_gate"], lw["e_w_up"], lw["e_w_down"])
    x, proj, kv, a_v, casted = _in_proj(x, lw, t, y_prev, gates_prev, emit_av=step, cast=cast)
    if experts_bf16 is None:
        experts_bf16 = casted
    if not step:
        yb = _attention(proj, lw, b, t)
        yc, tail, ya = _conv_gmlp(proj, lw, b, t)
        yd, sf = _gla(proj, lw, b, t)
    else:
        ck, cv, cc, cs = caches
        yb = _attention(proj, lw, b, t, ck.reshape(b * B_WINDOW, GRP), cv.reshape(b * B_WINDOW, GRP))
        halo = jnp.pad(cc, ((0, 0), (HALO - C_BUF, 0), (0, 0))).reshape(b * HALO, GRP)
        yc, tail, ya = _conv_gmlp(proj, lw, b, t, halo)
        yd, sf = _gla(proj, lw, b, t, cs.reshape(b * GRP, HEAD_DIM))
        a_v = a_v.reshape(b, t, GRP)
    x2, xn_packed, route_i, route_f, counts = _out_proj(ya, yb, yc, yd, x, lw)
    y = _moe(xn_packed, route_i, counts, experts_bf16)
    keep = min(B_WINDOW, t)
    new_k = kv[0].reshape(b, keep, HEADS, HEAD_DIM)
    new_v = kv[1].reshape(b, keep, HEADS, HEAD_DIM)
    new_buf = tail.reshape(b, HALO, GRP)[:, HALO - C_BUF:]
    states = (new_k, new_v, new_buf, sf.reshape(b, HEADS, HEAD_DIM, HEAD_DIM), a_v)
    return x2, (y, route_f), states, experts_bf16


def kernel(x_prompt, x_sample, cache_b_k, cache_b_v, state_c_conv, state_d_gla, norm1_g, w_in, a_vnorm_g, a_ws, a_bs, b_qnorm_g, b_knorm_g, b_rel_bias, c_dw, c_dw_b, c_ln_g, c_ln_b, d_wg2, d_bg, d_onorm_g, w_out, norm2_g, r_group_w, r_group_b, r_expert_w, r_expert_b, e_w_gate, e_w_up, e_w_down):
    params = dict(norm1_g=norm1_g, w_in=w_in, a_vnorm_g=a_vnorm_g, a_ws=a_ws, a_bs=a_bs, b_qnorm_g=b_qnorm_g,
                  b_knorm_g=b_knorm_g, b_rel_bias=b_rel_bias, c_dw=c_dw, c_dw_b=c_dw_b, c_ln_g=c_ln_g, c_ln_b=c_ln_b,
                  d_wg2=d_wg2, d_bg=d_bg, d_onorm_g=d_onorm_g, w_out=w_out, norm2_g=norm2_g, r_group_w=r_group_w,
                  r_group_b=r_group_b, r_expert_w=r_expert_w, r_expert_b=r_expert_b, e_w_gate=e_w_gate,
                  e_w_up=e_w_up, e_w_down=e_w_down)
    depth = w_in.shape[0]
    bp, tp, _ = x_prompt.shape
    bs, ts, _ = x_sample.shape
    xp = x_prompt.reshape(bp * tp, D_MODEL)
    xs = x_sample.reshape(bs * ts, D_MODEL)
    pend_p = pend_s = None
    st_p, st_s = [], []
    for l in range(depth):
        lw = _layer_weights(l, params)
        xp, pend_p, sp, experts = _mix_and_route(xp, lw, bp, tp, pend_p, None, None)
        xs, pend_s, ss, _ = _mix_and_route(xs, lw, bs, ts, pend_s,
                                           (cache_b_k[l], cache_b_v[l], state_c_conv[l], state_d_gla[l]), experts)
        st_p.append(sp)
        st_s.append(ss)
    yp = _combine(xp, pend_p[0], pend_p[1]).reshape(bp, tp, D_MODEL)
    ys = _combine(xs, pend_s[0], pend_s[1]).reshape(bs, ts, D_MODEL)
    stack = lambda sts, i: jnp.stack([s[i] for s in sts])
    return (yp, ys, stack(st_p, 0), stack(st_p, 1), stack(st_p, 2), stack(st_p, 3),
            stack(st_s, 0), stack(st_s, 1), stack(st_s, 2), stack(st_s, 3), stack(st_s, 4))
```

```python
import functools

import numpy as np
import jax
import jax.numpy as jnp
from jax import lax
from jax.experimental import pallas as pl
from jax.experimental.pallas import tpu as pltpu
from jax.experimental.pallas import tpu_sc as plsc

F32 = jnp.float32
BF16 = jnp.bfloat16
I32 = jnp.int32
U32 = jnp.uint32

D_MODEL = 1024
GRP = 256
HEADS = 4
HEAD_DIM = 64
CHUNK = 64
A_CHUNK = 128
B_WINDOW = 512
REL_CLIP = 128
C_WIDTH = 31
C_BUF = C_WIDTH - 1
HALO = 32
GATE_RANK = 16
GLA_TAU = 16.0
GLA_SUB = 16
N_GROUPS = 4
PER_GROUP = 8
N_EXPERTS = 32
D_EXPERT = 512
EPS = 1e-6
NEG_INF = -1e30
LANES = 128
VMEM_LIMIT = 48 * 1024 * 1024

PK, PV, PQ, PAU, PAV, PGLU, PDQ, PDK, PDV, PDR, PLA = range(11)
N_PROJ = 11
_REF_GROUPS = (PAU, PAV, PQ, PK, PV, None, None, PDQ, PDK, PDV, PDR)

SC_WORKERS = 32
SC_WIN = 128


def _cparams(sem):
    return pltpu.CompilerParams(dimension_semantics=sem, vmem_limit_bytes=VMEM_LIMIT)


def _sigmoid(x):
    return 1.0 / (1.0 + jnp.exp(-x))


def _gelu_tanh(x):
    c = np.float32(np.sqrt(2.0 / np.pi))
    return 0.5 * x * (1.0 + jnp.tanh(c * (x + np.float32(0.044715) * (x * x * x))))


def _pack_halves(y):
    half = y.shape[1] // 2
    hi = pltpu.bitcast(y[:, :half].astype(BF16).astype(F32), U32)
    lo = pltpu.bitcast(y[:, half:].astype(BF16).astype(F32), U32)
    return hi | (lo >> np.uint32(16))


def _unpack_hi(w):
    return pltpu.bitcast(w & np.uint32(0xFFFF0000), F32)


def _unpack_lo(w):
    return pltpu.bitcast(w << np.uint32(16), F32)


def _head_id(shape, axis, size):
    return lax.broadcasted_iota(I32, shape, axis) // size


def _bd_stack(x, rows):
    x4 = jnp.concatenate([x] * HEADS, axis=0)
    shape = (HEADS * rows, GRP)
    keep = _head_id(shape, 0, rows) == _head_id(shape, 1, HEAD_DIM)
    return jnp.where(keep, x4, jnp.zeros_like(x4))


def _bd_unstack(o, rows):
    lane_h = _head_id((rows, GRP), 1, HEAD_DIM)
    out = o[(HEADS - 1) * rows:HEADS * rows]
    for h in range(HEADS - 2, -1, -1):
        out = jnp.where(lane_h == h, o[h * rows:(h + 1) * rows], out)
    return out


def _head_meansq(o, hsum_ref):
    sq = (o * o).astype(BF16)
    return jnp.dot(sq, hsum_ref[...], preferred_element_type=F32) * np.float32(1.0 / HEAD_DIM)


def _in_kernel(*refs, combine, emit_av, tiles_per_stream, n_cast):
    refs = list(refs)
    x_ref = refs.pop(0)
    if combine:
        y_ref = refs.pop(0)
        gate_ref = refs.pop(0)
    g1_ref, w_ref, wcg_ref, wdg_ref, wg2_ref, bg_ref, gq_ref, gk_ref, gav_ref, hsum_ref = refs[:10]
    cast_in = refs[10:10 + n_cast]
    refs = refs[10 + n_cast:]
    if combine:
        xo_ref = refs.pop(0)
    p_ref = refs.pop(0)
    kc_ref = refs.pop(0)
    vc_ref = refs.pop(0)
    if emit_av:
        av_ref = refs.pop(0)
    cast_out = refs[:n_cast]
    raw = refs[n_cast]
    n_sub = 1
    sub = x_ref.shape[0] // n_sub

    def prologue(s):
        rows = slice(s * sub, (s + 1) * sub)
        x = x_ref[rows, :]
        if combine:
            half = D_MODEL // 2
            g = gate_ref[rows, :]
            g0 = g[:, 0:1]
            g1 = g[:, 1:2]
            w0 = y_ref[0, rows, :]
            w1 = y_ref[1, rows, :]
            xa = x[:, :half] + g0 * _unpack_hi(w0) + g1 * _unpack_hi(w1)
            xb = x[:, half:] + g0 * _unpack_lo(w0) + g1 * _unpack_lo(w1)
            xo_ref[rows, :half] = xa
            xo_ref[rows, half:] = xb
            x = jnp.concatenate([xa, xb], axis=1)
        rs = lax.rsqrt(jnp.mean(x * x, axis=-1, keepdims=True) + EPS)
        return rows, rs, (x * g1_ref[...]).astype(BF16)

    n_slots = PLA + 2

    def matmul(tile, slot):
        rows, _, h = tile
        if slot < PLA:
            raw[rows, slot * GRP:(slot + 1) * GRP] = jnp.dot(h, w_ref[:, slot * GRP:(slot + 1) * GRP],
                                                             preferred_element_type=F32)
        elif slot == PLA:
            raw[rows, PLA * GRP:(PLA + 1) * GRP] = jnp.dot(h, wcg_ref[...], preferred_element_type=F32)
        else:
            raw[rows, (PLA + 1) * GRP:] = jnp.dot(h, wdg_ref[...], preferred_element_type=F32)

    def epilogue(tile, slot):
        rows, rs, _ = tile

        def proj(g):
            return raw[rows, g * GRP:(g + 1) * GRP] * rs

        def put(g, val):
            p_ref[rows, g * GRP:(g + 1) * GRP] = val.astype(BF16)

        if slot == PK:
            r = proj(PK)
            put(PK, r * lax.rsqrt(_head_meansq(r, hsum_ref) + EPS) * gk_ref[...])
        elif slot == PV:
            put(PV, proj(PV))

            @pl.when(pl.program_id(0) % tiles_per_stream == tiles_per_stream - 1)
            def _():
                r = proj(PK)
                kc_ref[rows, :] = r * lax.rsqrt(_head_meansq(r, hsum_ref) + EPS) * gk_ref[...]
                vc_ref[rows, :] = proj(PV)
        elif slot == PQ:
            r = proj(PQ)
            put(PQ, r * lax.rsqrt(_head_meansq(r, hsum_ref) + EPS) * (gq_ref[...] * np.float32(HEAD_DIM ** -0.5)))
        elif slot == PAU:
            put(PAU, _gelu_tanh(proj(PAU)))
        elif slot == PAV:
            r = _gelu_tanh(proj(PAV))
            av = r * lax.rsqrt(jnp.mean(r * r, axis=-1, keepdims=True) + EPS) * gav_ref[...]
            put(PAV, av)
            if emit_av:
                av_ref[rows, :] = av
        elif slot == PGLU:
            pass
        elif slot == PDQ:
            put(PDQ, proj(PDQ) * np.float32(HEAD_DIM ** -0.5))
        elif slot in (PDK, PDV):
            put(slot, proj(slot))
        elif slot == PDR:
            r = proj(PDR)
            put(PDR, r * _sigmoid(r))
        elif slot == PLA:
            put(PGLU, proj(PGLU) * _sigmoid(proj(PLA)))
        else:
            dg = raw[rows, (PLA + 1) * GRP:] * rs
            z = jnp.dot(dg.astype(BF16), wg2_ref[...], preferred_element_type=F32) + bg_ref[...]
            logsig = jnp.minimum(z, 0.0) - jnp.log(1.0 + jnp.exp(-jnp.abs(z)))
            put(PLA, logsig * np.float32(1.0 / GLA_TAU))

    order = (PLA + 1, PK, PQ, PAV, PAU, PGLU, PLA, PDR, PV, PDQ, PDK, PDV)
    assert sorted(order) == list(range(n_slots))
    pairs = [(s, slot) for s in range(n_sub) for slot in order]
    lag = 2
    tiles = {0: prologue(0)}
    for i in range(len(pairs) + lag):
        if i < len(pairs):
            matmul(tiles[pairs[i][0]], pairs[i][1])
        if i == lag and n_sub > 1:
            tiles[1] = prologue(1)
        if i >= lag:
            epilogue(tiles[pairs[i - lag][0]], pairs[i - lag][1])
        if i % 3 == 2 and i // 3 < n_cast:
            cast_out[i // 3][...] = cast_in[i // 3][...].astype(BF16)


def _in_proj(x, lw, t, y=None, gates=None, emit_av=False, cast=()):
    n = x.shape[0]
    tm = min(512, n)
    steps = n // tm
    combine = y is not None
    keep = min(B_WINDOW, t)
    tps = max(t // tm, 1)
    assert tps == 1 or keep == tm
    row = lambda i: (i, 0)
    const = lambda i: (0, 0)
    ins, specs = [x], [pl.BlockSpec((tm, D_MODEL), row)]
    if combine:
        ins += [y, gates]
        specs += [pl.BlockSpec((2, tm, D_MODEL // 2), lambda i: (0, i, 0)), pl.BlockSpec((tm, LANES), row)]
    consts = [lw["g1"], lw["w_in"], lw["w_cg"], lw["w_dg"], lw["wg2"], lw["bg"], lw["gq"], lw["gk"], lw["gav"], lw["hsum"]]
    ins += consts
    specs += [pl.BlockSpec(c.shape, const) for c in consts]
    cast_shapes, cast_specs = [], []
    layer = lw["layer"]
    for arr in cast:
        depth, rows, cols = arr.shape[0], arr.shape[1] * arr.shape[2], arr.shape[3]
        slab = rows // steps
        assert slab * steps == rows and slab % 16 == 0
        ins.append(arr.reshape(depth * rows, cols))
        specs.append(pl.BlockSpec((slab, cols), lambda i: (layer * steps + i, 0)))
        cast_shapes.append(jax.ShapeDtypeStruct((rows, cols), BF16))
        cast_specs.append(pl.BlockSpec((slab, cols), row))
    newest = jax.ShapeDtypeStruct((n // tps, GRP), F32)
    newest_spec = pl.BlockSpec((tm, GRP), lambda i: (i // tps, 0))
    out_shape = [jax.ShapeDtypeStruct((n, N_PROJ * GRP), BF16), newest, newest]
    out_specs = [pl.BlockSpec((tm, N_PROJ * GRP), row), newest_spec, newest_spec]
    if combine:
        out_shape = [jax.ShapeDtypeStruct((n, D_MODEL), F32)] + out_shape
        out_specs = [pl.BlockSpec((tm, D_MODEL), row)] + out_specs
    if emit_av:
        out_shape.append(jax.ShapeDtypeStruct((n, GRP), F32))
        out_specs.append(pl.BlockSpec((tm, GRP), row))
    out_shape += cast_shapes
    out_specs += cast_specs
    outs = list(pl.pallas_call(
        functools.partial(_in_kernel, combine=combine, emit_av=emit_av, tiles_per_stream=tps, n_cast=len(cast)),
        grid=(steps,), in_specs=specs, out_specs=out_specs, out_shape=out_shape,
        scratch_shapes=[pltpu.VMEM((tm, (PLA + 1) * GRP + LANES), F32)],
        compiler_params=_cparams(("arbitrary",)), name="in_proj",
    )(*ins))
    x_new = outs.pop(0) if combine else x
    proj, k_new, v_new = outs[0], outs[1], outs[2]
    a_v = outs[3] if emit_av else None
    casted = [o.reshape(a.shape[1:]) for o, a in zip(outs[3 + int(emit_av):], cast)]
    return x_new, proj, (k_new, v_new), a_v, casted


def _gmlp_body(u_ref, v_ref, ws_ref, bs_ref, o_ref, chunk, n_chunks):
    lane_h = _head_id((chunk, GRP), 1, HEAD_DIM)
    ri = lax.broadcasted_iota(I32, (chunk, chunk), 0)
    ci = lax.broadcasted_iota(I32, (chunk, chunk), 1)
    wm = [jnp.where(ci <= ri, ws_ref[h], 0.0).astype(BF16) for h in range(HEADS)]
    for c in range(n_chunks):
        rows = slice(c * chunk, (c + 1) * chunk)
        v = v_ref[rows, :]
        sv = jnp.dot(wm[HEADS - 1], v, preferred_element_type=F32)
        for h in range(HEADS - 2, -1, -1):
            sv = jnp.where(lane_h == h, jnp.dot(wm[h], v, preferred_element_type=F32), sv)
        o_ref[rows, :] = (u_ref[rows, :].astype(F32) * (sv + bs_ref[...])).astype(BF16)


def _attn_kernel(q_ref, kc_ref, vc_ref, kp_ref, vp_ref, bias_ref, o_ref, kbuf, vbuf, *, chunk, n_chunks, first_has_past):
    tq = chunk * n_chunks
    win = B_WINDOW + chunk
    kbuf[0:B_WINDOW, :] = kp_ref[...].astype(BF16)
    vbuf[0:B_WINDOW, :] = vp_ref[...].astype(BF16)
    kbuf[B_WINDOW:B_WINDOW + tq, :] = kc_ref[...]
    vbuf[B_WINDOW:B_WINDOW + tq, :] = vc_ref[...]
    col = lax.broadcasted_iota(I32, (HEADS * chunk, win), 1)

    def chunks(no_past):
        for c in range(n_chunks):
            q = q_ref[c * chunk:(c + 1) * chunk, :]
            kk = kbuf[c * chunk:c * chunk + win, :]
            vv = vbuf[c * chunk:c * chunk + win, :]
            s = lax.dot_general(_bd_stack(q, chunk), kk, (((1,), (1,)), ((), ())), preferred_element_type=F32)
            s = s + bias_ref[...]
            if no_past:
                s = jnp.where(col + c * chunk >= B_WINDOW, s, NEG_INF)
            m = jnp.max(s, axis=-1, keepdims=True)
            p = jnp.exp(s - m)
            l = jnp.sum(p, axis=-1, keepdims=True)
            o = jnp.dot(p.astype(BF16), vv, preferred_element_type=F32) * (1.0 / l)
            o_ref[c * chunk:(c + 1) * chunk, :] = _bd_unstack(o, chunk).astype(BF16)

    if first_has_past:
        chunks(False)
    else:
        pl.when(pl.program_id(1) == 0)(functools.partial(chunks, True))
        pl.when(pl.program_id(1) > 0)(functools.partial(chunks, False))


def _attention(proj, lw, b, t, cache_k=None, cache_v=None):
    n = proj.shape[0]
    step = cache_k is not None
    chunk = min(t, CHUNK)
    tq = min(t, B_WINDOW)
    nt = t // tq
    rel = lw["b_rel"]
    win = B_WINDOW + chunk
    lo = REL_CLIP - (chunk - 1)
    n_far = (chunk - 1) + win - (2 * REL_CLIP + 1 - lo)
    by_dist = jnp.concatenate([rel[:, lo:], jnp.broadcast_to(rel[:, -1:], (HEADS, n_far))], axis=1)
    by_key = by_dist[:, ::-1]
    n_k = chunk - 1 + win
    wrapped = jnp.tile(jnp.pad(by_key, ((0, 0), (0, 1))), (1, chunk))[:, :chunk * n_k].reshape(HEADS, chunk, n_k)
    bias = wrapped[:, :, chunk - 1:].astype(F32).reshape(HEADS * chunk, win)
    cur = lambda g: pl.BlockSpec((tq, GRP), lambda bi, j: (bi * nt + j, g))
    if step:
        prev_k = pl.BlockSpec((B_WINDOW, GRP), lambda bi, j: (bi, 0))
        prev_v = prev_k
        pk_arr, pv_arr = cache_k, cache_v
    else:
        assert tq == B_WINDOW
        prev_k = pl.BlockSpec((B_WINDOW, GRP), lambda bi, j: (bi * nt + jnp.maximum(j - 1, 0), PK))
        prev_v = pl.BlockSpec((B_WINDOW, GRP), lambda bi, j: (bi * nt + jnp.maximum(j - 1, 0), PV))
        pk_arr, pv_arr = proj, proj
    return pl.pallas_call(
        functools.partial(_attn_kernel, chunk=chunk, n_chunks=tq // chunk, first_has_past=step),
        grid=(b, nt),
        in_specs=[cur(PQ), cur(PK), cur(PV), prev_k, prev_v, pl.BlockSpec(bias.shape, lambda bi, j: (0, 0))],
        out_specs=pl.BlockSpec((tq, GRP), lambda bi, j: (bi * nt + j, 0)),
        out_shape=jax.ShapeDtypeStruct((n, GRP), BF16),
        scratch_shapes=[pltpu.VMEM((B_WINDOW + tq, GRP), BF16), pltpu.VMEM((B_WINDOW + tq, GRP), BF16)],
        compiler_params=_cparams(("arbitrary", "arbitrary")), name="band_attn",
    )(proj, proj, proj, pk_arr, pv_arr, bias)


def _conv_kernel(g_ref, halo_ref, dw_ref, dwb_ref, lng_ref, lnb_ref, u_ref, v_ref, ws_ref, bs_ref,
                 o_ref, tail_ref, ya_ref, xp, zbuf, *, tc, sub, first_has_past, a_chunk):
    _gmlp_body(u_ref, v_ref, ws_ref, bs_ref, ya_ref, a_chunk, tc // a_chunk)
    halo = halo_ref[...].astype(F32)
    has_past = jnp.logical_or(pl.program_id(1) > 0, first_has_past)
    xp[0:HALO, :] = jnp.where(has_past, halo, 0.0)
    xp[HALO:HALO + tc, :] = g_ref[...].astype(F32)
    xp[HALO + tc:, :] = jnp.zeros((xp.shape[0] - HALO - tc, GRP), F32)

    @pl.when(pl.program_id(1) == pl.num_programs(1) - 1)
    def _():
        tail_ref[...] = xp[tc:tc + HALO, :]

    lead = HALO - C_BUF
    sl = 8
    for s in range(tc // sub):
        acc = None
        for r in range(sl):
            taps = [p for p in range(r, lead + C_WIDTH, sl) if p >= lead]
            z = None
            for p in taps:
                a0 = s * sub + p - r
                term = dw_ref[p - lead:p - lead + 1, :] * xp[a0:a0 + sub + sl, :]
                z = term if z is None else z + term
            zbuf[r] = z
            part = zbuf[r, r:r + sub, :]
            acc = part if acc is None else acc + part
        y = acc + dwb_ref[...]
        mu = jnp.mean(y, axis=-1, keepdims=True)
        yc = y - mu
        y = yc * lax.rsqrt(jnp.mean(yc * yc, axis=-1, keepdims=True) + EPS) * lng_ref[...] + lnb_ref[...]
        o_ref[s * sub:(s + 1) * sub, :] = (y * _sigmoid(y)).astype(BF16)


def _conv_gmlp(proj, lw, b, t, state=None):
    n = proj.shape[0]
    step = state is not None
    tc = min(t, 1024)
    nt = t // tc
    sub = min(tc, 64)
    a_chunk = min(t, A_CHUNK)
    ws = lw["a_ws"][:, :a_chunk, :a_chunk]
    bs = lw["a_bs_rows"][:a_chunk]
    if step:
        halo_arr = state
        halo_spec = pl.BlockSpec((HALO, GRP), lambda bi, j: (bi, 0))
    else:
        per = tc // HALO
        halo_arr = proj
        halo_spec = pl.BlockSpec((HALO, GRP), lambda bi, j: (jnp.maximum((bi * nt + j) * per - 1, 0), PGLU))
    vec = pl.BlockSpec((1, GRP), lambda bi, j: (0, 0))
    cur = lambda g: pl.BlockSpec((tc, GRP), lambda bi, j: (bi * nt + j, g))
    return pl.pallas_call(
        functools.partial(_conv_kernel, tc=tc, sub=sub, first_has_past=step, a_chunk=a_chunk),
        grid=(b, nt),
        in_specs=[cur(PGLU), halo_spec, pl.BlockSpec((C_WIDTH, GRP), lambda bi, j: (0, 0)), vec, vec, vec,
                  cur(PAU), cur(PAV), pl.BlockSpec(ws.shape, lambda bi, j: (0, 0, 0)),
                  pl.BlockSpec(bs.shape, lambda bi, j: (0, 0))],
        out_specs=[cur(0), pl.BlockSpec((HALO, GRP), lambda bi, j: (bi, 0)), cur(0)],
        out_shape=[jax.ShapeDtypeStruct((n, GRP), BF16), jax.ShapeDtypeStruct((b * HALO, GRP), F32),
                   jax.ShapeDtypeStruct((n, GRP), BF16)],
        scratch_shapes=[pltpu.VMEM((HALO + tc + 8, GRP), F32), pltpu.VMEM((8, sub + 8, GRP), F32)],
        compiler_params=_cparams(("arbitrary", "arbitrary")), name="conv_gmlp",
    )(proj, halo_arr, lw["c_dw"], lw["c_dw_b"], lw["c_ln_g"], lw["c_ln_b"], proj, proj, ws, bs)


def _gla_tables(L):
    i = np.arange(L)[:, None]
    t = np.arange(L)[None, :]
    masks = []
    s = GLA_SUB
    masks.append(((i // s) == (t // s)) & (t <= i))
    s *= 2
    while s <= L:
        h = s // 2
        masks.append(((i // s) == (t // s)) & (i % s >= h) & (t % s < h))
        s *= 2
    tri = (t <= i).astype(np.float32)
    mask = np.stack([np.tile(m.astype(np.float32), (1, HEADS)) for m in masks], axis=0)
    return tri, mask


def _gla_anchor(cum, row, L, size, first_half):
    out = None
    for start in range(0, L, size):
        ar = start + size // 2 - 1 if first_half else start - 1
        val = jnp.zeros((L, GRP), F32) if ar < 0 else jnp.broadcast_to(cum[ar:ar + 1, :], (L, GRP))
        out = val if out is None else jnp.where(row >= start, val, out)
    return out


def _gla_kernel(q_ref, k_ref, v_ref, la_ref, dr_ref, s0_ref, tri_ref, lmask_ref, bdmask_ref, hsum_ref, gon_ref,
                o_ref, sf_ref, st, o_all, *, L, n_chunks, n_levels, first_has_state):
    j = pl.program_id(1)

    @pl.when(j == 0)
    def _():
        st[...] = jnp.zeros_like(st)
        if first_has_state:
            for h in range(HEADS):
                blk = slice(h * HEAD_DIM, (h + 1) * HEAD_DIM)
                st[blk, blk] = s0_ref[blk, :].T

    row = lax.broadcasted_iota(I32, (L, GRP), 0)
    dn_t = (((1,), (1,)), ((), ()))

    def prep(c):
        rows = slice(c * L, (c + 1) * L)
        q = q_ref[rows, :].astype(F32)
        k = k_ref[rows, :].astype(F32)
        v = v_ref[rows, :]
        cum = jnp.dot(tri_ref[...], la_ref[rows, :], preferred_element_type=F32)
        total = cum[L - 1:L, :]
        pairs = []
        for lvl in range(n_levels):
            size = GLA_SUB << lvl
            if lvl == 0:
                local = cum - _gla_anchor(cum, row, L, size, False)
                ql = q * jnp.exp(local)
                kl = k * jnp.exp(-local)
            else:
                upper = (row & (size - 1)) >= (size // 2)
                d = cum - _gla_anchor(cum, row, L, size, True)
                w = jnp.exp(jnp.where(upper, d, -d))
                ql = jnp.where(upper, q * w, 0.0)
                kl = jnp.where(upper, 0.0, k * w)
            pairs.append((ql.astype(BF16), _bd_stack(kl.astype(BF16), L)))
        return dict(rows=rows, v=v, qp=(q * jnp.exp(cum)).astype(BF16), kst=(k * jnp.exp(total - cum)).astype(BF16),
                    decay=jnp.exp(total), pairs=pairs)

    def intra(p):
        att = None
        for lvl, (ql, kbd) in enumerate(p["pairs"]):
            a = lax.dot_general(ql, kbd, dn_t, preferred_element_type=F32) * lmask_ref[lvl]
            att = a if att is None else att + a
        p["o_intra"] = jnp.dot(att.astype(BF16), _bd_stack(p["v"], L), preferred_element_type=F32)
        p["upd"] = lax.dot_general(p["v"], p["kst"], (((0,), (0,)), ((), ())),
                                   preferred_element_type=F32) * bdmask_ref[...]
        return p

    def finish(p, s_t):
        o_all[p["rows"], :] = lax.dot_general(p["qp"], s_t.astype(BF16), dn_t, preferred_element_type=F32) + p["o_intra"]
        return s_t * p["decay"] + p["upd"]

    s_t = st[...]
    stage1, stage2 = {}, {}
    for step in range(n_chunks + 2):
        if step < n_chunks:
            stage1[step] = prep(step)
        if 0 <= step - 1 < n_chunks:
            stage2[step - 1] = intra(stage1.pop(step - 1))
        if 0 <= step - 2 < n_chunks:
            s_t = finish(stage2.pop(step - 2), s_t)
    st[...] = s_t
    o = o_all[...]
    y = o * lax.rsqrt(_head_meansq(o, hsum_ref) + EPS) * gon_ref[...] * dr_ref[...].astype(F32)
    o_ref[...] = y.astype(BF16)

    @pl.when(j == pl.num_programs(1) - 1)
    def _():
        for h in range(HEADS):
            blk = slice(h * HEAD_DIM, (h + 1) * HEAD_DIM)
            sf_ref[blk, :] = st[blk, blk].T


def _gla(proj, lw, b, t, s0=None):
    n = proj.shape[0]
    step = s0 is not None
    L = min(t, 64)
    td = min(t, 1024)
    nt = t // td
    n_levels = int(np.log2(L // GLA_SUB)) + 1
    tri, lmask = _gla_tables(L)
    tri = jnp.asarray(tri, BF16)
    lmask = jnp.asarray(lmask, F32)
    if not step:
        s0 = jnp.zeros((GRP, HEAD_DIM), F32)
        s0_spec = pl.BlockSpec((GRP, HEAD_DIM), lambda bi, j: (0, 0))
    else:
        s0_spec = pl.BlockSpec((GRP, HEAD_DIM), lambda bi, j: (bi, 0))
    cur = lambda g: pl.BlockSpec((td, GRP), lambda bi, j: (bi * nt + j, g))
    c2 = lambda bi, j: (0, 0)
    return pl.pallas_call(
        functools.partial(_gla_kernel, L=L, n_chunks=td // L, n_levels=n_levels, first_has_state=step),
        grid=(b, nt),
        in_specs=[cur(PDQ), cur(PDK), cur(PDV), cur(PLA), cur(PDR), s0_spec,
                  pl.BlockSpec(tri.shape, c2), pl.BlockSpec(lmask.shape, lambda bi, j: (0, 0, 0)),
                  pl.BlockSpec((GRP, GRP), c2), pl.BlockSpec((GRP, GRP), c2), pl.BlockSpec((1, GRP), c2)],
        out_specs=[pl.BlockSpec((td, GRP), lambda bi, j: (bi * nt + j, 0)),
                   pl.BlockSpec((GRP, HEAD_DIM), lambda bi, j: (bi, 0))],
        out_shape=[jax.ShapeDtypeStruct((n, GRP), BF16), jax.ShapeDtypeStruct((b * GRP, HEAD_DIM), F32)],
        scratch_shapes=[pltpu.VMEM((GRP, GRP), F32), pltpu.VMEM((td, GRP), F32)],
        compiler_params=_cparams(("arbitrary", "arbitrary")), name="gla",
    )(proj, proj, proj, proj, proj, s0, tri, lmask, lw["bdmask"], lw["hsum"], lw["gon"])


def _out_kernel(ya_ref, yb_ref, yc_ref, yd_ref, x_ref, wo_ref, g2_ref, wr_ref, br_ref, tri_ref,
                xo_ref, xn_ref, ri_ref, rf_ref, cnt_ref, *, tm, n_sub):
    @pl.when(pl.program_id(0) == 0)
    def _():
        cnt_ref[...] = jnp.zeros_like(cnt_ref)

    sub = tm // n_sub
    lane = lax.broadcasted_iota(I32, (sub, LANES), 1)
    lane_f = lane.astype(F32)
    lane_grp_f = (lane // PER_GROUP).astype(F32)
    is_grp = jnp.logical_and(lane >= N_EXPERTS, lane < N_EXPERTS + N_GROUPS)
    far = np.float32(1 << 20)

    def mix(s):
        rows = slice(s * sub, (s + 1) * sub)
        ycat = jnp.concatenate([ya_ref[rows, :], yb_ref[rows, :], yc_ref[rows, :], yd_ref[rows, :]], axis=1)
        x = x_ref[rows, :] + jnp.dot(ycat, wo_ref[...], preferred_element_type=F32)
        xo_ref[rows, :] = x
        return x

    def norm_logits(s, x):
        rows = slice(s * sub, (s + 1) * sub)
        xn = x * lax.rsqrt(jnp.mean(x * x, axis=-1, keepdims=True) + EPS) * g2_ref[...]
        xn_ref[rows, :] = _pack_halves(xn)
        both = jnp.dot(xn.astype(BF16), wr_ref[...], preferred_element_type=F32)
        return both[:, :LANES] + both[:, LANES:] + br_ref[...]

    def route(s, logits):
        rows = slice(s * sub, (s + 1) * sub)

        def first_max(masked):
            v = jnp.max(masked, axis=-1, keepdims=True)
            return v, jnp.min(jnp.where(masked == v, lane_f, far), axis=-1, keepdims=True)

        grp_logits = jnp.where(is_grp, logits, -jnp.inf)
        gmax, gidx = first_max(grp_logits)
        p_grp = 1.0 / jnp.sum(jnp.exp(grp_logits - gmax), axis=-1, keepdims=True)
        in_grp = lane_grp_f == gidx - np.float32(N_EXPERTS)
        exp_logits = jnp.where(in_grp, logits, -jnp.inf)
        v1, i1 = first_max(exp_logits)
        v2, i2 = first_max(jnp.where(lane_f == i1, -jnp.inf, exp_logits))
        e21 = jnp.exp(v2 - v1)
        gate1 = p_grp / (1.0 + e21)
        gate2 = p_grp * e21 / (1.0 + e21)

        oh1 = lane_f == i1
        oh2 = lane_f == i2
        ones = jnp.where(oh1, 1.0, jnp.where(oh2, 1.0, 0.0))
        before = jnp.dot(tri_ref[...], ones.astype(BF16), preferred_element_type=F32) + cnt_ref[...].astype(F32)
        rank1 = jnp.sum(jnp.where(oh1, before, 0.0), axis=-1, keepdims=True)
        rank2 = jnp.sum(jnp.where(oh2, before, 0.0), axis=-1, keepdims=True)
        cnt_ref[...] = cnt_ref[...] + jnp.sum(ones, axis=0, keepdims=True).astype(I32)

        ri = jnp.where(lane == 0, i1, jnp.where(lane == 1, i2, jnp.where(lane == 2, rank1,
                                                                          jnp.where(lane == 3, rank2, 0.0))))
        ri_ref[rows, :] = ri.astype(I32)
        rf_ref[rows, :] = jnp.where(lane == 0, gate1, jnp.where(lane == 1, gate2, 0.0))

    xs, lg = {}, {}
    for step in range(n_sub + 2):
        if step < n_sub:
            xs[step] = mix(step)
        if 0 <= step - 1 < n_sub:
            lg[step - 1] = norm_logits(step - 1, xs.pop(step - 1))
        if 0 <= step - 2 < n_sub:
            route(step - 2, lg.pop(step - 2))


def _out_proj(ya, yb, yc, yd, x, lw):
    n = x.shape[0]
    tm = min(1024, n)
    row = lambda i: (i, 0)
    const = lambda i: (0, 0)
    n_sub = 4 if tm >= 1024 else 1
    sub = tm // n_sub
    tri = jnp.asarray(np.tril(np.ones((sub, sub), np.float32), -1), BF16)
    consts = [lw["w_out"], lw["g2"], lw["wr"], lw["br"], tri]
    yspec = pl.BlockSpec((tm, GRP), row)
    return pl.pallas_call(
        functools.partial(_out_kernel, tm=tm, n_sub=n_sub),
        grid=(n // tm,),
        in_specs=[yspec, yspec, yspec, yspec, pl.BlockSpec((tm, D_MODEL), row)] + [pl.BlockSpec(c.shape, const) for c in consts],
        out_specs=[pl.BlockSpec((tm, D_MODEL), row), pl.BlockSpec((tm, D_MODEL // 2), row),
                   pl.BlockSpec((tm, LANES), row), pl.BlockSpec((tm, LANES), row), pl.BlockSpec((1, LANES), const)],
        out_shape=[jax.ShapeDtypeStruct((n, D_MODEL), F32), jax.ShapeDtypeStruct((n, D_MODEL // 2), U32),
                   jax.ShapeDtypeStruct((n, LANES), I32), jax.ShapeDtypeStruct((n, LANES), F32),
                   jax.ShapeDtypeStruct((1, LANES), I32)],
        compiler_params=_cparams(("arbitrary",)), name="out_proj_router",
    )(ya, yb, yc, yd, x, *consts)


def _sc_scatter_rows(x, idx, n_out):
    n, d = x.shape
    kk = idx.shape[0]
    per_w = n // SC_WORKERS
    win = min(SC_WIN, per_w)
    n_win = per_w // win
    assert n_win * win * SC_WORKERS == n
    mesh = plsc.VectorSubcoreMesh(core_axis_name="c", subcore_axis_name="s")

    @functools.partial(
        pl.kernel, mesh=mesh, out_type=jax.ShapeDtypeStruct((n_out, d), x.dtype),
        scratch_types=[pltpu.VMEM((kk, win), I32), pltpu.VMEM((win, d), x.dtype)],
        name="sc_scatter_rows")
    def k(x_hbm, idx_hbm, o_hbm, idx_v, rows_v):
        wid = lax.axis_index("s") * 2 + lax.axis_index("c")
        base = wid * per_w

        @pl.loop(0, n_win)
        def _(w):
            off = base + w * win
            pltpu.sync_copy(x_hbm.at[pl.ds(off, win)], rows_v)
            for j in range(kk):
                pltpu.sync_copy(idx_hbm.at[j, pl.ds(off, win)], idx_v.at[j])
                pltpu.sync_copy(rows_v, o_hbm.at[idx_v.at[j]])

    return k(x, idx)


def _sc_gather_rows(y, idx):
    _, d = y.shape
    kk, n = idx.shape
    per_w = n // SC_WORKERS
    win = min(SC_WIN, per_w)
    n_win = per_w // win
    assert n_win * win * SC_WORKERS == n
    mesh = plsc.VectorSubcoreMesh(core_axis_name="c", subcore_axis_name="s")

    @functools.partial(
        pl.kernel, mesh=mesh, out_type=jax.ShapeDtypeStruct((kk, n, d), y.dtype),
        scratch_types=[pltpu.VMEM((kk, win), I32), pltpu.VMEM((win, d), y.dtype)],
        name="sc_gather_rows")
    def k(y_hbm, idx_hbm, o_hbm, idx_v, rows_v):
        wid = lax.axis_index("s") * 2 + lax.axis_index("c")
        base = wid * per_w

        @pl.loop(0, n_win)
        def _(w):
            off = base + w * win
            for j in range(kk):
                pltpu.sync_copy(idx_hbm.at[j, pl.ds(off, win)], idx_v.at[j])
                pltpu.sync_copy(y_hbm.at[idx_v.at[j]], rows_v)
                pltpu.sync_copy(rows_v, o_hbm.at[j, pl.ds(off, win)])

    return k(y, idx)


def _moe_kernel(bexp_ref, nused_ref, x_ref, wg_ref, wu_ref, wd_ref, o_ref, *, n_sub):
    del bexp_ref

    @pl.when(pl.program_id(0) < nused_ref[0])
    def _():
        sub = x_ref.shape[0] // n_sub

        def up(s):
            w = x_ref[s * sub:(s + 1) * sub, :]
            x = jnp.concatenate([_unpack_hi(w).astype(BF16), _unpack_lo(w).astype(BF16)], axis=1)
            hg = jnp.dot(x, wg_ref[...], preferred_element_type=F32)
            hu = jnp.dot(x, wu_ref[...], preferred_element_type=F32)
            return (hg * _sigmoid(hg) * hu).astype(BF16)

        def down(s, h):
            o_ref[s * sub:(s + 1) * sub, :] = _pack_halves(jnp.dot(h, wd_ref[...], preferred_element_type=F32))

        h = up(0)
        for s in range(n_sub):
            nxt = up(s + 1) if s + 1 < n_sub else None
            down(s, h)
            h = nxt


def _moe_experts(xs, blk_exp, n_used, w_gate, w_up, w_down, bm):
    p = xs.shape[0]
    n_blocks = p // bm
    live = lambda i, be, nu: jnp.minimum(i, jnp.maximum(nu[0] - 1, 0))
    wspec = lambda shape: pl.BlockSpec((None,) + shape, lambda i, be, nu: (be[live(i, be, nu)], 0, 0))
    grid_spec = pltpu.PrefetchScalarGridSpec(
        num_scalar_prefetch=2, grid=(n_blocks,),
        in_specs=[pl.BlockSpec((bm, D_MODEL // 2), lambda i, be, nu: (live(i, be, nu), 0)),
                  wspec((D_MODEL, D_EXPERT)), wspec((D_MODEL, D_EXPERT)), wspec((D_EXPERT, D_MODEL))],
        out_specs=pl.BlockSpec((bm, D_MODEL // 2), lambda i, be, nu: (live(i, be, nu), 0)))
    return pl.pallas_call(
        functools.partial(_moe_kernel, n_sub=2 if bm >= 512 else 1),
        grid_spec=grid_spec, out_shape=jax.ShapeDtypeStruct((p, D_MODEL // 2), U32),
        compiler_params=_cparams(("arbitrary",)), name="moe_experts",
    )(blk_exp, n_used, xs, w_gate, w_up, w_down)


def _moe_block_rows(n):
    return 512 if n >= 16384 else 128


def _moe(xn_packed, route_i, counts, experts_bf16):
    n = xn_packed.shape[0]
    bm = _moe_block_rows(n)
    n_blocks = -(-(2 * n + N_EXPERTS * (bm - 1)) // bm)
    cnt = counts[0, :N_EXPERTS]
    padded = (cnt + bm - 1) // bm * bm
    pad_end = jnp.cumsum(padded)
    pad_start = pad_end - padded
    experts = jnp.arange(N_EXPERTS, dtype=I32)
    eid = route_i[:, 0:2].T
    start_of = jnp.sum(jnp.where(eid[:, :, None] == experts, pad_start, 0), axis=-1)
    dest = (start_of + route_i[:, 2:4].T).astype(I32)
    first_row = jnp.arange(n_blocks, dtype=I32) * bm
    blk_exp = jnp.minimum(jnp.sum((pad_end[None, :] <= first_row[:, None]).astype(I32), axis=1), N_EXPERTS - 1)
    n_used = (pad_end[-1:] // bm).astype(I32)
    xs = _sc_scatter_rows(xn_packed, dest, n_blocks * bm)
    ys = _moe_experts(xs, blk_exp, n_used, *experts_bf16, bm)
    return _sc_gather_rows(ys, dest)


def _combine_kernel(x_ref, y_ref, gate_ref, o_ref):
    half = D_MODEL // 2
    x = x_ref[...]
    g = gate_ref[...]
    g0 = g[:, 0:1]
    g1 = g[:, 1:2]
    w0 = y_ref[0]
    w1 = y_ref[1]
    o_ref[:, :half] = x[:, :half] + g0 * _unpack_hi(w0) + g1 * _unpack_hi(w1)
    o_ref[:, half:] = x[:, half:] + g0 * _unpack_lo(w0) + g1 * _unpack_lo(w1)


def _combine(x, y, gates):
    n = x.shape[0]
    tm = min(1024, n)
    row = lambda i: (i, 0)
    return pl.pallas_call(
        _combine_kernel, grid=(n // tm,),
        in_specs=[pl.BlockSpec((tm, D_MODEL), row), pl.BlockSpec((2, tm, D_MODEL // 2), lambda i: (0, i, 0)),
                  pl.BlockSpec((tm, LANES), row)],
        out_specs=pl.BlockSpec((tm, D_MODEL), row), out_shape=jax.ShapeDtypeStruct((n, D_MODEL), F32),
        compiler_params=_cparams(("arbitrary",)), name="moe_combine",
    )(x, y, gates)


def _layer_weights(l, p):
    w_in = p["w_in"][l]
    cols = [w_in[:, i * GRP:(i + 1) * GRP] for i in range(11)]
    by_group = [None] * N_PROJ
    for ref_i, g in enumerate(_REF_GROUPS):
        if g is not None:
            by_group[g] = cols[ref_i]
    by_group[PGLU] = cols[5]
    w_dg = jnp.zeros((D_MODEL, LANES), F32).at[:, :GATE_RANK].set(w_in[:, 11 * GRP:])
    wg2 = jnp.zeros((LANES, GRP), F32).at[:GATE_RANK].set(p["d_wg2"][l])
    tile4 = lambda v: jnp.tile(v, HEADS)[None, :]
    hid = np.arange(GRP) // HEAD_DIM
    bd = (hid[:, None] == hid[None, :]).astype(np.float32)
    wr = jnp.zeros((D_MODEL, LANES), F32).at[:, :N_EXPERTS].set(p["r_expert_w"][l])
    wr = wr.at[:, N_EXPERTS:N_EXPERTS + N_GROUPS].set(p["r_group_w"][l])
    wr_hi = wr.astype(BF16)
    br = jnp.zeros((1, LANES), F32).at[0, :N_EXPERTS].set(p["r_expert_b"][l])
    br = br.at[0, N_EXPERTS:N_EXPERTS + N_GROUPS].set(p["r_group_b"][l])
    return {
        "g1": p["norm1_g"][l][None, :],
        "w_in": jnp.concatenate(by_group[:PLA], axis=1).astype(BF16),
        "w_cg": cols[6].astype(BF16),
        "w_dg": w_dg.astype(BF16),
        "wg2": wg2.astype(BF16),
        "bg": p["d_bg"][l][None, :],
        "gq": tile4(p["b_qnorm_g"][l]), "gk": tile4(p["b_knorm_g"][l]), "gav": p["a_vnorm_g"][l][None, :],
        "gon": tile4(p["d_onorm_g"][l]),
        "hsum": jnp.asarray(bd, BF16), "bdmask": jnp.asarray(bd, F32),
        "a_ws": p["a_ws"][l], "a_bs_rows": jnp.repeat(p["a_bs"][l].T, HEAD_DIM, axis=1),
        "b_rel": p["b_rel_bias"][l],
        "c_dw": p["c_dw"][l], "c_dw_b": p["c_dw_b"][l][None, :],
        "c_ln_g": p["c_ln_g"][l][None, :], "c_ln_b": p["c_ln_b"][l][None, :],
        "w_out": p["w_out"][l].astype(BF16),
        "g2": p["norm2_g"][l][None, :],
        "wr": jnp.concatenate([wr_hi, (wr - wr_hi.astype(F32)).astype(BF16)], axis=1), "br": br,
        "e_w_gate": p["e_w_gate"], "e_w_up": p["e_w_up"], "e_w_down": p["e_w_down"], "layer": l,
    }


def _mix_and_route(x, lw, b, t, pending, caches, experts_bf16):
    step = caches is not None
    y_prev, gates_prev = pending if pending is not None else (None, None)
    cast = () if experts_bf16 is not None else (lw["e_w_gate"], lw["e_w_up"], lw["e_w_down"])
    x, proj, kv, a_v, casted = _in_proj(x, lw, t, y_prev, gates_prev, emit_av=step, cast=cast)
    if experts_bf16 is None:
        experts_bf16 = casted
    if not step:
        yb = _attention(proj, lw, b, t)
        yc, tail, ya = _conv_gmlp(proj, lw, b, t)
        yd, sf = _gla(proj, lw, b, t)
    else:
        ck, cv, cc, cs = caches
        yb = _attention(proj, lw, b, t, ck.reshape(b * B_WINDOW, GRP), cv.reshape(b * B_WINDOW, GRP))
        halo = jnp.pad(cc, ((0, 0), (HALO - C_BUF, 0), (0, 0))).reshape(b * HALO, GRP)
        yc, tail, ya = _conv_gmlp(proj, lw, b, t, halo)
        yd, sf = _gla(proj, lw, b, t, cs.reshape(b * GRP, HEAD_DIM))
        a_v = a_v.reshape(b, t, GRP)
    x2, xn_packed, route_i, route_f, counts = _out_proj(ya, yb, yc, yd, x, lw)
    y = _moe(xn_packed, route_i, counts, experts_bf16)
    keep = min(B_WINDOW, t)
    new_k = kv[0].reshape(b, keep, HEADS, HEAD_DIM)
    new_v = kv[1].reshape(b, keep, HEADS, HEAD_DIM)
    new_buf = tail.reshape(b, HALO, GRP)[:, HALO - C_BUF:]
    states = (new_k, new_v, new_buf, sf.reshape(b, HEADS, HEAD_DIM, HEAD_DIM), a_v)
    return x2, (y, route_f), states, experts_bf16


def kernel(x_prompt, x_sample, cache_b_k, cache_b_v, state_c_conv, state_d_gla, norm1_g, w_in, a_vnorm_g, a_ws, a_bs, b_qnorm_g, b_knorm_g, b_rel_bias, c_dw, c_dw_b, c_ln_g, c_ln_b, d_wg2, d_bg, d_onorm_g, w_out, norm2_g, r_group_w, r_group_b, r_expert_w, r_expert_b, e_w_gate, e_w_up, e_w_down):
    params = dict(norm1_g=norm1_g, w_in=w_in, a_vnorm_g=a_vnorm_g, a_ws=a_ws, a_bs=a_bs, b_qnorm_g=b_qnorm_g,
                  b_knorm_g=b_knorm_g, b_rel_bias=b_rel_bias, c_dw=c_dw, c_dw_b=c_dw_b, c_ln_g=c_ln_g, c_ln_b=c_ln_b,
                  d_wg2=d_wg2, d_bg=d_bg, d_onorm_g=d_onorm_g, w_out=w_out, norm2_g=norm2_g, r_group_w=r_group_w,
                  r_group_b=r_group_b, r_expert_w=r_expert_w, r_expert_b=r_expert_b, e_w_gate=e_w_gate,
                  e_w_up=e_w_up, e_w_down=e_w_down)
    depth = w_in.shape[0]
    bp, tp, _ = x_prompt.shape
    bs, ts, _ = x_sample.shape
    xp = x_prompt.reshape(bp * tp, D_MODEL)
    xs = x_sample.reshape(bs * ts, D_MODEL)
    pend_p = pend_s = None
    st_p, st_s = [], []
    for l in range(depth):
        lw = _layer_weights(l, params)
        xp, pend_p, sp, experts = _mix_and_route(xp, lw, bp, tp, pend_p, None, None)
        xs, pend_s, ss, _ = _mix_and_route(xs, lw, bs, ts, pend_s,
                                           (cache_b_k[l], cache_b_v[l], state_c_conv[l], state_d_gla[l]), experts)
        st_p.append(sp)
        st_s.append(ss)
    yp = _combine(xp, pend_p[0], pend_p[1]).reshape(bp, tp, D_MODEL)
    ys = _combine(xs, pend_s[0], pend_s[1]).reshape(bs, ts, D_MODEL)
    stack = lambda sts, i: jnp.stack([s[i] for s in sts])
    return (yp, ys, stack(st_p, 0), stack(st_p, 1), stack(st_p, 2), stack(st_p, 3),
            stack(st_s, 0), stack(st_s, 1), stack(st_s, 2), stack(st_s, 3), stack(st_s, 4))
```

```python
import functools

import numpy as np
import jax
import jax.numpy as jnp
from jax import lax
from jax.experimental import pallas as pl
from jax.experimental.pallas import tpu as pltpu
from jax.experimental.pallas import tpu_sc as plsc

F32 = jnp.float32
BF16 = jnp.bfloat16
I32 = jnp.int32
U32 = jnp.uint32

D_MODEL = 1024
GRP = 256
HEADS = 4
HEAD_DIM = 64
CHUNK = 64
A_CHUNK = 128
B_WINDOW = 512
REL_CLIP = 128
C_WIDTH = 31
C_BUF = C_WIDTH - 1
HALO = 32
GATE_RANK = 16
GLA_TAU = 16.0
GLA_SUB = 16
N_GROUPS = 4
PER_GROUP = 8
N_EXPERTS = 32
D_EXPERT = 512
EPS = 1e-6
NEG_INF = -1e30
LANES = 128
VMEM_LIMIT = 48 * 1024 * 1024
TILE_ROWS = 512
WIDE_TILE_ROWS = 1024
MOE_BLOCK_ROWS = 512
MOE_BLOCK_ROWS_SMALL = 128

PK, PV, PQ, PAU, PAV, PGLU, PDQ, PDK, PDV, PDR, PLA = range(11)
N_PROJ = 11
_REF_GROUPS = (PAU, PAV, PQ, PK, PV, None, None, PDQ, PDK, PDV, PDR)

SC_WORKERS = 32
SC_WIN = 128


def _cparams(sem):
    return pltpu.CompilerParams(dimension_semantics=sem, vmem_limit_bytes=VMEM_LIMIT)


def _sigmoid(x):
    return 1.0 / (1.0 + jnp.exp(-x))


def _gelu_tanh(x):
    c = np.float32(np.sqrt(2.0 / np.pi))
    return 0.5 * x * (1.0 + jnp.tanh(c * (x + np.float32(0.044715) * (x * x * x))))


def _pack_halves(y):
    half = y.shape[1] // 2
    hi = pltpu.bitcast(y[:, :half].astype(BF16).astype(F32), U32)
    lo = pltpu.bitcast(y[:, half:].astype(BF16).astype(F32), U32)
    return hi | (lo >> np.uint32(16))


def _unpack_hi(w):
    return pltpu.bitcast(w & np.uint32(0xFFFF0000), F32)


def _unpack_lo(w):
    return pltpu.bitcast(w << np.uint32(16), F32)


def _head_id(shape, axis, size):
    return lax.broadcasted_iota(I32, shape, axis) // size


def _bd_stack(x, rows):
    x4 = jnp.concatenate([x] * HEADS, axis=0)
    shape = (HEADS * rows, GRP)
    keep = _head_id(shape, 0, rows) == _head_id(shape, 1, HEAD_DIM)
    return jnp.where(keep, x4, jnp.zeros_like(x4))


def _bd_unstack(o, rows):
    lane_h = _head_id((rows, GRP), 1, HEAD_DIM)
    out = o[(HEADS - 1) * rows:HEADS * rows]
    for h in range(HEADS - 2, -1, -1):
        out = jnp.where(lane_h == h, o[h * rows:(h + 1) * rows], out)
    return out


def _head_meansq(o, hsum_ref):
    sq = (o * o).astype(BF16)
    return jnp.dot(sq, hsum_ref[...], preferred_element_type=F32) * np.float32(1.0 / HEAD_DIM)


def _in_kernel(*refs, combine, emit_av, tiles_per_stream, n_cast):
    refs = list(refs)
    x_ref = refs.pop(0)
    if combine:
        y_ref = refs.pop(0)
        gate_ref = refs.pop(0)
    g1_ref, w_ref, wcg_ref, wdg_ref, wg2_ref, bg_ref, gq_ref, gk_ref, gav_ref, hsum_ref = refs[:10]
    cast_in = refs[10:10 + n_cast]
    refs = refs[10 + n_cast:]
    if combine:
        xo_ref = refs.pop(0)
    p_ref = refs.pop(0)
    kc_ref = refs.pop(0)
    vc_ref = refs.pop(0)
    if emit_av:
        av_ref = refs.pop(0)
    cast_out = refs[:n_cast]
    raw = refs[n_cast]
    n_sub = 1
    sub = x_ref.shape[0] // n_sub

    def prologue(s):
        rows = slice(s * sub, (s + 1) * sub)
        x = x_ref[rows, :]
        if combine:
            half = D_MODEL // 2
            g = gate_ref[rows, :]
            g0 = g[:, 0:1]
            g1 = g[:, 1:2]
            w0 = y_ref[0, rows, :]
            w1 = y_ref[1, rows, :]
            xa = x[:, :half] + g0 * _unpack_hi(w0) + g1 * _unpack_hi(w1)
            xb = x[:, half:] + g0 * _unpack_lo(w0) + g1 * _unpack_lo(w1)
            xo_ref[rows, :half] = xa
            xo_ref[rows, half:] = xb
            x = jnp.concatenate([xa, xb], axis=1)
        rs = lax.rsqrt(jnp.mean(x * x, axis=-1, keepdims=True) + EPS)
        return rows, rs, (x * g1_ref[...]).astype(BF16)

    n_slots = PLA + 2

    def matmul(tile, slot):
        rows, _, h = tile
        if slot < PLA:
            raw[rows, slot * GRP:(slot + 1) * GRP] = jnp.dot(h, w_ref[:, slot * GRP:(slot + 1) * GRP],
                                                             preferred_element_type=F32)
        elif slot == PLA:
            raw[rows, PLA * GRP:(PLA + 1) * GRP] = jnp.dot(h, wcg_ref[...], preferred_element_type=F32)
        else:
            raw[rows, (PLA + 1) * GRP:] = jnp.dot(h, wdg_ref[...], preferred_element_type=F32)

    def epilogue(tile, slot):
        rows, rs, _ = tile

        def proj(g):
            return raw[rows, g * GRP:(g + 1) * GRP] * rs

        def put(g, val):
            p_ref[rows, g * GRP:(g + 1) * GRP] = val.astype(BF16)

        if slot == PK:
            r = proj(PK)
            put(PK, r * lax.rsqrt(_head_meansq(r, hsum_ref) + EPS) * gk_ref[...])
        elif slot == PV:
            put(PV, proj(PV))

            @pl.when(pl.program_id(0) % tiles_per_stream == tiles_per_stream - 1)
            def _():
                r = proj(PK)
                kc_ref[rows, :] = r * lax.rsqrt(_head_meansq(r, hsum_ref) + EPS) * gk_ref[...]
                vc_ref[rows, :] = proj(PV)
        elif slot == PQ:
            r = proj(PQ)
            put(PQ, r * lax.rsqrt(_head_meansq(r, hsum_ref) + EPS) * (gq_ref[...] * np.float32(HEAD_DIM ** -0.5)))
        elif slot == PAU:
            put(PAU, _gelu_tanh(proj(PAU)))
        elif slot == PAV:
            r = _gelu_tanh(proj(PAV))
            av = r * lax.rsqrt(jnp.mean(r * r, axis=-1, keepdims=True) + EPS) * gav_ref[...]
            put(PAV, av)
            if emit_av:
                av_ref[rows, :] = av
        elif slot == PGLU:
            pass
        elif slot == PDQ:
            put(PDQ, proj(PDQ) * np.float32(HEAD_DIM ** -0.5))
        elif slot in (PDK, PDV):
            put(slot, proj(slot))
        elif slot == PDR:
            r = proj(PDR)
            put(PDR, r * _sigmoid(r))
        elif slot == PLA:
            put(PGLU, proj(PGLU) * _sigmoid(proj(PLA)))
        else:
            dg = raw[rows, (PLA + 1) * GRP:] * rs
            z = jnp.dot(dg.astype(BF16), wg2_ref[...], preferred_element_type=F32) + bg_ref[...]
            logsig = jnp.minimum(z, 0.0) - jnp.log(1.0 + jnp.exp(-jnp.abs(z)))
            put(PLA, logsig * np.float32(1.0 / GLA_TAU))

    order = (PLA + 1, PK, PQ, PAV, PAU, PGLU, PLA, PDR, PV, PDQ, PDK, PDV)
    assert sorted(order) == list(range(n_slots))
    pairs = [(s, slot) for s in range(n_sub) for slot in order]
    lag = 2
    tiles = {0: prologue(0)}
    for i in range(len(pairs) + lag):
        if i < len(pairs):
            matmul(tiles[pairs[i][0]], pairs[i][1])
        if i == lag and n_sub > 1:
            tiles[1] = prologue(1)
        if i >= lag:
            epilogue(tiles[pairs[i - lag][0]], pairs[i - lag][1])
        if i % 3 == 2 and i // 3 < n_cast:
            cast_out[i // 3][...] = cast_in[i // 3][...].astype(BF16)


def _in_proj(x, lw, t, y=None, gates=None, emit_av=False, cast=()):
    n = x.shape[0]
    tm = min(TILE_ROWS, n)
    steps = n // tm
    combine = y is not None
    keep = min(B_WINDOW, t)
    tps = max(t // tm, 1)
    assert tps == 1 or keep == tm
    row = lambda i: (i, 0)
    const = lambda i: (0, 0)
    ins, specs = [x], [pl.BlockSpec((tm, D_MODEL), row)]
    if combine:
        ins += [y, gates]
        specs += [pl.BlockSpec((2, tm, D_MODEL // 2), lambda i: (0, i, 0)), pl.BlockSpec((tm, LANES), row)]
    consts = [lw["g1"], lw["w_in"], lw["w_cg"], lw["w_dg"], lw["wg2"], lw["bg"], lw["gq"], lw["gk"], lw["gav"], lw["hsum"]]
    ins += consts
    specs += [pl.BlockSpec(c.shape, const) for c in consts]
    cast_shapes, cast_specs = [], []
    layer = lw["layer"]
    for arr in cast:
        depth, rows, cols = arr.shape[0], arr.shape[1] * arr.shape[2], arr.shape[3]
        slab = rows // steps
        assert slab * steps == rows and slab % 16 == 0
        ins.append(arr.reshape(depth * rows, cols))
        specs.append(pl.BlockSpec((slab, cols), lambda i: (layer * steps + i, 0)))
        cast_shapes.append(jax.ShapeDtypeStruct((rows, cols), BF16))
        cast_specs.append(pl.BlockSpec((slab, cols), row))
    newest = jax.ShapeDtypeStruct((n // tps, GRP), F32)
    newest_spec = pl.BlockSpec((tm, GRP), lambda i: (i // tps, 0))
    out_shape = [jax.ShapeDtypeStruct((n, N_PROJ * GRP), BF16), newest, newest]
    out_specs = [pl.BlockSpec((tm, N_PROJ * GRP), row), newest_spec, newest_spec]
    if combine:
        out_shape = [jax.ShapeDtypeStruct((n, D_MODEL), F32)] + out_shape
        out_specs = [pl.BlockSpec((tm, D_MODEL), row)] + out_specs
    if emit_av:
        out_shape.append(jax.ShapeDtypeStruct((n, GRP), F32))
        out_specs.append(pl.BlockSpec((tm, GRP), row))
    out_shape += cast_shapes
    out_specs += cast_specs
    outs = list(pl.pallas_call(
        functools.partial(_in_kernel, combine=combine, emit_av=emit_av, tiles_per_stream=tps, n_cast=len(cast)),
        grid=(steps,), in_specs=specs, out_specs=out_specs, out_shape=out_shape,
        scratch_shapes=[pltpu.VMEM((tm, (PLA + 1) * GRP + LANES), F32)],
        compiler_params=_cparams(("arbitrary",)), name="in_proj",
    )(*ins))
    x_new = outs.pop(0) if combine else x
    proj, k_new, v_new = outs[0], outs[1], outs[2]
    a_v = outs[3] if emit_av else None
    casted = [o.reshape(a.shape[1:]) for o, a in zip(outs[3 + int(emit_av):], cast)]
    return x_new, proj, (k_new, v_new), a_v, casted


def _gmlp_body(u_ref, v_ref, ws_ref, bs_ref, o_ref, chunk, n_chunks):
    lane_h = _head_id((chunk, GRP), 1, HEAD_DIM)
    ri = lax.broadcasted_iota(I32, (chunk, chunk), 0)
    ci = lax.broadcasted_iota(I32, (chunk, chunk), 1)
    wm = [jnp.where(ci <= ri, ws_ref[h], 0.0).astype(BF16) for h in range(HEADS)]
    for c in range(n_chunks):
        rows = slice(c * chunk, (c + 1) * chunk)
        v = v_ref[rows, :]
        sv = jnp.dot(wm[HEADS - 1], v, preferred_element_type=F32)
        for h in range(HEADS - 2, -1, -1):
            sv = jnp.where(lane_h == h, jnp.dot(wm[h], v, preferred_element_type=F32), sv)
        o_ref[rows, :] = (u_ref[rows, :].astype(F32) * (sv + bs_ref[...])).astype(BF16)


def _attn_kernel(q_ref, kc_ref, vc_ref, kp_ref, vp_ref, bias_ref, o_ref, kbuf, vbuf, *, chunk, n_chunks, first_has_past):
    tq = chunk * n_chunks
    win = B_WINDOW + chunk
    kbuf[0:B_WINDOW, :] = kp_ref[...].astype(BF16)
    vbuf[0:B_WINDOW, :] = vp_ref[...].astype(BF16)
    kbuf[B_WINDOW:B_WINDOW + tq, :] = kc_ref[...]
    vbuf[B_WINDOW:B_WINDOW + tq, :] = vc_ref[...]
    col = lax.broadcasted_iota(I32, (HEADS * chunk, win), 1)

    def chunks(no_past):
        for c in range(n_chunks):
            q = q_ref[c * chunk:(c + 1) * chunk, :]
            kk = kbuf[c * chunk:c * chunk + win, :]
            vv = vbuf[c * chunk:c * chunk + win, :]
            s = lax.dot_general(_bd_stack(q, chunk), kk, (((1,), (1,)), ((), ())), preferred_element_type=F32)
            s = s + bias_ref[...]
            if no_past:
                s = jnp.where(col + c * chunk >= B_WINDOW, s, NEG_INF)
            m = jnp.max(s, axis=-1, keepdims=True)
            p = jnp.exp(s - m)
            l = jnp.sum(p, axis=-1, keepdims=True)
            o = jnp.dot(p.astype(BF16), vv, preferred_element_type=F32) * (1.0 / l)
            o_ref[c * chunk:(c + 1) * chunk, :] = _bd_unstack(o, chunk).astype(BF16)

    if first_has_past:
        chunks(False)
    else:
        pl.when(pl.program_id(1) == 0)(functools.partial(chunks, True))
        pl.when(pl.program_id(1) > 0)(functools.partial(chunks, False))


def _attention(proj, lw, b, t, cache_k=None, cache_v=None):
    n = proj.shape[0]
    step = cache_k is not None
    chunk = min(t, CHUNK)
    tq = min(t, WIDE_TILE_ROWS)
    nt = t // tq
    rel = lw["b_rel"]
    win = B_WINDOW + chunk
    lo = REL_CLIP - (chunk - 1)
    n_far = (chunk - 1) + win - (2 * REL_CLIP + 1 - lo)
    by_dist = jnp.concatenate([rel[:, lo:], jnp.broadcast_to(rel[:, -1:], (HEADS, n_far))], axis=1)
    by_key = by_dist[:, ::-1]
    n_k = chunk - 1 + win
    wrapped = jnp.tile(jnp.pad(by_key, ((0, 0), (0, 1))), (1, chunk))[:, :chunk * n_k].reshape(HEADS, chunk, n_k)
    bias = wrapped[:, :, chunk - 1:].astype(F32).reshape(HEADS * chunk, win)
    cur = lambda g: pl.BlockSpec((tq, GRP), lambda bi, j: (bi * nt + j, g))
    if step:
        prev_k = pl.BlockSpec((B_WINDOW, GRP), lambda bi, j: (bi, 0))
        prev_v = prev_k
        pk_arr, pv_arr = cache_k, cache_v
    else:
        per = tq // B_WINDOW
        assert per * B_WINDOW == tq
        past = lambda bi, j: jnp.maximum((bi * nt + j) * per - 1, 0)
        prev_k = pl.BlockSpec((B_WINDOW, GRP), lambda bi, j: (past(bi, j), PK))
        prev_v = pl.BlockSpec((B_WINDOW, GRP), lambda bi, j: (past(bi, j), PV))
        pk_arr, pv_arr = proj, proj
    return pl.pallas_call(
        functools.partial(_attn_kernel, chunk=chunk, n_chunks=tq // chunk, first_has_past=step),
        grid=(b, nt),
        in_specs=[cur(PQ), cur(PK), cur(PV), prev_k, prev_v, pl.BlockSpec(bias.shape, lambda bi, j: (0, 0))],
        out_specs=pl.BlockSpec((tq, GRP), lambda bi, j: (bi * nt + j, 0)),
        out_shape=jax.ShapeDtypeStruct((n, GRP), BF16),
        scratch_shapes=[pltpu.VMEM((B_WINDOW + tq, GRP), BF16), pltpu.VMEM((B_WINDOW + tq, GRP), BF16)],
        compiler_params=_cparams(("arbitrary", "arbitrary")), name="band_attn",
    )(proj, proj, proj, pk_arr, pv_arr, bias)


def _conv_kernel(g_ref, halo_ref, dw_ref, dwb_ref, lng_ref, lnb_ref, u_ref, v_ref, ws_ref, bs_ref,
                 o_ref, tail_ref, ya_ref, xp, zbuf, *, tc, sub, first_has_past, a_chunk):
    _gmlp_body(u_ref, v_ref, ws_ref, bs_ref, ya_ref, a_chunk, tc // a_chunk)
    halo = halo_ref[...].astype(F32)
    has_past = jnp.logical_or(pl.program_id(1) > 0, first_has_past)
    xp[0:HALO, :] = jnp.where(has_past, halo, 0.0)
    xp[HALO:HALO + tc, :] = g_ref[...].astype(F32)
    xp[HALO + tc:, :] = jnp.zeros((xp.shape[0] - HALO - tc, GRP), F32)

    @pl.when(pl.program_id(1) == pl.num_programs(1) - 1)
    def _():
        tail_ref[...] = xp[tc:tc + HALO, :]

    lead = HALO - C_BUF
    sl = 8
    for s in range(tc // sub):
        acc = None
        for r in range(sl):
            taps = [p for p in range(r, lead + C_WIDTH, sl) if p >= lead]
            z = None
            for p in taps:
                a0 = s * sub + p - r
                term = dw_ref[p - lead:p - lead + 1, :] * xp[a0:a0 + sub + sl, :]
                z = term if z is None else z + term
            zbuf[r] = z
            part = zbuf[r, r:r + sub, :]
            acc = part if acc is None else acc + part
        y = acc + dwb_ref[...]
        mu = jnp.mean(y, axis=-1, keepdims=True)
        yc = y - mu
        y = yc * lax.rsqrt(jnp.mean(yc * yc, axis=-1, keepdims=True) + EPS) * lng_ref[...] + lnb_ref[...]
        o_ref[s * sub:(s + 1) * sub, :] = (y * _sigmoid(y)).astype(BF16)


def _conv_gmlp(proj, lw, b, t, state=None):
    n = proj.shape[0]
    step = state is not None
    tc = min(t, WIDE_TILE_ROWS)
    nt = t // tc
    sub = min(tc, 64)
    a_chunk = min(t, A_CHUNK)
    ws = lw["a_ws"][:, :a_chunk, :a_chunk]
    bs = lw["a_bs_rows"][:a_chunk]
    if step:
        halo_arr = state
        halo_spec = pl.BlockSpec((HALO, GRP), lambda bi, j: (bi, 0))
    else:
        per = tc // HALO
        halo_arr = proj
        halo_spec = pl.BlockSpec((HALO, GRP), lambda bi, j: (jnp.maximum((bi * nt + j) * per - 1, 0), PGLU))
    vec = pl.BlockSpec((1, GRP), lambda bi, j: (0, 0))
    cur = lambda g: pl.BlockSpec((tc, GRP), lambda bi, j: (bi * nt + j, g))
    return pl.pallas_call(
        functools.partial(_conv_kernel, tc=tc, sub=sub, first_has_past=step, a_chunk=a_chunk),
        grid=(b, nt),
        in_specs=[cur(PGLU), halo_spec, pl.BlockSpec((C_WIDTH, GRP), lambda bi, j: (0, 0)), vec, vec, vec,
                  cur(PAU), cur(PAV), pl.BlockSpec(ws.shape, lambda bi, j: (0, 0, 0)),
                  pl.BlockSpec(bs.shape, lambda bi, j: (0, 0))],
        out_specs=[cur(0), pl.BlockSpec((HALO, GRP), lambda bi, j: (bi, 0)), cur(0)],
        out_shape=[jax.ShapeDtypeStruct((n, GRP), BF16), jax.ShapeDtypeStruct((b * HALO, GRP), F32),
                   jax.ShapeDtypeStruct((n, GRP), BF16)],
        scratch_shapes=[pltpu.VMEM((HALO + tc + 8, GRP), F32), pltpu.VMEM((8, sub + 8, GRP), F32)],
        compiler_params=_cparams(("arbitrary", "arbitrary")), name="conv_gmlp",
    )(proj, halo_arr, lw["c_dw"], lw["c_dw_b"], lw["c_ln_g"], lw["c_ln_b"], proj, proj, ws, bs)


def _gla_tables(L):
    i = np.arange(L)[:, None]
    t = np.arange(L)[None, :]
    masks = []
    s = GLA_SUB
    masks.append(((i // s) == (t // s)) & (t <= i))
    s *= 2
    while s <= L:
        h = s // 2
        masks.append(((i // s) == (t // s)) & (i % s >= h) & (t % s < h))
        s *= 2
    tri = (t <= i).astype(np.float32)
    mask = np.stack([np.tile(m.astype(np.float32), (1, HEADS)) for m in masks], axis=0)
    return tri, mask


def _gla_anchor(cum, row, L, size, first_half):
    out = None
    for start in range(0, L, size):
        ar = start + size // 2 - 1 if first_half else start - 1
        val = jnp.zeros((L, GRP), F32) if ar < 0 else jnp.broadcast_to(cum[ar:ar + 1, :], (L, GRP))
        out = val if out is None else jnp.where(row >= start, val, out)
    return out


def _gla_kernel(q_ref, k_ref, v_ref, la_ref, dr_ref, s0_ref, tri_ref, lmask_ref, bdmask_ref, hsum_ref, gon_ref,
                o_ref, sf_ref, st, o_all, *, L, n_chunks, n_levels, first_has_state):
    j = pl.program_id(1)

    @pl.when(j == 0)
    def _():
        st[...] = jnp.zeros_like(st)
        if first_has_state:
            for h in range(HEADS):
                blk = slice(h * HEAD_DIM, (h + 1) * HEAD_DIM)
                st[blk, blk] = s0_ref[blk, :].T

    row = lax.broadcasted_iota(I32, (L, GRP), 0)
    dn_t = (((1,), (1,)), ((), ()))

    def prep(c):
        rows = slice(c * L, (c + 1) * L)
        q = q_ref[rows, :].astype(F32)
        k = k_ref[rows, :].astype(F32)
        v = v_ref[rows, :]
        cum = jnp.dot(tri_ref[...], la_ref[rows, :], preferred_element_type=F32)
        total = cum[L - 1:L, :]
        pairs = []
        for lvl in range(n_levels):
            size = GLA_SUB << lvl
            if lvl == 0:
                local = cum - _gla_anchor(cum, row, L, size, False)
                ql = q * jnp.exp(local)
                kl = k * jnp.exp(-local)
            else:
                upper = (row & (size - 1)) >= (size // 2)
                d = cum - _gla_anchor(cum, row, L, size, True)
                w = jnp.exp(jnp.where(upper, d, -d))
                ql = jnp.where(upper, q * w, 0.0)
                kl = jnp.where(upper, 0.0, k * w)
            pairs.append((ql.astype(BF16), _bd_stack(kl.astype(BF16), L)))
        return dict(rows=rows, v=v, qp=(q * jnp.exp(cum)).astype(BF16), kst=(k * jnp.exp(total - cum)).astype(BF16),
                    decay=jnp.exp(total), pairs=pairs)

    def intra(p):
        att = None
        for lvl, (ql, kbd) in enumerate(p["pairs"]):
            a = lax.dot_general(ql, kbd, dn_t, preferred_element_type=F32) * lmask_ref[lvl]
            att = a if att is None else att + a
        p["o_intra"] = jnp.dot(att.astype(BF16), _bd_stack(p["v"], L), preferred_element_type=F32)
        p["upd"] = lax.dot_general(p["v"], p["kst"], (((0,), (0,)), ((), ())),
                                   preferred_element_type=F32) * bdmask_ref[...]
        return p

    def finish(p, s_t):
        o_all[p["rows"], :] = lax.dot_general(p["qp"], s_t.astype(BF16), dn_t, preferred_element_type=F32) + p["o_intra"]
        return s_t * p["decay"] + p["upd"]

    s_t = st[...]
    stage1, stage2 = {}, {}
    for step in range(n_chunks + 2):
        if step < n_chunks:
            stage1[step] = prep(step)
        if 0 <= step - 1 < n_chunks:
            stage2[step - 1] = intra(stage1.pop(step - 1))
        if 0 <= step - 2 < n_chunks:
            s_t = finish(stage2.pop(step - 2), s_t)
    st[...] = s_t
    o = o_all[...]
    y = o * lax.rsqrt(_head_meansq(o, hsum_ref) + EPS) * gon_ref[...] * dr_ref[...].astype(F32)
    o_ref[...] = y.astype(BF16)

    @pl.when(j == pl.num_programs(1) - 1)
    def _():
        for h in range(HEADS):
            blk = slice(h * HEAD_DIM, (h + 1) * HEAD_DIM)
            sf_ref[blk, :] = st[blk, blk].T


def _gla(proj, lw, b, t, s0=None):
    n = proj.shape[0]
    step = s0 is not None
    L = min(t, 64)
    td = min(t, WIDE_TILE_ROWS)
    nt = t // td
    n_levels = int(np.log2(L // GLA_SUB)) + 1
    tri, lmask = _gla_tables(L)
    tri = jnp.asarray(tri, BF16)
    lmask = jnp.asarray(lmask, F32)
    if not step:
        s0 = jnp.zeros((GRP, HEAD_DIM), F32)
        s0_spec = pl.BlockSpec((GRP, HEAD_DIM), lambda bi, j: (0, 0))
    else:
        s0_spec = pl.BlockSpec((GRP, HEAD_DIM), lambda bi, j: (bi, 0))
    cur = lambda g: pl.BlockSpec((td, GRP), lambda bi, j: (bi * nt + j, g))
    c2 = lambda bi, j: (0, 0)
    return pl.pallas_call(
        functools.partial(_gla_kernel, L=L, n_chunks=td // L, n_levels=n_levels, first_has_state=step),
        grid=(b, nt),
        in_specs=[cur(PDQ), cur(PDK), cur(PDV), cur(PLA), cur(PDR), s0_spec,
                  pl.BlockSpec(tri.shape, c2), pl.BlockSpec(lmask.shape, lambda bi, j: (0, 0, 0)),
                  pl.BlockSpec((GRP, GRP), c2), pl.BlockSpec((GRP, GRP), c2), pl.BlockSpec((1, GRP), c2)],
        out_specs=[pl.BlockSpec((td, GRP), lambda bi, j: (bi * nt + j, 0)),
                   pl.BlockSpec((GRP, HEAD_DIM), lambda bi, j: (bi, 0))],
        out_shape=[jax.ShapeDtypeStruct((n, GRP), BF16), jax.ShapeDtypeStruct((b * GRP, HEAD_DIM), F32)],
        scratch_shapes=[pltpu.VMEM((GRP, GRP), F32), pltpu.VMEM((td, GRP), F32)],
        compiler_params=_cparams(("arbitrary", "arbitrary")), name="gla",
    )(proj, proj, proj, proj, proj, s0, tri, lmask, lw["bdmask"], lw["hsum"], lw["gon"])


def _out_kernel(ya_ref, yb_ref, yc_ref, yd_ref, x_ref, wo_ref, g2_ref, wr_ref, br_ref, tri_ref,
                xo_ref, xn_ref, ri_ref, rf_ref, cnt_ref, *, tm, n_sub):
    @pl.when(pl.program_id(0) == 0)
    def _():
        cnt_ref[...] = jnp.zeros_like(cnt_ref)

    sub = tm // n_sub
    lane = lax.broadcasted_iota(I32, (sub, LANES), 1)
    lane_f = lane.astype(F32)
    lane_grp_f = (lane // PER_GROUP).astype(F32)
    is_grp = jnp.logical_and(lane >= N_EXPERTS, lane < N_EXPERTS + N_GROUPS)
    far = np.float32(1 << 20)

    def mix(s):
        rows = slice(s * sub, (s + 1) * sub)
        ycat = jnp.concatenate([ya_ref[rows, :], yb_ref[rows, :], yc_ref[rows, :], yd_ref[rows, :]], axis=1)
        x = x_ref[rows, :] + jnp.dot(ycat, wo_ref[...], preferred_element_type=F32)
        xo_ref[rows, :] = x
        return x

    def norm_logits(s, x):
        rows = slice(s * sub, (s + 1) * sub)
        xn = x * lax.rsqrt(jnp.mean(x * x, axis=-1, keepdims=True) + EPS) * g2_ref[...]
        xn_ref[rows, :] = _pack_halves(xn)
        both = jnp.dot(xn.astype(BF16), wr_ref[...], preferred_element_type=F32)
        return both[:, :LANES] + both[:, LANES:] + br_ref[...]

    def route(s, logits):
        rows = slice(s * sub, (s + 1) * sub)

        def first_max(masked):
            v = jnp.max(masked, axis=-1, keepdims=True)
            return v, jnp.min(jnp.where(masked == v, lane_f, far), axis=-1, keepdims=True)

        grp_logits = jnp.where(is_grp, logits, -jnp.inf)
        gmax, gidx = first_max(grp_logits)
        p_grp = 1.0 / jnp.sum(jnp.exp(grp_logits - gmax), axis=-1, keepdims=True)
        in_grp = lane_grp_f == gidx - np.float32(N_EXPERTS)
        exp_logits = jnp.where(in_grp, logits, -jnp.inf)
        v1, i1 = first_max(exp_logits)
        v2, i2 = first_max(jnp.where(lane_f == i1, -jnp.inf, exp_logits))
        e21 = jnp.exp(v2 - v1)
        gate1 = p_grp / (1.0 + e21)
        gate2 = p_grp * e21 / (1.0 + e21)

        oh1 = lane_f == i1
        oh2 = lane_f == i2
        ones = jnp.where(oh1, 1.0, jnp.where(oh2, 1.0, 0.0))
        before = jnp.dot(tri_ref[...], ones.astype(BF16), preferred_element_type=F32) + cnt_ref[...].astype(F32)
        rank1 = jnp.sum(jnp.where(oh1, before, 0.0), axis=-1, keepdims=True)
        rank2 = jnp.sum(jnp.where(oh2, before, 0.0), axis=-1, keepdims=True)
        cnt_ref[...] = cnt_ref[...] + jnp.sum(ones, axis=0, keepdims=True).astype(I32)

        ri = jnp.where(lane == 0, i1, jnp.where(lane == 1, i2, jnp.where(lane == 2, rank1,
                                                                          jnp.where(lane == 3, rank2, 0.0))))
        ri_ref[rows, :] = ri.astype(I32)
        rf_ref[rows, :] = jnp.where(lane == 0, gate1, jnp.where(lane == 1, gate2, 0.0))

    xs, lg = {}, {}
    for step in range(n_sub + 2):
        if step < n_sub:
            xs[step] = mix(step)
        if 0 <= step - 1 < n_sub:
            lg[step - 1] = norm_logits(step - 1, xs.pop(step - 1))
        if 0 <= step - 2 < n_sub:
            route(step - 2, lg.pop(step - 2))


def _out_proj(ya, yb, yc, yd, x, lw):
    n = x.shape[0]
    tm = min(WIDE_TILE_ROWS, n)
    row = lambda i: (i, 0)
    const = lambda i: (0, 0)
    n_sub = 4 if tm >= WIDE_TILE_ROWS else 1
    sub = tm // n_sub
    tri = jnp.asarray(np.tril(np.ones((sub, sub), np.float32), -1), BF16)
    consts = [lw["w_out"], lw["g2"], lw["wr"], lw["br"], tri]
    yspec = pl.BlockSpec((tm, GRP), row)
    return pl.pallas_call(
        functools.partial(_out_kernel, tm=tm, n_sub=n_sub),
        grid=(n // tm,),
        in_specs=[yspec, yspec, yspec, yspec, pl.BlockSpec((tm, D_MODEL), row)] + [pl.BlockSpec(c.shape, const) for c in consts],
        out_specs=[pl.BlockSpec((tm, D_MODEL), row), pl.BlockSpec((tm, D_MODEL // 2), row),
                   pl.BlockSpec((tm, LANES), row), pl.BlockSpec((tm, LANES), row), pl.BlockSpec((1, LANES), const)],
        out_shape=[jax.ShapeDtypeStruct((n, D_MODEL), F32), jax.ShapeDtypeStruct((n, D_MODEL // 2), U32),
                   jax.ShapeDtypeStruct((n, LANES), I32), jax.ShapeDtypeStruct((n, LANES), F32),
                   jax.ShapeDtypeStruct((1, LANES), I32)],
        compiler_params=_cparams(("arbitrary",)), name="out_proj_router",
    )(ya, yb, yc, yd, x, *consts)


def _sc_scatter_rows(x, idx, n_out):
    n, d = x.shape
    kk = idx.shape[0]
    per_w = n // SC_WORKERS
    win = min(SC_WIN, per_w)
    n_win = per_w // win
    assert n_win * win * SC_WORKERS == n
    mesh = plsc.VectorSubcoreMesh(core_axis_name="c", subcore_axis_name="s")

    @functools.partial(
        pl.kernel, mesh=mesh, out_type=jax.ShapeDtypeStruct((n_out, d), x.dtype),
        scratch_types=[pltpu.VMEM((kk, win), I32), pltpu.VMEM((win, d), x.dtype)],
        name="sc_scatter_rows")
    def k(x_hbm, idx_hbm, o_hbm, idx_v, rows_v):
        wid = lax.axis_index("s") * 2 + lax.axis_index("c")
        base = wid * per_w

        @pl.loop(0, n_win)
        def _(w):
            off = base + w * win
            pltpu.sync_copy(x_hbm.at[pl.ds(off, win)], rows_v)
            for j in range(kk):
                pltpu.sync_copy(idx_hbm.at[j, pl.ds(off, win)], idx_v.at[j])
                pltpu.sync_copy(rows_v, o_hbm.at[idx_v.at[j]])

    return k(x, idx)


def _sc_gather_rows(y, idx):
    _, d = y.shape
    kk, n = idx.shape
    per_w = n // SC_WORKERS
    win = min(SC_WIN, per_w)
    n_win = per_w // win
    assert n_win * win * SC_WORKERS == n
    mesh = plsc.VectorSubcoreMesh(core_axis_name="c", subcore_axis_name="s")

    @functools.partial(
        pl.kernel, mesh=mesh, out_type=jax.ShapeDtypeStruct((kk, n, d), y.dtype),
        scratch_types=[pltpu.VMEM((kk, win), I32), pltpu.VMEM((win, d), y.dtype)],
        name="sc_gather_rows")
    def k(y_hbm, idx_hbm, o_hbm, idx_v, rows_v):
        wid = lax.axis_index("s") * 2 + lax.axis_index("c")
        base = wid * per_w

        @pl.loop(0, n_win)
        def _(w):
            off = base + w * win
            for j in range(kk):
                pltpu.sync_copy(idx_hbm.at[j, pl.ds(off, win)], idx_v.at[j])
                pltpu.sync_copy(y_hbm.at[idx_v.at[j]], rows_v)
                pltpu.sync_copy(rows_v, o_hbm.at[j, pl.ds(off, win)])

    return k(y, idx)


def _moe_kernel(bexp_ref, nused_ref, x_ref, wg_ref, wu_ref, wd_ref, o_ref, *, n_sub):
    del bexp_ref

    @pl.when(pl.program_id(0) < nused_ref[0])
    def _():
        sub = x_ref.shape[0] // n_sub

        def up(s):
            w = x_ref[s * sub:(s + 1) * sub, :]
            x = jnp.concatenate([_unpack_hi(w).astype(BF16), _unpack_lo(w).astype(BF16)], axis=1)
            hg = jnp.dot(x, wg_ref[...], preferred_element_type=F32)
            hu = jnp.dot(x, wu_ref[...], preferred_element_type=F32)
            return (hg * _sigmoid(hg) * hu).astype(BF16)

        def down(s, h):
            o_ref[s * sub:(s + 1) * sub, :] = _pack_halves(jnp.dot(h, wd_ref[...], preferred_element_type=F32))

        h = up(0)
        for s in range(n_sub):
            nxt = up(s + 1) if s + 1 < n_sub else None
            down(s, h)
            h = nxt


def _moe_experts(xs, blk_exp, n_used, w_gate, w_up, w_down, bm):
    p = xs.shape[0]
    n_blocks = p // bm
    live = lambda i, be, nu: jnp.minimum(i, jnp.maximum(nu[0] - 1, 0))
    wspec = lambda shape: pl.BlockSpec((None,) + shape, lambda i, be, nu: (be[live(i, be, nu)], 0, 0))
    grid_spec = pltpu.PrefetchScalarGridSpec(
        num_scalar_prefetch=2, grid=(n_blocks,),
        in_specs=[pl.BlockSpec((bm, D_MODEL // 2), lambda i, be, nu: (live(i, be, nu), 0)),
                  wspec((D_MODEL, D_EXPERT)), wspec((D_MODEL, D_EXPERT)), wspec((D_EXPERT, D_MODEL))],
        out_specs=pl.BlockSpec((bm, D_MODEL // 2), lambda i, be, nu: (live(i, be, nu), 0)))
    return pl.pallas_call(
        functools.partial(_moe_kernel, n_sub=2 if bm >= MOE_BLOCK_ROWS else 1),
        grid_spec=grid_spec, out_shape=jax.ShapeDtypeStruct((p, D_MODEL // 2), U32),
        compiler_params=_cparams(("arbitrary",)), name="moe_experts",
    )(blk_exp, n_used, xs, w_gate, w_up, w_down)


def _moe_block_rows(n):
    return MOE_BLOCK_ROWS if 2 * n >= N_EXPERTS * 2 * MOE_BLOCK_ROWS else MOE_BLOCK_ROWS_SMALL


def _moe(xn_packed, route_i, counts, experts_bf16):
    n = xn_packed.shape[0]
    bm = _moe_block_rows(n)
    n_blocks = -(-(2 * n + N_EXPERTS * (bm - 1)) // bm)
    cnt = counts[0, :N_EXPERTS]
    padded = (cnt + bm - 1) // bm * bm
    pad_end = jnp.cumsum(padded)
    pad_start = pad_end - padded
    experts = jnp.arange(N_EXPERTS, dtype=I32)
    eid = route_i[:, 0:2].T
    start_of = jnp.sum(jnp.where(eid[:, :, None] == experts, pad_start, 0), axis=-1)
    dest = (start_of + route_i[:, 2:4].T).astype(I32)
    first_row = jnp.arange(n_blocks, dtype=I32) * bm
    blk_exp = jnp.minimum(jnp.sum((pad_end[None, :] <= first_row[:, None]).astype(I32), axis=1), N_EXPERTS - 1)
    n_used = (pad_end[-1:] // bm).astype(I32)
    xs = _sc_scatter_rows(xn_packed, dest, n_blocks * bm)
    ys = _moe_experts(xs, blk_exp, n_used, *experts_bf16, bm)
    return _sc_gather_rows(ys, dest)


def _combine_kernel(x_ref, y_ref, gate_ref, o_ref):
    half = D_MODEL // 2
    x = x_ref[...]
    g = gate_ref[...]
    g0 = g[:, 0:1]
    g1 = g[:, 1:2]
    w0 = y_ref[0]
    w1 = y_ref[1]
    o_ref[:, :half] = x[:, :half] + g0 * _unpack_hi(w0) + g1 * _unpack_hi(w1)
    o_ref[:, half:] = x[:, half:] + g0 * _unpack_lo(w0) + g1 * _unpack_lo(w1)


def _combine(x, y, gates):
    n = x.shape[0]
    tm = min(WIDE_TILE_ROWS, n)
    row = lambda i: (i, 0)
    return pl.pallas_call(
        _combine_kernel, grid=(n // tm,),
        in_specs=[pl.BlockSpec((tm, D_MODEL), row), pl.BlockSpec((2, tm, D_MODEL // 2), lambda i: (0, i, 0)),
                  pl.BlockSpec((tm, LANES), row)],
        out_specs=pl.BlockSpec((tm, D_MODEL), row), out_shape=jax.ShapeDtypeStruct((n, D_MODEL), F32),
        compiler_params=_cparams(("arbitrary",)), name="moe_combine",
    )(x, y, gates)


def _layer_weights(l, p):
    w_in = p["w_in"][l]
    cols = [w_in[:, i * GRP:(i + 1) * GRP] for i in range(11)]
    by_group = [None] * N_PROJ
    for ref_i, g in enumerate(_REF_GROUPS):
        if g is not None:
            by_group[g] = cols[ref_i]
    by_group[PGLU] = cols[5]
    w_dg = jnp.zeros((D_MODEL, LANES), F32).at[:, :GATE_RANK].set(w_in[:, 11 * GRP:])
    wg2 = jnp.zeros((LANES, GRP), F32).at[:GATE_RANK].set(p["d_wg2"][l])
    tile4 = lambda v: jnp.tile(v, HEADS)[None, :]
    hid = np.arange(GRP) // HEAD_DIM
    bd = (hid[:, None] == hid[None, :]).astype(np.float32)
    wr = jnp.zeros((D_MODEL, LANES), F32).at[:, :N_EXPERTS].set(p["r_expert_w"][l])
    wr = wr.at[:, N_EXPERTS:N_EXPERTS + N_GROUPS].set(p["r_group_w"][l])
    wr_hi = wr.astype(BF16)
    br = jnp.zeros((1, LANES), F32).at[0, :N_EXPERTS].set(p["r_expert_b"][l])
    br = br.at[0, N_EXPERTS:N_EXPERTS + N_GROUPS].set(p["r_group_b"][l])
    return {
        "g1": p["norm1_g"][l][None, :],
        "w_in": jnp.concatenate(by_group[:PLA], axis=1).astype(BF16),
        "w_cg": cols[6].astype(BF16),
        "w_dg": w_dg.astype(BF16),
        "wg2": wg2.astype(BF16),
        "bg": p["d_bg"][l][None, :],
        "gq": tile4(p["b_qnorm_g"][l]), "gk": tile4(p["b_knorm_g"][l]), "gav": p["a_vnorm_g"][l][None, :],
        "gon": tile4(p["d_onorm_g"][l]),
        "hsum": jnp.asarray(bd, BF16), "bdmask": jnp.asarray(bd, F32),
        "a_ws": p["a_ws"][l], "a_bs_rows": jnp.repeat(p["a_bs"][l].T, HEAD_DIM, axis=1),
        "b_rel": p["b_rel_bias"][l],
        "c_dw": p["c_dw"][l], "c_dw_b": p["c_dw_b"][l][None, :],
        "c_ln_g": p["c_ln_g"][l][None, :], "c_ln_b": p["c_ln_b"][l][None, :],
        "w_out": p["w_out"][l].astype(BF16),
        "g2": p["norm2_g"][l][None, :],
        "wr": jnp.concatenate([wr_hi, (wr - wr_hi.astype(F32)).astype(BF16)], axis=1), "br": br,
        "e_w_gate": p["e_w_gate"], "e_w_up": p["e_w_up"], "e_w_down": p["e_w_down"], "layer": l,
    }


def _mix_and_route(x, lw, b, t, pending, caches, experts_bf16):
    step = caches is not None
    y_prev, gates_prev = pending if pending is not None else (None, None)
    cast = () if experts_bf16 is not None else (lw["e_w_gate"], lw["e_w_up"], lw["e_w_down"])
    x, proj, kv, a_v, casted = _in_proj(x, lw, t, y_prev, gates_prev, emit_av=step, cast=cast)
    if experts_bf16 is None:
        experts_bf16 = casted
    if not step:
        yb = _attention(proj, lw, b, t)
        yc, tail, ya = _conv_gmlp(proj, lw, b, t)
        yd, sf = _gla(proj, lw, b, t)
    else:
        ck, cv, cc, cs = caches
        yb = _attention(proj, lw, b, t, ck.reshape(b * B_WINDOW, GRP), cv.reshape(b * B_WINDOW, GRP))
        halo = jnp.pad(cc, ((0, 0), (HALO - C_BUF, 0), (0, 0))).reshape(b * HALO, GRP)
        yc, tail, ya = _conv_gmlp(proj, lw, b, t, halo)
        yd, sf = _gla(proj, lw, b, t, cs.reshape(b * GRP, HEAD_DIM))
        a_v = a_v.reshape(b, t, GRP)
    x2, xn_packed, route_i, route_f, counts = _out_proj(ya, yb, yc, yd, x, lw)
    y = _moe(xn_packed, route_i, counts, experts_bf16)
    keep = min(B_WINDOW, t)
    new_k = kv[0].reshape(b, keep, HEADS, HEAD_DIM)
    new_v = kv[1].reshape(b, keep, HEADS, HEAD_DIM)
    new_buf = tail.reshape(b, HALO, GRP)[:, HALO - C_BUF:]
    states = (new_k, new_v, new_buf, sf.reshape(b, HEADS, HEAD_DIM, HEAD_DIM), a_v)
    return x2, (y, route_f), states, experts_bf16


def kernel(x_prompt, x_sample, cache_b_k, cache_b_v, state_c_conv, state_d_gla, norm1_g, w_in, a_vnorm_g, a_ws, a_bs, b_qnorm_g, b_knorm_g, b_rel_bias, c_dw, c_dw_b, c_ln_g, c_ln_b, d_wg2, d_bg, d_onorm_g, w_out, norm2_g, r_group_w, r_group_b, r_expert_w, r_expert_b, e_w_gate, e_w_up, e_w_down):
    params = dict(norm1_g=norm1_g, w_in=w_in, a_vnorm_g=a_vnorm_g, a_ws=a_ws, a_bs=a_bs, b_qnorm_g=b_qnorm_g,
                  b_knorm_g=b_knorm_g, b_rel_bias=b_rel_bias, c_dw=c_dw, c_dw_b=c_dw_b, c_ln_g=c_ln_g, c_ln_b=c_ln_b,
                  d_wg2=d_wg2, d_bg=d_bg, d_onorm_g=d_onorm_g, w_out=w_out, norm2_g=norm2_g, r_group_w=r_group_w,
                  r_group_b=r_group_b, r_expert_w=r_expert_w, r_expert_b=r_expert_b, e_w_gate=e_w_gate,
                  e_w_up=e_w_up, e_w_down=e_w_down)
    depth = w_in.shape[0]
    bp, tp, _ = x_prompt.shape
    bs, ts, _ = x_sample.shape
    xp = x_prompt.reshape(bp * tp, D_MODEL)
    xs = x_sample.reshape(bs * ts, D_MODEL)
    pend_p = pend_s = None
    st_p, st_s = [], []
    for l in range(depth):
        lw = _layer_weights(l, params)
        xp, pend_p, sp, experts = _mix_and_route(xp, lw, bp, tp, pend_p, None, None)
        xs, pend_s, ss, _ = _mix_and_route(xs, lw, bs, ts, pend_s,
                                           (cache_b_k[l], cache_b_v[l], state_c_conv[l], state_d_gla[l]), experts)
        st_p.append(sp)
        st_s.append(ss)
    yp = _combine(xp, pend_p[0], pend_p[1]).reshape(bp, tp, D_MODEL)
    ys = _combine(xs, pend_s[0], pend_s[1]).reshape(bs, ts, D_MODEL)
    stack = lambda sts, i: jnp.stack([s[i] for s in sts])
    return (yp, ys, stack(st_p, 0), stack(st_p, 1), stack(st_p, 2), stack(st_p, 3),
            stack(st_s, 0), stack(st_s, 1), stack(st_s, 2), stack(st_s, 3), stack(st_s, 4))
```

```python
import functools

import numpy as np
import jax
import jax.numpy as jnp
from jax import lax
from jax.experimental import pallas as pl
from jax.experimental.pallas import tpu as pltpu
from jax.experimental.pallas import tpu_sc as plsc

F32 = jnp.float32
BF16 = jnp.bfloat16
I32 = jnp.int32
U32 = jnp.uint32

D_MODEL = 1024
GRP = 256
HEADS = 4
HEAD_DIM = 64
CHUNK = 64
A_CHUNK = 128
B_WINDOW = 512
REL_CLIP = 128
C_WIDTH = 31
C_BUF = C_WIDTH - 1
HALO = 32
GATE_RANK = 16
GLA_TAU = 16.0
GLA_SUB = 16
N_GROUPS = 4
PER_GROUP = 8
N_EXPERTS = 32
D_EXPERT = 512
EPS = 1e-6
NEG_INF = -1e30
LANES = 128
VMEM_LIMIT = 48 * 1024 * 1024
TILE_ROWS = 512
WIDE_TILE_ROWS = 1024
MOE_BLOCK_ROWS = 512
MOE_BLOCK_ROWS_SMALL = 128

PK, PV, PQ, PAU, PAV, PGLU, PDQ, PDK, PDV, PDR, PLA = range(11)
N_PROJ = 11
_REF_GROUPS = (PAU, PAV, PQ, PK, PV, None, None, PDQ, PDK, PDV, PDR)

SC_WORKERS = 32
SC_WIN = 128


def _cparams(sem):
    return pltpu.CompilerParams(dimension_semantics=sem, vmem_limit_bytes=VMEM_LIMIT)


def _sigmoid(x):
    return 1.0 / (1.0 + jnp.exp(-x))


def _gelu_tanh(x):
    c = np.float32(np.sqrt(2.0 / np.pi))
    return 0.5 * x * (1.0 + jnp.tanh(c * (x + np.float32(0.044715) * (x * x * x))))


def _pack_halves(y):
    half = y.shape[1] // 2
    hi = pltpu.bitcast(y[:, :half].astype(BF16).astype(F32), U32)
    lo = pltpu.bitcast(y[:, half:].astype(BF16).astype(F32), U32)
    return hi | (lo >> np.uint32(16))


def _unpack_hi(w):
    return pltpu.bitcast(w & np.uint32(0xFFFF0000), F32)


def _unpack_lo(w):
    return pltpu.bitcast(w << np.uint32(16), F32)


def _head_id(shape, axis, size):
    return lax.broadcasted_iota(I32, shape, axis) // size


def _bd_stack(x, rows):
    x4 = jnp.concatenate([x] * HEADS, axis=0)
    shape = (HEADS * rows, GRP)
    keep = _head_id(shape, 0, rows) == _head_id(shape, 1, HEAD_DIM)
    return jnp.where(keep, x4, jnp.zeros_like(x4))


def _bd_unstack(o, rows):
    lane_h = _head_id((rows, GRP), 1, HEAD_DIM)
    out = o[(HEADS - 1) * rows:HEADS * rows]
    for h in range(HEADS - 2, -1, -1):
        out = jnp.where(lane_h == h, o[h * rows:(h + 1) * rows], out)
    return out


def _head_meansq(o, hsum_ref):
    sq = (o * o).astype(BF16)
    return jnp.dot(sq, hsum_ref[...], preferred_element_type=F32) * np.float32(1.0 / HEAD_DIM)


def _in_kernel(*refs, combine, emit_av, tiles_per_stream, n_cast):
    refs = list(refs)
    x_ref = refs.pop(0)
    if combine:
        y_ref = refs.pop(0)
        gate_ref = refs.pop(0)
    g1_ref, w_ref, wcg_ref, wdg_ref, wg2_ref, bg_ref, gq_ref, gk_ref, gav_ref, hsum_ref = refs[:10]
    cast_in = refs[10:10 + n_cast]
    refs = refs[10 + n_cast:]
    if combine:
        xo_ref = refs.pop(0)
    p_ref = refs.pop(0)
    kc_ref = refs.pop(0)
    vc_ref = refs.pop(0)
    if emit_av:
        av_ref = refs.pop(0)
    cast_out = refs[:n_cast]
    raw = refs[n_cast]
    n_sub = 1
    sub = x_ref.shape[0] // n_sub

    def prologue(s):
        rows = slice(s * sub, (s + 1) * sub)
        x = x_ref[rows, :]
        if combine:
            half = D_MODEL // 2
            g = gate_ref[rows, :]
            g0 = g[:, 0:1]
            g1 = g[:, 1:2]
            w0 = y_ref[0, rows, :]
            w1 = y_ref[1, rows, :]
            xa = x[:, :half] + g0 * _unpack_hi(w0) + g1 * _unpack_hi(w1)
            xb = x[:, half:] + g0 * _unpack_lo(w0) + g1 * _unpack_lo(w1)
            xo_ref[rows, :half] = xa
            xo_ref[rows, half:] = xb
            x = jnp.concatenate([xa, xb], axis=1)
        rs = lax.rsqrt(jnp.mean(x * x, axis=-1, keepdims=True) + EPS)
        return rows, rs, (x * g1_ref[...]).astype(BF16)

    n_slots = PLA + 2

    def matmul(tile, slot):
        rows, _, h = tile
        if slot < PLA:
            raw[rows, slot * GRP:(slot + 1) * GRP] = jnp.dot(h, w_ref[:, slot * GRP:(slot + 1) * GRP],
                                                             preferred_element_type=F32)
        elif slot == PLA:
            raw[rows, PLA * GRP:(PLA + 1) * GRP] = jnp.dot(h, wcg_ref[...], preferred_element_type=F32)
        else:
            raw[rows, (PLA + 1) * GRP:] = jnp.dot(h, wdg_ref[...], preferred_element_type=F32)

    def epilogue(tile, slot):
        rows, rs, _ = tile

        def proj(g):
            return raw[rows, g * GRP:(g + 1) * GRP] * rs

        def put(g, val):
            p_ref[rows, g * GRP:(g + 1) * GRP] = val.astype(BF16)

        if slot == PK:
            r = proj(PK)
            put(PK, r * lax.rsqrt(_head_meansq(r, hsum_ref) + EPS) * gk_ref[...])
        elif slot == PV:
            put(PV, proj(PV))

            @pl.when(pl.program_id(0) % tiles_per_stream == tiles_per_stream - 1)
            def _():
                r = proj(PK)
                kc_ref[rows, :] = r * lax.rsqrt(_head_meansq(r, hsum_ref) + EPS) * gk_ref[...]
                vc_ref[rows, :] = proj(PV)
        elif slot == PQ:
            r = proj(PQ)
            put(PQ, r * lax.rsqrt(_head_meansq(r, hsum_ref) + EPS) * (gq_ref[...] * np.float32(HEAD_DIM ** -0.5)))
        elif slot == PAU:
            put(PAU, _gelu_tanh(proj(PAU)))
        elif slot == PAV:
            r = _gelu_tanh(proj(PAV))
            av = r * lax.rsqrt(jnp.mean(r * r, axis=-1, keepdims=True) + EPS) * gav_ref[...]
            put(PAV, av)
            if emit_av:
                av_ref[rows, :] = av
        elif slot == PGLU:
            pass
        elif slot == PDQ:
            put(PDQ, proj(PDQ) * np.float32(HEAD_DIM ** -0.5))
        elif slot in (PDK, PDV):
            put(slot, proj(slot))
        elif slot == PDR:
            r = proj(PDR)
            put(PDR, r * _sigmoid(r))
        elif slot == PLA:
            put(PGLU, proj(PGLU) * _sigmoid(proj(PLA)))
        else:
            dg = raw[rows, (PLA + 1) * GRP:] * rs
            z = jnp.dot(dg.astype(BF16), wg2_ref[...], preferred_element_type=F32) + bg_ref[...]
            logsig = jnp.minimum(z, 0.0) - jnp.log(1.0 + jnp.exp(-jnp.abs(z)))
            put(PLA, logsig * np.float32(1.0 / GLA_TAU))

    order = (PLA + 1, PK, PQ, PAV, PAU, PGLU, PLA, PDR, PV, PDQ, PDK, PDV)
    assert sorted(order) == list(range(n_slots))
    pairs = [(s, slot) for s in range(n_sub) for slot in order]
    lag = 2
    tiles = {0: prologue(0)}
    for i in range(len(pairs) + lag):
        if i < len(pairs):
            matmul(tiles[pairs[i][0]], pairs[i][1])
        if i == lag and n_sub > 1:
            tiles[1] = prologue(1)
        if i >= lag:
            epilogue(tiles[pairs[i - lag][0]], pairs[i - lag][1])
        if i % 3 == 2 and i // 3 < n_cast:
            cast_out[i // 3][...] = cast_in[i // 3][...].astype(BF16)


def _in_proj(x, lw, t, y=None, gates=None, emit_av=False, cast=()):
    n = x.shape[0]
    tm = min(TILE_ROWS, n)
    steps = n // tm
    combine = y is not None
    keep = min(B_WINDOW, t)
    tps = max(t // tm, 1)
    assert tps == 1 or keep == tm
    row = lambda i: (i, 0)
    const = lambda i: (0, 0)
    ins, specs = [x], [pl.BlockSpec((tm, D_MODEL), row)]
    if combine:
        ins += [y, gates]
        specs += [pl.BlockSpec((2, tm, D_MODEL // 2), lambda i: (0, i, 0)), pl.BlockSpec((tm, LANES), row)]
    consts = [lw["g1"], lw["w_in"], lw["w_cg"], lw["w_dg"], lw["wg2"], lw["bg"], lw["gq"], lw["gk"], lw["gav"], lw["hsum"]]
    ins += consts
    specs += [pl.BlockSpec(c.shape, const) for c in consts]
    cast_shapes, cast_specs = [], []
    layer = lw["layer"]
    for arr in cast:
        depth, rows, cols = arr.shape[0], arr.shape[1] * arr.shape[2], arr.shape[3]
        slab = rows // steps
        assert slab * steps == rows and slab % 16 == 0
        ins.append(arr.reshape(depth * rows, cols))
        specs.append(pl.BlockSpec((slab, cols), lambda i: (layer * steps + i, 0)))
        cast_shapes.append(jax.ShapeDtypeStruct((rows, cols), BF16))
        cast_specs.append(pl.BlockSpec((slab, cols), row))
    newest = jax.ShapeDtypeStruct((n // tps, GRP), F32)
    newest_spec = pl.BlockSpec((tm, GRP), lambda i: (i // tps, 0))
    out_shape = [jax.ShapeDtypeStruct((n, N_PROJ * GRP), BF16), newest, newest]
    out_specs = [pl.BlockSpec((tm, N_PROJ * GRP), row), newest_spec, newest_spec]
    if combine:
        out_shape = [jax.ShapeDtypeStruct((n, D_MODEL), F32)] + out_shape
        out_specs = [pl.BlockSpec((tm, D_MODEL), row)] + out_specs
    if emit_av:
        out_shape.append(jax.ShapeDtypeStruct((n, GRP), F32))
        out_specs.append(pl.BlockSpec((tm, GRP), row))
    out_shape += cast_shapes
    out_specs += cast_specs
    outs = list(pl.pallas_call(
        functools.partial(_in_kernel, combine=combine, emit_av=emit_av, tiles_per_stream=tps, n_cast=len(cast)),
        grid=(steps,), in_specs=specs, out_specs=out_specs, out_shape=out_shape,
        scratch_shapes=[pltpu.VMEM((tm, (PLA + 1) * GRP + LANES), F32)],
        compiler_params=_cparams(("arbitrary",)), name="in_proj",
    )(*ins))
    x_new = outs.pop(0) if combine else x
    proj, k_new, v_new = outs[0], outs[1], outs[2]
    a_v = outs[3] if emit_av else None
    casted = [o.reshape(a.shape[1:]) for o, a in zip(outs[3 + int(emit_av):], cast)]
    return x_new, proj, (k_new, v_new), a_v, casted


def _gmlp_chunks(u_ref, v_ref, ws_ref, bs_ref, o_ref, chunk):
    lane_h = _head_id((chunk, GRP), 1, HEAD_DIM)
    ri = lax.broadcasted_iota(I32, (chunk, chunk), 0)
    ci = lax.broadcasted_iota(I32, (chunk, chunk), 1)
    wm = [jnp.where(ci <= ri, ws_ref[h], 0.0).astype(BF16) for h in range(HEADS)]

    def one(c):
        rows = slice(c * chunk, (c + 1) * chunk)
        v = v_ref[rows, :]
        sv = jnp.dot(wm[HEADS - 1], v, preferred_element_type=F32)
        for h in range(HEADS - 2, -1, -1):
            sv = jnp.where(lane_h == h, jnp.dot(wm[h], v, preferred_element_type=F32), sv)
        o_ref[rows, :] = (u_ref[rows, :].astype(F32) * (sv + bs_ref[...])).astype(BF16)

    return one


def _attn_kernel(q_ref, kc_ref, vc_ref, kp_ref, vp_ref, bias_ref, o_ref, kbuf, vbuf, *, chunk, n_chunks, first_has_past):
    tq = chunk * n_chunks
    win = B_WINDOW + chunk
    kbuf[0:B_WINDOW, :] = kp_ref[...].astype(BF16)
    vbuf[0:B_WINDOW, :] = vp_ref[...].astype(BF16)
    kbuf[B_WINDOW:B_WINDOW + tq, :] = kc_ref[...]
    vbuf[B_WINDOW:B_WINDOW + tq, :] = vc_ref[...]
    col = lax.broadcasted_iota(I32, (HEADS * chunk, win), 1)

    def chunks(no_past):
        for c in range(n_chunks):
            q = q_ref[c * chunk:(c + 1) * chunk, :]
            kk = kbuf[c * chunk:c * chunk + win, :]
            vv = vbuf[c * chunk:c * chunk + win, :]
            s = lax.dot_general(_bd_stack(q, chunk), kk, (((1,), (1,)), ((), ())), preferred_element_type=F32)
            s = s + bias_ref[...]
            if no_past:
                s = jnp.where(col + c * chunk >= B_WINDOW, s, NEG_INF)
            m = jnp.max(s, axis=-1, keepdims=True)
            p = jnp.exp(s - m)
            l = jnp.sum(p, axis=-1, keepdims=True)
            o = jnp.dot(p.astype(BF16), vv, preferred_element_type=F32) * (1.0 / l)
            o_ref[c * chunk:(c + 1) * chunk, :] = _bd_unstack(o, chunk).astype(BF16)

    if first_has_past:
        chunks(False)
    else:
        pl.when(pl.program_id(1) == 0)(functools.partial(chunks, True))
        pl.when(pl.program_id(1) > 0)(functools.partial(chunks, False))


def _attention(proj, lw, b, t, cache_k=None, cache_v=None):
    n = proj.shape[0]
    step = cache_k is not None
    chunk = min(t, CHUNK)
    tq = min(t, WIDE_TILE_ROWS)
    nt = t // tq
    rel = lw["b_rel"]
    win = B_WINDOW + chunk
    lo = REL_CLIP - (chunk - 1)
    n_far = (chunk - 1) + win - (2 * REL_CLIP + 1 - lo)
    by_dist = jnp.concatenate([rel[:, lo:], jnp.broadcast_to(rel[:, -1:], (HEADS, n_far))], axis=1)
    by_key = by_dist[:, ::-1]
    n_k = chunk - 1 + win
    wrapped = jnp.tile(jnp.pad(by_key, ((0, 0), (0, 1))), (1, chunk))[:, :chunk * n_k].reshape(HEADS, chunk, n_k)
    bias = wrapped[:, :, chunk - 1:].astype(F32).reshape(HEADS * chunk, win)
    cur = lambda g: pl.BlockSpec((tq, GRP), lambda bi, j: (bi * nt + j, g))
    if step:
        prev_k = pl.BlockSpec((B_WINDOW, GRP), lambda bi, j: (bi, 0))
        prev_v = prev_k
        pk_arr, pv_arr = cache_k, cache_v
    else:
        per = tq // B_WINDOW
        assert per * B_WINDOW == tq
        past = lambda bi, j: jnp.maximum((bi * nt + j) * per - 1, 0)
        prev_k = pl.BlockSpec((B_WINDOW, GRP), lambda bi, j: (past(bi, j), PK))
        prev_v = pl.BlockSpec((B_WINDOW, GRP), lambda bi, j: (past(bi, j), PV))
        pk_arr, pv_arr = proj, proj
    return pl.pallas_call(
        functools.partial(_attn_kernel, chunk=chunk, n_chunks=tq // chunk, first_has_past=step),
        grid=(b, nt),
        in_specs=[cur(PQ), cur(PK), cur(PV), prev_k, prev_v, pl.BlockSpec(bias.shape, lambda bi, j: (0, 0))],
        out_specs=pl.BlockSpec((tq, GRP), lambda bi, j: (bi * nt + j, 0)),
        out_shape=jax.ShapeDtypeStruct((n, GRP), BF16),
        scratch_shapes=[pltpu.VMEM((B_WINDOW + tq, GRP), BF16), pltpu.VMEM((B_WINDOW + tq, GRP), BF16)],
        compiler_params=_cparams(("arbitrary", "arbitrary")), name="band_attn",
    )(proj, proj, proj, pk_arr, pv_arr, bias)


def _conv_blocks(g_ref, halo_ref, dw_ref, dwb_ref, lng_ref, lnb_ref, o_ref, tail_ref, xp, zbuf,
                 *, tc, sub, first_has_past):
    halo = halo_ref[...].astype(F32)
    has_past = jnp.logical_or(pl.program_id(1) > 0, first_has_past)
    xp[0:HALO, :] = jnp.where(has_past, halo, 0.0)
    xp[HALO:HALO + tc, :] = g_ref[...].astype(F32)
    xp[HALO + tc:, :] = jnp.zeros((xp.shape[0] - HALO - tc, GRP), F32)

    @pl.when(pl.program_id(1) == pl.num_programs(1) - 1)
    def _():
        tail_ref[...] = xp[tc:tc + HALO, :]

    lead = HALO - C_BUF
    sl = 8

    def one(s):
        acc = None
        for r in range(sl):
            taps = [p for p in range(r, lead + C_WIDTH, sl) if p >= lead]
            z = None
            for p in taps:
                a0 = s * sub + p - r
                term = dw_ref[p - lead:p - lead + 1, :] * xp[a0:a0 + sub + sl, :]
                z = term if z is None else z + term
            zbuf[r] = z
            part = zbuf[r, r:r + sub, :]
            acc = part if acc is None else acc + part
        y = acc + dwb_ref[...]
        mu = jnp.mean(y, axis=-1, keepdims=True)
        yc = y - mu
        y = yc * lax.rsqrt(jnp.mean(yc * yc, axis=-1, keepdims=True) + EPS) * lng_ref[...] + lnb_ref[...]
        o_ref[s * sub:(s + 1) * sub, :] = (y * _sigmoid(y)).astype(BF16)

    return one


def _gla_tables(L):
    i = np.arange(L)[:, None]
    t = np.arange(L)[None, :]
    masks = []
    s = GLA_SUB
    masks.append(((i // s) == (t // s)) & (t <= i))
    s *= 2
    while s <= L:
        h = s // 2
        masks.append(((i // s) == (t // s)) & (i % s >= h) & (t % s < h))
        s *= 2
    tri = (t <= i).astype(np.float32)
    mask = np.stack([np.tile(m.astype(np.float32), (1, HEADS)) for m in masks], axis=0)
    return tri, mask


def _gla_anchor(cum, row, L, size, first_half):
    out = None
    for start in range(0, L, size):
        ar = start + size // 2 - 1 if first_half else start - 1
        val = jnp.zeros((L, GRP), F32) if ar < 0 else jnp.broadcast_to(cum[ar:ar + 1, :], (L, GRP))
        out = val if out is None else jnp.where(row >= start, val, out)
    return out


def _gla_kernel(q_ref, k_ref, v_ref, la_ref, dr_ref, s0_ref, tri_ref, lmask_ref, bdmask_ref, hsum_ref, gon_ref,
                au_ref, av_ref, ws_ref, bs_ref, g_ref, halo_ref, dw_ref, dwb_ref, lng_ref, lnb_ref,
                o_ref, sf_ref, ya_ref, yc_ref, tail_ref, st, o_all, xp, zbuf,
                *, L, n_chunks, n_levels, first_has_state, a_chunk, c_sub):
    j = pl.program_id(1)
    td = L * n_chunks
    gmlp_chunk = _gmlp_chunks(au_ref, av_ref, ws_ref, bs_ref, ya_ref, a_chunk)
    conv_block = _conv_blocks(g_ref, halo_ref, dw_ref, dwb_ref, lng_ref, lnb_ref, yc_ref, tail_ref, xp, zbuf,
                              tc=td, sub=c_sub, first_has_past=first_has_state)
    gmlps = [functools.partial(gmlp_chunk, g) for g in range(td // a_chunk)]
    convs = [functools.partial(conv_block, s) for s in range(td // c_sub)]
    extras = sorted(gmlps + convs, key=lambda f: (f.args[0] + 0.5) / (len(gmlps) if f.func is gmlp_chunk else len(convs)))

    @pl.when(j == 0)
    def _():
        st[...] = jnp.zeros_like(st)
        if first_has_state:
            for h in range(HEADS):
                blk = slice(h * HEAD_DIM, (h + 1) * HEAD_DIM)
                st[blk, blk] = s0_ref[blk, :].T

    row = lax.broadcasted_iota(I32, (L, GRP), 0)
    dn_t = (((1,), (1,)), ((), ()))

    def prep(c):
        rows = slice(c * L, (c + 1) * L)
        q = q_ref[rows, :].astype(F32)
        k = k_ref[rows, :].astype(F32)
        v = v_ref[rows, :]
        cum = jnp.dot(tri_ref[...], la_ref[rows, :], preferred_element_type=F32)
        total = cum[L - 1:L, :]
        pairs = []
        for lvl in range(n_levels):
            size = GLA_SUB << lvl
            if lvl == 0:
                local = cum - _gla_anchor(cum, row, L, size, False)
                ql = q * jnp.exp(local)
                kl = k * jnp.exp(-local)
            else:
                upper = (row & (size - 1)) >= (size // 2)
                d = cum - _gla_anchor(cum, row, L, size, True)
                w = jnp.exp(jnp.where(upper, d, -d))
                ql = jnp.where(upper, q * w, 0.0)
                kl = jnp.where(upper, 0.0, k * w)
            pairs.append((ql.astype(BF16), _bd_stack(kl.astype(BF16), L)))
        return dict(rows=rows, v=v, qp=(q * jnp.exp(cum)).astype(BF16), kst=(k * jnp.exp(total - cum)).astype(BF16),
                    decay=jnp.exp(total), pairs=pairs)

    def intra(p):
        att = None
        for lvl, (ql, kbd) in enumerate(p["pairs"]):
            a = lax.dot_general(ql, kbd, dn_t, preferred_element_type=F32) * lmask_ref[lvl]
            att = a if att is None else att + a
        p["o_intra"] = jnp.dot(att.astype(BF16), _bd_stack(p["v"], L), preferred_element_type=F32)
        p["upd"] = lax.dot_general(p["v"], p["kst"], (((0,), (0,)), ((), ())),
                                   preferred_element_type=F32) * bdmask_ref[...]
        return p

    def finish(p, s_t):
        o_all[p["rows"], :] = lax.dot_general(p["qp"], s_t.astype(BF16), dn_t, preferred_element_type=F32) + p["o_intra"]
        return s_t * p["decay"] + p["upd"]

    s_t = st[...]
    stage1, stage2 = {}, {}
    for step in range(n_chunks + 2):
        if step < n_chunks:
            stage1[step] = prep(step)
        if 0 <= step - 1 < n_chunks:
            stage2[step - 1] = intra(stage1.pop(step - 1))
        if 0 <= step - 2 < n_chunks:
            s_t = finish(stage2.pop(step - 2), s_t)
        for e, extra in enumerate(extras):
            if e * (n_chunks + 2) // len(extras) == step:
                extra()
    st[...] = s_t
    o = o_all[...]
    y = o * lax.rsqrt(_head_meansq(o, hsum_ref) + EPS) * gon_ref[...] * dr_ref[...].astype(F32)
    o_ref[...] = y.astype(BF16)

    @pl.when(j == pl.num_programs(1) - 1)
    def _():
        for h in range(HEADS):
            blk = slice(h * HEAD_DIM, (h + 1) * HEAD_DIM)
            sf_ref[blk, :] = st[blk, blk].T


def _mixers_acd(proj, lw, b, t, s0=None, conv_state=None):
    n = proj.shape[0]
    step = s0 is not None
    L = min(t, 64)
    td = min(t, WIDE_TILE_ROWS)
    nt = t // td
    n_levels = int(np.log2(L // GLA_SUB)) + 1
    tri, lmask = _gla_tables(L)
    tri = jnp.asarray(tri, BF16)
    lmask = jnp.asarray(lmask, F32)
    if not step:
        s0 = jnp.zeros((GRP, HEAD_DIM), F32)
        s0_spec = pl.BlockSpec((GRP, HEAD_DIM), lambda bi, j: (0, 0))
    else:
        s0_spec = pl.BlockSpec((GRP, HEAD_DIM), lambda bi, j: (bi, 0))
    cur = lambda g: pl.BlockSpec((td, GRP), lambda bi, j: (bi * nt + j, g))
    c2 = lambda bi, j: (0, 0)
    a_chunk = min(t, A_CHUNK)
    ws = lw["a_ws"][:, :a_chunk, :a_chunk]
    bs = lw["a_bs_rows"][:a_chunk]
    c_sub = min(td, 64)
    if step:
        halo_arr = conv_state
        halo_spec = pl.BlockSpec((HALO, GRP), lambda bi, j: (bi, 0))
    else:
        per = td // HALO
        halo_arr = proj
        halo_spec = pl.BlockSpec((HALO, GRP), lambda bi, j: (jnp.maximum((bi * nt + j) * per - 1, 0), PGLU))
    vec = pl.BlockSpec((1, GRP), c2)
    act = jax.ShapeDtypeStruct((n, GRP), BF16)
    return pl.pallas_call(
        functools.partial(_gla_kernel, L=L, n_chunks=td // L, n_levels=n_levels, first_has_state=step,
                          a_chunk=a_chunk, c_sub=c_sub),
        grid=(b, nt),
        in_specs=[cur(PDQ), cur(PDK), cur(PDV), cur(PLA), cur(PDR), s0_spec,
                  pl.BlockSpec(tri.shape, c2), pl.BlockSpec(lmask.shape, lambda bi, j: (0, 0, 0)),
                  pl.BlockSpec((GRP, GRP), c2), pl.BlockSpec((GRP, GRP), c2), vec,
                  cur(PAU), cur(PAV), pl.BlockSpec(ws.shape, lambda bi, j: (0, 0, 0)), pl.BlockSpec(bs.shape, c2),
                  cur(PGLU), halo_spec, pl.BlockSpec((C_WIDTH, GRP), c2), vec, vec, vec],
        out_specs=[cur(0), pl.BlockSpec((GRP, HEAD_DIM), lambda bi, j: (bi, 0)), cur(0), cur(0),
                   pl.BlockSpec((HALO, GRP), lambda bi, j: (bi, 0))],
        out_shape=[act, jax.ShapeDtypeStruct((b * GRP, HEAD_DIM), F32), act, act,
                   jax.ShapeDtypeStruct((b * HALO, GRP), F32)],
        scratch_shapes=[pltpu.VMEM((GRP, GRP), F32), pltpu.VMEM((td, GRP), F32),
                        pltpu.VMEM((HALO + td + 8, GRP), F32), pltpu.VMEM((8, c_sub + 8, GRP), F32)],
        compiler_params=_cparams(("arbitrary", "arbitrary")), name="mixers_acd",
    )(proj, proj, proj, proj, proj, s0, tri, lmask, lw["bdmask"], lw["hsum"], lw["gon"], proj, proj, ws, bs,
      proj, halo_arr, lw["c_dw"], lw["c_dw_b"], lw["c_ln_g"], lw["c_ln_b"])


def _out_kernel(ya_ref, yb_ref, yc_ref, yd_ref, x_ref, wo_ref, g2_ref, wr_ref, br_ref, tri_ref,
                xo_ref, xn_ref, ri_ref, rf_ref, cnt_ref, *, tm, n_sub):
    @pl.when(pl.program_id(0) == 0)
    def _():
        cnt_ref[...] = jnp.zeros_like(cnt_ref)

    sub = tm // n_sub
    lane = lax.broadcasted_iota(I32, (sub, LANES), 1)
    lane_f = lane.astype(F32)
    lane_grp_f = (lane // PER_GROUP).astype(F32)
    is_grp = jnp.logical_and(lane >= N_EXPERTS, lane < N_EXPERTS + N_GROUPS)
    far = np.float32(1 << 20)

    def mix(s):
        rows = slice(s * sub, (s + 1) * sub)
        ycat = jnp.concatenate([ya_ref[rows, :], yb_ref[rows, :], yc_ref[rows, :], yd_ref[rows, :]], axis=1)
        x = x_ref[rows, :] + jnp.dot(ycat, wo_ref[...], preferred_element_type=F32)
        xo_ref[rows, :] = x
        return x

    def norm_logits(s, x):
        rows = slice(s * sub, (s + 1) * sub)
        xn = x * lax.rsqrt(jnp.mean(x * x, axis=-1, keepdims=True) + EPS) * g2_ref[...]
        xn_ref[rows, :] = _pack_halves(xn)
        both = jnp.dot(xn.astype(BF16), wr_ref[...], preferred_element_type=F32)
        return both[:, :LANES] + both[:, LANES:] + br_ref[...]

    def route(s, logits):
        rows = slice(s * sub, (s + 1) * sub)

        def first_max(masked):
            v = jnp.max(masked, axis=-1, keepdims=True)
            return v, jnp.min(jnp.where(masked == v, lane_f, far), axis=-1, keepdims=True)

        grp_logits = jnp.where(is_grp, logits, -jnp.inf)
        gmax, gidx = first_max(grp_logits)
        p_grp = 1.0 / jnp.sum(jnp.exp(grp_logits - gmax), axis=-1, keepdims=True)
        in_grp = lane_grp_f == gidx - np.float32(N_EXPERTS)
        exp_logits = jnp.where(in_grp, logits, -jnp.inf)
        v1, i1 = first_max(exp_logits)
        v2, i2 = first_max(jnp.where(lane_f == i1, -jnp.inf, exp_logits))
        e21 = jnp.exp(v2 - v1)
        gate1 = p_grp / (1.0 + e21)
        gate2 = p_grp * e21 / (1.0 + e21)

        oh1 = lane_f == i1
        oh2 = lane_f == i2
        ones = jnp.where(oh1, 1.0, jnp.where(oh2, 1.0, 0.0))
        before = jnp.dot(tri_ref[...], ones.astype(BF16), preferred_element_type=F32) + cnt_ref[...].astype(F32)
        rank1 = jnp.sum(jnp.where(oh1, before, 0.0), axis=-1, keepdims=True)
        rank2 = jnp.sum(jnp.where(oh2, before, 0.0), axis=-1, keepdims=True)
        cnt_ref[...] = cnt_ref[...] + jnp.sum(ones, axis=0, keepdims=True).astype(I32)

        ri = jnp.where(lane == 0, i1, jnp.where(lane == 1, i2, jnp.where(lane == 2, rank1,
                                                                          jnp.where(lane == 3, rank2, 0.0))))
        ri_ref[rows, :] = ri.astype(I32)
        rf_ref[rows, :] = jnp.where(lane == 0, gate1, jnp.where(lane == 1, gate2, 0.0))

    xs, lg = {}, {}
    for step in range(n_sub + 2):
        if step < n_sub:
            xs[step] = mix(step)
        if 0 <= step - 1 < n_sub:
            lg[step - 1] = norm_logits(step - 1, xs.pop(step - 1))
        if 0 <= step - 2 < n_sub:
            route(step - 2, lg.pop(step - 2))


def _out_proj(ya, yb, yc, yd, x, lw):
    n = x.shape[0]
    tm = min(WIDE_TILE_ROWS, n)
    row = lambda i: (i, 0)
    const = lambda i: (0, 0)
    n_sub = 4 if tm >= WIDE_TILE_ROWS else 1
    sub = tm // n_sub
    tri = jnp.asarray(np.tril(np.ones((sub, sub), np.float32), -1), BF16)
    consts = [lw["w_out"], lw["g2"], lw["wr"], lw["br"], tri]
    yspec = pl.BlockSpec((tm, GRP), row)
    return pl.pallas_call(
        functools.partial(_out_kernel, tm=tm, n_sub=n_sub),
        grid=(n // tm,),
        in_specs=[yspec, yspec, yspec, yspec, pl.BlockSpec((tm, D_MODEL), row)] + [pl.BlockSpec(c.shape, const) for c in consts],
        out_specs=[pl.BlockSpec((tm, D_MODEL), row), pl.BlockSpec((tm, D_MODEL // 2), row),
                   pl.BlockSpec((tm, LANES), row), pl.BlockSpec((tm, LANES), row), pl.BlockSpec((1, LANES), const)],
        out_shape=[jax.ShapeDtypeStruct((n, D_MODEL), F32), jax.ShapeDtypeStruct((n, D_MODEL // 2), U32),
                   jax.ShapeDtypeStruct((n, LANES), I32), jax.ShapeDtypeStruct((n, LANES), F32),
                   jax.ShapeDtypeStruct((1, LANES), I32)],
        compiler_params=_cparams(("arbitrary",)), name="out_proj_router",
    )(ya, yb, yc, yd, x, *consts)


def _sc_scatter_rows(x, idx, n_out):
    n, d = x.shape
    kk = idx.shape[0]
    per_w = n // SC_WORKERS
    win = min(SC_WIN, per_w)
    n_win = per_w // win
    assert n_win * win * SC_WORKERS == n
    mesh = plsc.VectorSubcoreMesh(core_axis_name="c", subcore_axis_name="s")

    @functools.partial(
        pl.kernel, mesh=mesh, out_type=jax.ShapeDtypeStruct((n_out, d), x.dtype),
        scratch_types=[pltpu.VMEM((kk, win), I32), pltpu.VMEM((win, d), x.dtype)],
        name="sc_scatter_rows")
    def k(x_hbm, idx_hbm, o_hbm, idx_v, rows_v):
        wid = lax.axis_index("s") * 2 + lax.axis_index("c")
        base = wid * per_w

        @pl.loop(0, n_win)
        def _(w):
            off = base + w * win
            pltpu.sync_copy(x_hbm.at[pl.ds(off, win)], rows_v)
            for j in range(kk):
                pltpu.sync_copy(idx_hbm.at[j, pl.ds(off, win)], idx_v.at[j])
                pltpu.sync_copy(rows_v, o_hbm.at[idx_v.at[j]])

    return k(x, idx)


def _sc_gather_rows(y, idx):
    _, d = y.shape
    kk, n = idx.shape
    per_w = n // SC_WORKERS
    win = min(SC_WIN, per_w)
    n_win = per_w // win
    assert n_win * win * SC_WORKERS == n
    mesh = plsc.VectorSubcoreMesh(core_axis_name="c", subcore_axis_name="s")

    @functools.partial(
        pl.kernel, mesh=mesh, out_type=jax.ShapeDtypeStruct((kk, n, d), y.dtype),
        scratch_types=[pltpu.VMEM((kk, win), I32), pltpu.VMEM((win, d), y.dtype)],
        name="sc_gather_rows")
    def k(y_hbm, idx_hbm, o_hbm, idx_v, rows_v):
        wid = lax.axis_index("s") * 2 + lax.axis_index("c")
        base = wid * per_w

        @pl.loop(0, n_win)
        def _(w):
            off = base + w * win
            for j in range(kk):
                pltpu.sync_copy(idx_hbm.at[j, pl.ds(off, win)], idx_v.at[j])
                pltpu.sync_copy(y_hbm.at[idx_v.at[j]], rows_v)
                pltpu.sync_copy(rows_v, o_hbm.at[j, pl.ds(off, win)])

    return k(y, idx)


def _moe_kernel(bexp_ref, nused_ref, x_ref, wg_ref, wu_ref, wd_ref, o_ref, *, n_sub):
    del bexp_ref

    @pl.when(pl.program_id(0) < nused_ref[0])
    def _():
        sub = x_ref.shape[0] // n_sub

        def up(s):
            w = x_ref[s * sub:(s + 1) * sub, :]
            x = jnp.concatenate([_unpack_hi(w).astype(BF16), _unpack_lo(w).astype(BF16)], axis=1)
            hg = jnp.dot(x, wg_ref[...], preferred_element_type=F32)
            hu = jnp.dot(x, wu_ref[...], preferred_element_type=F32)
            return (hg * _sigmoid(hg) * hu).astype(BF16)

        def down(s, h):
            o_ref[s * sub:(s + 1) * sub, :] = _pack_halves(jnp.dot(h, wd_ref[...], preferred_element_type=F32))

        h = up(0)
        for s in range(n_sub):
            nxt = up(s + 1) if s + 1 < n_sub else None
            down(s, h)
            h = nxt


def _moe_experts(xs, blk_exp, n_used, w_gate, w_up, w_down, bm):
    p = xs.shape[0]
    n_blocks = p // bm
    live = lambda i, be, nu: jnp.minimum(i, jnp.maximum(nu[0] - 1, 0))
    wspec = lambda shape: pl.BlockSpec((None,) + shape, lambda i, be, nu: (be[live(i, be, nu)], 0, 0))
    grid_spec = pltpu.PrefetchScalarGridSpec(
        num_scalar_prefetch=2, grid=(n_blocks,),
        in_specs=[pl.BlockSpec((bm, D_MODEL // 2), lambda i, be, nu: (live(i, be, nu), 0)),
                  wspec((D_MODEL, D_EXPERT)), wspec((D_MODEL, D_EXPERT)), wspec((D_EXPERT, D_MODEL))],
        out_specs=pl.BlockSpec((bm, D_MODEL // 2), lambda i, be, nu: (live(i, be, nu), 0)))
    return pl.pallas_call(
        functools.partial(_moe_kernel, n_sub=2 if bm >= MOE_BLOCK_ROWS else 1),
        grid_spec=grid_spec, out_shape=jax.ShapeDtypeStruct((p, D_MODEL // 2), U32),
        compiler_params=_cparams(("arbitrary",)), name="moe_experts",
    )(blk_exp, n_used, xs, w_gate, w_up, w_down)


def _moe_block_rows(n):
    return MOE_BLOCK_ROWS if 2 * n >= N_EXPERTS * 2 * MOE_BLOCK_ROWS else MOE_BLOCK_ROWS_SMALL


def _moe(xn_packed, route_i, counts, experts_bf16):
    n = xn_packed.shape[0]
    bm = _moe_block_rows(n)
    n_blocks = -(-(2 * n + N_EXPERTS * (bm - 1)) // bm)
    cnt = counts[0, :N_EXPERTS]
    padded = (cnt + bm - 1) // bm * bm
    pad_end = jnp.cumsum(padded)
    pad_start = pad_end - padded
    experts = jnp.arange(N_EXPERTS, dtype=I32)
    eid = route_i[:, 0:2].T
    start_of = jnp.sum(jnp.where(eid[:, :, None] == experts, pad_start, 0), axis=-1)
    dest = (start_of + route_i[:, 2:4].T).astype(I32)
    first_row = jnp.arange(n_blocks, dtype=I32) * bm
    blk_exp = jnp.minimum(jnp.sum((pad_end[None, :] <= first_row[:, None]).astype(I32), axis=1), N_EXPERTS - 1)
    n_used = (pad_end[-1:] // bm).astype(I32)
    xs = _sc_scatter_rows(xn_packed, dest, n_blocks * bm)
    ys = _moe_experts(xs, blk_exp, n_used, *experts_bf16, bm)
    return _sc_gather_rows(ys, dest)


def _combine_kernel(x_ref, y_ref, gate_ref, o_ref):
    half = D_MODEL // 2
    x = x_ref[...]
    g = gate_ref[...]
    g0 = g[:, 0:1]
    g1 = g[:, 1:2]
    w0 = y_ref[0]
    w1 = y_ref[1]
    o_ref[:, :half] = x[:, :half] + g0 * _unpack_hi(w0) + g1 * _unpack_hi(w1)
    o_ref[:, half:] = x[:, half:] + g0 * _unpack_lo(w0) + g1 * _unpack_lo(w1)


def _combine(x, y, gates):
    n = x.shape[0]
    tm = min(WIDE_TILE_ROWS, n)
    row = lambda i: (i, 0)
    return pl.pallas_call(
        _combine_kernel, grid=(n // tm,),
        in_specs=[pl.BlockSpec((tm, D_MODEL), row), pl.BlockSpec((2, tm, D_MODEL // 2), lambda i: (0, i, 0)),
                  pl.BlockSpec((tm, LANES), row)],
        out_specs=pl.BlockSpec((tm, D_MODEL), row), out_shape=jax.ShapeDtypeStruct((n, D_MODEL), F32),
        compiler_params=_cparams(("arbitrary",)), name="moe_combine",
    )(x, y, gates)


def _layer_weights(l, p):
    w_in = p["w_in"][l]
    cols = [w_in[:, i * GRP:(i + 1) * GRP] for i in range(11)]
    by_group = [None] * N_PROJ
    for ref_i, g in enumerate(_REF_GROUPS):
        if g is not None:
            by_group[g] = cols[ref_i]
    by_group[PGLU] = cols[5]
    w_dg = jnp.zeros((D_MODEL, LANES), F32).at[:, :GATE_RANK].set(w_in[:, 11 * GRP:])
    wg2 = jnp.zeros((LANES, GRP), F32).at[:GATE_RANK].set(p["d_wg2"][l])
    tile4 = lambda v: jnp.tile(v, HEADS)[None, :]
    hid = np.arange(GRP) // HEAD_DIM
    bd = (hid[:, None] == hid[None, :]).astype(np.float32)
    wr = jnp.zeros((D_MODEL, LANES), F32).at[:, :N_EXPERTS].set(p["r_expert_w"][l])
    wr = wr.at[:, N_EXPERTS:N_EXPERTS + N_GROUPS].set(p["r_group_w"][l])
    wr_hi = wr.astype(BF16)
    br = jnp.zeros((1, LANES), F32).at[0, :N_EXPERTS].set(p["r_expert_b"][l])
    br = br.at[0, N_EXPERTS:N_EXPERTS + N_GROUPS].set(p["r_group_b"][l])
    return {
        "g1": p["norm1_g"][l][None, :],
        "w_in": jnp.concatenate(by_group[:PLA], axis=1).astype(BF16),
        "w_cg": cols[6].astype(BF16),
        "w_dg": w_dg.astype(BF16),
        "wg2": wg2.astype(BF16),
        "bg": p["d_bg"][l][None, :],
        "gq": tile4(p["b_qnorm_g"][l]), "gk": tile4(p["b_knorm_g"][l]), "gav": p["a_vnorm_g"][l][None, :],
        "gon": tile4(p["d_onorm_g"][l]),
        "hsum": jnp.asarray(bd, BF16), "bdmask": jnp.asarray(bd, F32),
        "a_ws": p["a_ws"][l], "a_bs_rows": jnp.repeat(p["a_bs"][l].T, HEAD_DIM, axis=1),
        "b_rel": p["b_rel_bias"][l],
        "c_dw": p["c_dw"][l], "c_dw_b": p["c_dw_b"][l][None, :],
        "c_ln_g": p["c_ln_g"][l][None, :], "c_ln_b": p["c_ln_b"][l][None, :],
        "w_out": p["w_out"][l].astype(BF16),
        "g2": p["norm2_g"][l][None, :],
        "wr": jnp.concatenate([wr_hi, (wr - wr_hi.astype(F32)).astype(BF16)], axis=1), "br": br,
        "e_w_gate": p["e_w_gate"], "e_w_up": p["e_w_up"], "e_w_down": p["e_w_down"], "layer": l,
    }


def _mix_and_route(x, lw, b, t, pending, caches, experts_bf16):
    step = caches is not None
    y_prev, gates_prev = pending if pending is not None else (None, None)
    cast = () if experts_bf16 is not None else (lw["e_w_gate"], lw["e_w_up"], lw["e_w_down"])
    x, proj, kv, a_v, casted = _in_proj(x, lw, t, y_prev, gates_prev, emit_av=step, cast=cast)
    if experts_bf16 is None:
        experts_bf16 = casted
    if not step:
        yb = _attention(proj, lw, b, t)
        yd, sf, ya, yc, tail = _mixers_acd(proj, lw, b, t)
    else:
        ck, cv, cc, cs = caches
        yb = _attention(proj, lw, b, t, ck.reshape(b * B_WINDOW, GRP), cv.reshape(b * B_WINDOW, GRP))
        halo = jnp.pad(cc, ((0, 0), (HALO - C_BUF, 0), (0, 0))).reshape(b * HALO, GRP)
        yd, sf, ya, yc, tail = _mixers_acd(proj, lw, b, t, cs.reshape(b * GRP, HEAD_DIM), halo)
        a_v = a_v.reshape(b, t, GRP)
    x2, xn_packed, route_i, route_f, counts = _out_proj(ya, yb, yc, yd, x, lw)
    y = _moe(xn_packed, route_i, counts, experts_bf16)
    keep = min(B_WINDOW, t)
    new_k = kv[0].reshape(b, keep, HEADS, HEAD_DIM)
    new_v = kv[1].reshape(b, keep, HEADS, HEAD_DIM)
    new_buf = tail.reshape(b, HALO, GRP)[:, HALO - C_BUF:]
    states = (new_k, new_v, new_buf, sf.reshape(b, HEADS, HEAD_DIM, HEAD_DIM), a_v)
    return x2, (y, route_f), states, experts_bf16


def kernel(x_prompt, x_sample, cache_b_k, cache_b_v, state_c_conv, state_d_gla, norm1_g, w_in, a_vnorm_g, a_ws, a_bs, b_qnorm_g, b_knorm_g, b_rel_bias, c_dw, c_dw_b, c_ln_g, c_ln_b, d_wg2, d_bg, d_onorm_g, w_out, norm2_g, r_group_w, r_group_b, r_expert_w, r_expert_b, e_w_gate, e_w_up, e_w_down):
    params = dict(norm1_g=norm1_g, w_in=w_in, a_vnorm_g=a_vnorm_g, a_ws=a_ws, a_bs=a_bs, b_qnorm_g=b_qnorm_g,
                  b_knorm_g=b_knorm_g, b_rel_bias=b_rel_bias, c_dw=c_dw, c_dw_b=c_dw_b, c_ln_g=c_ln_g, c_ln_b=c_ln_b,
                  d_wg2=d_wg2, d_bg=d_bg, d_onorm_g=d_onorm_g, w_out=w_out, norm2_g=norm2_g, r_group_w=r_group_w,
                  r_group_b=r_group_b, r_expert_w=r_expert_w, r_expert_b=r_expert_b, e_w_gate=e_w_gate,
                  e_w_up=e_w_up, e_w_down=e_w_down)
    depth = w_in.shape[0]
    bp, tp, _ = x_prompt.shape
    bs, ts, _ = x_sample.shape
    xp = x_prompt.reshape(bp * tp, D_MODEL)
    xs = x_sample.reshape(bs * ts, D_MODEL)
    pend_p = pend_s = None
    st_p, st_s = [], []
    for l in range(depth):
        lw = _layer_weights(l, params)
        xp, pend_p, sp, experts = _mix_and_route(xp, lw, bp, tp, pend_p, None, None)
        xs, pend_s, ss, _ = _mix_and_route(xs, lw, bs, ts, pend_s,
                                           (cache_b_k[l], cache_b_v[l], state_c_conv[l], state_d_gla[l]), experts)
        st_p.append(sp)
        st_s.append(ss)
    yp = _combine(xp, pend_p[0], pend_p[1]).reshape(bp, tp, D_MODEL)
    ys = _combine(xs, pend_s[0], pend_s[1]).reshape(bs, ts, D_MODEL)
    stack = lambda sts, i: jnp.stack([s[i] for s in sts])
    return (yp, ys, stack(st_p, 0), stack(st_p, 1), stack(st_p, 2), stack(st_p, 3),
            stack(st_s, 0), stack(st_s, 1), stack(st_s, 2), stack(st_s, 3), stack(st_s, 4))
```

```python
import functools

import numpy as np
import jax
import jax.numpy as jnp
from jax import lax
from jax.experimental import pallas as pl
from jax.experimental.pallas import tpu as pltpu
from jax.experimental.pallas import tpu_sc as plsc

F32 = jnp.float32
BF16 = jnp.bfloat16
I32 = jnp.int32
U32 = jnp.uint32

D_MODEL = 1024
GRP = 256
HEADS = 4
HEAD_DIM = 64
CHUNK = 64
A_CHUNK = 128
B_WINDOW = 512
REL_CLIP = 128
C_WIDTH = 31
C_BUF = C_WIDTH - 1
HALO = 32
GATE_RANK = 16
GLA_TAU = 16.0
GLA_SUB = 16
N_GROUPS = 4
PER_GROUP = 8
N_EXPERTS = 32
D_EXPERT = 512
EPS = 1e-6
NEG_INF = -1e30
LANES = 128
VMEM_LIMIT = 48 * 1024 * 1024
TILE_ROWS = 512
WIDE_TILE_ROWS = 1024
MOE_BLOCK_ROWS = 512
MOE_BLOCK_ROWS_SMALL = 128

PK, PV, PQ, PAU, PAV, PGLU, PDQ, PDK, PDV, PDR, PLA = range(11)
N_PROJ = 11
_REF_GROUPS = (PAU, PAV, PQ, PK, PV, None, None, PDQ, PDK, PDV, PDR)

SC_WORKERS = 32
SC_WIN = 128


def _cparams(sem):
    return pltpu.CompilerParams(dimension_semantics=sem, vmem_limit_bytes=VMEM_LIMIT)


def _sigmoid(x):
    return 1.0 / (1.0 + jnp.exp(-x))


def _gelu_tanh(x):
    c = np.float32(np.sqrt(2.0 / np.pi))
    return 0.5 * x * (1.0 + jnp.tanh(c * (x + np.float32(0.044715) * (x * x * x))))


def _pack_halves(y):
    half = y.shape[1] // 2
    hi = pltpu.bitcast(y[:, :half].astype(BF16).astype(F32), U32)
    lo = pltpu.bitcast(y[:, half:].astype(BF16).astype(F32), U32)
    return hi | (lo >> np.uint32(16))


def _unpack_hi(w):
    return pltpu.bitcast(w & np.uint32(0xFFFF0000), F32)


def _unpack_lo(w):
    return pltpu.bitcast(w << np.uint32(16), F32)


def _head_id(shape, axis, size):
    return lax.broadcasted_iota(I32, shape, axis) // size


def _bd_stack(x, rows):
    x4 = jnp.concatenate([x] * HEADS, axis=0)
    shape = (HEADS * rows, GRP)
    keep = _head_id(shape, 0, rows) == _head_id(shape, 1, HEAD_DIM)
    return jnp.where(keep, x4, jnp.zeros_like(x4))


def _bd_unstack(o, rows):
    lane_h = _head_id((rows, GRP), 1, HEAD_DIM)
    out = o[(HEADS - 1) * rows:HEADS * rows]
    for h in range(HEADS - 2, -1, -1):
        out = jnp.where(lane_h == h, o[h * rows:(h + 1) * rows], out)
    return out


def _head_meansq(o, hsum_ref):
    sq = (o * o).astype(BF16)
    return jnp.dot(sq, hsum_ref[...], preferred_element_type=F32) * np.float32(1.0 / HEAD_DIM)


def _in_kernel(*refs, combine, emit_av, tiles_per_stream, n_cast):
    refs = list(refs)
    x_ref = refs.pop(0)
    if combine:
        y_ref = refs.pop(0)
        gate_ref = refs.pop(0)
    g1_ref, w_ref, wcg_ref, wdg_ref, wg2_ref, bg_ref, gq_ref, gk_ref, gav_ref, hsum_ref = refs[:10]
    cast_in = refs[10:10 + n_cast]
    refs = refs[10 + n_cast:]
    if combine:
        xo_ref = refs.pop(0)
    p_ref = refs.pop(0)
    kc_ref = refs.pop(0)
    vc_ref = refs.pop(0)
    if emit_av:
        av_ref = refs.pop(0)
    cast_out = refs[:n_cast]
    raw = refs[n_cast]
    n_sub = 1
    sub = x_ref.shape[0] // n_sub

    def prologue(s):
        rows = slice(s * sub, (s + 1) * sub)
        x = x_ref[rows, :]
        if combine:
            half = D_MODEL // 2
            g = gate_ref[rows, :]
            g0 = g[:, 0:1]
            g1 = g[:, 1:2]
            w0 = y_ref[0, rows, :]
            w1 = y_ref[1, rows, :]
            xa = x[:, :half] + g0 * _unpack_hi(w0) + g1 * _unpack_hi(w1)
            xb = x[:, half:] + g0 * _unpack_lo(w0) + g1 * _unpack_lo(w1)
            xo_ref[rows, :half] = xa
            xo_ref[rows, half:] = xb
            x = jnp.concatenate([xa, xb], axis=1)
        rs = lax.rsqrt(jnp.mean(x * x, axis=-1, keepdims=True) + EPS)
        return rows, rs, (x * g1_ref[...]).astype(BF16)

    n_slots = PLA + 2

    def matmul(tile, slot):
        rows, _, h = tile
        if slot < PLA:
            raw[rows, slot * GRP:(slot + 1) * GRP] = jnp.dot(h, w_ref[:, slot * GRP:(slot + 1) * GRP],
                                                             preferred_element_type=F32)
        elif slot == PLA:
            raw[rows, PLA * GRP:(PLA + 1) * GRP] = jnp.dot(h, wcg_ref[...], preferred_element_type=F32)
        else:
            raw[rows, (PLA + 1) * GRP:] = jnp.dot(h, wdg_ref[...], preferred_element_type=F32)

    def epilogue(tile, slot):
        rows, rs, _ = tile

        def proj(g):
            return raw[rows, g * GRP:(g + 1) * GRP] * rs

        def put(g, val):
            p_ref[rows, g * GRP:(g + 1) * GRP] = val.astype(BF16)

        if slot == PK:
            r = proj(PK)
            put(PK, r * lax.rsqrt(_head_meansq(r, hsum_ref) + EPS) * gk_ref[...])
        elif slot == PV:
            put(PV, proj(PV))

            @pl.when(pl.program_id(0) % tiles_per_stream == tiles_per_stream - 1)
            def _():
                r = proj(PK)
                kc_ref[rows, :] = r * lax.rsqrt(_head_meansq(r, hsum_ref) + EPS) * gk_ref[...]
                vc_ref[rows, :] = proj(PV)
        elif slot == PQ:
            r = proj(PQ)
            put(PQ, r * lax.rsqrt(_head_meansq(r, hsum_ref) + EPS) * (gq_ref[...] * np.float32(HEAD_DIM ** -0.5)))
        elif slot == PAU:
            put(PAU, _gelu_tanh(proj(PAU)))
        elif slot == PAV:
            r = _gelu_tanh(proj(PAV))
            av = r * lax.rsqrt(jnp.mean(r * r, axis=-1, keepdims=True) + EPS) * gav_ref[...]
            put(PAV, av)
            if emit_av:
                av_ref[rows, :] = av
        elif slot == PGLU:
            pass
        elif slot == PDQ:
            put(PDQ, proj(PDQ) * np.float32(HEAD_DIM ** -0.5))
        elif slot in (PDK, PDV):
            put(slot, proj(slot))
        elif slot == PDR:
            r = proj(PDR)
            put(PDR, r * _sigmoid(r))
        elif slot == PLA:
            put(PGLU, proj(PGLU) * _sigmoid(proj(PLA)))
        else:
            dg = raw[rows, (PLA + 1) * GRP:] * rs
            z = jnp.dot(dg.astype(BF16), wg2_ref[...], preferred_element_type=F32) + bg_ref[...]
            logsig = jnp.minimum(z, 0.0) - jnp.log(1.0 + jnp.exp(-jnp.abs(z)))
            put(PLA, logsig * np.float32(1.0 / GLA_TAU))

    order = (PLA + 1, PK, PQ, PAV, PAU, PGLU, PLA, PDR, PV, PDQ, PDK, PDV)
    assert sorted(order) == list(range(n_slots))
    pairs = [(s, slot) for s in range(n_sub) for slot in order]
    lag = 1
    tiles = {0: prologue(0)}
    for i in range(len(pairs) + lag):
        if i < len(pairs):
            matmul(tiles[pairs[i][0]], pairs[i][1])
        if i == lag and n_sub > 1:
            tiles[1] = prologue(1)
        if i >= lag:
            epilogue(tiles[pairs[i - lag][0]], pairs[i - lag][1])
        if i % 3 == 2 and i // 3 < n_cast:
            cast_out[i // 3][...] = cast_in[i // 3][...].astype(BF16)


def _in_proj(x, lw, t, y=None, gates=None, emit_av=False, cast=()):
    n = x.shape[0]
    tm = min(TILE_ROWS, n)
    steps = n // tm
    combine = y is not None
    keep = min(B_WINDOW, t)
    tps = max(t // tm, 1)
    assert tps == 1 or keep == tm
    row = lambda i: (i, 0)
    const = lambda i: (0, 0)
    ins, specs = [x], [pl.BlockSpec((tm, D_MODEL), row)]
    if combine:
        ins += [y, gates]
        specs += [pl.BlockSpec((2, tm, D_MODEL // 2), lambda i: (0, i, 0)), pl.BlockSpec((tm, LANES), row)]
    consts = [lw["g1"], lw["w_in"], lw["w_cg"], lw["w_dg"], lw["wg2"], lw["bg"], lw["gq"], lw["gk"], lw["gav"], lw["hsum"]]
    ins += consts
    specs += [pl.BlockSpec(c.shape, const) for c in consts]
    cast_shapes, cast_specs = [], []
    layer = lw["layer"]
    for arr in cast:
        depth, rows, cols = arr.shape[0], arr.shape[1] * arr.shape[2], arr.shape[3]
        slab = rows // steps
        assert slab * steps == rows and slab % 16 == 0
        ins.append(arr.reshape(depth * rows, cols))
        specs.append(pl.BlockSpec((slab, cols), lambda i: (layer * steps + i, 0)))
        cast_shapes.append(jax.ShapeDtypeStruct((rows, cols), BF16))
        cast_specs.append(pl.BlockSpec((slab, cols), row))
    newest = jax.ShapeDtypeStruct((n // tps, GRP), F32)
    newest_spec = pl.BlockSpec((tm, GRP), lambda i: (i // tps, 0))
    out_shape = [jax.ShapeDtypeStruct((n, N_PROJ * GRP), BF16), newest, newest]
    out_specs = [pl.BlockSpec((tm, N_PROJ * GRP), row), newest_spec, newest_spec]
    if combine:
        out_shape = [jax.ShapeDtypeStruct((n, D_MODEL), F32)] + out_shape
        out_specs = [pl.BlockSpec((tm, D_MODEL), row)] + out_specs
    if emit_av:
        out_shape.append(jax.ShapeDtypeStruct((n, GRP), F32))
        out_specs.append(pl.BlockSpec((tm, GRP), row))
    out_shape += cast_shapes
    out_specs += cast_specs
    outs = list(pl.pallas_call(
        functools.partial(_in_kernel, combine=combine, emit_av=emit_av, tiles_per_stream=tps, n_cast=len(cast)),
        grid=(steps,), in_specs=specs, out_specs=out_specs, out_shape=out_shape,
        scratch_shapes=[pltpu.VMEM((tm, (PLA + 1) * GRP + LANES), F32)],
        compiler_params=_cparams(("arbitrary",)), name="in_proj",
    )(*ins))
    x_new = outs.pop(0) if combine else x
    proj, k_new, v_new = outs[0], outs[1], outs[2]
    a_v = outs[3] if emit_av else None
    casted = [o.reshape(a.shape[1:]) for o, a in zip(outs[3 + int(emit_av):], cast)]
    return x_new, proj, (k_new, v_new), a_v, casted


def _gmlp_chunks(u_ref, v_ref, ws_ref, bs_ref, o_ref, chunk):
    lane_h = _head_id((chunk, GRP), 1, HEAD_DIM)
    ri = lax.broadcasted_iota(I32, (chunk, chunk), 0)
    ci = lax.broadcasted_iota(I32, (chunk, chunk), 1)
    wm = [jnp.where(ci <= ri, ws_ref[h], 0.0).astype(BF16) for h in range(HEADS)]

    def one(c):
        rows = slice(c * chunk, (c + 1) * chunk)
        v = v_ref[rows, :]
        sv = jnp.dot(wm[HEADS - 1], v, preferred_element_type=F32)
        for h in range(HEADS - 2, -1, -1):
            sv = jnp.where(lane_h == h, jnp.dot(wm[h], v, preferred_element_type=F32), sv)
        o_ref[rows, :] = (u_ref[rows, :].astype(F32) * (sv + bs_ref[...])).astype(BF16)

    return one


def _attn_kernel(q_ref, kc_ref, vc_ref, kp_ref, vp_ref, bias_ref, o_ref, kbuf, vbuf, *, chunk, n_chunks, first_has_past):
    tq = chunk * n_chunks
    win = B_WINDOW + chunk
    kbuf[0:B_WINDOW, :] = kp_ref[...].astype(BF16)
    vbuf[0:B_WINDOW, :] = vp_ref[...].astype(BF16)
    kbuf[B_WINDOW:B_WINDOW + tq, :] = kc_ref[...]
    vbuf[B_WINDOW:B_WINDOW + tq, :] = vc_ref[...]
    col = lax.broadcasted_iota(I32, (HEADS * chunk, win), 1)

    def chunks(no_past):
        for c in range(n_chunks):
            q = q_ref[c * chunk:(c + 1) * chunk, :]
            kk = kbuf[c * chunk:c * chunk + win, :]
            vv = vbuf[c * chunk:c * chunk + win, :]
            s = lax.dot_general(_bd_stack(q, chunk), kk, (((1,), (1,)), ((), ())), preferred_element_type=F32)
            s = s + bias_ref[...]
            if no_past:
                s = jnp.where(col + c * chunk >= B_WINDOW, s, NEG_INF)
            m = jnp.max(s, axis=-1, keepdims=True)
            p = jnp.exp(s - m)
            l = jnp.sum(p, axis=-1, keepdims=True)
            o = jnp.dot(p.astype(BF16), vv, preferred_element_type=F32) * (1.0 / l)
            o_ref[c * chunk:(c + 1) * chunk, :] = _bd_unstack(o, chunk).astype(BF16)

    if first_has_past:
        chunks(False)
    else:
        pl.when(pl.program_id(1) == 0)(functools.partial(chunks, True))
        pl.when(pl.program_id(1) > 0)(functools.partial(chunks, False))


def _attention(proj, lw, b, t, cache_k=None, cache_v=None):
    n = proj.shape[0]
    step = cache_k is not None
    chunk = min(t, CHUNK)
    tq = min(t, WIDE_TILE_ROWS)
    nt = t // tq
    rel = lw["b_rel"]
    win = B_WINDOW + chunk
    lo = REL_CLIP - (chunk - 1)
    n_far = (chunk - 1) + win - (2 * REL_CLIP + 1 - lo)
    by_dist = jnp.concatenate([rel[:, lo:], jnp.broadcast_to(rel[:, -1:], (HEADS, n_far))], axis=1)
    by_key = by_dist[:, ::-1]
    n_k = chunk - 1 + win
    wrapped = jnp.tile(jnp.pad(by_key, ((0, 0), (0, 1))), (1, chunk))[:, :chunk * n_k].reshape(HEADS, chunk, n_k)
    bias = wrapped[:, :, chunk - 1:].astype(F32).reshape(HEADS * chunk, win)
    cur = lambda g: pl.BlockSpec((tq, GRP), lambda bi, j: (bi * nt + j, g))
    if step:
        prev_k = pl.BlockSpec((B_WINDOW, GRP), lambda bi, j: (bi, 0))
        prev_v = prev_k
        pk_arr, pv_arr = cache_k, cache_v
    else:
        per = tq // B_WINDOW
        assert per * B_WINDOW == tq
        past = lambda bi, j: jnp.maximum((bi * nt + j) * per - 1, 0)
        prev_k = pl.BlockSpec((B_WINDOW, GRP), lambda bi, j: (past(bi, j), PK))
        prev_v = pl.BlockSpec((B_WINDOW, GRP), lambda bi, j: (past(bi, j), PV))
        pk_arr, pv_arr = proj, proj
    return pl.pallas_call(
        functools.partial(_attn_kernel, chunk=chunk, n_chunks=tq // chunk, first_has_past=step),
        grid=(b, nt),
        in_specs=[cur(PQ), cur(PK), cur(PV), prev_k, prev_v, pl.BlockSpec(bias.shape, lambda bi, j: (0, 0))],
        out_specs=pl.BlockSpec((tq, GRP), lambda bi, j: (bi * nt + j, 0)),
        out_shape=jax.ShapeDtypeStruct((n, GRP), BF16),
        scratch_shapes=[pltpu.VMEM((B_WINDOW + tq, GRP), BF16), pltpu.VMEM((B_WINDOW + tq, GRP), BF16)],
        compiler_params=_cparams(("arbitrary", "arbitrary")), name="band_attn",
    )(proj, proj, proj, pk_arr, pv_arr, bias)


def _conv_blocks(g_ref, halo_ref, dw_ref, dwb_ref, lng_ref, lnb_ref, o_ref, tail_ref, xp, zbuf,
                 *, tc, sub, first_has_past):
    halo = halo_ref[...].astype(F32)
    has_past = jnp.logical_or(pl.program_id(1) > 0, first_has_past)
    xp[0:HALO, :] = jnp.where(has_past, halo, 0.0)
    xp[HALO:HALO + tc, :] = g_ref[...].astype(F32)
    xp[HALO + tc:, :] = jnp.zeros((xp.shape[0] - HALO - tc, GRP), F32)

    @pl.when(pl.program_id(1) == pl.num_programs(1) - 1)
    def _():
        tail_ref[...] = xp[tc:tc + HALO, :]

    lead = HALO - C_BUF
    sl = 8

    def one(s):
        acc = None
        for r in range(sl):
            taps = [p for p in range(r, lead + C_WIDTH, sl) if p >= lead]
            z = None
            for p in taps:
                a0 = s * sub + p - r
                term = dw_ref[p - lead:p - lead + 1, :] * xp[a0:a0 + sub + sl, :]
                z = term if z is None else z + term
            zbuf[r] = z
            part = zbuf[r, r:r + sub, :]
            acc = part if acc is None else acc + part
        y = acc + dwb_ref[...]
        mu = jnp.mean(y, axis=-1, keepdims=True)
        yc = y - mu
        y = yc * lax.rsqrt(jnp.mean(yc * yc, axis=-1, keepdims=True) + EPS) * lng_ref[...] + lnb_ref[...]
        o_ref[s * sub:(s + 1) * sub, :] = (y * _sigmoid(y)).astype(BF16)

    return one


def _gla_tables(L):
    i = np.arange(L)[:, None]
    t = np.arange(L)[None, :]
    masks = []
    s = GLA_SUB
    masks.append(((i // s) == (t // s)) & (t <= i))
    s *= 2
    while s <= L:
        h = s // 2
        masks.append(((i // s) == (t // s)) & (i % s >= h) & (t % s < h))
        s *= 2
    tri = (t <= i).astype(np.float32)
    mask = np.stack([np.tile(m.astype(np.float32), (1, HEADS)) for m in masks], axis=0)
    return tri, mask


def _gla_anchor(cum, row, L, size, first_half):
    out = None
    for start in range(0, L, size):
        ar = start + size // 2 - 1 if first_half else start - 1
        val = jnp.zeros((L, GRP), F32) if ar < 0 else jnp.broadcast_to(cum[ar:ar + 1, :], (L, GRP))
        out = val if out is None else jnp.where(row >= start, val, out)
    return out


def _gla_kernel(q_ref, k_ref, v_ref, la_ref, dr_ref, s0_ref, tri_ref, lmask_ref, bdmask_ref, hsum_ref, gon_ref,
                au_ref, av_ref, ws_ref, bs_ref, g_ref, halo_ref, dw_ref, dwb_ref, lng_ref, lnb_ref,
                o_ref, sf_ref, ya_ref, yc_ref, tail_ref, st, o_all, xp, zbuf,
                *, L, n_chunks, n_levels, first_has_state, a_chunk, c_sub):
    j = pl.program_id(1)
    td = L * n_chunks
    gmlp_chunk = _gmlp_chunks(au_ref, av_ref, ws_ref, bs_ref, ya_ref, a_chunk)
    conv_block = _conv_blocks(g_ref, halo_ref, dw_ref, dwb_ref, lng_ref, lnb_ref, yc_ref, tail_ref, xp, zbuf,
                              tc=td, sub=c_sub, first_has_past=first_has_state)
    gmlps = [functools.partial(gmlp_chunk, g) for g in range(td // a_chunk)]
    convs = [functools.partial(conv_block, s) for s in range(td // c_sub)]
    extras = sorted(gmlps + convs, key=lambda f: (f.args[0] + 0.5) / (len(gmlps) if f.func is gmlp_chunk else len(convs)))

    @pl.when(j == 0)
    def _():
        st[...] = jnp.zeros_like(st)
        if first_has_state:
            for h in range(HEADS):
                blk = slice(h * HEAD_DIM, (h + 1) * HEAD_DIM)
                st[blk, blk] = s0_ref[blk, :].T

    row = lax.broadcasted_iota(I32, (L, GRP), 0)
    dn_t = (((1,), (1,)), ((), ()))

    def prep(c):
        rows = slice(c * L, (c + 1) * L)
        q = q_ref[rows, :].astype(F32)
        k = k_ref[rows, :].astype(F32)
        v = v_ref[rows, :]
        cum = jnp.dot(tri_ref[...], la_ref[rows, :], preferred_element_type=F32)
        total = cum[L - 1:L, :]
        pairs = []
        for lvl in range(n_levels):
            size = GLA_SUB << lvl
            if lvl == 0:
                local = cum - _gla_anchor(cum, row, L, size, False)
                ql = q * jnp.exp(local)
                kl = k * jnp.exp(-local)
            else:
                upper = (row & (size - 1)) >= (size // 2)
                d = cum - _gla_anchor(cum, row, L, size, True)
                w = jnp.exp(jnp.where(upper, d, -d))
                ql = jnp.where(upper, q * w, 0.0)
                kl = jnp.where(upper, 0.0, k * w)
            pairs.append((ql.astype(BF16), _bd_stack(kl.astype(BF16), L)))
        return dict(rows=rows, v=v, qp=(q * jnp.exp(cum)).astype(BF16), kst=(k * jnp.exp(total - cum)).astype(BF16),
                    decay=jnp.exp(total), pairs=pairs)

    def intra(p):
        att = None
        for lvl, (ql, kbd) in enumerate(p["pairs"]):
            a = lax.dot_general(ql, kbd, dn_t, preferred_element_type=F32) * lmask_ref[lvl]
            att = a if att is None else att + a
        p["o_intra"] = jnp.dot(att.astype(BF16), _bd_stack(p["v"], L), preferred_element_type=F32)
        p["upd"] = lax.dot_general(p["v"], p["kst"], (((0,), (0,)), ((), ())),
                                   preferred_element_type=F32) * bdmask_ref[...]
        return p

    def finish(p, s_t):
        o_all[p["rows"], :] = lax.dot_general(p["qp"], s_t.astype(BF16), dn_t, preferred_element_type=F32) + p["o_intra"]
        return s_t * p["decay"] + p["upd"]

    s_t = st[...]
    stage1, stage2 = {}, {}
    for step in range(n_chunks + 2):
        if step < n_chunks:
            stage1[step] = prep(step)
        if 0 <= step - 1 < n_chunks:
            stage2[step - 1] = intra(stage1.pop(step - 1))
        if 0 <= step - 2 < n_chunks:
            s_t = finish(stage2.pop(step - 2), s_t)
        for e, extra in enumerate(extras):
            if e * (n_chunks + 2) // len(extras) == step:
                extra()
    st[...] = s_t
    o = o_all[...]
    y = o * lax.rsqrt(_head_meansq(o, hsum_ref) + EPS) * gon_ref[...] * dr_ref[...].astype(F32)
    o_ref[...] = y.astype(BF16)

    @pl.when(j == pl.num_programs(1) - 1)
    def _():
        for h in range(HEADS):
            blk = slice(h * HEAD_DIM, (h + 1) * HEAD_DIM)
            sf_ref[blk, :] = st[blk, blk].T


def _mixers_acd(proj, lw, b, t, s0=None, conv_state=None):
    n = proj.shape[0]
    step = s0 is not None
    L = min(t, 64)
    td = min(t, WIDE_TILE_ROWS)
    nt = t // td
    n_levels = int(np.log2(L // GLA_SUB)) + 1
    tri, lmask = _gla_tables(L)
    tri = jnp.asarray(tri, BF16)
    lmask = jnp.asarray(lmask, F32)
    if not step:
        s0 = jnp.zeros((GRP, HEAD_DIM), F32)
        s0_spec = pl.BlockSpec((GRP, HEAD_DIM), lambda bi, j: (0, 0))
    else:
        s0_spec = pl.BlockSpec((GRP, HEAD_DIM), lambda bi, j: (bi, 0))
    cur = lambda g: pl.BlockSpec((td, GRP), lambda bi, j: (bi * nt + j, g))
    c2 = lambda bi, j: (0, 0)
    a_chunk = min(t, A_CHUNK)
    ws = lw["a_ws"][:, :a_chunk, :a_chunk]
    bs = lw["a_bs_rows"][:a_chunk]
    c_sub = min(td, 128)
    if step:
        halo_arr = conv_state
        halo_spec = pl.BlockSpec((HALO, GRP), lambda bi, j: (bi, 0))
    else:
        per = td // HALO
        halo_arr = proj
        halo_spec = pl.BlockSpec((HALO, GRP), lambda bi, j: (jnp.maximum((bi * nt + j) * per - 1, 0), PGLU))
    vec = pl.BlockSpec((1, GRP), c2)
    act = jax.ShapeDtypeStruct((n, GRP), BF16)
    return pl.pallas_call(
        functools.partial(_gla_kernel, L=L, n_chunks=td // L, n_levels=n_levels, first_has_state=step,
                          a_chunk=a_chunk, c_sub=c_sub),
        grid=(b, nt),
        in_specs=[cur(PDQ), cur(PDK), cur(PDV), cur(PLA), cur(PDR), s0_spec,
                  pl.BlockSpec(tri.shape, c2), pl.BlockSpec(lmask.shape, lambda bi, j: (0, 0, 0)),
                  pl.BlockSpec((GRP, GRP), c2), pl.BlockSpec((GRP, GRP), c2), vec,
                  cur(PAU), cur(PAV), pl.BlockSpec(ws.shape, lambda bi, j: (0, 0, 0)), pl.BlockSpec(bs.shape, c2),
                  cur(PGLU), halo_spec, pl.BlockSpec((C_WIDTH, GRP), c2), vec, vec, vec],
        out_specs=[cur(0), pl.BlockSpec((GRP, HEAD_DIM), lambda bi, j: (bi, 0)), cur(0), cur(0),
                   pl.BlockSpec((HALO, GRP), lambda bi, j: (bi, 0))],
        out_shape=[act, jax.ShapeDtypeStruct((b * GRP, HEAD_DIM), F32), act, act,
                   jax.ShapeDtypeStruct((b * HALO, GRP), F32)],
        scratch_shapes=[pltpu.VMEM((GRP, GRP), F32), pltpu.VMEM((td, GRP), F32),
                        pltpu.VMEM((HALO + td + 8, GRP), F32), pltpu.VMEM((8, c_sub + 8, GRP), F32)],
        compiler_params=_cparams(("arbitrary", "arbitrary")), name="mixers_acd",
    )(proj, proj, proj, proj, proj, s0, tri, lmask, lw["bdmask"], lw["hsum"], lw["gon"], proj, proj, ws, bs,
      proj, halo_arr, lw["c_dw"], lw["c_dw_b"], lw["c_ln_g"], lw["c_ln_b"])


def _out_kernel(ya_ref, yb_ref, yc_ref, yd_ref, x_ref, wo_ref, g2_ref, wr_ref, br_ref, tri_ref,
                xo_ref, xn_ref, ri_ref, rf_ref, cnt_ref, *, tm, n_sub):
    @pl.when(pl.program_id(0) == 0)
    def _():
        cnt_ref[...] = jnp.zeros_like(cnt_ref)

    sub = tm // n_sub
    lane = lax.broadcasted_iota(I32, (sub, LANES), 1)
    lane_f = lane.astype(F32)
    lane_grp_f = (lane // PER_GROUP).astype(F32)
    is_grp = jnp.logical_and(lane >= N_EXPERTS, lane < N_EXPERTS + N_GROUPS)
    far = np.float32(1 << 20)

    def mix(s):
        rows = slice(s * sub, (s + 1) * sub)
        ycat = jnp.concatenate([ya_ref[rows, :], yb_ref[rows, :], yc_ref[rows, :], yd_ref[rows, :]], axis=1)
        x = x_ref[rows, :] + jnp.dot(ycat, wo_ref[...], preferred_element_type=F32)
        xo_ref[rows, :] = x
        return x

    def norm_logits(s, x):
        rows = slice(s * sub, (s + 1) * sub)
        xn = x * lax.rsqrt(jnp.mean(x * x, axis=-1, keepdims=True) + EPS) * g2_ref[...]
        xn_ref[rows, :] = _pack_halves(xn)
        both = jnp.dot(xn.astype(BF16), wr_ref[...], preferred_element_type=F32)
        return both[:, :LANES] + both[:, LANES:] + br_ref[...]

    def route(s, logits):
        rows = slice(s * sub, (s + 1) * sub)

        def first_max(masked):
            v = jnp.max(masked, axis=-1, keepdims=True)
            return v, jnp.min(jnp.where(masked == v, lane_f, far), axis=-1, keepdims=True)

        grp_logits = jnp.where(is_grp, logits, -jnp.inf)
        gmax, gidx = first_max(grp_logits)
        p_grp = 1.0 / jnp.sum(jnp.exp(grp_logits - gmax), axis=-1, keepdims=True)
        in_grp = lane_grp_f == gidx - np.float32(N_EXPERTS)
        exp_logits = jnp.where(in_grp, logits, -jnp.inf)
        v1, i1 = first_max(exp_logits)
        v2, i2 = first_max(jnp.where(lane_f == i1, -jnp.inf, exp_logits))
        e21 = jnp.exp(v2 - v1)
        gate1 = p_grp / (1.0 + e21)
        gate2 = p_grp * e21 / (1.0 + e21)

        oh1 = lane_f == i1
        oh2 = lane_f == i2
        ones = jnp.where(oh1, 1.0, jnp.where(oh2, 1.0, 0.0))
        before = jnp.dot(tri_ref[...], ones.astype(BF16), preferred_element_type=F32) + cnt_ref[...].astype(F32)
        rank1 = jnp.sum(jnp.where(oh1, before, 0.0), axis=-1, keepdims=True)
        rank2 = jnp.sum(jnp.where(oh2, before, 0.0), axis=-1, keepdims=True)
        cnt_ref[...] = cnt_ref[...] + jnp.sum(ones, axis=0, keepdims=True).astype(I32)

        ri = jnp.where(lane == 0, i1, jnp.where(lane == 1, i2, jnp.where(lane == 2, rank1,
                                                                          jnp.where(lane == 3, rank2, 0.0))))
        ri_ref[rows, :] = ri.astype(I32)
        rf_ref[rows, :] = jnp.where(lane == 0, gate1, jnp.where(lane == 1, gate2, 0.0))

    xs, lg = {}, {}
    for step in range(n_sub + 2):
        if step < n_sub:
            xs[step] = mix(step)
        if 0 <= step - 1 < n_sub:
            lg[step - 1] = norm_logits(step - 1, xs.pop(step - 1))
        if 0 <= step - 2 < n_sub:
            route(step - 2, lg.pop(step - 2))


def _out_proj(ya, yb, yc, yd, x, lw):
    n = x.shape[0]
    tm = min(WIDE_TILE_ROWS, n)
    row = lambda i: (i, 0)
    const = lambda i: (0, 0)
    n_sub = 4 if tm >= WIDE_TILE_ROWS else 1
    sub = tm // n_sub
    tri = jnp.asarray(np.tril(np.ones((sub, sub), np.float32), -1), BF16)
    consts = [lw["w_out"], lw["g2"], lw["wr"], lw["br"], tri]
    yspec = pl.BlockSpec((tm, GRP), row)
    return pl.pallas_call(
        functools.partial(_out_kernel, tm=tm, n_sub=n_sub),
        grid=(n // tm,),
        in_specs=[yspec, yspec, yspec, yspec, pl.BlockSpec((tm, D_MODEL), row)] + [pl.BlockSpec(c.shape, const) for c in consts],
        out_specs=[pl.BlockSpec((tm, D_MODEL), row), pl.BlockSpec((tm, D_MODEL // 2), row),
                   pl.BlockSpec((tm, LANES), row), pl.BlockSpec((tm, LANES), row), pl.BlockSpec((1, LANES), const)],
        out_shape=[jax.ShapeDtypeStruct((n, D_MODEL), F32), jax.ShapeDtypeStruct((n, D_MODEL // 2), U32),
                   jax.ShapeDtypeStruct((n, LANES), I32), jax.ShapeDtypeStruct((n, LANES), F32),
                   jax.ShapeDtypeStruct((1, LANES), I32)],
        compiler_params=_cparams(("arbitrary",)), name="out_proj_router",
    )(ya, yb, yc, yd, x, *consts)


def _sc_scatter_rows(x, idx, n_out):
    n, d = x.shape
    kk = idx.shape[0]
    per_w = n // SC_WORKERS
    win = min(SC_WIN, per_w)
    n_win = per_w // win
    assert n_win * win * SC_WORKERS == n
    mesh = plsc.VectorSubcoreMesh(core_axis_name="c", subcore_axis_name="s")

    @functools.partial(
        pl.kernel, mesh=mesh, out_type=jax.ShapeDtypeStruct((n_out, d), x.dtype),
        scratch_types=[pltpu.VMEM((kk, win), I32), pltpu.VMEM((win, d), x.dtype)],
        name="sc_scatter_rows")
    def k(x_hbm, idx_hbm, o_hbm, idx_v, rows_v):
        wid = lax.axis_index("s") * 2 + lax.axis_index("c")
        base = wid * per_w

        @pl.loop(0, n_win)
        def _(w):
            off = base + w * win
            pltpu.sync_copy(x_hbm.at[pl.ds(off, win)], rows_v)
            for j in range(kk):
                pltpu.sync_copy(idx_hbm.at[j, pl.ds(off, win)], idx_v.at[j])
                pltpu.sync_copy(rows_v, o_hbm.at[idx_v.at[j]])

    return k(x, idx)


def _sc_gather_rows(y, idx):
    _, d = y.shape
    kk, n = idx.shape
    per_w = n // SC_WORKERS
    win = min(SC_WIN, per_w)
    n_win = per_w // win
    assert n_win * win * SC_WORKERS == n
    mesh = plsc.VectorSubcoreMesh(core_axis_name="c", subcore_axis_name="s")

    @functools.partial(
        pl.kernel, mesh=mesh, out_type=jax.ShapeDtypeStruct((kk, n, d), y.dtype),
        scratch_types=[pltpu.VMEM((kk, win), I32), pltpu.VMEM((win, d), y.dtype)],
        name="sc_gather_rows")
    def k(y_hbm, idx_hbm, o_hbm, idx_v, rows_v):
        wid = lax.axis_index("s") * 2 + lax.axis_index("c")
        base = wid * per_w

        @pl.loop(0, n_win)
        def _(w):
            off = base + w * win
            for j in range(kk):
                pltpu.sync_copy(idx_hbm.at[j, pl.ds(off, win)], idx_v.at[j])
                pltpu.sync_copy(y_hbm.at[idx_v.at[j]], rows_v)
                pltpu.sync_copy(rows_v, o_hbm.at[j, pl.ds(off, win)])

    return k(y, idx)


def _moe_kernel(bexp_ref, nused_ref, x_ref, wg_ref, wu_ref, wd_ref, o_ref, *, n_sub):
    del bexp_ref

    @pl.when(pl.program_id(0) < nused_ref[0])
    def _():
        sub = x_ref.shape[0] // n_sub

        def up(s):
            w = x_ref[s * sub:(s + 1) * sub, :]
            x = jnp.concatenate([_unpack_hi(w).astype(BF16), _unpack_lo(w).astype(BF16)], axis=1)
            hg = jnp.dot(x, wg_ref[...], preferred_element_type=F32)
            hu = jnp.dot(x, wu_ref[...], preferred_element_type=F32)
            return (hg * _sigmoid(hg) * hu).astype(BF16)

        def down(s, h):
            o_ref[s * sub:(s + 1) * sub, :] = _pack_halves(jnp.dot(h, wd_ref[...], preferred_element_type=F32))

        h = up(0)
        for s in range(n_sub):
            nxt = up(s + 1) if s + 1 < n_sub else None
            down(s, h)
            h = nxt


def _moe_experts(xs, blk_exp, n_used, w_gate, w_up, w_down, bm):
    p = xs.shape[0]
    n_blocks = p // bm
    live = lambda i, be, nu: jnp.minimum(i, jnp.maximum(nu[0] - 1, 0))
    wspec = lambda shape: pl.BlockSpec((None,) + shape, lambda i, be, nu: (be[live(i, be, nu)], 0, 0))
    grid_spec = pltpu.PrefetchScalarGridSpec(
        num_scalar_prefetch=2, grid=(n_blocks,),
        in_specs=[pl.BlockSpec((bm, D_MODEL // 2), lambda i, be, nu: (live(i, be, nu), 0)),
                  wspec((D_MODEL, D_EXPERT)), wspec((D_MODEL, D_EXPERT)), wspec((D_EXPERT, D_MODEL))],
        out_specs=pl.BlockSpec((bm, D_MODEL // 2), lambda i, be, nu: (live(i, be, nu), 0)))
    return pl.pallas_call(
        functools.partial(_moe_kernel, n_sub=2 if bm >= MOE_BLOCK_ROWS else 1),
        grid_spec=grid_spec, out_shape=jax.ShapeDtypeStruct((p, D_MODEL // 2), U32),
        compiler_params=_cparams(("arbitrary",)), name="moe_experts",
    )(blk_exp, n_used, xs, w_gate, w_up, w_down)


def _moe_block_rows(n):
    return MOE_BLOCK_ROWS if 2 * n >= N_EXPERTS * 2 * MOE_BLOCK_ROWS else MOE_BLOCK_ROWS_SMALL


def _moe(xn_packed, route_i, counts, experts_bf16):
    n = xn_packed.shape[0]
    bm = _moe_block_rows(n)
    n_blocks = -(-(2 * n + N_EXPERTS * (bm - 1)) // bm)
    cnt = counts[0, :N_EXPERTS]
    padded = (cnt + bm - 1) // bm * bm
    pad_end = jnp.cumsum(padded)
    pad_start = pad_end - padded
    experts = jnp.arange(N_EXPERTS, dtype=I32)
    eid = route_i[:, 0:2].T
    start_of = jnp.sum(jnp.where(eid[:, :, None] == experts, pad_start, 0), axis=-1)
    dest = (start_of + route_i[:, 2:4].T).astype(I32)
    first_row = jnp.arange(n_blocks, dtype=I32) * bm
    blk_exp = jnp.minimum(jnp.sum((pad_end[None, :] <= first_row[:, None]).astype(I32), axis=1), N_EXPERTS - 1)
    n_used = (pad_end[-1:] // bm).astype(I32)
    xs = _sc_scatter_rows(xn_packed, dest, n_blocks * bm)
    ys = _moe_experts(xs, blk_exp, n_used, *experts_bf16, bm)
    return _sc_gather_rows(ys, dest)


def _combine_kernel(x_ref, y_ref, gate_ref, o_ref):
    half = D_MODEL // 2
    x = x_ref[...]
    g = gate_ref[...]
    g0 = g[:, 0:1]
    g1 = g[:, 1:2]
    w0 = y_ref[0]
    w1 = y_ref[1]
    o_ref[:, :half] = x[:, :half] + g0 * _unpack_hi(w0) + g1 * _unpack_hi(w1)
    o_ref[:, half:] = x[:, half:] + g0 * _unpack_lo(w0) + g1 * _unpack_lo(w1)


def _combine(x, y, gates):
    n = x.shape[0]
    tm = min(WIDE_TILE_ROWS, n)
    row = lambda i: (i, 0)
    return pl.pallas_call(
        _combine_kernel, grid=(n // tm,),
        in_specs=[pl.BlockSpec((tm, D_MODEL), row), pl.BlockSpec((2, tm, D_MODEL // 2), lambda i: (0, i, 0)),
                  pl.BlockSpec((tm, LANES), row)],
        out_specs=pl.BlockSpec((tm, D_MODEL), row), out_shape=jax.ShapeDtypeStruct((n, D_MODEL), F32),
        compiler_params=_cparams(("arbitrary",)), name="moe_combine",
    )(x, y, gates)


def _layer_weights(l, p):
    w_in = p["w_in"][l]
    cols = [w_in[:, i * GRP:(i + 1) * GRP] for i in range(11)]
    by_group = [None] * N_PROJ
    for ref_i, g in enumerate(_REF_GROUPS):
        if g is not None:
            by_group[g] = cols[ref_i]
    by_group[PGLU] = cols[5]
    w_dg = jnp.zeros((D_MODEL, LANES), F32).at[:, :GATE_RANK].set(w_in[:, 11 * GRP:])
    wg2 = jnp.zeros((LANES, GRP), F32).at[:GATE_RANK].set(p["d_wg2"][l])
    tile4 = lambda v: jnp.tile(v, HEADS)[None, :]
    hid = np.arange(GRP) // HEAD_DIM
    bd = (hid[:, None] == hid[None, :]).astype(np.float32)
    wr = jnp.zeros((D_MODEL, LANES), F32).at[:, :N_EXPERTS].set(p["r_expert_w"][l])
    wr = wr.at[:, N_EXPERTS:N_EXPERTS + N_GROUPS].set(p["r_group_w"][l])
    wr_hi = wr.astype(BF16)
    br = jnp.zeros((1, LANES), F32).at[0, :N_EXPERTS].set(p["r_expert_b"][l])
    br = br.at[0, N_EXPERTS:N_EXPERTS + N_GROUPS].set(p["r_group_b"][l])
    return {
        "g1": p["norm1_g"][l][None, :],
        "w_in": jnp.concatenate(by_group[:PLA], axis=1).astype(BF16),
        "w_cg": cols[6].astype(BF16),
        "w_dg": w_dg.astype(BF16),
        "wg2": wg2.astype(BF16),
        "bg": p["d_bg"][l][None, :],
        "gq": tile4(p["b_qnorm_g"][l]), "gk": tile4(p["b_knorm_g"][l]), "gav": p["a_vnorm_g"][l][None, :],
        "gon": tile4(p["d_onorm_g"][l]),
        "hsum": jnp.asarray(bd, BF16), "bdmask": jnp.asarray(bd, F32),
        "a_ws": p["a_ws"][l], "a_bs_rows": jnp.repeat(p["a_bs"][l].T, HEAD_DIM, axis=1),
        "b_rel": p["b_rel_bias"][l],
        "c_dw": p["c_dw"][l], "c_dw_b": p["c_dw_b"][l][None, :],
        "c_ln_g": p["c_ln_g"][l][None, :], "c_ln_b": p["c_ln_b"][l][None, :],
        "w_out": p["w_out"][l].astype(BF16),
        "g2": p["norm2_g"][l][None, :],
        "wr": jnp.concatenate([wr_hi, (wr - wr_hi.astype(F32)).astype(BF16)], axis=1), "br": br,
        "e_w_gate": p["e_w_gate"], "e_w_up": p["e_w_up"], "e_w_down": p["e_w_down"], "layer": l,
    }


def _mix_and_route(x, lw, b, t, pending, caches, experts_bf16):
    step = caches is not None
    y_prev, gates_prev = pending if pending is not None else (None, None)
    cast = () if experts_bf16 is not None else (lw["e_w_gate"], lw["e_w_up"], lw["e_w_down"])
    x, proj, kv, a_v, casted = _in_proj(x, lw, t, y_prev, gates_prev, emit_av=step, cast=cast)
    if experts_bf16 is None:
        experts_bf16 = casted
    if not step:
        yb = _attention(proj, lw, b, t)
        yd, sf, ya, yc, tail = _mixers_acd(proj, lw, b, t)
    else:
        ck, cv, cc, cs = caches
        yb = _attention(proj, lw, b, t, ck.reshape(b * B_WINDOW, GRP), cv.reshape(b * B_WINDOW, GRP))
        halo = jnp.pad(cc, ((0, 0), (HALO - C_BUF, 0), (0, 0))).reshape(b * HALO, GRP)
        yd, sf, ya, yc, tail = _mixers_acd(proj, lw, b, t, cs.reshape(b * GRP, HEAD_DIM), halo)
        a_v = a_v.reshape(b, t, GRP)
    x2, xn_packed, route_i, route_f, counts = _out_proj(ya, yb, yc, yd, x, lw)
    y = _moe(xn_packed, route_i, counts, experts_bf16)
    keep = min(B_WINDOW, t)
    new_k = kv[0].reshape(b, keep, HEADS, HEAD_DIM)
    new_v = kv[1].reshape(b, keep, HEADS, HEAD_DIM)
    new_buf = tail.reshape(b, HALO, GRP)[:, HALO - C_BUF:]
    states = (new_k, new_v, new_buf, sf.reshape(b, HEADS, HEAD_DIM, HEAD_DIM), a_v)
    return x2, (y, route_f), states, experts_bf16


def kernel(x_prompt, x_sample, cache_b_k, cache_b_v, state_c_conv, state_d_gla, norm1_g, w_in, a_vnorm_g, a_ws, a_bs, b_qnorm_g, b_knorm_g, b_rel_bias, c_dw, c_dw_b, c_ln_g, c_ln_b, d_wg2, d_bg, d_onorm_g, w_out, norm2_g, r_group_w, r_group_b, r_expert_w, r_expert_b, e_w_gate, e_w_up, e_w_down):
    params = dict(norm1_g=norm1_g, w_in=w_in, a_vnorm_g=a_vnorm_g, a_ws=a_ws, a_bs=a_bs, b_qnorm_g=b_qnorm_g,
                  b_knorm_g=b_knorm_g, b_rel_bias=b_rel_bias, c_dw=c_dw, c_dw_b=c_dw_b, c_ln_g=c_ln_g, c_ln_b=c_ln_b,
                  d_wg2=d_wg2, d_bg=d_bg, d_onorm_g=d_onorm_g, w_out=w_out, norm2_g=norm2_g, r_group_w=r_group_w,
                  r_group_b=r_group_b, r_expert_w=r_expert_w, r_expert_b=r_expert_b, e_w_gate=e_w_gate,
                  e_w_up=e_w_up, e_w_down=e_w_down)
    depth = w_in.shape[0]
    bp, tp, _ = x_prompt.shape
    bs, ts, _ = x_sample.shape
    xp = x_prompt.reshape(bp * tp, D_MODEL)
    xs = x_sample.reshape(bs * ts, D_MODEL)
    pend_p = pend_s = None
    st_p, st_s = [], []
    for l in range(depth):
        lw = _layer_weights(l, params)
        xp, pend_p, sp, experts = _mix_and_route(xp, lw, bp, tp, pend_p, None, None)
        xs, pend_s, ss, _ = _mix_and_route(xs, lw, bs, ts, pend_s,
                                           (cache_b_k[l], cache_b_v[l], state_c_conv[l], state_d_gla[l]), experts)
        st_p.append(sp)
        st_s.append(ss)
    yp = _combine(xp, pend_p[0], pend_p[1]).reshape(bp, tp, D_MODEL)
    ys = _combine(xs, pend_s[0], pend_s[1]).reshape(bs, ts, D_MODEL)
    stack = lambda sts, i: jnp.stack([s[i] for s in sts])
    return (yp, ys, stack(st_p, 0), stack(st_p, 1), stack(st_p, 2), stack(st_p, 3),
            stack(st_s, 0), stack(st_s, 1), stack(st_s, 2), stack(st_s, 3), stack(st_s, 4))
```

```python
import functools

import numpy as np
import jax
import jax.numpy as jnp
from jax import lax
from jax.experimental import pallas as pl
from jax.experimental.pallas import tpu as pltpu
from jax.experimental.pallas import tpu_sc as plsc

F32 = jnp.float32
BF16 = jnp.bfloat16
I32 = jnp.int32
U32 = jnp.uint32

D_MODEL = 1024
GRP = 256
HEADS = 4
HEAD_DIM = 64
CHUNK = 64
A_CHUNK = 128
B_WINDOW = 512
REL_CLIP = 128
C_WIDTH = 31
C_BUF = C_WIDTH - 1
HALO = 32
GATE_RANK = 16
GLA_TAU = 16.0
GLA_SUB = 16
N_GROUPS = 4
PER_GROUP = 8
N_EXPERTS = 32
D_EXPERT = 512
EPS = 1e-6
NEG_INF = -1e30
LANES = 128
VMEM_LIMIT = 48 * 1024 * 1024
TILE_ROWS = 512
WIDE_TILE_ROWS = 1024
MOE_BLOCK_ROWS = 512
MOE_BLOCK_ROWS_SMALL = 128

PK, PV, PQ, PAU, PAV, PGLU, PDQ, PDK, PDV, PDR, PLA = range(11)
N_PROJ = 11
_REF_GROUPS = (PAU, PAV, PQ, PK, PV, None, None, PDQ, PDK, PDV, PDR)

SC_WORKERS = 32
SC_WIN = 128


def _cparams(sem):
    return pltpu.CompilerParams(dimension_semantics=sem, vmem_limit_bytes=VMEM_LIMIT)


def _sigmoid(x):
    return 1.0 / (1.0 + jnp.exp(-x))


def _gelu_tanh(x):
    c = np.float32(np.sqrt(2.0 / np.pi))
    return 0.5 * x * (1.0 + jnp.tanh(c * (x + np.float32(0.044715) * (x * x * x))))


def _pack_halves(y):
    half = y.shape[1] // 2
    hi = pltpu.bitcast(y[:, :half].astype(BF16).astype(F32), U32)
    lo = pltpu.bitcast(y[:, half:].astype(BF16).astype(F32), U32)
    return hi | (lo >> np.uint32(16))


def _unpack_hi(w):
    return pltpu.bitcast(w & np.uint32(0xFFFF0000), F32)


def _unpack_lo(w):
    return pltpu.bitcast(w << np.uint32(16), F32)


def _head_id(shape, axis, size):
    return lax.broadcasted_iota(I32, shape, axis) // size


def _bd_stack(x, rows):
    x4 = jnp.concatenate([x] * HEADS, axis=0)
    shape = (HEADS * rows, GRP)
    keep = _head_id(shape, 0, rows) == _head_id(shape, 1, HEAD_DIM)
    return jnp.where(keep, x4, jnp.zeros_like(x4))


def _bd_unstack(o, rows):
    lane_h = _head_id((rows, GRP), 1, HEAD_DIM)
    out = o[(HEADS - 1) * rows:HEADS * rows]
    for h in range(HEADS - 2, -1, -1):
        out = jnp.where(lane_h == h, o[h * rows:(h + 1) * rows], out)
    return out


def _head_meansq(o, hsum_ref):
    sq = (o * o).astype(BF16)
    return jnp.dot(sq, hsum_ref[...], preferred_element_type=F32) * np.float32(1.0 / HEAD_DIM)


def _in_kernel(*refs, combine, emit_av, tiles_per_stream, n_cast):
    refs = list(refs)
    x_ref = refs.pop(0)
    if combine:
        y_ref = refs.pop(0)
        gate_ref = refs.pop(0)
    g1_ref, w_ref, wcg_ref, wdg_ref, wg2_ref, bg_ref, gq_ref, gk_ref, gav_ref, hsum_ref = refs[:10]
    cast_in = refs[10:10 + n_cast]
    refs = refs[10 + n_cast:]
    if combine:
        xo_ref = refs.pop(0)
    p_ref = refs.pop(0)
    kc_ref = refs.pop(0)
    vc_ref = refs.pop(0)
    if emit_av:
        av_ref = refs.pop(0)
    cast_out = refs[:n_cast]
    raw = refs[n_cast]
    n_sub = 1
    sub = x_ref.shape[0] // n_sub

    def prologue(s):
        rows = slice(s * sub, (s + 1) * sub)
        x = x_ref[rows, :]
        if combine:
            half = D_MODEL // 2
            g = gate_ref[rows, :]
            g0 = g[:, 0:1]
            g1 = g[:, 1:2]
            w0 = y_ref[0, rows, :]
            w1 = y_ref[1, rows, :]
            xa = x[:, :half] + g0 * _unpack_hi(w0) + g1 * _unpack_hi(w1)
            xb = x[:, half:] + g0 * _unpack_lo(w0) + g1 * _unpack_lo(w1)
            xo_ref[rows, :half] = xa
            xo_ref[rows, half:] = xb
            x = jnp.concatenate([xa, xb], axis=1)
        rs = lax.rsqrt(jnp.mean(x * x, axis=-1, keepdims=True) + EPS)
        return rows, rs, (x * g1_ref[...]).astype(BF16)

    n_slots = PLA + 2

    def matmul(tile, slot):
        rows, _, h = tile
        if slot < PLA:
            raw[rows, slot * GRP:(slot + 1) * GRP] = jnp.dot(h, w_ref[:, slot * GRP:(slot + 1) * GRP],
                                                             preferred_element_type=F32)
        elif slot == PLA:
            raw[rows, PLA * GRP:(PLA + 1) * GRP] = jnp.dot(h, wcg_ref[...], preferred_element_type=F32)
        else:
            raw[rows, (PLA + 1) * GRP:] = jnp.dot(h, wdg_ref[...], preferred_element_type=F32)

    def epilogue(tile, slot):
        rows, rs, _ = tile

        def proj(g):
            return raw[rows, g * GRP:(g + 1) * GRP] * rs

        def put(g, val):
            p_ref[rows, g * GRP:(g + 1) * GRP] = val.astype(BF16)

        if slot == PK:
            r = proj(PK)
            put(PK, r * lax.rsqrt(_head_meansq(r, hsum_ref) + EPS) * gk_ref[...])
        elif slot == PV:
            put(PV, proj(PV))

            @pl.when(pl.program_id(0) % tiles_per_stream == tiles_per_stream - 1)
            def _():
                r = proj(PK)
                kc_ref[rows, :] = r * lax.rsqrt(_head_meansq(r, hsum_ref) + EPS) * gk_ref[...]
                vc_ref[rows, :] = proj(PV)
        elif slot == PQ:
            r = proj(PQ)
            put(PQ, r * lax.rsqrt(_head_meansq(r, hsum_ref) + EPS) * (gq_ref[...] * np.float32(HEAD_DIM ** -0.5)))
        elif slot == PAU:
            put(PAU, _gelu_tanh(proj(PAU)))
        elif slot == PAV:
            r = _gelu_tanh(proj(PAV))
            av = r * lax.rsqrt(jnp.mean(r * r, axis=-1, keepdims=True) + EPS) * gav_ref[...]
            put(PAV, av)
            if emit_av:
                av_ref[rows, :] = av
        elif slot == PGLU:
            pass
        elif slot == PDQ:
            put(PDQ, proj(PDQ) * np.float32(HEAD_DIM ** -0.5))
        elif slot in (PDK, PDV):
            put(slot, proj(slot))
        elif slot == PDR:
            r = proj(PDR)
            put(PDR, r * _sigmoid(r))
        elif slot == PLA:
            put(PGLU, proj(PGLU) * _sigmoid(proj(PLA)))
        else:
            dg = raw[rows, (PLA + 1) * GRP:] * rs
            z = jnp.dot(dg.astype(BF16), wg2_ref[...], preferred_element_type=F32) + bg_ref[...]
            logsig = jnp.minimum(z, 0.0) - jnp.log(1.0 + jnp.exp(-jnp.abs(z)))
            put(PLA, logsig * np.float32(1.0 / GLA_TAU))

    order = (PLA + 1, PK, PQ, PAV, PAU, PGLU, PLA, PDR, PV, PDQ, PDK, PDV)
    assert sorted(order) == list(range(n_slots))
    pairs = [(s, slot) for s in range(n_sub) for slot in order]
    lag = 1
    tiles = {0: prologue(0)}
    for i in range(len(pairs) + lag):
        if i < len(pairs):
            matmul(tiles[pairs[i][0]], pairs[i][1])
        if i == lag and n_sub > 1:
            tiles[1] = prologue(1)
        if i >= lag:
            epilogue(tiles[pairs[i - lag][0]], pairs[i - lag][1])
        if i % 3 == 2 and i // 3 < n_cast:
            cast_out[i // 3][...] = cast_in[i // 3][...].astype(BF16)


def _in_proj(x, lw, t, y=None, gates=None, emit_av=False, cast=()):
    n = x.shape[0]
    tm = min(TILE_ROWS, n)
    steps = n // tm
    combine = y is not None
    keep = min(B_WINDOW, t)
    tps = max(t // tm, 1)
    assert tps == 1 or keep == tm
    row = lambda i: (i, 0)
    const = lambda i: (0, 0)
    ins, specs = [x], [pl.BlockSpec((tm, D_MODEL), row)]
    if combine:
        ins += [y, gates]
        specs += [pl.BlockSpec((2, tm, D_MODEL // 2), lambda i: (0, i, 0)), pl.BlockSpec((tm, LANES), row)]
    consts = [lw["g1"], lw["w_in"], lw["w_cg"], lw["w_dg"], lw["wg2"], lw["bg"], lw["gq"], lw["gk"], lw["gav"], lw["hsum"]]
    ins += consts
    specs += [pl.BlockSpec(c.shape, const) for c in consts]
    cast_shapes, cast_specs = [], []
    layer = lw["layer"]
    for arr in cast:
        depth, rows, cols = arr.shape[0], arr.shape[1] * arr.shape[2], arr.shape[3]
        slab = rows // steps
        assert slab * steps == rows and slab % 16 == 0
        ins.append(arr.reshape(depth * rows, cols))
        specs.append(pl.BlockSpec((slab, cols), lambda i: (layer * steps + i, 0)))
        cast_shapes.append(jax.ShapeDtypeStruct((rows, cols), BF16))
        cast_specs.append(pl.BlockSpec((slab, cols), row))
    newest = jax.ShapeDtypeStruct((n // tps, GRP), F32)
    newest_spec = pl.BlockSpec((tm, GRP), lambda i: (i // tps, 0))
    out_shape = [jax.ShapeDtypeStruct((n, N_PROJ * GRP), BF16), newest, newest]
    out_specs = [pl.BlockSpec((tm, N_PROJ * GRP), row), newest_spec, newest_spec]
    if combine:
        out_shape = [jax.ShapeDtypeStruct((n, D_MODEL), F32)] + out_shape
        out_specs = [pl.BlockSpec((tm, D_MODEL), row)] + out_specs
    if emit_av:
        out_shape.append(jax.ShapeDtypeStruct((n, GRP), F32))
        out_specs.append(pl.BlockSpec((tm, GRP), row))
    out_shape += cast_shapes
    out_specs += cast_specs
    outs = list(pl.pallas_call(
        functools.partial(_in_kernel, combine=combine, emit_av=emit_av, tiles_per_stream=tps, n_cast=len(cast)),
        grid=(steps,), in_specs=specs, out_specs=out_specs, out_shape=out_shape,
        scratch_shapes=[pltpu.VMEM((tm, (PLA + 1) * GRP + LANES), F32)],
        compiler_params=_cparams(("arbitrary",)), name="in_proj",
    )(*ins))
    x_new = outs.pop(0) if combine else x
    proj, k_new, v_new = outs[0], outs[1], outs[2]
    a_v = outs[3] if emit_av else None
    casted = [o.reshape(a.shape[1:]) for o, a in zip(outs[3 + int(emit_av):], cast)]
    return x_new, proj, (k_new, v_new), a_v, casted


def _gmlp_chunks(u_ref, v_ref, ws_ref, bs_ref, o_ref, chunk):
    lane_h = _head_id((chunk, GRP), 1, HEAD_DIM)
    ri = lax.broadcasted_iota(I32, (chunk, chunk), 0)
    ci = lax.broadcasted_iota(I32, (chunk, chunk), 1)
    wm = [jnp.where(ci <= ri, ws_ref[h], 0.0).astype(BF16) for h in range(HEADS)]

    def one(c):
        rows = slice(c * chunk, (c + 1) * chunk)
        v = v_ref[rows, :]
        sv = jnp.dot(wm[HEADS - 1], v, preferred_element_type=F32)
        for h in range(HEADS - 2, -1, -1):
            sv = jnp.where(lane_h == h, jnp.dot(wm[h], v, preferred_element_type=F32), sv)
        o_ref[rows, :] = (u_ref[rows, :].astype(F32) * (sv + bs_ref[...])).astype(BF16)

    return one


def _attn_kernel(q_ref, kc_ref, vc_ref, kp_ref, vp_ref, bias_ref, o_ref, kbuf, vbuf, *, chunk, n_chunks, first_has_past):
    tq = chunk * n_chunks
    win = B_WINDOW + chunk
    kbuf[0:B_WINDOW, :] = kp_ref[...].astype(BF16)
    vbuf[0:B_WINDOW, :] = vp_ref[...].astype(BF16)
    kbuf[B_WINDOW:B_WINDOW + tq, :] = kc_ref[...]
    vbuf[B_WINDOW:B_WINDOW + tq, :] = vc_ref[...]
    col = lax.broadcasted_iota(I32, (HEADS * chunk, win), 1)

    def chunks(no_past):
        for c in range(n_chunks):
            q = q_ref[c * chunk:(c + 1) * chunk, :]
            kk = kbuf[c * chunk:c * chunk + win, :]
            vv = vbuf[c * chunk:c * chunk + win, :]
            s = lax.dot_general(_bd_stack(q, chunk), kk, (((1,), (1,)), ((), ())), preferred_element_type=F32)
            s = s + bias_ref[...]
            if no_past:
                s = jnp.where(col + c * chunk >= B_WINDOW, s, NEG_INF)
            m = jnp.max(s, axis=-1, keepdims=True)
            p = jnp.exp(s - m)
            l = jnp.sum(p, axis=-1, keepdims=True)
            o = jnp.dot(p.astype(BF16), vv, preferred_element_type=F32) * (1.0 / l)
            o_ref[c * chunk:(c + 1) * chunk, :] = _bd_unstack(o, chunk).astype(BF16)

    if first_has_past:
        chunks(False)
    else:
        pl.when(pl.program_id(1) == 0)(functools.partial(chunks, True))
        pl.when(pl.program_id(1) > 0)(functools.partial(chunks, False))


def _attention(proj, lw, b, t, cache_k=None, cache_v=None):
    n = proj.shape[0]
    step = cache_k is not None
    chunk = min(t, CHUNK)
    tq = min(t, WIDE_TILE_ROWS)
    nt = t // tq
    rel = lw["b_rel"]
    win = B_WINDOW + chunk
    lo = REL_CLIP - (chunk - 1)
    n_far = (chunk - 1) + win - (2 * REL_CLIP + 1 - lo)
    by_dist = jnp.concatenate([rel[:, lo:], jnp.broadcast_to(rel[:, -1:], (HEADS, n_far))], axis=1)
    by_key = by_dist[:, ::-1]
    n_k = chunk - 1 + win
    wrapped = jnp.tile(jnp.pad(by_key, ((0, 0), (0, 1))), (1, chunk))[:, :chunk * n_k].reshape(HEADS, chunk, n_k)
    bias = wrapped[:, :, chunk - 1:].astype(F32).reshape(HEADS * chunk, win)
    cur = lambda g: pl.BlockSpec((tq, GRP), lambda bi, j: (bi * nt + j, g))
    if step:
        prev_k = pl.BlockSpec((B_WINDOW, GRP), lambda bi, j: (bi, 0))
        prev_v = prev_k
        pk_arr, pv_arr = cache_k, cache_v
    else:
        per = tq // B_WINDOW
        assert per * B_WINDOW == tq
        past = lambda bi, j: jnp.maximum((bi * nt + j) * per - 1, 0)
        prev_k = pl.BlockSpec((B_WINDOW, GRP), lambda bi, j: (past(bi, j), PK))
        prev_v = pl.BlockSpec((B_WINDOW, GRP), lambda bi, j: (past(bi, j), PV))
        pk_arr, pv_arr = proj, proj
    return pl.pallas_call(
        functools.partial(_attn_kernel, chunk=chunk, n_chunks=tq // chunk, first_has_past=step),
        grid=(b, nt),
        in_specs=[cur(PQ), cur(PK), cur(PV), prev_k, prev_v, pl.BlockSpec(bias.shape, lambda bi, j: (0, 0))],
        out_specs=pl.BlockSpec((tq, GRP), lambda bi, j: (bi * nt + j, 0)),
        out_shape=jax.ShapeDtypeStruct((n, GRP), BF16),
        scratch_shapes=[pltpu.VMEM((B_WINDOW + tq, GRP), BF16), pltpu.VMEM((B_WINDOW + tq, GRP), BF16)],
        compiler_params=_cparams(("arbitrary", "arbitrary")), name="band_attn",
    )(proj, proj, proj, pk_arr, pv_arr, bias)


def _conv_blocks(g_ref, halo_ref, dw_ref, dwb_ref, lng_ref, lnb_ref, o_ref, tail_ref, xp, zbuf,
                 *, tc, sub, first_has_past):
    halo = halo_ref[...].astype(F32)
    has_past = jnp.logical_or(pl.program_id(1) > 0, first_has_past)
    xp[0:HALO, :] = jnp.where(has_past, halo, 0.0)
    xp[HALO:HALO + tc, :] = g_ref[...].astype(F32)
    xp[HALO + tc:, :] = jnp.zeros((xp.shape[0] - HALO - tc, GRP), F32)

    @pl.when(pl.program_id(1) == pl.num_programs(1) - 1)
    def _():
        tail_ref[...] = xp[tc:tc + HALO, :]

    lead = HALO - C_BUF
    sl = 8

    def one(s):
        acc = None
        for r in range(sl):
            taps = [p for p in range(r, lead + C_WIDTH, sl) if p >= lead]
            z = None
            for p in taps:
                a0 = s * sub + p - r
                term = dw_ref[p - lead:p - lead + 1, :] * xp[a0:a0 + sub + sl, :]
                z = term if z is None else z + term
            zbuf[r] = z
            part = zbuf[r, r:r + sub, :]
            acc = part if acc is None else acc + part
        y = acc + dwb_ref[...]
        mu = jnp.mean(y, axis=-1, keepdims=True)
        yc = y - mu
        y = yc * lax.rsqrt(jnp.mean(yc * yc, axis=-1, keepdims=True) + EPS) * lng_ref[...] + lnb_ref[...]
        o_ref[s * sub:(s + 1) * sub, :] = (y * _sigmoid(y)).astype(BF16)

    return one


def _gla_tables(L):
    i = np.arange(L)[:, None]
    t = np.arange(L)[None, :]
    masks = []
    s = GLA_SUB
    masks.append(((i // s) == (t // s)) & (t <= i))
    s *= 2
    while s <= L:
        h = s // 2
        masks.append(((i // s) == (t // s)) & (i % s >= h) & (t % s < h))
        s *= 2
    tri = (t <= i).astype(np.float32)
    mask = np.stack([np.tile(m.astype(np.float32), (1, HEADS)) for m in masks], axis=0)
    return tri, mask


def _gla_anchor(cum, row, L, size, first_half):
    out = None
    for start in range(0, L, size):
        ar = start + size // 2 - 1 if first_half else start - 1
        val = jnp.zeros((L, GRP), F32) if ar < 0 else jnp.broadcast_to(cum[ar:ar + 1, :], (L, GRP))
        out = val if out is None else jnp.where(row >= start, val, out)
    return out


def _gla_kernel(q_ref, k_ref, v_ref, la_ref, dr_ref, s0_ref, tri_ref, lmask_ref, bdmask_ref, hsum_ref, gon_ref,
                au_ref, av_ref, ws_ref, bs_ref, g_ref, halo_ref, dw_ref, dwb_ref, lng_ref, lnb_ref,
                o_ref, sf_ref, ya_ref, yc_ref, tail_ref, st, o_all, xp, zbuf,
                *, L, n_chunks, n_levels, first_has_state, a_chunk, c_sub):
    j = pl.program_id(1)
    td = L * n_chunks
    gmlp_chunk = _gmlp_chunks(au_ref, av_ref, ws_ref, bs_ref, ya_ref, a_chunk)
    conv_block = _conv_blocks(g_ref, halo_ref, dw_ref, dwb_ref, lng_ref, lnb_ref, yc_ref, tail_ref, xp, zbuf,
                              tc=td, sub=c_sub, first_has_past=first_has_state)
    gmlps = [functools.partial(gmlp_chunk, g) for g in range(td // a_chunk)]
    convs = [functools.partial(conv_block, s) for s in range(td // c_sub)]
    extras = sorted(gmlps + convs, key=lambda f: (f.args[0] + 0.5) / (len(gmlps) if f.func is gmlp_chunk else len(convs)))

    @pl.when(j == 0)
    def _():
        st[...] = jnp.zeros_like(st)
        if first_has_state:
            for h in range(HEADS):
                blk = slice(h * HEAD_DIM, (h + 1) * HEAD_DIM)
                st[blk, blk] = s0_ref[blk, :].T

    row = lax.broadcasted_iota(I32, (L, GRP), 0)
    dn_t = (((1,), (1,)), ((), ()))

    def prep(c):
        rows = slice(c * L, (c + 1) * L)
        q = q_ref[rows, :].astype(F32)
        k = k_ref[rows, :].astype(F32)
        v = v_ref[rows, :]
        cum = jnp.dot(tri_ref[...], la_ref[rows, :], preferred_element_type=F32)
        total = cum[L - 1:L, :]
        pairs = []
        for lvl in range(n_levels):
            size = GLA_SUB << lvl
            if lvl == 0:
                local = cum - _gla_anchor(cum, row, L, size, False)
                ql = q * jnp.exp(local)
                kl = k * jnp.exp(-local)
            else:
                upper = (row & (size - 1)) >= (size // 2)
                d = cum - _gla_anchor(cum, row, L, size, True)
                w = jnp.exp(jnp.where(upper, d, -d))
                ql = jnp.where(upper, q * w, 0.0)
                kl = jnp.where(upper, 0.0, k * w)
            pairs.append((ql.astype(BF16), _bd_stack(kl.astype(BF16), L)))
        return dict(rows=rows, v=v, qp=(q * jnp.exp(cum)).astype(BF16), kst=(k * jnp.exp(total - cum)).astype(BF16),
                    decay=jnp.exp(total), pairs=pairs)

    def intra(p):
        att = None
        for lvl, (ql, kbd) in enumerate(p["pairs"]):
            a = lax.dot_general(ql, kbd, dn_t, preferred_element_type=F32) * lmask_ref[lvl]
            att = a if att is None else att + a
        p["o_intra"] = jnp.dot(att.astype(BF16), _bd_stack(p["v"], L), preferred_element_type=F32)
        p["upd"] = lax.dot_general(p["v"], p["kst"], (((0,), (0,)), ((), ())),
                                   preferred_element_type=F32) * bdmask_ref[...]
        return p

    def finish(p, s_t):
        o_all[p["rows"], :] = lax.dot_general(p["qp"], s_t.astype(BF16), dn_t, preferred_element_type=F32) + p["o_intra"]
        return s_t * p["decay"] + p["upd"]

    s_t = st[...]
    stage1, stage2 = {}, {}
    for step in range(n_chunks + 2):
        if step < n_chunks:
            stage1[step] = prep(step)
        if 0 <= step - 1 < n_chunks:
            stage2[step - 1] = intra(stage1.pop(step - 1))
        if 0 <= step - 2 < n_chunks:
            s_t = finish(stage2.pop(step - 2), s_t)
        for e, extra in enumerate(extras):
            if e * (n_chunks + 2) // len(extras) == step:
                extra()
    st[...] = s_t
    o = o_all[...]
    y = o * lax.rsqrt(_head_meansq(o, hsum_ref) + EPS) * gon_ref[...] * dr_ref[...].astype(F32)
    o_ref[...] = y.astype(BF16)

    @pl.when(j == pl.num_programs(1) - 1)
    def _():
        for h in range(HEADS):
            blk = slice(h * HEAD_DIM, (h + 1) * HEAD_DIM)
            sf_ref[blk, :] = st[blk, blk].T


def _mixers_acd(proj, lw, b, t, s0=None, conv_state=None):
    n = proj.shape[0]
    step = s0 is not None
    L = min(t, 64)
    td = min(t, 2 * WIDE_TILE_ROWS)
    nt = t // td
    n_levels = int(np.log2(L // GLA_SUB)) + 1
    tri, lmask = _gla_tables(L)
    tri = jnp.asarray(tri, BF16)
    lmask = jnp.asarray(lmask, F32)
    if not step:
        s0 = jnp.zeros((GRP, HEAD_DIM), F32)
        s0_spec = pl.BlockSpec((GRP, HEAD_DIM), lambda bi, j: (0, 0))
    else:
        s0_spec = pl.BlockSpec((GRP, HEAD_DIM), lambda bi, j: (bi, 0))
    cur = lambda g: pl.BlockSpec((td, GRP), lambda bi, j: (bi * nt + j, g))
    c2 = lambda bi, j: (0, 0)
    a_chunk = min(t, A_CHUNK)
    ws = lw["a_ws"][:, :a_chunk, :a_chunk]
    bs = lw["a_bs_rows"][:a_chunk]
    c_sub = min(td, 128)
    if step:
        halo_arr = conv_state
        halo_spec = pl.BlockSpec((HALO, GRP), lambda bi, j: (bi, 0))
    else:
        per = td // HALO
        halo_arr = proj
        halo_spec = pl.BlockSpec((HALO, GRP), lambda bi, j: (jnp.maximum((bi * nt + j) * per - 1, 0), PGLU))
    vec = pl.BlockSpec((1, GRP), c2)
    act = jax.ShapeDtypeStruct((n, GRP), BF16)
    return pl.pallas_call(
        functools.partial(_gla_kernel, L=L, n_chunks=td // L, n_levels=n_levels, first_has_state=step,
                          a_chunk=a_chunk, c_sub=c_sub),
        grid=(b, nt),
        in_specs=[cur(PDQ), cur(PDK), cur(PDV), cur(PLA), cur(PDR), s0_spec,
                  pl.BlockSpec(tri.shape, c2), pl.BlockSpec(lmask.shape, lambda bi, j: (0, 0, 0)),
                  pl.BlockSpec((GRP, GRP), c2), pl.BlockSpec((GRP, GRP), c2), vec,
                  cur(PAU), cur(PAV), pl.BlockSpec(ws.shape, lambda bi, j: (0, 0, 0)), pl.BlockSpec(bs.shape, c2),
                  cur(PGLU), halo_spec, pl.BlockSpec((C_WIDTH, GRP), c2), vec, vec, vec],
        out_specs=[cur(0), pl.BlockSpec((GRP, HEAD_DIM), lambda bi, j: (bi, 0)), cur(0), cur(0),
                   pl.BlockSpec((HALO, GRP), lambda bi, j: (bi, 0))],
        out_shape=[act, jax.ShapeDtypeStruct((b * GRP, HEAD_DIM), F32), act, act,
                   jax.ShapeDtypeStruct((b * HALO, GRP), F32)],
        scratch_shapes=[pltpu.VMEM((GRP, GRP), F32), pltpu.VMEM((td, GRP), F32),
                        pltpu.VMEM((HALO + td + 8, GRP), F32), pltpu.VMEM((8, c_sub + 8, GRP), F32)],
        compiler_params=_cparams(("arbitrary", "arbitrary")), name="mixers_acd",
    )(proj, proj, proj, proj, proj, s0, tri, lmask, lw["bdmask"], lw["hsum"], lw["gon"], proj, proj, ws, bs,
      proj, halo_arr, lw["c_dw"], lw["c_dw_b"], lw["c_ln_g"], lw["c_ln_b"])


def _out_kernel(ya_ref, yb_ref, yc_ref, yd_ref, x_ref, wo_ref, g2_ref, wr_ref, br_ref, tri_ref,
                xo_ref, xn_ref, ri_ref, rf_ref, cnt_ref, *, tm, n_sub):
    @pl.when(pl.program_id(0) == 0)
    def _():
        cnt_ref[...] = jnp.zeros_like(cnt_ref)

    sub = tm // n_sub
    lane = lax.broadcasted_iota(I32, (sub, LANES), 1)
    lane_f = lane.astype(F32)
    lane_grp_f = (lane // PER_GROUP).astype(F32)
    is_grp = jnp.logical_and(lane >= N_EXPERTS, lane < N_EXPERTS + N_GROUPS)
    far = np.float32(1 << 20)

    def mix(s):
        rows = slice(s * sub, (s + 1) * sub)
        ycat = jnp.concatenate([ya_ref[rows, :], yb_ref[rows, :], yc_ref[rows, :], yd_ref[rows, :]], axis=1)
        x = x_ref[rows, :] + jnp.dot(ycat, wo_ref[...], preferred_element_type=F32)
        xo_ref[rows, :] = x
        return x

    def norm_logits(s, x):
        rows = slice(s * sub, (s + 1) * sub)
        xn = x * lax.rsqrt(jnp.mean(x * x, axis=-1, keepdims=True) + EPS) * g2_ref[...]
        xn_ref[rows, :] = _pack_halves(xn)
        both = jnp.dot(xn.astype(BF16), wr_ref[...], preferred_element_type=F32)
        return both[:, :LANES] + both[:, LANES:] + br_ref[...]

    def route(s, logits):
        rows = slice(s * sub, (s + 1) * sub)

        def first_max(masked):
            v = jnp.max(masked, axis=-1, keepdims=True)
            return v, jnp.min(jnp.where(masked == v, lane_f, far), axis=-1, keepdims=True)

        grp_logits = jnp.where(is_grp, logits, -jnp.inf)
        gmax, gidx = first_max(grp_logits)
        p_grp = 1.0 / jnp.sum(jnp.exp(grp_logits - gmax), axis=-1, keepdims=True)
        in_grp = lane_grp_f == gidx - np.float32(N_EXPERTS)
        exp_logits = jnp.where(in_grp, logits, -jnp.inf)
        v1, i1 = first_max(exp_logits)
        v2, i2 = first_max(jnp.where(lane_f == i1, -jnp.inf, exp_logits))
        e21 = jnp.exp(v2 - v1)
        gate1 = p_grp / (1.0 + e21)
        gate2 = p_grp * e21 / (1.0 + e21)

        oh1 = lane_f == i1
        oh2 = lane_f == i2
        ones = jnp.where(oh1, 1.0, jnp.where(oh2, 1.0, 0.0))
        before = jnp.dot(tri_ref[...], ones.astype(BF16), preferred_element_type=F32) + cnt_ref[...].astype(F32)
        rank1 = jnp.sum(jnp.where(oh1, before, 0.0), axis=-1, keepdims=True)
        rank2 = jnp.sum(jnp.where(oh2, before, 0.0), axis=-1, keepdims=True)
        cnt_ref[...] = cnt_ref[...] + jnp.sum(ones, axis=0, keepdims=True).astype(I32)

        ri = jnp.where(lane == 0, i1, jnp.where(lane == 1, i2, jnp.where(lane == 2, rank1,
                                                                          jnp.where(lane == 3, rank2, 0.0))))
        ri_ref[rows, :] = ri.astype(I32)
        rf_ref[rows, :] = jnp.where(lane == 0, gate1, jnp.where(lane == 1, gate2, 0.0))

    xs, lg = {}, {}
    for step in range(n_sub + 2):
        if step < n_sub:
            xs[step] = mix(step)
        if 0 <= step - 1 < n_sub:
            lg[step - 1] = norm_logits(step - 1, xs.pop(step - 1))
        if 0 <= step - 2 < n_sub:
            route(step - 2, lg.pop(step - 2))


def _out_proj(ya, yb, yc, yd, x, lw):
    n = x.shape[0]
    tm = min(WIDE_TILE_ROWS, n)
    row = lambda i: (i, 0)
    const = lambda i: (0, 0)
    n_sub = 4 if tm >= WIDE_TILE_ROWS else 1
    sub = tm // n_sub
    tri = jnp.asarray(np.tril(np.ones((sub, sub), np.float32), -1), BF16)
    consts = [lw["w_out"], lw["g2"], lw["wr"], lw["br"], tri]
    yspec = pl.BlockSpec((tm, GRP), row)
    return pl.pallas_call(
        functools.partial(_out_kernel, tm=tm, n_sub=n_sub),
        grid=(n // tm,),
        in_specs=[yspec, yspec, yspec, yspec, pl.BlockSpec((tm, D_MODEL), row)] + [pl.BlockSpec(c.shape, const) for c in consts],
        out_specs=[pl.BlockSpec((tm, D_MODEL), row), pl.BlockSpec((tm, D_MODEL // 2), row),
                   pl.BlockSpec((tm, LANES), row), pl.BlockSpec((tm, LANES), row), pl.BlockSpec((1, LANES), const)],
        out_shape=[jax.ShapeDtypeStruct((n, D_MODEL), F32), jax.ShapeDtypeStruct((n, D_MODEL // 2), U32),
                   jax.ShapeDtypeStruct((n, LANES), I32), jax.ShapeDtypeStruct((n, LANES), F32),
                   jax.ShapeDtypeStruct((1, LANES), I32)],
        compiler_params=_cparams(("arbitrary",)), name="out_proj_router",
    )(ya, yb, yc, yd, x, *consts)


def _sc_scatter_rows(x, idx, n_out):
    n, d = x.shape
    kk = idx.shape[0]
    per_w = n // SC_WORKERS
    win = min(SC_WIN, per_w)
    n_win = per_w // win
    assert n_win * win * SC_WORKERS == n
    mesh = plsc.VectorSubcoreMesh(core_axis_name="c", subcore_axis_name="s")

    @functools.partial(
        pl.kernel, mesh=mesh, out_type=jax.ShapeDtypeStruct((n_out, d), x.dtype),
        scratch_types=[pltpu.VMEM((kk, win), I32), pltpu.VMEM((win, d), x.dtype)],
        name="sc_scatter_rows")
    def k(x_hbm, idx_hbm, o_hbm, idx_v, rows_v):
        wid = lax.axis_index("s") * 2 + lax.axis_index("c")
        base = wid * per_w

        @pl.loop(0, n_win)
        def _(w):
            off = base + w * win
            pltpu.sync_copy(x_hbm.at[pl.ds(off, win)], rows_v)
            for j in range(kk):
                pltpu.sync_copy(idx_hbm.at[j, pl.ds(off, win)], idx_v.at[j])
                pltpu.sync_copy(rows_v, o_hbm.at[idx_v.at[j]])

    return k(x, idx)


def _sc_gather_rows(y, idx):
    _, d = y.shape
    kk, n = idx.shape
    per_w = n // SC_WORKERS
    win = min(SC_WIN, per_w)
    n_win = per_w // win
    assert n_win * win * SC_WORKERS == n
    mesh = plsc.VectorSubcoreMesh(core_axis_name="c", subcore_axis_name="s")

    @functools.partial(
        pl.kernel, mesh=mesh, out_type=jax.ShapeDtypeStruct((kk, n, d), y.dtype),
        scratch_types=[pltpu.VMEM((kk, win), I32), pltpu.VMEM((win, d), y.dtype)],
        name="sc_gather_rows")
    def k(y_hbm, idx_hbm, o_hbm, idx_v, rows_v):
        wid = lax.axis_index("s") * 2 + lax.axis_index("c")
        base = wid * per_w

        @pl.loop(0, n_win)
        def _(w):
            off = base + w * win
            for j in range(kk):
                pltpu.sync_copy(idx_hbm.at[j, pl.ds(off, win)], idx_v.at[j])
                pltpu.sync_copy(y_hbm.at[idx_v.at[j]], rows_v)
                pltpu.sync_copy(rows_v, o_hbm.at[j, pl.ds(off, win)])

    return k(y, idx)


def _moe_kernel(bexp_ref, nused_ref, x_ref, wg_ref, wu_ref, wd_ref, o_ref, *, n_sub):
    del bexp_ref

    @pl.when(pl.program_id(0) < nused_ref[0])
    def _():
        sub = x_ref.shape[0] // n_sub

        def up(s):
            w = x_ref[s * sub:(s + 1) * sub, :]
            x = jnp.concatenate([_unpack_hi(w).astype(BF16), _unpack_lo(w).astype(BF16)], axis=1)
            hg = jnp.dot(x, wg_ref[...], preferred_element_type=F32)
            hu = jnp.dot(x, wu_ref[...], preferred_element_type=F32)
            return (hg * _sigmoid(hg) * hu).astype(BF16)

        def down(s, h):
            o_ref[s * sub:(s + 1) * sub, :] = _pack_halves(jnp.dot(h, wd_ref[...], preferred_element_type=F32))

        h = up(0)
        for s in range(n_sub):
            nxt = up(s + 1) if s + 1 < n_sub else None
            down(s, h)
            h = nxt


def _moe_experts(xs, blk_exp, n_used, w_gate, w_up, w_down, bm):
    p = xs.shape[0]
    n_blocks = p // bm
    live = lambda i, be, nu: jnp.minimum(i, jnp.maximum(nu[0] - 1, 0))
    wspec = lambda shape: pl.BlockSpec((None,) + shape, lambda i, be, nu: (be[live(i, be, nu)], 0, 0))
    grid_spec = pltpu.PrefetchScalarGridSpec(
        num_scalar_prefetch=2, grid=(n_blocks,),
        in_specs=[pl.BlockSpec((bm, D_MODEL // 2), lambda i, be, nu: (live(i, be, nu), 0)),
                  wspec((D_MODEL, D_EXPERT)), wspec((D_MODEL, D_EXPERT)), wspec((D_EXPERT, D_MODEL))],
        out_specs=pl.BlockSpec((bm, D_MODEL // 2), lambda i, be, nu: (live(i, be, nu), 0)))
    return pl.pallas_call(
        functools.partial(_moe_kernel, n_sub=2 if bm >= MOE_BLOCK_ROWS else 1),
        grid_spec=grid_spec, out_shape=jax.ShapeDtypeStruct((p, D_MODEL // 2), U32),
        compiler_params=_cparams(("arbitrary",)), name="moe_experts",
    )(blk_exp, n_used, xs, w_gate, w_up, w_down)


def _moe_block_rows(n):
    return MOE_BLOCK_ROWS if 2 * n >= N_EXPERTS * 2 * MOE_BLOCK_ROWS else MOE_BLOCK_ROWS_SMALL


def _moe(xn_packed, route_i, counts, experts_bf16):
    n = xn_packed.shape[0]
    bm = _moe_block_rows(n)
    n_blocks = -(-(2 * n + N_EXPERTS * (bm - 1)) // bm)
    cnt = counts[0, :N_EXPERTS]
    padded = (cnt + bm - 1) // bm * bm
    pad_end = jnp.cumsum(padded)
    pad_start = pad_end - padded
    experts = jnp.arange(N_EXPERTS, dtype=I32)
    eid = route_i[:, 0:2].T
    start_of = jnp.sum(jnp.where(eid[:, :, None] == experts, pad_start, 0), axis=-1)
    dest = (start_of + route_i[:, 2:4].T).astype(I32)
    first_row = jnp.arange(n_blocks, dtype=I32) * bm
    blk_exp = jnp.minimum(jnp.sum((pad_end[None, :] <= first_row[:, None]).astype(I32), axis=1), N_EXPERTS - 1)
    n_used = (pad_end[-1:] // bm).astype(I32)
    xs = _sc_scatter_rows(xn_packed, dest, n_blocks * bm)
    ys = _moe_experts(xs, blk_exp, n_used, *experts_bf16, bm)
    return _sc_gather_rows(ys, dest)


def _combine_kernel(x_ref, y_ref, gate_ref, o_ref):
    half = D_MODEL // 2
    x = x_ref[...]
    g = gate_ref[...]
    g0 = g[:, 0:1]
    g1 = g[:, 1:2]
    w0 = y_ref[0]
    w1 = y_ref[1]
    o_ref[:, :half] = x[:, :half] + g0 * _unpack_hi(w0) + g1 * _unpack_hi(w1)
    o_ref[:, half:] = x[:, half:] + g0 * _unpack_lo(w0) + g1 * _unpack_lo(w1)


def _combine(x, y, gates):
    n = x.shape[0]
    tm = min(WIDE_TILE_ROWS, n)
    row = lambda i: (i, 0)
    return pl.pallas_call(
        _combine_kernel, grid=(n // tm,),
        in_specs=[pl.BlockSpec((tm, D_MODEL), row), pl.BlockSpec((2, tm, D_MODEL // 2), lambda i: (0, i, 0)),
                  pl.BlockSpec((tm, LANES), row)],
        out_specs=pl.BlockSpec((tm, D_MODEL), row), out_shape=jax.ShapeDtypeStruct((n, D_MODEL), F32),
        compiler_params=_cparams(("arbitrary",)), name="moe_combine",
    )(x, y, gates)


def _layer_weights(l, p):
    w_in = p["w_in"][l]
    cols = [w_in[:, i * GRP:(i + 1) * GRP] for i in range(11)]
    by_group = [None] * N_PROJ
    for ref_i, g in enumerate(_REF_GROUPS):
        if g is not None:
            by_group[g] = cols[ref_i]
    by_group[PGLU] = cols[5]
    w_dg = jnp.zeros((D_MODEL, LANES), F32).at[:, :GATE_RANK].set(w_in[:, 11 * GRP:])
    wg2 = jnp.zeros((LANES, GRP), F32).at[:GATE_RANK].set(p["d_wg2"][l])
    tile4 = lambda v: jnp.tile(v, HEADS)[None, :]
    hid = np.arange(GRP) // HEAD_DIM
    bd = (hid[:, None] == hid[None, :]).astype(np.float32)
    wr = jnp.zeros((D_MODEL, LANES), F32).at[:, :N_EXPERTS].set(p["r_expert_w"][l])
    wr = wr.at[:, N_EXPERTS:N_EXPERTS + N_GROUPS].set(p["r_group_w"][l])
    wr_hi = wr.astype(BF16)
    br = jnp.zeros((1, LANES), F32).at[0, :N_EXPERTS].set(p["r_expert_b"][l])
    br = br.at[0, N_EXPERTS:N_EXPERTS + N_GROUPS].set(p["r_group_b"][l])
    return {
        "g1": p["norm1_g"][l][None, :],
        "w_in": jnp.concatenate(by_group[:PLA], axis=1).astype(BF16),
        "w_cg": cols[6].astype(BF16),
        "w_dg": w_dg.astype(BF16),
        "wg2": wg2.astype(BF16),
        "bg": p["d_bg"][l][None, :],
        "gq": tile4(p["b_qnorm_g"][l]), "gk": tile4(p["b_knorm_g"][l]), "gav": p["a_vnorm_g"][l][None, :],
        "gon": tile4(p["d_onorm_g"][l]),
        "hsum": jnp.asarray(bd, BF16), "bdmask": jnp.asarray(bd, F32),
        "a_ws": p["a_ws"][l], "a_bs_rows": jnp.repeat(p["a_bs"][l].T, HEAD_DIM, axis=1),
        "b_rel": p["b_rel_bias"][l],
        "c_dw": p["c_dw"][l], "c_dw_b": p["c_dw_b"][l][None, :],
        "c_ln_g": p["c_ln_g"][l][None, :], "c_ln_b": p["c_ln_b"][l][None, :],
        "w_out": p["w_out"][l].astype(BF16),
        "g2": p["norm2_g"][l][None, :],
        "wr": jnp.concatenate([wr_hi, (wr - wr_hi.astype(F32)).astype(BF16)], axis=1), "br": br,
        "e_w_gate": p["e_w_gate"], "e_w_up": p["e_w_up"], "e_w_down": p["e_w_down"], "layer": l,
    }


def _mix_and_route(x, lw, b, t, pending, caches, experts_bf16):
    step = caches is not None
    y_prev, gates_prev = pending if pending is not None else (None, None)
    cast = () if experts_bf16 is not None else (lw["e_w_gate"], lw["e_w_up"], lw["e_w_down"])
    x, proj, kv, a_v, casted = _in_proj(x, lw, t, y_prev, gates_prev, emit_av=step, cast=cast)
    if experts_bf16 is None:
        experts_bf16 = casted
    if not step:
        yb = _attention(proj, lw, b, t)
        yd, sf, ya, yc, tail = _mixers_acd(proj, lw, b, t)
    else:
        ck, cv, cc, cs = caches
        yb = _attention(proj, lw, b, t, ck.reshape(b * B_WINDOW, GRP), cv.reshape(b * B_WINDOW, GRP))
        halo = jnp.pad(cc, ((0, 0), (HALO - C_BUF, 0), (0, 0))).reshape(b * HALO, GRP)
        yd, sf, ya, yc, tail = _mixers_acd(proj, lw, b, t, cs.reshape(b * GRP, HEAD_DIM), halo)
        a_v = a_v.reshape(b, t, GRP)
    x2, xn_packed, route_i, route_f, counts = _out_proj(ya, yb, yc, yd, x, lw)
    y = _moe(xn_packed, route_i, counts, experts_bf16)
    keep = min(B_WINDOW, t)
    new_k = kv[0].reshape(b, keep, HEADS, HEAD_DIM)
    new_v = kv[1].reshape(b, keep, HEADS, HEAD_DIM)
    new_buf = tail.reshape(b, HALO, GRP)[:, HALO - C_BUF:]
    states = (new_k, new_v, new_buf, sf.reshape(b, HEADS, HEAD_DIM, HEAD_DIM), a_v)
    return x2, (y, route_f), states, experts_bf16


def kernel(x_prompt, x_sample, cache_b_k, cache_b_v, state_c_conv, state_d_gla, norm1_g, w_in, a_vnorm_g, a_ws, a_bs, b_qnorm_g, b_knorm_g, b_rel_bias, c_dw, c_dw_b, c_ln_g, c_ln_b, d_wg2, d_bg, d_onorm_g, w_out, norm2_g, r_group_w, r_group_b, r_expert_w, r_expert_b, e_w_gate, e_w_up, e_w_down):
    params = dict(norm1_g=norm1_g, w_in=w_in, a_vnorm_g=a_vnorm_g, a_ws=a_ws, a_bs=a_bs, b_qnorm_g=b_qnorm_g,
                  b_knorm_g=b_knorm_g, b_rel_bias=b_rel_bias, c_dw=c_dw, c_dw_b=c_dw_b, c_ln_g=c_ln_g, c_ln_b=c_ln_b,
                  d_wg2=d_wg2, d_bg=d_bg, d_onorm_g=d_onorm_g, w_out=w_out, norm2_g=norm2_g, r_group_w=r_group_w,
                  r_group_b=r_group_b, r_expert_w=r_expert_w, r_expert_b=r_expert_b, e_w_gate=e_w_gate,
                  e_w_up=e_w_up, e_w_down=e_w_down)
    depth = w_in.shape[0]
    bp, tp, _ = x_prompt.shape
    bs, ts, _ = x_sample.shape
    xp = x_prompt.reshape(bp * tp, D_MODEL)
    xs = x_sample.reshape(bs * ts, D_MODEL)
    pend_p = pend_s = None
    st_p, st_s = [], []
    for l in range(depth):
        lw = _layer_weights(l, params)
        xp, pend_p, sp, experts = _mix_and_route(xp, lw, bp, tp, pend_p, None, None)
        xs, pend_s, ss, _ = _mix_and_route(xs, lw, bs, ts, pend_s,
                                           (cache_b_k[l], cache_b_v[l], state_c_conv[l], state_d_gla[l]), experts)
        st_p.append(sp)
        st_s.append(ss)
    yp = _combine(xp, pend_p[0], pend_p[1]).reshape(bp, tp, D_MODEL)
    ys = _combine(xs, pend_s[0], pend_s[1]).reshape(bs, ts, D_MODEL)
    stack = lambda sts, i: jnp.stack([s[i] for s in sts])
    return (yp, ys, stack(st_p, 0), stack(st_p, 1), stack(st_p, 2), stack(st_p, 3),
            stack(st_s, 0), stack(st_s, 1), stack(st_s, 2), stack(st_s, 3), stack(st_s, 4))
```
